```python
import jax, jax.numpy as jnp
from jax import lax
import numpy as np

D_MODEL = 1024
BATCH = 16
SEQ = 2048
DEPTH = 2

CHUNK = 64
RET_HEADS = 4
RET_QK_DIM = 128
RET_V_DIM = 256
RET_QK = RET_HEADS * RET_QK_DIM
RET_V = RET_HEADS * RET_V_DIM
SC_WIDTH = D_MODEL
SC_KERNEL = 3
CF_WIDTH = D_MODEL
CF_KERNEL = 31
N_BRANCH = 3
D_FF = 4 * D_MODEL
ROPE_BASE = 10000.0
NORM_EPS = 1e-6
LN_EPS = 1e-5
N_NORMS = 6
IN_SPLITS = (RET_QK, RET_QK, RET_V, RET_V, SC_WIDTH, SC_WIDTH, SC_WIDTH, 2 * CF_WIDTH, N_BRANCH * D_MODEL)
D_IN = 2 * RET_QK + 2 * RET_V + 3 * SC_WIDTH + 2 * CF_WIDTH + N_BRANCH * D_MODEL

kernel_name = "hybrid_retention_conv_macaron_encoder"


def rms_norm(x, g):
    xf = x.astype(jnp.float32)
    y = xf * lax.rsqrt(jnp.mean(xf * xf, axis=-1, keepdims=True) + NORM_EPS)
    return (y * g.astype(jnp.float32)).astype(x.dtype)


def layer_norm(x, g, b):
    xf = x.astype(jnp.float32)
    mu = jnp.mean(xf, axis=-1, keepdims=True)
    var = jnp.mean(jnp.square(xf - mu), axis=-1, keepdims=True)
    y = (xf - mu) * lax.rsqrt(var + LN_EPS)
    return (y * g.astype(jnp.float32) + b.astype(jnp.float32)).astype(x.dtype)


def swiglu_ffn(h, w_gu, w_down):
    gate, up = jnp.split(h @ w_gu, 2, axis=-1)
    return (jax.nn.silu(gate) * up) @ w_down


def causal_depthwise_conv(x, w):
    k = w.shape[0]
    return lax.conv_general_dilated(
        x, w[:, None, :].astype(x.dtype), window_strides=(1,), padding=[(k - 1, 0)],
        dimension_numbers=("NWC", "WIO", "NWC"), feature_group_count=x.shape[-1])


def rotary(x, positions):
    half = x.shape[-1] // 2
    inv_freq = ROPE_BASE ** (-jnp.arange(half, dtype=jnp.float32) / half)
    ang = positions.astype(jnp.float32)[..., None] * inv_freq
    cos = jnp.cos(ang)[:, :, None, :]
    sin = jnp.sin(ang)[:, :, None, :]
    x1, x2 = x[..., :half], x[..., half:]
    return jnp.concatenate([x1 * cos - x2 * sin, x2 * cos + x1 * sin], axis=-1)


def chunkwise_retention(q, k, v, positions):
    b, s, h, dk = q.shape
    dv = v.shape[-1]
    n = s // CHUNK
    q = rotary(q, positions)
    k = rotary(k, positions) * (dk ** -0.5)
    log_g = jnp.log(1.0 - 2.0 ** (-5.0 - jnp.arange(h, dtype=jnp.float32)))
    idx = jnp.arange(CHUNK, dtype=jnp.float32)
    decay_intra = jnp.exp(log_g[:, None, None] * jnp.abs(idx[:, None] - idx[None, :]))
    xi = jnp.exp(log_g[None, :] * (idx[:, None] + 1.0))
    zeta = jnp.exp(log_g[None, :] * (CHUNK - 1.0 - idx[:, None]))
    g_chunk = jnp.exp(log_g * CHUNK)

    qc = q.reshape(b, n, CHUNK, h, dk)
    kc = k.reshape(b, n, CHUNK, h, dk)
    vc = v.reshape(b, n, CHUNK, h, dv)
    scores = jnp.einsum("bnihd,bnjhd->bnhij", qc, kc) * decay_intra
    intra = jnp.einsum("bnhij,bnjhe->bnihe", scores, vc)

    kz = kc * zeta[None, None, :, :, None]

    def step(state, inp):
        q_n, k_n, v_n = inp
        inter = jnp.einsum("bihd,bhde->bihe", q_n, state) * xi[None, :, :, None]
        state = state * g_chunk[None, :, None, None] + jnp.einsum("bjhd,bjhe->bhde", k_n, v_n)
        return state, inter

    state0 = jnp.zeros((b, h, dk, dv), jnp.float32)
    _, inter = lax.scan(step, state0, (qc.swapaxes(0, 1), kz.swapaxes(0, 1), vc.swapaxes(0, 1)))
    out = intra + inter.swapaxes(0, 1)
    return out.reshape(b, s, h, dv)


def head_norm(o):
    mu = jnp.mean(o, axis=-1, keepdims=True)
    var = jnp.mean(jnp.square(o - mu), axis=-1, keepdims=True)
    return (o - mu) * lax.rsqrt(var + LN_EPS)


def hybrid_mixer(h, positions, w_in, w_ret_o, sc_conv_w, w_sc_o,
                 cf_dw_w, cf_dw_b, cf_ln_g, cf_ln_b, w_cf_o, w_o):
    b, s, _ = h.shape
    split_at = [int(v) for v in np.cumsum(IN_SPLITS)[:-1]]
    q, k, v, g_ret, sc_b, sc_c, sc_x, cf_in, gate_logits = jnp.split(h @ w_in, split_at, axis=-1)

    o = chunkwise_retention(
        q.reshape(b, s, RET_HEADS, RET_QK_DIM).astype(jnp.float32),
        k.reshape(b, s, RET_HEADS, RET_QK_DIM).astype(jnp.float32),
        v.reshape(b, s, RET_HEADS, RET_V_DIM).astype(jnp.float32),
        positions)
    o = head_norm(o).reshape(b, s, RET_V).astype(h.dtype)
    y_a = (jax.nn.silu(g_ret) * o) @ w_ret_o

    y_b = (sc_b * causal_depthwise_conv(sc_c * sc_x, sc_conv_w)) @ w_sc_o

    glu_a, glu_b = jnp.split(cf_in, 2, axis=-1)
    u = glu_a * jax.nn.sigmoid(glu_b)
    u = causal_depthwise_conv(u, cf_dw_w) + cf_dw_b
    u = jax.nn.silu(layer_norm(u, cf_ln_g, cf_ln_b))
    y_c = u @ w_cf_o

    gates = jax.nn.sigmoid(gate_logits).reshape(b, s, N_BRANCH, D_MODEL)
    merged = gates[:, :, 0] * y_a + gates[:, :, 1] * y_b + gates[:, :, 2] * y_c
    return merged @ w_o


def _fwd_setup_inputs(seed: int = 0) -> dict:
    key = jax.random.key(seed)
    ks = jax.random.split(key, 20)
    f32 = jnp.float32

    def dense(k, shape, fan_in):
        return jax.random.normal(k, shape, f32) * (fan_in ** -0.5)

    x = jax.random.normal(ks[0], (BATCH, SEQ, D_MODEL), f32)
    offsets = jax.random.randint(ks[1], (BATCH, 1), 0, 64, dtype=jnp.int32) * CHUNK
    positions = offsets + jnp.arange(SEQ, dtype=jnp.int32)[None, :]
    return {
        "x": x,
        "positions": positions,
        "norm_g": 1.0 + 0.02 * jax.random.normal(ks[2], (DEPTH, N_NORMS, D_MODEL), f32),
        "ffn1_w_gu": dense(ks[3], (DEPTH, D_MODEL, 2 * D_FF), D_MODEL),
        "ffn1_w_down": dense(ks[4], (DEPTH, D_FF, D_MODEL), D_FF),
        "w_in": dense(ks[5], (DEPTH, D_MODEL, D_IN), D_MODEL),
        "w_ret_o": dense(ks[6], (DEPTH, RET_V, D_MODEL), RET_V),
        "sc_conv_w": dense(ks[7], (DEPTH, SC_KERNEL, SC_WIDTH), SC_KERNEL),
        "w_sc_o": dense(ks[8], (DEPTH, SC_WIDTH, D_MODEL), SC_WIDTH),
        "cf_dw_w": dense(ks[9], (DEPTH, CF_KERNEL, CF_WIDTH), CF_KERNEL),
        "cf_dw_b": 0.02 * jax.random.normal(ks[10], (DEPTH, CF_WIDTH), f32),
        "cf_ln_g": 1.0 + 0.02 * jax.random.normal(ks[11], (DEPTH, CF_WIDTH), f32),
        "cf_ln_b": 0.02 * jax.random.normal(ks[12], (DEPTH, CF_WIDTH), f32),
        "w_cf_o": dense(ks[13], (DEPTH, CF_WIDTH, D_MODEL), CF_WIDTH),
        "w_o": dense(ks[14], (DEPTH, D_MODEL, D_MODEL), D_MODEL),
        "ffn2_w_gu": dense(ks[15], (DEPTH, D_MODEL, 2 * D_FF), D_MODEL),
        "ffn2_w_down": dense(ks[16], (DEPTH, D_FF, D_MODEL), D_FF),
    }


def _fwd_reference(x, positions, norm_g, ffn1_w_gu, ffn1_w_down, w_in, w_ret_o, sc_conv_w, w_sc_o,
              cf_dw_w, cf_dw_b, cf_ln_g, cf_ln_b, w_cf_o, w_o, ffn2_w_gu, ffn2_w_down):
    for l in range(DEPTH):
        g = norm_g[l]
        x = x + 0.5 * rms_norm(swiglu_ffn(rms_norm(x, g[0]), ffn1_w_gu[l], ffn1_w_down[l]), g[1])
        m = hybrid_mixer(rms_norm(x, g[2]), positions, w_in[l], w_ret_o[l], sc_conv_w[l], w_sc_o[l],
                         cf_dw_w[l], cf_dw_b[l], cf_ln_g[l], cf_ln_b[l], w_cf_o[l], w_o[l])
        x = x + rms_norm(m, g[3])
        x = x + 0.5 * rms_norm(swiglu_ffn(rms_norm(x, g[4]), ffn2_w_gu[l], ffn2_w_down[l]), g[5])
    return x


import jax as _jax
import jax.numpy as _jnp

TWIN_FORMAT = 'train_step'
FWD_PARAMS = ['x', 'positions', 'norm_g', 'ffn1_w_gu', 'ffn1_w_down', 'w_in', 'w_ret_o', 'sc_conv_w', 'w_sc_o', 'cf_dw_w', 'cf_dw_b', 'cf_ln_g', 'cf_ln_b', 'w_cf_o', 'w_o', 'ffn2_w_gu', 'ffn2_w_down']
TWIN_WEIGHTS = ['norm_g', 'ffn1_w_gu', 'ffn1_w_down', 'w_in', 'w_ret_o', 'sc_conv_w', 'w_sc_o', 'cf_dw_w', 'cf_dw_b', 'cf_ln_g', 'cf_ln_b', 'w_cf_o', 'w_o', 'ffn2_w_gu', 'ffn2_w_down']
TWIN_DIFF_INPUT = 'x'
TWIN_INPUTS = ['x', 'positions', 'norm_g', 'ffn1_w_gu', 'ffn1_w_down', 'w_in', 'w_ret_o', 'sc_conv_w', 'w_sc_o', 'cf_dw_w', 'cf_dw_b', 'cf_ln_g', 'cf_ln_b', 'w_cf_o', 'w_o', 'ffn2_w_gu', 'ffn2_w_down', 'loss_target', 'm_norm_g', 'm_ffn1_w_gu', 'm_ffn1_w_down', 'm_w_in', 'm_w_ret_o', 'm_sc_conv_w', 'm_w_sc_o', 'm_cf_dw_w', 'm_cf_dw_b', 'm_cf_ln_g', 'm_cf_ln_b', 'm_w_cf_o', 'm_w_o', 'm_ffn2_w_gu', 'm_ffn2_w_down', 'v_norm_g', 'v_ffn1_w_gu', 'v_ffn1_w_down', 'v_w_in', 'v_w_ret_o', 'v_sc_conv_w', 'v_w_sc_o', 'v_cf_dw_w', 'v_cf_dw_b', 'v_cf_ln_g', 'v_cf_ln_b', 'v_w_cf_o', 'v_w_o', 'v_ffn2_w_gu', 'v_ffn2_w_down']
TWIN_OUTPUTS = ['loss', 'grad_x', 'grad_norm_g', 'grad_ffn1_w_gu', 'grad_ffn1_w_down', 'grad_w_in', 'grad_w_ret_o', 'grad_sc_conv_w', 'grad_w_sc_o', 'grad_cf_dw_w', 'grad_cf_dw_b', 'grad_cf_ln_g', 'grad_cf_ln_b', 'grad_w_cf_o', 'grad_w_o', 'grad_ffn2_w_gu', 'grad_ffn2_w_down', 'delta_norm_g', 'delta_ffn1_w_gu', 'delta_ffn1_w_down', 'delta_w_in', 'delta_w_ret_o', 'delta_sc_conv_w', 'delta_w_sc_o', 'delta_cf_dw_w', 'delta_cf_dw_b', 'delta_cf_ln_g', 'delta_cf_ln_b', 'delta_w_cf_o', 'delta_w_o', 'delta_ffn2_w_gu', 'delta_ffn2_w_down', 'new_m_norm_g', 'new_m_ffn1_w_gu', 'new_m_ffn1_w_down', 'new_m_w_in', 'new_m_w_ret_o', 'new_m_sc_conv_w', 'new_m_w_sc_o', 'new_m_cf_dw_w', 'new_m_cf_dw_b', 'new_m_cf_ln_g', 'new_m_cf_ln_b', 'new_m_w_cf_o', 'new_m_w_o', 'new_m_ffn2_w_gu', 'new_m_ffn2_w_down', 'new_v_norm_g', 'new_v_ffn1_w_gu', 'new_v_ffn1_w_down', 'new_v_w_in', 'new_v_w_ret_o', 'new_v_sc_conv_w', 'new_v_w_sc_o', 'new_v_cf_dw_w', 'new_v_cf_dw_b', 'new_v_cf_ln_g', 'new_v_cf_ln_b', 'new_v_w_cf_o', 'new_v_w_o', 'new_v_ffn2_w_gu', 'new_v_ffn2_w_down']
TWIN_LEAF_KINDS = {'loss': 'loss', 'grad_x': 'grad_x', 'grad_norm_g': 'grad_w', 'grad_ffn1_w_gu': 'grad_w', 'grad_ffn1_w_down': 'grad_w', 'grad_w_in': 'grad_w', 'grad_w_ret_o': 'grad_w', 'grad_sc_conv_w': 'grad_w', 'grad_w_sc_o': 'grad_w', 'grad_cf_dw_w': 'grad_w', 'grad_cf_dw_b': 'grad_w', 'grad_cf_ln_g': 'grad_w', 'grad_cf_ln_b': 'grad_w', 'grad_w_cf_o': 'grad_w', 'grad_w_o': 'grad_w', 'grad_ffn2_w_gu': 'grad_w', 'grad_ffn2_w_down': 'grad_w', 'delta_norm_g': 'delta_w', 'delta_ffn1_w_gu': 'delta_w', 'delta_ffn1_w_down': 'delta_w', 'delta_w_in': 'delta_w', 'delta_w_ret_o': 'delta_w', 'delta_sc_conv_w': 'delta_w', 'delta_w_sc_o': 'delta_w', 'delta_cf_dw_w': 'delta_w', 'delta_cf_dw_b': 'delta_w', 'delta_cf_ln_g': 'delta_w', 'delta_cf_ln_b': 'delta_w', 'delta_w_cf_o': 'delta_w', 'delta_w_o': 'delta_w', 'delta_ffn2_w_gu': 'delta_w', 'delta_ffn2_w_down': 'delta_w', 'new_m_norm_g': 'new_m', 'new_m_ffn1_w_gu': 'new_m', 'new_m_ffn1_w_down': 'new_m', 'new_m_w_in': 'new_m', 'new_m_w_ret_o': 'new_m', 'new_m_sc_conv_w': 'new_m', 'new_m_w_sc_o': 'new_m', 'new_m_cf_dw_w': 'new_m', 'new_m_cf_dw_b': 'new_m', 'new_m_cf_ln_g': 'new_m', 'new_m_cf_ln_b': 'new_m', 'new_m_w_cf_o': 'new_m', 'new_m_w_o': 'new_m', 'new_m_ffn2_w_gu': 'new_m', 'new_m_ffn2_w_down': 'new_m', 'new_v_norm_g': 'new_v', 'new_v_ffn1_w_gu': 'new_v', 'new_v_ffn1_w_down': 'new_v', 'new_v_w_in': 'new_v', 'new_v_w_ret_o': 'new_v', 'new_v_sc_conv_w': 'new_v', 'new_v_w_sc_o': 'new_v', 'new_v_cf_dw_w': 'new_v', 'new_v_cf_dw_b': 'new_v', 'new_v_cf_ln_g': 'new_v', 'new_v_cf_ln_b': 'new_v', 'new_v_w_cf_o': 'new_v', 'new_v_w_o': 'new_v', 'new_v_ffn2_w_gu': 'new_v', 'new_v_ffn2_w_down': 'new_v'}


def _forward(args):
    return _fwd_reference(*[args[k] for k in FWD_PARAMS])


def _output_shape():
    out = _jax.eval_shape(lambda: _forward(_fwd_setup_inputs(0)))
    return out.shape, out.dtype

N_MICROBATCH = 1
ADAM_LR = 0.001
ADAM_B1 = 0.9
ADAM_B2 = 0.999
ADAM_EPS = 1e-08
ADAM_WD = 0.01
ADAM_STEP = 10
PER_EXAMPLE_BATCH_AXIS = {'x': 0, 'positions': 0, 'loss_target': 0}
SHARED_INPUTS = []
_WEIGHT_DTYPES = {'norm_g': _jnp.float32, 'ffn1_w_gu': _jnp.float32, 'ffn1_w_down': _jnp.float32, 'w_in': _jnp.float32, 'w_ret_o': _jnp.float32, 'sc_conv_w': _jnp.float32, 'w_sc_o': _jnp.float32, 'cf_dw_w': _jnp.float32, 'cf_dw_b': _jnp.float32, 'cf_ln_g': _jnp.float32, 'cf_ln_b': _jnp.float32, 'w_cf_o': _jnp.float32, 'w_o': _jnp.float32, 'ffn2_w_gu': _jnp.float32, 'ffn2_w_down': _jnp.float32}
MOMENT_SCALE = {'norm_g': 1.355213e+01, 'ffn1_w_gu': 2.552203e-01, 'ffn1_w_down': 5.137364e-01, 'w_in': 2.922494e-01, 'w_ret_o': 2.569565e-01, 'sc_conv_w': 4.196234e-01, 'w_sc_o': 4.292070e-01, 'cf_dw_w': 3.073552e-01, 'cf_dw_b': 2.546179e+00, 'cf_ln_g': 9.708988e-01, 'cf_ln_b': 1.556523e+00, 'w_cf_o': 5.726544e-01, 'w_o': 7.265504e-01, 'ffn2_w_gu': 1.563148e-01, 'ffn2_w_down': 3.424621e-01}


def _to_microbatches(a, axis):
    t = _jnp.moveaxis(a, axis, 0)
    t = t.reshape((N_MICROBATCH, t.shape[0] // N_MICROBATCH) + t.shape[1:])
    return _jnp.moveaxis(t, 1, axis + 1)


def setup_inputs(seed: int = 0) -> dict:
    inp = _fwd_setup_inputs(seed)
    key = _jax.random.fold_in(_jax.random.key(seed), 7919)
    shape, _ = _output_shape()
    out = dict(inp)
    out["loss_target"] = _jax.random.normal(_jax.random.fold_in(key, 0), shape, _jnp.float32)
    for i, name in enumerate(TWIN_WEIGHTS):
        w = inp[name].astype(_jnp.float32)
        if MOMENT_SCALE is None:
            s = _jnp.sqrt(_jnp.mean(_jnp.square(w)) + 1e-30)
        else:
            s = MOMENT_SCALE[name]
        km, kv = _jax.random.split(_jax.random.fold_in(key, i + 1))
        out[name] = w
        out["m_" + name] = s * _jax.random.normal(km, w.shape, _jnp.float32)
        out["v_" + name] = (s * s) * _jax.random.uniform(kv, w.shape, _jnp.float32, 0.5, 1.5)
    if N_MICROBATCH > 1:
        for name, axis in PER_EXAMPLE_BATCH_AXIS.items():
            out[name] = _to_microbatches(out[name], axis)
    return {'x': out['x'], 'positions': out['positions'], 'norm_g': out['norm_g'], 'ffn1_w_gu': out['ffn1_w_gu'], 'ffn1_w_down': out['ffn1_w_down'], 'w_in': out['w_in'], 'w_ret_o': out['w_ret_o'], 'sc_conv_w': out['sc_conv_w'], 'w_sc_o': out['w_sc_o'], 'cf_dw_w': out['cf_dw_w'], 'cf_dw_b': out['cf_dw_b'], 'cf_ln_g': out['cf_ln_g'], 'cf_ln_b': out['cf_ln_b'], 'w_cf_o': out['w_cf_o'], 'w_o': out['w_o'], 'ffn2_w_gu': out['ffn2_w_gu'], 'ffn2_w_down': out['ffn2_w_down'], 'loss_target': out['loss_target'], 'm_norm_g': out['m_norm_g'], 'm_ffn1_w_gu': out['m_ffn1_w_gu'], 'm_ffn1_w_down': out['m_ffn1_w_down'], 'm_w_in': out['m_w_in'], 'm_w_ret_o': out['m_w_ret_o'], 'm_sc_conv_w': out['m_sc_conv_w'], 'm_w_sc_o': out['m_w_sc_o'], 'm_cf_dw_w': out['m_cf_dw_w'], 'm_cf_dw_b': out['m_cf_dw_b'], 'm_cf_ln_g': out['m_cf_ln_g'], 'm_cf_ln_b': out['m_cf_ln_b'], 'm_w_cf_o': out['m_w_cf_o'], 'm_w_o': out['m_w_o'], 'm_ffn2_w_gu': out['m_ffn2_w_gu'], 'm_ffn2_w_down': out['m_ffn2_w_down'], 'v_norm_g': out['v_norm_g'], 'v_ffn1_w_gu': out['v_ffn1_w_gu'], 'v_ffn1_w_down': out['v_ffn1_w_down'], 'v_w_in': out['v_w_in'], 'v_w_ret_o': out['v_w_ret_o'], 'v_sc_conv_w': out['v_sc_conv_w'], 'v_w_sc_o': out['v_w_sc_o'], 'v_cf_dw_w': out['v_cf_dw_w'], 'v_cf_dw_b': out['v_cf_dw_b'], 'v_cf_ln_g': out['v_cf_ln_g'], 'v_cf_ln_b': out['v_cf_ln_b'], 'v_w_cf_o': out['v_w_cf_o'], 'v_w_o': out['v_w_o'], 'v_ffn2_w_gu': out['v_ffn2_w_gu'], 'v_ffn2_w_down': out['v_ffn2_w_down']}


def _loss(weights, diff, rest, loss_target):
    with _jax.named_scope("forward"):
        args = {**rest, TWIN_DIFF_INPUT: diff, **{k: w.astype(_WEIGHT_DTYPES[k]) for k, w in weights.items()}}
        y = _forward(args)
    with _jax.named_scope("loss_head"):
        err = _jnp.square(y.astype(_jnp.float32) - loss_target)
        return 0.5 * _jnp.sum(_jnp.mean(err, axis=-1)) if err.ndim else 0.5 * err


def _adamw(w, g, m, v):
    m = ADAM_B1 * m + (1.0 - ADAM_B1) * g
    v = ADAM_B2 * v + (1.0 - ADAM_B2) * _jnp.square(g)
    m_hat = m / (1.0 - ADAM_B1 ** ADAM_STEP)
    v_hat = v / (1.0 - ADAM_B2 ** ADAM_STEP)
    delta = -ADAM_LR * (m_hat / (_jnp.sqrt(v_hat) + ADAM_EPS) + ADAM_WD * w)
    return delta, m, v


def reference(x, positions, norm_g, ffn1_w_gu, ffn1_w_down, w_in, w_ret_o, sc_conv_w, w_sc_o, cf_dw_w, cf_dw_b, cf_ln_g, cf_ln_b, w_cf_o, w_o, ffn2_w_gu, ffn2_w_down, loss_target, m_norm_g, m_ffn1_w_gu, m_ffn1_w_down, m_w_in, m_w_ret_o, m_sc_conv_w, m_w_sc_o, m_cf_dw_w, m_cf_dw_b, m_cf_ln_g, m_cf_ln_b, m_w_cf_o, m_w_o, m_ffn2_w_gu, m_ffn2_w_down, v_norm_g, v_ffn1_w_gu, v_ffn1_w_down, v_w_in, v_w_ret_o, v_sc_conv_w, v_w_sc_o, v_cf_dw_w, v_cf_dw_b, v_cf_ln_g, v_cf_ln_b, v_w_cf_o, v_w_o, v_ffn2_w_gu, v_ffn2_w_down):
    given = dict(x=x, positions=positions, norm_g=norm_g, ffn1_w_gu=ffn1_w_gu, ffn1_w_down=ffn1_w_down, w_in=w_in, w_ret_o=w_ret_o, sc_conv_w=sc_conv_w, w_sc_o=w_sc_o, cf_dw_w=cf_dw_w, cf_dw_b=cf_dw_b, cf_ln_g=cf_ln_g, cf_ln_b=cf_ln_b, w_cf_o=w_cf_o, w_o=w_o, ffn2_w_gu=ffn2_w_gu, ffn2_w_down=ffn2_w_down, loss_target=loss_target, m_norm_g=m_norm_g, m_ffn1_w_gu=m_ffn1_w_gu, m_ffn1_w_down=m_ffn1_w_down, m_w_in=m_w_in, m_w_ret_o=m_w_ret_o, m_sc_conv_w=m_sc_conv_w, m_w_sc_o=m_w_sc_o, m_cf_dw_w=m_cf_dw_w, m_cf_dw_b=m_cf_dw_b, m_cf_ln_g=m_cf_ln_g, m_cf_ln_b=m_cf_ln_b, m_w_cf_o=m_w_cf_o, m_w_o=m_w_o, m_ffn2_w_gu=m_ffn2_w_gu, m_ffn2_w_down=m_ffn2_w_down, v_norm_g=v_norm_g, v_ffn1_w_gu=v_ffn1_w_gu, v_ffn1_w_down=v_ffn1_w_down, v_w_in=v_w_in, v_w_ret_o=v_w_ret_o, v_sc_conv_w=v_sc_conv_w, v_w_sc_o=v_w_sc_o, v_cf_dw_w=v_cf_dw_w, v_cf_dw_b=v_cf_dw_b, v_cf_ln_g=v_cf_ln_g, v_cf_ln_b=v_cf_ln_b, v_w_cf_o=v_w_cf_o, v_w_o=v_w_o, v_ffn2_w_gu=v_ffn2_w_gu, v_ffn2_w_down=v_ffn2_w_down)
    weights = {n: given[n] for n in TWIN_WEIGHTS}
    shared = {n: given[n] for n in SHARED_INPUTS}
    per_example = {n: given[n] for n in ['x', 'positions']}
    grad_fn = _jax.value_and_grad(_loss, argnums=(0, 1))

    def one_microbatch(ex, loss_target):
        ex = dict(ex)
        diff = ex.pop(TWIN_DIFF_INPUT)
        return grad_fn(weights, diff, {**shared, **ex}, loss_target)

    if N_MICROBATCH == 1:
        loss, (grad_w, grad_x) = one_microbatch(per_example, given["loss_target"])
    else:
        def body(carry, xs):
            loss_sum, grad_sum = carry
            l_k, (gw_k, gx_k) = one_microbatch(xs[0], xs[1])
            with _jax.named_scope("update"):
                return (loss_sum + l_k, _jax.tree.map(_jnp.add, grad_sum, gw_k)), gx_k

        init = (_jnp.zeros((), _jnp.float32), _jax.tree.map(_jnp.zeros_like, weights))
        (loss, grad_w), grad_x = _jax.lax.scan(body, init, (per_example, given["loss_target"]))
    with _jax.named_scope("update"):
        delta_w, new_m, new_v = {}, {}, {}
        for n in TWIN_WEIGHTS:
            delta_w[n], new_m[n], new_v[n] = _adamw(weights[n], grad_w[n], given["m_" + n], given["v_" + n])
    return (loss, grad_x, *[grad_w[n] for n in TWIN_WEIGHTS], *[delta_w[n] for n in TWIN_WEIGHTS],
            *[new_m[n] for n in TWIN_WEIGHTS], *[new_v[n] for n in TWIN_WEIGHTS])
```

```python
import functools

import jax
import jax.numpy as jnp
from jax import lax
from jax.experimental import pallas as pl
from jax.experimental.pallas import tpu as pltpu

F32 = jnp.float32
BF16 = jnp.bfloat16
MESH = pl.DeviceIdType.MESH

N_DEV = 8
N_CHIP = 4
CHUNK = 64
HEADS = 4
ROPE_BASE = 10000.0
NORM_EPS = 1e-6
LN_EPS = 1e-5
SC_KERNEL = 3
CF_KERNEL = 31
HALO = 32
ADAM_LR = 0.001
ADAM_B1 = 0.9
ADAM_B2 = 0.999
ADAM_EPS = 1e-08
ADAM_WD = 0.01
ADAM_STEP = 10
VMEM_LIMIT_V7X = 56 * 1024 * 1024


def _pallas(body, *, name, grid, in_specs, out_specs, out_shape, scratch_shapes=(), aliases=None, nprefetch=0):
    extra = {}
    params = pltpu.CompilerParams(dimension_semantics=("arbitrary",) * len(grid), vmem_limit_bytes=VMEM_LIMIT_V7X)
    spec = pltpu.PrefetchScalarGridSpec(num_scalar_prefetch=nprefetch, grid=grid, in_specs=in_specs,
                                        out_specs=out_specs, scratch_shapes=list(scratch_shapes))
    return pl.pallas_call(body, name=name, grid_spec=spec, out_shape=out_shape, compiler_params=params,
                          input_output_aliases=aliases or {}, **extra)


def _comm_call(body, *, name, in_specs, out_specs, out_shape, scratch_shapes):
    extra = {}
    return pl.pallas_call(body, name=name, in_specs=in_specs, out_specs=out_specs, out_shape=out_shape,
                          scratch_shapes=scratch_shapes, **extra)


def _sds(shape, dtype):
    return jax.ShapeDtypeStruct(shape, dtype)


def _tile(n, pref):
    t = min(n, pref)
    assert n % t == 0, (n, pref)
    return t


def _sigmoid(x):
    return jax.nn.sigmoid(x)


def _rms_fwd(x, g):
    r = lax.rsqrt(jnp.mean(x * x, axis=-1, keepdims=True) + NORM_EPS)
    return x * r * g


def _rms_bwd(x, g, dy):
    r = lax.rsqrt(jnp.mean(x * x, axis=-1, keepdims=True) + NORM_EPS)
    xh = x * r
    dg = jnp.sum(dy * xh, axis=0, keepdims=True)
    dxh = dy * g
    dx = r * (dxh - xh * jnp.mean(dxh * xh, axis=-1, keepdims=True))
    return dx, dg


def _ln_stats(x):
    mu = jnp.mean(x, axis=-1, keepdims=True)
    xc = x - mu
    rstd = lax.rsqrt(jnp.mean(xc * xc, axis=-1, keepdims=True) + LN_EPS)
    return xc * rstd, rstd


def _ln_bwd(xh, rstd, dxh):
    return rstd * (dxh - jnp.mean(dxh, axis=-1, keepdims=True) - xh * jnp.mean(dxh * xh, axis=-1, keepdims=True))


def _dsilu(x, s):
    return s * (1.0 + x * (1.0 - s))


def _rot(x, cosf, sinf):
    return x * cosf + pltpu.roll(x, x.shape[-1] // 2, 1) * sinf


def _unrot(d, cosf, sinf):
    return d * cosf - pltpu.roll(d, d.shape[-1] // 2, 1) * sinf


def _dot_nn(a, b):
    return jnp.dot(a, b, preferred_element_type=F32)


def _dot_nt(a, b):
    return lax.dot_general(a, b, (((1,), (1,)), ((), ())), preferred_element_type=F32)


def _dot_tn(a, b):
    return lax.dot_general(a, b, (((0,), (0,)), ((), ())), preferred_element_type=F32)


def _ffn_fwd(x, g_pre, g_post, wgu, wd, l, name):
    T, D = x.shape
    F = wd.shape[1]
    cb = wgu.shape[3]
    tm, tf = _tile(T, 512), _tile(cb, 512)
    nj, nb = F // tf, cb // tf

    def body(x_ref, gpre_ref, gpost_ref, wg_ref, wu_ref, wd_ref, xo_ref, h_ref, gate_ref, up_ref, f_ref, acc_ref):
        j = pl.program_id(1)

        @pl.when(j == 0)
        def _():
            h_ref[...] = _rms_fwd(x_ref[...], gpre_ref[...]).astype(BF16)
            acc_ref[...] = jnp.zeros_like(acc_ref)

        h = h_ref[...]
        gate = _dot_nn(h, wg_ref[...])
        up = _dot_nn(h, wu_ref[...])
        gate_ref[...] = gate.astype(BF16)
        up_ref[...] = up.astype(BF16)
        act = (gate * _sigmoid(gate) * up).astype(BF16)
        acc_ref[...] += _dot_nn(act, wd_ref[...])

        @pl.when(j == nj - 1)
        def _():
            f = acc_ref[...]
            f_ref[...] = f
            xo_ref[...] = x_ref[...] + 0.5 * _rms_fwd(f, gpost_ref[...])

    row = pl.BlockSpec((tm, D), lambda i, j: (i, 0))
    vec = pl.BlockSpec((1, D), lambda i, j: (0, 0))
    col = pl.BlockSpec((tm, tf), lambda i, j: (i, j))
    return _pallas(
        body, name=name, grid=(T // tm, nj),
        in_specs=[row, vec, vec,
                  pl.BlockSpec((None, None, D, tf), lambda i, j: (l, j // nb, 0, j % nb)),
                  pl.BlockSpec((None, None, D, tf), lambda i, j: (l, (nj + j) // nb, 0, j % nb)),
                  pl.BlockSpec((None, tf, D), lambda i, j: (l, j, 0))],
        out_specs=[row, row, col, col, row],
        out_shape=[_sds((T, D), F32), _sds((T, D), BF16), _sds((T, F), BF16), _sds((T, F), BF16), _sds((T, D), F32)],
        scratch_shapes=[pltpu.VMEM((tm, D), F32)],
    )(x, g_pre, g_post, wgu, wgu, wd)


def _ffn_bwd(dxo, x, f, gate, up, g_pre, g_post, wgu, wd, l, name):
    T, D = x.shape
    F = wd.shape[1]
    cb = wgu.shape[3]
    tm, tf = _tile(T, 512), _tile(cb, 512)
    nj, nb = F // tf, cb // tf

    def body(dxo_ref, x_ref, f_ref, gate_ref, up_ref, gpre_ref, gpost_ref, wg_ref, wu_ref, wd_ref,
             dx_ref, df_ref, dgu_ref, act_ref, dgpre_ref, dgpost_ref, acc_ref):
        i, j = pl.program_id(0), pl.program_id(1)

        @pl.when((i == 0) & (j == 0))
        def _():
            dgpre_ref[...] = jnp.zeros_like(dgpre_ref)
            dgpost_ref[...] = jnp.zeros_like(dgpost_ref)

        @pl.when(j == 0)
        def _():
            df, dgp = _rms_bwd(f_ref[...], gpost_ref[...], 0.5 * dxo_ref[...])
            df_ref[...] = df.astype(BF16)
            dgpost_ref[...] += dgp
            acc_ref[...] = jnp.zeros_like(acc_ref)

        dact = _dot_nt(df_ref[...], wd_ref[...])
        g = gate_ref[...].astype(F32)
        u = up_ref[...].astype(F32)
        s = _sigmoid(g)
        silu = g * s
        dgate = (dact * u * _dsilu(g, s)).astype(BF16)
        dup = (dact * silu).astype(BF16)
        act_ref[...] = (silu * u).astype(BF16)
        dgu_ref[:, :tf] = dgate
        dgu_ref[:, tf:] = dup
        acc_ref[...] += _dot_nt(dgate, wg_ref[...]) + _dot_nt(dup, wu_ref[...])

        @pl.when(j == nj - 1)
        def _():
            dxin, dgp = _rms_bwd(x_ref[...], gpre_ref[...], acc_ref[...])
            dx_ref[...] = dxo_ref[...] + dxin
            dgpre_ref[...] += dgp

    row = pl.BlockSpec((tm, D), lambda i, j: (i, 0))
    vec = pl.BlockSpec((1, D), lambda i, j: (0, 0))
    col = pl.BlockSpec((tm, tf), lambda i, j: (i, j))
    return _pallas(
        body, name=name, grid=(T // tm, nj),
        in_specs=[row, row, row, col, col, vec, vec,
                  pl.BlockSpec((None, None, D, tf), lambda i, j: (l, j // nb, 0, j % nb)),
                  pl.BlockSpec((None, None, D, tf), lambda i, j: (l, (nj + j) // nb, 0, j % nb)),
                  pl.BlockSpec((None, tf, D), lambda i, j: (l, j, 0))],
        out_specs=[row, row, pl.BlockSpec((tm, 2 * tf), lambda i, j: (i, j)), col, vec, vec],
        out_shape=[_sds((T, D), F32), _sds((T, D), BF16), _sds((T, 2 * F), BF16), _sds((T, F), BF16),
                   _sds((1, D), F32), _sds((1, D), F32)],
        scratch_shapes=[pltpu.VMEM((tm, D), F32)],
    )(dxo, x, f, gate, up, g_pre, g_post, wgu, wgu, wd), tf


def _mm_tn(a, b, l, n_layers, into, name, slab_cols=None, tile_order=None, tn_pref=1024):
    T, K = a.shape
    N = b.shape[1]
    tt = _tile(T, 512)
    tko = _tile(K, 1024)
    tn = tile_order[1] if tile_order else _tile(slab_cols or N, tn_pref)
    nat = tile_order[0] if tile_order else (lambda jn: jn)
    nt = T // tt

    def body(a_ref, b_ref, *rest):
        o_ref, acc_ref = rest[-2], rest[-1]
        t = pl.program_id(2)

        @pl.when(t == 0)
        def _():
            acc_ref[...] = jnp.zeros_like(acc_ref)

        acc_ref[...] += _dot_tn(a_ref[...], b_ref[...])

        @pl.when(t == nt - 1)
        def _():
            o_ref[...] = acc_ref[...].astype(o_ref.dtype)

    if slab_cols is None:
        shape = (n_layers, K, N)
        ospec = pl.BlockSpec((None, tko, tn), lambda k, jn, t: (l, k, nat(jn)))
    else:
        nb = slab_cols // tn
        shape = (n_layers, N // slab_cols, K, slab_cols)
        ospec = pl.BlockSpec((None, None, tko, tn), lambda k, jn, t: (l, nat(jn) // nb, k, nat(jn) % nb))
    in_specs = [pl.BlockSpec((tt, tko), lambda k, jn, t: (t, k)), pl.BlockSpec((tt, tn), lambda k, jn, t: (t, jn))]
    args = [a, b]
    aliases = None
    if into is not None:
        in_specs.append(pl.BlockSpec(memory_space=pl.ANY))
        args.append(into)
        aliases = {2: 0}
    return _pallas(body, name=name, grid=(K // tko, N // tn, nt), in_specs=in_specs, out_specs=ospec,
                   out_shape=_sds(shape, BF16), scratch_shapes=[pltpu.VMEM((tko, tn), F32)], aliases=aliases)(*args)


def _norm_proj(x, g, w, l, name):
    T, D = x.shape
    nd, cb = w.shape[1], w.shape[3]
    tm = _tile(T, 512)

    def body(x_ref, g_ref, w_ref, p_ref, h_ref):
        @pl.when(pl.program_id(1) == 0)
        def _():
            h_ref[...] = _rms_fwd(x_ref[...], g_ref[...]).astype(BF16)

        p_ref[...] = _dot_nn(h_ref[...], w_ref[...])

    row = pl.BlockSpec((tm, D), lambda i, j: (i, 0))
    return _pallas(
        body, name=name, grid=(T // tm, nd),
        in_specs=[row, pl.BlockSpec((1, D), lambda i, j: (0, 0)),
                  pl.BlockSpec((None, None, D, cb), lambda i, j: (l, j, 0, 0))],
        out_specs=[pl.BlockSpec((tm, cb), lambda i, j: (i, j)), row],
        out_shape=[_sds((T, nd * cb), F32), _sds((T, D), BF16)],
    )(x, g, w)


def _proj_bwd(dproj, w, x, g, dxo, l, name):
    T, D = x.shape
    nd, cb = w.shape[1], w.shape[3]
    tm = _tile(T, 512)

    def body(dp_ref, w_ref, x_ref, g_ref, dxo_ref, dx_ref, dg_ref, acc_ref):
        i, j = pl.program_id(0), pl.program_id(1)

        @pl.when((i == 0) & (j == 0))
        def _():
            dg_ref[...] = jnp.zeros_like(dg_ref)

        @pl.when(j == 0)
        def _():
            acc_ref[...] = jnp.zeros_like(acc_ref)

        acc_ref[...] += _dot_nt(dp_ref[...], w_ref[...])

        @pl.when(j == nd - 1)
        def _():
            dxin, dgp = _rms_bwd(x_ref[...], g_ref[...], acc_ref[...])
            dx_ref[...] = dxo_ref[...] + dxin
            dg_ref[...] += dgp

    row = pl.BlockSpec((tm, D), lambda i, j: (i, 0))
    vec = pl.BlockSpec((1, D), lambda i, j: (0, 0))
    return _pallas(
        body, name=name, grid=(T // tm, nd),
        in_specs=[pl.BlockSpec((tm, cb), lambda i, j: (i, j)),
                  pl.BlockSpec((None, None, D, cb), lambda i, j: (l, j, 0, 0)), row, vec, row],
        out_specs=[row, vec],
        out_shape=[_sds((T, D), F32), _sds((1, D), F32)],
        scratch_shapes=[pltpu.VMEM((tm, D), F32)],
    )(dproj, w, x, g, dxo)


def _rope_tables(pos, invf, sgn, name):
    T = pos.shape[0]
    dk = invf.shape[1]
    tm = _tile(T, 1024)

    def body(p_ref, f_ref, s_ref, c_out, s_out):
        ang = p_ref[...].astype(F32) * f_ref[...]
        c_out[...] = jnp.cos(ang)
        s_out[...] = jnp.sin(ang) * s_ref[...]

    vec = pl.BlockSpec((1, dk), lambda i: (0, 0))
    out = pl.BlockSpec((tm, dk), lambda i: (i, 0))
    return _pallas(body, name=name, grid=(T // tm,), in_specs=[pl.BlockSpec((tm, 1), lambda i: (i, 0)), vec, vec],
                   out_specs=[out, out], out_shape=[_sds((T, dk), F32), _sds((T, dk), F32)])(pos, invf, sgn)


def _decay(lg, r0, tq, S):
    r = r0 + lax.broadcasted_iota(jnp.int32, (tq, S), 0)
    c = lax.broadcasted_iota(jnp.int32, (tq, S), 1)
    rc, cc = r // CHUNK, c // CHUNK
    d = (r - c).astype(F32)
    e = jnp.where(rc == cc, jnp.abs(d), d)
    return jnp.where(cc > rc, 0.0, jnp.exp(lg * e))


def _ret_fwd(proj, cosf, sinf, lgs, B, S, D, name):
    T = B * S
    dk, dv = D // 8, D // 4
    tq = _tile(S, 256)
    nq = S // tq
    scale = dk ** -0.5

    def body(lg_ref, q_ref, k_ref, v_ref, cq_ref, sq_ref, ck_ref, sk_ref, o_ref, kr_ref, vb_ref):
        h, qi = pl.program_id(1), pl.program_id(2)

        @pl.when(qi == 0)
        def _():
            kr_ref[...] = (_rot(k_ref[...], ck_ref[...], sk_ref[...]) * scale).astype(BF16)
            vb_ref[...] = v_ref[...].astype(BF16)

        q = _rot(q_ref[...], cq_ref[...], sq_ref[...]).astype(BF16)
        s = _dot_nt(q, kr_ref[...])
        p = (s * _decay(lg_ref[h], qi * tq, tq, S)).astype(BF16)
        o_ref[...] = _dot_nn(p, vb_ref[...])

    qrow = lambda b, h, qi: b * nq + qi
    return _pallas(
        body, name=name, grid=(B, HEADS, nq),
        in_specs=[pl.BlockSpec(memory_space=pltpu.SMEM),
                  pl.BlockSpec((tq, dk), lambda b, h, qi: (qrow(b, h, qi), h)),
                  pl.BlockSpec((S, dk), lambda b, h, qi: (b, HEADS + h)),
                  pl.BlockSpec((S, dv), lambda b, h, qi: (b, HEADS + h)),
                  pl.BlockSpec((tq, dk), lambda b, h, qi: (qrow(b, h, qi), 0)),
                  pl.BlockSpec((tq, dk), lambda b, h, qi: (qrow(b, h, qi), 0)),
                  pl.BlockSpec((S, dk), lambda b, h, qi: (b, 0)),
                  pl.BlockSpec((S, dk), lambda b, h, qi: (b, 0))],
        out_specs=pl.BlockSpec((tq, dv), lambda b, h, qi: (qrow(b, h, qi), h)),
        out_shape=_sds((T, D), F32),
        scratch_shapes=[pltpu.VMEM((S, dk), BF16), pltpu.VMEM((S, dv), BF16)],
    )(lgs, proj, proj, proj, cosf, sinf, cosf, sinf)


def _ret_bwd(do, proj, cosf, sinf, lgs, B, S, D, name):
    T = B * S
    dk, dv = D // 8, D // 4
    tq = _tile(S, 256)
    nq = S // tq
    scale = dk ** -0.5

    def body(lg_ref, do_ref, q_ref, k_ref, v_ref, c_ref, s_ref, dq_ref, dk_ref, dv_ref, kr_ref, vb_ref, dka_ref, dva_ref):
        h = pl.program_id(1)
        lg = lg_ref[h]
        kr_ref[...] = (_rot(k_ref[...], c_ref[...], s_ref[...]) * scale).astype(BF16)
        vb_ref[...] = v_ref[...].astype(BF16)
        dka_ref[...] = jnp.zeros_like(dka_ref)
        dva_ref[...] = jnp.zeros_like(dva_ref)

        def step(qi, carry):
            rows = pl.ds(pl.multiple_of(qi * tq, tq), tq)
            cq, sq = c_ref[rows, :], s_ref[rows, :]
            q = _rot(q_ref[rows, :], cq, sq).astype(BF16)
            dout = do_ref[rows, :].astype(BF16)
            w = _decay(lg, qi * tq, tq, S)
            p = (_dot_nt(q, kr_ref[...]) * w).astype(BF16)
            ds = (_dot_nt(dout, vb_ref[...]) * w).astype(BF16)
            dq_ref[rows, :] = _unrot(_dot_nn(ds, kr_ref[...]), cq, sq).astype(BF16)
            dka_ref[...] += _dot_tn(ds, q)
            dva_ref[...] += _dot_tn(p, dout)
            return carry

        lax.fori_loop(0, nq, step, 0)
        dk_ref[...] = _unrot(dka_ref[...] * scale, c_ref[...], s_ref[...]).astype(BF16)
        dv_ref[...] = dva_ref[...].astype(BF16)

    return _pallas(
        body, name=name, grid=(B, HEADS),
        in_specs=[pl.BlockSpec(memory_space=pltpu.SMEM),
                  pl.BlockSpec((S, dv), lambda b, h: (b, h)),
                  pl.BlockSpec((S, dk), lambda b, h: (b, h)),
                  pl.BlockSpec((S, dk), lambda b, h: (b, HEADS + h)),
                  pl.BlockSpec((S, dv), lambda b, h: (b, HEADS + h)),
                  pl.BlockSpec((S, dk), lambda b, h: (b, 0)),
                  pl.BlockSpec((S, dk), lambda b, h: (b, 0))],
        out_specs=[pl.BlockSpec((S, dk), lambda b, h: (b, h)), pl.BlockSpec((S, dk), lambda b, h: (b, h)),
                   pl.BlockSpec((S, dv), lambda b, h: (b, h))],
        out_shape=[_sds((T, D // 2), BF16), _sds((T, D // 2), BF16), _sds((T, D), BF16)],
        scratch_shapes=[pltpu.VMEM((S, dk), BF16), pltpu.VMEM((S, dv), BF16), pltpu.VMEM((S, dk), F32), pltpu.VMEM((S, dv), F32)],
    )(lgs, do, proj, proj, proj, cosf, sinf)


COL_GRET, COL_SCB, COL_SCC, COL_SCX, COL_GLUA, COL_GLUB, COL_GL = 2, 3, 4, 5, 6, 7, 8


def _causal_conv(ext_ref, w_ref, taps, tm):
    acc = None
    for j in range(taps):
        term = ext_ref[pl.ds(HALO - (taps - 1) + j, tm), :] * w_ref[j:j + 1, :]
        acc = term if acc is None else acc + term
    return acc


def _mixer_mid_fwd(proj, o, sc_w, cf_w, cf_b, ln_g, ln_b, S, name):
    T, D = o.shape
    dv = D // HEADS
    tm = _tile(S, 256)
    per_seq = S // tm
    hb = tm // HALO

    def body(gret_ref, scb_ref, scc_ref, scx_ref, ga_ref, gb_ref, scc_h, scx_h, ga_h, gb_h, o_ref,
             scw_ref, cfw_ref, cfb_ref, lng_ref, lnb_ref, ya_ref, yb_ref, yc_ref, cv_ref, u1_ref, ext_ref):
        first = (pl.program_id(0) % per_seq) == 0
        keep = jnp.where(first, 0.0, 1.0)
        gr = gret_ref[...]
        sg = gr * _sigmoid(gr)
        for hh in range(HEADS):
            cols = slice(hh * dv, (hh + 1) * dv)
            hn, _ = _ln_stats(o_ref[:, cols])
            ya_ref[:, cols] = (sg[:, cols] * hn).astype(BF16)
        ext_ref[:HALO, :] = scc_h[...] * scx_h[...] * keep
        ext_ref[HALO:, :] = scc_ref[...] * scx_ref[...]
        cv = _causal_conv(ext_ref, scw_ref, SC_KERNEL, tm)
        cv_ref[...] = cv
        yb_ref[...] = (scb_ref[...] * cv).astype(BF16)
        ext_ref[:HALO, :] = ga_h[...] * _sigmoid(gb_h[...]) * keep
        ext_ref[HALO:, :] = ga_ref[...] * _sigmoid(gb_ref[...])
        u1 = _causal_conv(ext_ref, cfw_ref, CF_KERNEL, tm) + cfb_ref[...]
        u1_ref[...] = u1
        xh, _ = _ln_stats(u1)
        u2 = xh * lng_ref[...] + lnb_ref[...]
        yc_ref[...] = (u2 * _sigmoid(u2)).astype(BF16)

    def colblk(c):
        return pl.BlockSpec((tm, D), lambda i: (i, c))

    def halo(c):
        return pl.BlockSpec((HALO, D), lambda i: (jnp.maximum(i * hb - 1, 0), c))

    row = pl.BlockSpec((tm, D), lambda i: (i, 0))
    vec = pl.BlockSpec((1, D), lambda i: (0, 0))
    return _pallas(
        body, name=name, grid=(T // tm,),
        in_specs=[colblk(COL_GRET), colblk(COL_SCB), colblk(COL_SCC), colblk(COL_SCX), colblk(COL_GLUA), colblk(COL_GLUB),
                  halo(COL_SCC), halo(COL_SCX), halo(COL_GLUA), halo(COL_GLUB), row,
                  pl.BlockSpec((SC_KERNEL, D), lambda i: (0, 0)), pl.BlockSpec((CF_KERNEL, D), lambda i: (0, 0)), vec, vec, vec],
        out_specs=[row, row, row, row, row],
        out_shape=[_sds((T, D), BF16)] * 3 + [_sds((T, D), F32)] * 2,
        scratch_shapes=[pltpu.VMEM((HALO + tm, D), F32)],
    )(proj, proj, proj, proj, proj, proj, proj, proj, proj, proj, o, sc_w, cf_w, cf_b, ln_g, ln_b)


def _mixer_merge_fwd(x, proj, ya_in, yb_in, yc_in, g_post, w_ret, w_sc, w_cf, w_o, l, name):
    T, D = x.shape
    tm = _tile(T, 256)

    def body(x_ref, g0_ref, g1_ref, g2_ref, ya_in_ref, yb_in_ref, yc_in_ref, gp_ref, wr_ref, ws_ref, wc_ref, wo_ref,
             xo_ref, ya_ref, yb_ref, yc_ref, mg_ref, m_ref):
        ya = _dot_nn(ya_in_ref[...], wr_ref[...])
        yb = _dot_nn(yb_in_ref[...], ws_ref[...])
        yc = _dot_nn(yc_in_ref[...], wc_ref[...])
        ya_ref[...] = ya.astype(BF16)
        yb_ref[...] = yb.astype(BF16)
        yc_ref[...] = yc.astype(BF16)
        merged = (_sigmoid(g0_ref[...]) * ya + _sigmoid(g1_ref[...]) * yb + _sigmoid(g2_ref[...]) * yc).astype(BF16)
        mg_ref[...] = merged
        m = _dot_nn(merged, wo_ref[...])
        m_ref[...] = m
        xo_ref[...] = x_ref[...] + _rms_fwd(m, gp_ref[...])

    row = pl.BlockSpec((tm, D), lambda i: (i, 0))
    wsp = pl.BlockSpec((None, D, D), lambda i: (l, 0, 0))

    def colblk(c):
        return pl.BlockSpec((tm, D), lambda i: (i, c))

    return _pallas(
        body, name=name, grid=(T // tm,),
        in_specs=[row, colblk(COL_GL), colblk(COL_GL + 1), colblk(COL_GL + 2), row, row, row,
                  pl.BlockSpec((1, D), lambda i: (0, 0)), wsp, wsp, wsp, wsp],
        out_specs=[row] * 6,
        out_shape=[_sds((T, D), F32)] + [_sds((T, D), BF16)] * 4 + [_sds((T, D), F32)],
    )(x, proj, proj, proj, ya_in, yb_in, yc_in, g_post, w_ret, w_sc, w_cf, w_o)


def _mixer_bwd_a(dxo, m, g_post, ya, yb, yc, proj, o, cv, u1, ln_g, ln_b, w_ret, w_sc, w_cf, w_o, l, name):
    T, D = m.shape
    dv = D // HEADS
    tm = _tile(T, 128)

    def body(dxo_ref, m_ref, gp_ref, ya_ref, yb_ref, yc_ref, g0_ref, g1_ref, g2_ref, gret_ref, scb_ref, o_ref, cv_ref, u1_ref,
             lng_ref, lnb_ref, wr_ref, ws_ref, wc_ref, wo_ref,
             dm_ref, dya_ref, dyb_ref, dyc_ref, d2_ref, dgl_ref, do_ref, dcv_ref, du1_ref, dgp_ref, dlng_ref, dlnb_ref, dcfb_ref):
        @pl.when(pl.program_id(0) == 0)
        def _():
            for r in (dgp_ref, dlng_ref, dlnb_ref, dcfb_ref):
                r[...] = jnp.zeros_like(r)

        dm, dgp = _rms_bwd(m_ref[...], gp_ref[...], dxo_ref[...])
        dgp_ref[...] += dgp
        dmb = dm.astype(BF16)
        dm_ref[...] = dmb
        dmerged = _dot_nt(dmb, wo_ref[...])
        dys = []
        for k, (g_ref, y_ref, dy_ref) in enumerate(((g0_ref, ya_ref, dya_ref), (g1_ref, yb_ref, dyb_ref), (g2_ref, yc_ref, dyc_ref))):
            sg = _sigmoid(g_ref[...])
            dgl_ref[:, k * D:(k + 1) * D] = (dmerged * y_ref[...].astype(F32) * sg * (1.0 - sg)).astype(BF16)
            dy = (dmerged * sg).astype(BF16)
            dy_ref[...] = dy
            dys.append(dy)
        dya_in = _dot_nt(dys[0], wr_ref[...])
        gr = gret_ref[...]
        sr = _sigmoid(gr)
        for hh in range(HEADS):
            cols = slice(hh * dv, (hh + 1) * dv)
            hn, rstd = _ln_stats(o_ref[:, cols])
            d2_ref[:, cols] = (dya_in[:, cols] * hn * _dsilu(gr[:, cols], sr[:, cols])).astype(BF16)
            do_ref[:, cols] = _ln_bwd(hn, rstd, dya_in[:, cols] * gr[:, cols] * sr[:, cols])
        dyb_in = _dot_nt(dys[1], ws_ref[...])
        d2_ref[:, D:] = (dyb_in * cv_ref[...]).astype(BF16)
        dcv_ref[...] = dyb_in * scb_ref[...]
        dyc_in = _dot_nt(dys[2], wc_ref[...])
        xh, rstd = _ln_stats(u1_ref[...])
        u2 = xh * lng_ref[...] + lnb_ref[...]
        du2 = dyc_in * _dsilu(u2, _sigmoid(u2))
        dlng_ref[...] += jnp.sum(du2 * xh, axis=0, keepdims=True)
        dlnb_ref[...] += jnp.sum(du2, axis=0, keepdims=True)
        du1 = _ln_bwd(xh, rstd, du2 * lng_ref[...])
        du1_ref[...] = du1
        dcfb_ref[...] += jnp.sum(du1, axis=0, keepdims=True)

    row = pl.BlockSpec((tm, D), lambda i: (i, 0))
    vec = pl.BlockSpec((1, D), lambda i: (0, 0))
    wsp = pl.BlockSpec((None, D, D), lambda i: (l, 0, 0))

    def colblk(c):
        return pl.BlockSpec((tm, D), lambda i: (i, c))

    return _pallas(
        body, name=name, grid=(T // tm,),
        in_specs=[row, row, vec, row, row, row, colblk(COL_GL), colblk(COL_GL + 1), colblk(COL_GL + 2),
                  colblk(COL_GRET), colblk(COL_SCB), row, row, row, vec, vec, wsp, wsp, wsp, wsp],
        out_specs=[row, row, row, row, pl.BlockSpec((tm, 2 * D), lambda i: (i, 0)), pl.BlockSpec((tm, 3 * D), lambda i: (i, 0)),
                   row, row, row, vec, vec, vec, vec],
        out_shape=[_sds((T, D), BF16)] * 4 + [_sds((T, 2 * D), BF16), _sds((T, 3 * D), BF16)] + [_sds((T, D), F32)] * 3
                  + [_sds((1, D), F32)] * 4,
    )(dxo, m, g_post, ya, yb, yc, proj, proj, proj, proj, proj, o, cv, u1, ln_g, ln_b, w_ret, w_sc, w_cf, w_o)


def _mixer_bwd_b(dcv, du1, proj, sc_w, cf_w, S, name):
    T, D = dcv.shape
    tm = _tile(S, 256)
    per_seq = S // tm
    hb = tm // HALO
    last_hb = T // HALO - 1

    def anti_conv(dext_ref, w_ref, taps):
        acc = None
        for j in range(taps):
            term = dext_ref[pl.ds(taps - 1 - j, tm), :] * w_ref[j:j + 1, :]
            acc = term if acc is None else acc + term
        return acc

    def body(dcv_ref, du1_ref, dcv_n, du1_n, scc_ref, scx_ref, ga_ref, gb_ref, scc_h, scx_h, ga_h, gb_h, scw_ref, cfw_ref,
             d4_ref, dscw_ref, dcfw_ref, ext_ref, dext_ref):
        i = pl.program_id(0)
        keep_prev = jnp.where((i % per_seq) == 0, 0.0, 1.0)
        keep_next = jnp.where((i % per_seq) == per_seq - 1, 0.0, 1.0)

        @pl.when(i == 0)
        def _():
            dscw_ref[...] = jnp.zeros_like(dscw_ref)
            dcfw_ref[...] = jnp.zeros_like(dcfw_ref)

        dcv = dcv_ref[...]
        dext_ref[:tm, :] = dcv
        dext_ref[tm:, :] = dcv_n[...] * keep_next
        dz = anti_conv(dext_ref, scw_ref, SC_KERNEL)
        scc, scx = scc_ref[...], scx_ref[...]
        d4_ref[:, :D] = (dz * scx).astype(BF16)
        d4_ref[:, D:2 * D] = (dz * scc).astype(BF16)
        ext_ref[:HALO, :] = scc_h[...] * scx_h[...] * keep_prev
        ext_ref[HALO:, :] = scc * scx
        for j in range(SC_KERNEL):
            dscw_ref[j:j + 1, :] += jnp.sum(dcv * ext_ref[pl.ds(HALO - (SC_KERNEL - 1) + j, tm), :], axis=0, keepdims=True)
        du1 = du1_ref[...]
        dext_ref[:tm, :] = du1
        dext_ref[tm:, :] = du1_n[...] * keep_next
        du0 = anti_conv(dext_ref, cfw_ref, CF_KERNEL)
        ga = ga_ref[...]
        sb = _sigmoid(gb_ref[...])
        d4_ref[:, 2 * D:3 * D] = (du0 * sb).astype(BF16)
        d4_ref[:, 3 * D:] = (du0 * ga * sb * (1.0 - sb)).astype(BF16)
        ext_ref[:HALO, :] = ga_h[...] * _sigmoid(gb_h[...]) * keep_prev
        ext_ref[HALO:, :] = ga * sb
        for j in range(CF_KERNEL):
            dcfw_ref[j:j + 1, :] += jnp.sum(du1 * ext_ref[pl.ds(HALO - (CF_KERNEL - 1) + j, tm), :], axis=0, keepdims=True)

    row = pl.BlockSpec((tm, D), lambda i: (i, 0))
    nxt = pl.BlockSpec((HALO, D), lambda i: (jnp.minimum((i + 1) * hb, last_hb), 0))

    def colblk(c):
        return pl.BlockSpec((tm, D), lambda i: (i, c))

    def halo(c):
        return pl.BlockSpec((HALO, D), lambda i: (jnp.maximum(i * hb - 1, 0), c))

    return _pallas(
        body, name=name, grid=(T // tm,),
        in_specs=[row, row, nxt, nxt, colblk(COL_SCC), colblk(COL_SCX), colblk(COL_GLUA), colblk(COL_GLUB),
                  halo(COL_SCC), halo(COL_SCX), halo(COL_GLUA), halo(COL_GLUB),
                  pl.BlockSpec((SC_KERNEL, D), lambda i: (0, 0)), pl.BlockSpec((CF_KERNEL, D), lambda i: (0, 0))],
        out_specs=[pl.BlockSpec((tm, 4 * D), lambda i: (i, 0)), pl.BlockSpec((SC_KERNEL, D), lambda i: (0, 0)),
                   pl.BlockSpec((CF_KERNEL, D), lambda i: (0, 0))],
        out_shape=[_sds((T, 4 * D), BF16), _sds((SC_KERNEL, D), F32), _sds((CF_KERNEL, D), F32)],
        scratch_shapes=[pltpu.VMEM((HALO + tm, D), F32), pltpu.VMEM((tm + HALO, D), F32)],
    )(dcv, du1, dcv, du1, proj, proj, proj, proj, proj, proj, proj, proj, sc_w, cf_w)


def _loss_grad(y, tgt, name):
    T, D = y.shape
    tm = _tile(T, 512)

    def body(y_ref, t_ref, dy_ref, loss_ref):
        @pl.when(pl.program_id(0) == 0)
        def _():
            loss_ref[...] = jnp.zeros_like(loss_ref)

        e = y_ref[...] - t_ref[...]
        dy_ref[...] = e * (1.0 / D)
        loss_ref[...] += 0.5 * jnp.sum(jnp.sum(e * e, axis=-1, keepdims=True) * (1.0 / D), axis=0, keepdims=True)

    row = pl.BlockSpec((tm, D), lambda i: (i, 0))
    return _pallas(body, name=name, grid=(T // tm,), in_specs=[row, row],
                   out_specs=[row, pl.BlockSpec((1, 1), lambda i: (0, 0))],
                   out_shape=[_sds((T, D), F32), _sds((1, 1), F32)])(y, tgt)


def _mesh_pos():
    return lax.axis_index("x"), lax.axis_index("y"), lax.axis_index("c")


def _other_chips(x, y):
    return [(1 - x, y), (x, 1 - y), (1 - x, 1 - y)]


ANY = pl.BlockSpec(memory_space=pl.ANY)


def _allgather_slabs(shards, name):
    nt = len(shards)

    def body(*refs):
        ins, outs = refs[:nt], refs[nt:2 * nt]
        send_sems, recv_sems, local_sems = refs[2 * nt:]
        x, y, c = _mesh_pos()
        me, sibling = (x, y, c), (x, y, 1 - c)
        chips = _other_chips(x, y)

        def slab(t, px, py, pc):
            return outs[t].at[:, 4 * px + 2 * py + pc]

        def copy(t, k, block, to, src=None):
            return pltpu.make_async_remote_copy(
                src_ref=slab(t, *block) if src is None else src, dst_ref=slab(t, *block),
                send_sem=send_sems.at[t, k], recv_sem=recv_sems.at[t, k], device_id=to, device_id_type=MESH)

        mine = [pltpu.make_async_copy(ins[t], slab(t, *me), local_sems.at[t]) for t in range(nt)]
        for cp in mine:
            cp.start()
        first = []
        for t in range(nt):
            first.append(copy(t, 0, me, sibling, src=ins[t]))
            first += [copy(t, 1 + j, me, (*chip, c), src=ins[t]) for j, chip in enumerate(chips)]
        for cp in first:
            cp.start()
        passed = []
        for j, chip in enumerate(chips):
            for t in range(nt):
                copy(t, 1 + j, (*chip, c), me).wait_recv()
                fwd = copy(t, 4 + j, (*chip, c), sibling)
                fwd.start()
                passed.append(fwd)
        for t in range(nt):
            copy(t, 0, sibling, me).wait_recv()
            for j, chip in enumerate(chips):
                copy(t, 4 + j, (*chip, 1 - c), me).wait_recv()
        for cp in first + passed:
            cp.wait_send()
        for cp in mine:
            cp.wait()

    return _comm_call(
        body, name=name, in_specs=[ANY] * nt, out_specs=[ANY] * nt,
        out_shape=[_sds((s.shape[0], N_DEV) + s.shape[1:], s.dtype) for s in shards],
        scratch_shapes=[pltpu.SemaphoreType.DMA((nt, 7)), pltpu.SemaphoreType.DMA((nt, 7)), pltpu.SemaphoreType.DMA((nt,))],
    )(*shards)


def _exchange_sibling(grads, name):
    nt = len(grads)

    def body(*refs):
        ins, outs = refs[:nt], refs[nt:2 * nt]
        send_sems, recv_sems = refs[2 * nt:]
        x, y, c = _mesh_pos()
        copies = []
        for t in range(nt):
            src = ins[t].at[:, :, 1 - c]
            copies.append(pltpu.make_async_remote_copy(src_ref=src, dst_ref=outs[t], send_sem=send_sems.at[t],
                                                       recv_sem=recv_sems.at[t], device_id=(x, y, 1 - c), device_id_type=MESH))
        for cp in copies:
            cp.start()
        for cp in copies:
            cp.wait()

    return _comm_call(
        body, name=name, in_specs=[ANY] * nt, out_specs=[ANY] * nt,
        out_shape=[_sds((g.shape[0], N_CHIP) + g.shape[3:], g.dtype) for g in grads],
        scratch_shapes=[pltpu.SemaphoreType.DMA((nt,)), pltpu.SemaphoreType.DMA((nt,))],
    )(*grads)


def _exchange_chips(pairs, name):
    nt = len(pairs)

    def body(*refs):
        ins, outs = refs[:nt], refs[nt:2 * nt]
        send_sems, recv_sems = refs[2 * nt:]
        x, y, c = _mesh_pos()
        copies = []
        for t in range(nt):
            for k, (px, py) in enumerate(_other_chips(x, y)):
                copies.append(pltpu.make_async_remote_copy(
                    src_ref=ins[t].at[:, 2 * px + py], dst_ref=outs[t].at[:, k], send_sem=send_sems.at[t, k],
                    recv_sem=recv_sems.at[t, k], device_id=(px, py, c), device_id_type=MESH))
        for cp in copies:
            cp.start()
        for cp in copies:
            cp.wait()

    return _comm_call(
        body, name=name, in_specs=[ANY] * nt, out_specs=[ANY] * nt,
        out_shape=[_sds((p.shape[0], 3) + p.shape[2:], p.dtype) for p in pairs],
        scratch_shapes=[pltpu.SemaphoreType.DMA((nt, 3)), pltpu.SemaphoreType.DMA((nt, 3))],
    )(*pairs)


def _allgather_small(v, name):
    R, C = v.shape

    def body(x_ref, out_ref, send_sems, recv_sems, local_sem):
        x, y, c = _mesh_pos()
        me, sibling = (x, y, c), (x, y, 1 - c)
        chips = _other_chips(x, y)

        def slab(px, py, pc):
            return out_ref.at[4 * px + 2 * py + pc]

        def copy(k, block, to, src=None):
            return pltpu.make_async_remote_copy(
                src_ref=slab(*block) if src is None else src, dst_ref=slab(*block),
                send_sem=send_sems.at[k], recv_sem=recv_sems.at[k], device_id=to, device_id_type=MESH)

        mine = pltpu.make_async_copy(x_ref, slab(*me), local_sem)
        mine.start()
        first = [copy(0, me, sibling, src=x_ref)]
        first += [copy(1 + j, me, (*chip, c), src=x_ref) for j, chip in enumerate(chips)]
        for cp in first:
            cp.start()
        passed = [copy(4 + j, (*chip, c), sibling) for j, chip in enumerate(chips)]
        for j, chip in enumerate(chips):
            copy(1 + j, (*chip, c), me).wait_recv()
            passed[j].start()
        copy(0, sibling, me).wait_recv()
        for j, chip in enumerate(chips):
            copy(4 + j, (*chip, 1 - c), me).wait_recv()
        for cp in first + passed:
            cp.wait_send()
        mine.wait()

    return _comm_call(
        body, name=name, in_specs=[pl.BlockSpec(memory_space=pltpu.VMEM)], out_specs=pl.BlockSpec(memory_space=pltpu.VMEM),
        out_shape=_sds((N_DEV, R, C), v.dtype),
        scratch_shapes=[pltpu.SemaphoreType.DMA((7,)), pltpu.SemaphoreType.DMA((7,)), pltpu.SemaphoreType.DMA],
    )(v)


def _pair_sum(core, grad, got, name):
    L, _, _, R, C = grad.shape
    tr = _tile(R, 256)

    def body(core_ref, mine_ref, got_ref, out_ref):
        out_ref[...] = (mine_ref[...].astype(F32) + got_ref[...].astype(F32)).astype(BF16)

    blk = (None, None, tr, C)
    return _pallas(
        body, name=name, grid=(L, N_CHIP, R // tr), nprefetch=1,
        in_specs=[pl.BlockSpec((None, None, None, tr, C), lambda l, k, r, core: (l, k, core[0], r, 0)),
                  pl.BlockSpec(blk, lambda l, k, r, core: (l, k, r, 0))],
        out_specs=pl.BlockSpec(blk, lambda l, k, r, core: (l, k, r, 0)),
        out_shape=_sds((L, N_CHIP, R, C), BF16),
    )(core, grad, got)


def _adamw(w, g, m, v):
    m = ADAM_B1 * m + (1.0 - ADAM_B1) * g
    v = ADAM_B2 * v + (1.0 - ADAM_B2) * (g * g)
    m_hat = m / (1.0 - ADAM_B1 ** ADAM_STEP)
    v_hat = v / (1.0 - ADAM_B2 ** ADAM_STEP)
    delta = -ADAM_LR * (m_hat / (jnp.sqrt(v_hat) + ADAM_EPS) + ADAM_WD * w)
    return delta, m, v


def _final_adamw(chip, pairs, got, w, m, v, name):
    L, R, C = w.shape
    tr = _tile(R, 128)

    def body(chip_ref, own_ref, got_ref, w_ref, m_ref, v_ref, g_out, d_out, m_out, v_out):
        g = own_ref[...].astype(F32)
        for k in range(3):
            g = g + got_ref[k].astype(F32)
        d, mn, vn = _adamw(w_ref[...], g, m_ref[...], v_ref[...])
        g_out[...] = g
        d_out[...] = d
        m_out[...] = mn
        v_out[...] = vn

    loc = pl.BlockSpec((None, tr, C), lambda l, r, chip: (l, r, 0))
    return _pallas(
        body, name=name, grid=(L, R // tr), nprefetch=1,
        in_specs=[pl.BlockSpec((None, None, tr, C), lambda l, r, chip: (l, chip[0], r, 0)),
                  pl.BlockSpec((None, 3, tr, C), lambda l, r, chip: (l, 0, r, 0)), loc, loc, loc],
        out_specs=[loc] * 4, out_shape=[_sds((L, R, C), F32)] * 4,
    )(chip, pairs, got, w, m, v)


def _small_adamw(dev, allg, w, m, v, width, name):
    R = w.shape[0]

    def body(dev_ref, a_ref, w_ref, m_ref, v_ref, g_out, d_out, m_out, v_out):
        g = a_ref[0]
        for k in range(1, N_DEV):
            g = g + a_ref[k]
        d, mn, vn = _adamw(w_ref[...], g, m_ref[...], v_ref[...])
        g_out[...] = g
        d_out[...] = d
        m_out[...] = mn
        v_out[...] = vn

    full = allg.shape[2] == width
    loc = pl.BlockSpec((R, width), lambda i, dev: (0, 0))
    return _pallas(
        body, name=name, grid=(1,), nprefetch=1,
        in_specs=[pl.BlockSpec((N_DEV, R, width), lambda i, dev: (0, 0, 0 if full else dev[0])), loc, loc, loc],
        out_specs=[loc] * 4, out_shape=[_sds((R, width), F32)] * 4,
    )(dev, allg, w, m, v)


BIG = ("ffn1_w_gu", "ffn1_w_down", "w_in", "w_ret_o", "w_sc_o", "w_cf_o", "w_o", "ffn2_w_gu", "ffn2_w_down")


def kernel(x, positions, norm_g, ffn1_w_gu, ffn1_w_down, w_in, w_ret_o, sc_conv_w, w_sc_o, cf_dw_w, cf_dw_b, cf_ln_g, cf_ln_b, w_cf_o, w_o, ffn2_w_gu, ffn2_w_down, loss_target, m_norm_g, m_ffn1_w_gu, m_ffn1_w_down, m_w_in, m_w_ret_o, m_sc_conv_w, m_w_sc_o, m_cf_dw_w, m_cf_dw_b, m_cf_ln_g, m_cf_ln_b, m_w_cf_o, m_w_o, m_ffn2_w_gu, m_ffn2_w_down, v_norm_g, v_ffn1_w_gu, v_ffn1_w_down, v_w_in, v_w_ret_o, v_sc_conv_w, v_w_sc_o, v_cf_dw_w, v_cf_dw_b, v_cf_ln_g, v_cf_ln_b, v_w_cf_o, v_w_o, v_ffn2_w_gu, v_ffn2_w_down):
    B, S, D = x.shape
    T = B * S
    L = norm_g.shape[0]
    DL = norm_g.shape[2]
    dk = D // 8
    mx, my, mc = _mesh_pos()
    dev_idx = jnp.reshape(4 * mx + 2 * my + mc, (1,)).astype(jnp.int32)
    chip_idx = jnp.reshape(2 * mx + my, (1,)).astype(jnp.int32)
    core_idx = jnp.reshape(mc, (1,)).astype(jnp.int32)

    w32 = dict(ffn1_w_gu=ffn1_w_gu, ffn1_w_down=ffn1_w_down, w_in=w_in, w_ret_o=w_ret_o, w_sc_o=w_sc_o, w_cf_o=w_cf_o,
               w_o=w_o, ffn2_w_gu=ffn2_w_gu, ffn2_w_down=ffn2_w_down)
    m32 = dict(ffn1_w_gu=m_ffn1_w_gu, ffn1_w_down=m_ffn1_w_down, w_in=m_w_in, w_ret_o=m_w_ret_o, w_sc_o=m_w_sc_o,
               w_cf_o=m_w_cf_o, w_o=m_w_o, ffn2_w_gu=m_ffn2_w_gu, ffn2_w_down=m_ffn2_w_down)
    v32 = dict(ffn1_w_gu=v_ffn1_w_gu, ffn1_w_down=v_ffn1_w_down, w_in=v_w_in, w_ret_o=v_w_ret_o, w_sc_o=v_w_sc_o,
               w_cf_o=v_w_cf_o, w_o=v_w_o, ffn2_w_gu=v_ffn2_w_gu, ffn2_w_down=v_ffn2_w_down)

    gathered = _allgather_slabs([w32[n].astype(BF16) for n in BIG], "allgather_weights")
    W = dict(zip(BIG, gathered))
    for n in ("ffn1_w_down", "ffn2_w_down", "w_ret_o", "w_sc_o", "w_cf_o", "w_o"):
        s = W[n].shape
        W[n] = W[n].reshape(s[0], s[1] * s[2], s[3])
    n_sh = 6 + SC_KERNEL + CF_KERNEL
    small_sh = jnp.concatenate([norm_g, sc_conv_w, cf_dw_w], axis=1).reshape(L * n_sh, DL)
    sh_all = _allgather_small(small_sh, "allgather_small_params")
    sh_full = jnp.transpose(sh_all, (1, 0, 2)).reshape(L, n_sh, D)
    norm_full = sh_full[:, :6]
    scw_full = sh_full[:, 6:6 + SC_KERNEL]
    cfw_full = sh_full[:, 6 + SC_KERNEL:]

    half = dk // 2
    inv_freq = ROPE_BASE ** (-jnp.arange(half, dtype=F32) / half)
    invf = jnp.concatenate([inv_freq, inv_freq])[None, :]
    sgn = jnp.concatenate([-jnp.ones((half,), F32), jnp.ones((half,), F32)])[None, :]
    cosf, sinf = _rope_tables(positions.reshape(T, 1), invf, sgn, "rope_tables")
    lgs = jnp.log(1.0 - 2.0 ** (-5.0 - jnp.arange(HEADS, dtype=F32)))

    def vec(a):
        return a.reshape(1, D)

    xc = x.reshape(T, D)
    saved = []
    for l in range(L):
        g = norm_full[l]
        sv = {}
        sv["x0"] = xc
        xc, sv["h1"], sv["gate1"], sv["up1"], sv["f1"] = _ffn_fwd(xc, vec(g[0]), vec(g[1]), W["ffn1_w_gu"], W["ffn1_w_down"], l, f"ffn1_fwd_{l}")
        sv["x1"] = xc
        proj, sv["h2"] = _norm_proj(xc, vec(g[2]), W["w_in"], l, f"proj_fwd_{l}")
        sv["proj"] = proj
        o = _ret_fwd(proj, cosf, sinf, lgs, B, S, D, f"retention_fwd_{l}")
        sv["o"] = o
        ya_in, yb_in, yc_in, sv["cv"], sv["u1"] = _mixer_mid_fwd(proj, o, scw_full[l], cfw_full[l], vec(cf_dw_b[l]), vec(cf_ln_g[l]),
                                                                 vec(cf_ln_b[l]), S, f"mixer_mid_fwd_{l}")
        sv["ya_in"], sv["yb_in"], sv["yc_in"] = ya_in, yb_in, yc_in
        xc, sv["ya"], sv["yb"], sv["yc"], sv["merged"], sv["m"] = _mixer_merge_fwd(
            xc, proj, ya_in, yb_in, yc_in, vec(g[3]), W["w_ret_o"], W["w_sc_o"], W["w_cf_o"], W["w_o"], l, f"mixer_merge_fwd_{l}")
        sv["x2"] = xc
        xc, sv["h3"], sv["gate2"], sv["up2"], sv["f2"] = _ffn_fwd(xc, vec(g[4]), vec(g[5]), W["ffn2_w_gu"], W["ffn2_w_down"], l, f"ffn2_fwd_{l}")
        saved.append(sv)

    dx, loss_part = _loss_grad(xc, loss_target.reshape(T, D), "loss")
    loss = lax.psum(loss_part[0, 0], ("x", "y", "c"))

    G = {n: None for n in BIG}
    small_rows = [None] * L
    F = W["ffn1_w_down"].shape[1]
    cb_gu = W["ffn1_w_gu"].shape[3]
    cb_in = W["w_in"].shape[3]

    def ffn_grads(tag, wgu_name, wd_name, dxo, x_in, h, gate, up, f, g_pre, g_post, l):
        (dxi, df, dgu, act, dgpre, dgpost), tf = _ffn_bwd(dxo, x_in, f, gate, up, g_pre, g_post, W[wgu_name], W[wd_name], l, f"{tag}_bwd_{l}")
        nj = F // tf
        order = (lambda jn: (jn % 2) * nj + jn // 2, tf)
        G[wgu_name] = _mm_tn(h, dgu, l, L, G[wgu_name], f"{tag}_dwgu_{l}", slab_cols=cb_gu, tile_order=order)
        G[wd_name] = _mm_tn(act, df, l, L, G[wd_name], f"{tag}_dwd_{l}")
        return dxi, dgpre, dgpost

    for l in reversed(range(L)):
        sv = saved[l]
        g = norm_full[l]
        dx, dg4, dg5 = ffn_grads("ffn2", "ffn2_w_gu", "ffn2_w_down", dx, sv["x2"], sv["h3"], sv["gate2"], sv["up2"], sv["f2"],
                                 vec(g[4]), vec(g[5]), l)
        (dm, dya, dyb, dyc, d2, dgl, do, dcv, du1, dg3, dlng, dlnb, dcfb) = _mixer_bwd_a(
            dx, sv["m"], vec(g[3]), sv["ya"], sv["yb"], sv["yc"], sv["proj"], sv["o"], sv["cv"], sv["u1"],
            vec(cf_ln_g[l]), vec(cf_ln_b[l]), W["w_ret_o"], W["w_sc_o"], W["w_cf_o"], W["w_o"], l, f"mixer_bwd_a_{l}")
        G["w_o"] = _mm_tn(sv["merged"], dm, l, L, G["w_o"], f"dw_o_{l}")
        G["w_ret_o"] = _mm_tn(sv["ya_in"], dya, l, L, G["w_ret_o"], f"dw_ret_o_{l}")
        G["w_sc_o"] = _mm_tn(sv["yb_in"], dyb, l, L, G["w_sc_o"], f"dw_sc_o_{l}")
        G["w_cf_o"] = _mm_tn(sv["yc_in"], dyc, l, L, G["w_cf_o"], f"dw_cf_o_{l}")
        d4, dscw, dcfw = _mixer_bwd_b(dcv, du1, sv["proj"], scw_full[l], cfw_full[l], S, f"mixer_bwd_b_{l}")
        dq, dkk, dvv = _ret_bwd(do, sv["proj"], cosf, sinf, lgs, B, S, D, f"retention_bwd_{l}")
        dproj = jnp.concatenate([dq, dkk, dvv, d2, d4, dgl], axis=1)
        G["w_in"] = _mm_tn(sv["h2"], dproj, l, L, G["w_in"], f"dw_in_{l}", slab_cols=cb_in, tn_pref=cb_in)
        dx, dg2 = _proj_bwd(dproj, W["w_in"], sv["x1"], vec(g[2]), dx, l, f"proj_bwd_{l}")
        dx, dg0, dg1 = ffn_grads("ffn1", "ffn1_w_gu", "ffn1_w_down", dx, sv["x0"], sv["h1"], sv["gate1"], sv["up1"], sv["f1"],
                                 vec(g[0]), vec(g[1]), l)
        small_rows[l] = (jnp.concatenate([dg0, dg1, dg2, dg3, dg4, dg5, dscw, dcfw], axis=0),
                         jnp.concatenate([dcfb, dlng, dlnb], axis=0))
    grad_x = dx.reshape(B, S, D)

    glist = []
    for n in BIG:
        a = G[n]
        if a.ndim == 3:
            a = a.reshape(a.shape[0], N_DEV, a.shape[1] // N_DEV, a.shape[2])
        glist.append(a.reshape(a.shape[0], N_CHIP, 2, a.shape[2], a.shape[3]))
    from_sibling = _exchange_sibling(glist, "grads_to_sibling")
    pairs = [_pair_sum(core_idx, a, b, f"pair_sum_{n}") for n, a, b in zip(BIG, glist, from_sibling)]
    from_chips = _exchange_chips(pairs, "grads_to_chips")
    big_out = {}
    for n, p, r in zip(BIG, pairs, from_chips):
        big_out[n] = _final_adamw(chip_idx, p, r, w32[n], m32[n], v32[n], f"adamw_{n}")

    sh_part = jnp.concatenate([small_rows[l][0] for l in range(L)], axis=0)
    rep_part = jnp.concatenate([small_rows[l][1] for l in range(L)] + [jnp.zeros((8 - 3 * L % 8, D), F32)] * (1 if 3 * L % 8 else 0), axis=0)
    sh_g = _allgather_small(sh_part, "allgather_small_grads")
    rep_g = _allgather_small(rep_part, "allgather_replicated_grads")

    def pack_sh(a, b, c):
        return jnp.concatenate([a, b, c], axis=1).reshape(L * n_sh, DL)

    def pack_rep(a, b, c):
        rows = jnp.stack([a, b, c], axis=1).reshape(3 * L, D)
        return jnp.concatenate([rows, jnp.ones((rep_part.shape[0] - 3 * L, D), F32)], axis=0)

    sh_res = _small_adamw(dev_idx, sh_g, pack_sh(norm_g, sc_conv_w, cf_dw_w), pack_sh(m_norm_g, m_sc_conv_w, m_cf_dw_w),
                          pack_sh(v_norm_g, v_sc_conv_w, v_cf_dw_w), DL, "adamw_small_sharded")
    rep_res = _small_adamw(dev_idx, rep_g, pack_rep(cf_dw_b, cf_ln_g, cf_ln_b), pack_rep(m_cf_dw_b, m_cf_ln_g, m_cf_ln_b),
                           pack_rep(v_cf_dw_b, v_cf_ln_g, v_cf_ln_b), D, "adamw_small_replicated")

    def unpack_sh(a):
        a = a.reshape(L, n_sh, DL)
        return {"norm_g": a[:, :6], "sc_conv_w": a[:, 6:6 + SC_KERNEL], "cf_dw_w": a[:, 6 + SC_KERNEL:]}

    def unpack_rep(a):
        a = a[:3 * L].reshape(L, 3, D)
        return {"cf_dw_b": a[:, 0], "cf_ln_g": a[:, 1], "cf_ln_b": a[:, 2]}

    order = ("norm_g", "ffn1_w_gu", "ffn1_w_down", "w_in", "w_ret_o", "sc_conv_w", "w_sc_o", "cf_dw_w", "cf_dw_b", "cf_ln_g",
             "cf_ln_b", "w_cf_o", "w_o", "ffn2_w_gu", "ffn2_w_down")
    outs = []
    for kind in range(4):
        small = {**unpack_sh(sh_res[kind]), **unpack_rep(rep_res[kind])}
        outs += [big_out[n][kind] if n in big_out else small[n] for n in order]
    return (loss, grad_x, *outs)
```

```python
import functools

import jax
import jax.numpy as jnp
from jax import lax
from jax.experimental import pallas as pl
from jax.experimental.pallas import tpu as pltpu

F32 = jnp.float32
BF16 = jnp.bfloat16
MESH = pl.DeviceIdType.MESH
ANY = pl.BlockSpec(memory_space=pl.ANY)

N_DEV = 8
N_CHIP = 4
CHUNK = 64
HEADS = 4
ROPE_BASE = 10000.0
NORM_EPS = 1e-6
LN_EPS = 1e-5
SC_KERNEL = 3
CF_KERNEL = 31
HALO = 32
ADAM_LR = 0.001
ADAM_B1 = 0.9
ADAM_B2 = 0.999
ADAM_EPS = 1e-08
ADAM_WD = 0.01
ADAM_STEP = 10
VMEM_LIMIT_V7X = 56 * 1024 * 1024


class _Job:
    def __init__(self, ins, out_shapes, n_sems, n_local, start, finish):
        self.ins, self.out_shapes, self.n_sems, self.n_local = list(ins), list(out_shapes), n_sems, n_local
        self.start, self.finish = start, finish


def _pallas(body, *, name, grid, in_specs, out_specs, out_shape, scratch_shapes=(), aliases=None, nprefetch=0, jobs=()):
    extra = {}
    single = not isinstance(out_shape, (list, tuple))
    out_shape = [out_shape] if single else list(out_shape)
    out_specs = [out_specs] if single else list(out_specs)
    in_specs, scratch = list(in_specs), list(scratch_shapes)
    n_in, n_out, n_scr = len(in_specs), len(out_shape), len(scratch)
    for jb in jobs:
        in_specs += [ANY] * len(jb.ins)
        out_specs += [ANY] * len(jb.out_shapes)
        out_shape += jb.out_shapes
        scratch += [pltpu.SemaphoreType.DMA((jb.n_sems,)), pltpu.SemaphoreType.DMA((jb.n_sems,)),
                    pltpu.SemaphoreType.DMA((max(jb.n_local, 1),))]

    def with_jobs(*refs):
        pre, refs = refs[:nprefetch], refs[nprefetch:]
        ins, p = refs[:n_in], n_in
        jins = []
        for jb in jobs:
            jins.append(refs[p:p + len(jb.ins)])
            p += len(jb.ins)
        outs = refs[p:p + n_out]
        p += n_out
        jouts = []
        for jb in jobs:
            jouts.append(refs[p:p + len(jb.out_shapes)])
            p += len(jb.out_shapes)
        scr = refs[p:p + n_scr]
        p += n_scr
        pids = [pl.program_id(d) for d in range(len(grid))]
        first = functools.reduce(jnp.logical_and, [pid == 0 for pid in pids])
        last = functools.reduce(jnp.logical_and, [pid == g - 1 for pid, g in zip(pids, grid)])

        @pl.when(first)
        def _():
            for i, jb in enumerate(jobs):
                jb.start(jins[i], jouts[i], *refs[p + 3 * i:p + 3 * i + 3])

        body(*pre, *ins, *outs, *scr)

        @pl.when(last)
        def _():
            for i, jb in enumerate(jobs):
                jb.finish(jins[i], jouts[i], *refs[p + 3 * i:p + 3 * i + 3])

    params = pltpu.CompilerParams(dimension_semantics=("arbitrary",) * len(grid), vmem_limit_bytes=VMEM_LIMIT_V7X)
    spec = pltpu.PrefetchScalarGridSpec(num_scalar_prefetch=nprefetch, grid=grid, in_specs=in_specs,
                                        out_specs=out_specs, scratch_shapes=scratch)
    call = pl.pallas_call(with_jobs if jobs else body, name=name, grid_spec=spec, out_shape=out_shape, compiler_params=params,
                          input_output_aliases=aliases or {}, **extra)

    def run(*args):
        res = call(*args, *[a for jb in jobs for a in jb.ins])
        own = res[0] if single else list(res[:n_out])
        if not jobs:
            return own
        jres, p = [], n_out
        for jb in jobs:
            jres.append(list(res[p:p + len(jb.out_shapes)]))
            p += len(jb.out_shapes)
        return own, jres

    return run


def _comm_call(body, *, name, in_specs, out_specs, out_shape, scratch_shapes):
    extra = {}
    return pl.pallas_call(body, name=name, in_specs=in_specs, out_specs=out_specs, out_shape=out_shape,
                          scratch_shapes=scratch_shapes, **extra)


def _sds(shape, dtype):
    return jax.ShapeDtypeStruct(shape, dtype)


def _tile(n, pref):
    t = min(n, pref)
    assert n % t == 0, (n, pref)
    return t


def _sigmoid(x):
    return jax.nn.sigmoid(x)


def _rms_fwd(x, g):
    r = lax.rsqrt(jnp.mean(x * x, axis=-1, keepdims=True) + NORM_EPS)
    return x * r * g


def _rms_bwd(x, g, dy):
    r = lax.rsqrt(jnp.mean(x * x, axis=-1, keepdims=True) + NORM_EPS)
    xh = x * r
    dg = jnp.sum(dy * xh, axis=0, keepdims=True)
    dxh = dy * g
    dx = r * (dxh - xh * jnp.mean(dxh * xh, axis=-1, keepdims=True))
    return dx, dg


def _ln_stats(x):
    mu = jnp.mean(x, axis=-1, keepdims=True)
    xc = x - mu
    rstd = lax.rsqrt(jnp.mean(xc * xc, axis=-1, keepdims=True) + LN_EPS)
    return xc * rstd, rstd


def _ln_bwd(xh, rstd, dxh):
    return rstd * (dxh - jnp.mean(dxh, axis=-1, keepdims=True) - xh * jnp.mean(dxh * xh, axis=-1, keepdims=True))


def _dsilu(x, s):
    return s * (1.0 + x * (1.0 - s))


def _rot(x, cosf, sinf):
    return x * cosf + pltpu.roll(x, x.shape[-1] // 2, 1) * sinf


def _unrot(d, cosf, sinf):
    return d * cosf - pltpu.roll(d, d.shape[-1] // 2, 1) * sinf


def _dot_nn(a, b):
    return jnp.dot(a, b, preferred_element_type=F32)


def _dot_nt(a, b):
    return lax.dot_general(a, b, (((1,), (1,)), ((), ())), preferred_element_type=F32)


def _dot_tn(a, b):
    return lax.dot_general(a, b, (((0,), (0,)), ((), ())), preferred_element_type=F32)


def _ffn_tf(cb):
    return _tile(cb, 512)


def _ffn_fwd(x, g_pre, g_post, wgu, wd, name, jobs=()):
    T, D = x.shape
    F = wd.shape[0]
    cb = wgu.shape[2]
    tm, tf = _tile(T, 512), _ffn_tf(cb)
    nj, nb = F // tf, cb // tf

    def body(x_ref, gpre_ref, gpost_ref, wg_ref, wu_ref, wd_ref, xo_ref, h_ref, gate_ref, up_ref, f_ref, acc_ref):
        j = pl.program_id(1)

        @pl.when(j == 0)
        def _():
            h_ref[...] = _rms_fwd(x_ref[...], gpre_ref[...]).astype(BF16)
            acc_ref[...] = jnp.zeros_like(acc_ref)

        h = h_ref[...]
        gate = _dot_nn(h, wg_ref[...])
        up = _dot_nn(h, wu_ref[...])
        gate_ref[...] = gate.astype(BF16)
        up_ref[...] = up.astype(BF16)
        act = (gate * _sigmoid(gate) * up).astype(BF16)
        acc_ref[...] += _dot_nn(act, wd_ref[...])

        @pl.when(j == nj - 1)
        def _():
            f = acc_ref[...]
            f_ref[...] = f
            xo_ref[...] = x_ref[...] + 0.5 * _rms_fwd(f, gpost_ref[...])

    row = pl.BlockSpec((tm, D), lambda i, j: (i, 0))
    vec = pl.BlockSpec((1, D), lambda i, j: (0, 0))
    col = pl.BlockSpec((tm, tf), lambda i, j: (i, j))
    return _pallas(
        body, name=name, grid=(T // tm, nj),
        in_specs=[row, vec, vec,
                  pl.BlockSpec((None, D, tf), lambda i, j: (j // nb, 0, j % nb)),
                  pl.BlockSpec((None, D, tf), lambda i, j: ((nj + j) // nb, 0, j % nb)),
                  pl.BlockSpec((tf, D), lambda i, j: (j, 0))],
        out_specs=[row, row, col, col, row],
        out_shape=[_sds((T, D), F32), _sds((T, D), BF16), _sds((T, F), BF16), _sds((T, F), BF16), _sds((T, D), F32)],
        scratch_shapes=[pltpu.VMEM((tm, D), F32)], jobs=jobs,
    )(x, g_pre, g_post, wgu, wgu, wd)


def _ffn_bwd(dxo, x, f, gate, up, g_pre, g_post, wgu, wd, name, jobs=()):
    T, D = x.shape
    F = wd.shape[0]
    cb = wgu.shape[2]
    tm, tf = _tile(T, 512), _ffn_tf(cb)
    nj, nb = F // tf, cb // tf

    def body(dxo_ref, x_ref, f_ref, gate_ref, up_ref, gpre_ref, gpost_ref, wg_ref, wu_ref, wd_ref,
             dx_ref, df_ref, dgu_ref, act_ref, dgpre_ref, dgpost_ref, acc_ref):
        i, j = pl.program_id(0), pl.program_id(1)

        @pl.when((i == 0) & (j == 0))
        def _():
            dgpre_ref[...] = jnp.zeros_like(dgpre_ref)
            dgpost_ref[...] = jnp.zeros_like(dgpost_ref)

        @pl.when(j == 0)
        def _():
            df, dgp = _rms_bwd(f_ref[...], gpost_ref[...], 0.5 * dxo_ref[...])
            df_ref[...] = df.astype(BF16)
            dgpost_ref[...] += dgp
            acc_ref[...] = jnp.zeros_like(acc_ref)

        dact = _dot_nt(df_ref[...], wd_ref[...])
        g = gate_ref[...].astype(F32)
        u = up_ref[...].astype(F32)
        s = _sigmoid(g)
        silu = g * s
        dgate = (dact * u * _dsilu(g, s)).astype(BF16)
        dup = (dact * silu).astype(BF16)
        act_ref[...] = (silu * u).astype(BF16)
        dgu_ref[:, :tf] = dgate
        dgu_ref[:, tf:] = dup
        acc_ref[...] += _dot_nt(dgate, wg_ref[...]) + _dot_nt(dup, wu_ref[...])

        @pl.when(j == nj - 1)
        def _():
            dxin, dgp = _rms_bwd(x_ref[...], gpre_ref[...], acc_ref[...])
            dx_ref[...] = dxo_ref[...] + dxin
            dgpre_ref[...] += dgp

    row = pl.BlockSpec((tm, D), lambda i, j: (i, 0))
    vec = pl.BlockSpec((1, D), lambda i, j: (0, 0))
    col = pl.BlockSpec((tm, tf), lambda i, j: (i, j))
    return _pallas(
        body, name=name, grid=(T // tm, nj),
        in_specs=[row, row, row, col, col, vec, vec,
                  pl.BlockSpec((None, D, tf), lambda i, j: (j // nb, 0, j % nb)),
                  pl.BlockSpec((None, D, tf), lambda i, j: ((nj + j) // nb, 0, j % nb)),
                  pl.BlockSpec((tf, D), lambda i, j: (j, 0))],
        out_specs=[row, row, pl.BlockSpec((tm, 2 * tf), lambda i, j: (i, j)), col, vec, vec],
        out_shape=[_sds((T, D), F32), _sds((T, D), BF16), _sds((T, 2 * F), BF16), _sds((T, F), BF16),
                   _sds((1, D), F32), _sds((1, D), F32)],
        scratch_shapes=[pltpu.VMEM((tm, D), F32)], jobs=jobs,
    )(dxo, x, f, gate, up, g_pre, g_post, wgu, wgu, wd)


def _mm_tn(a, b, name, slab_cols=None, tile_order=None, tn_pref=1024, jobs=()):
    T, K = a.shape
    N = b.shape[1]
    tt = _tile(T, 512)
    tko = _tile(K, 1024)
    tn = tile_order[1] if tile_order else _tile(slab_cols or N, tn_pref)
    nat = tile_order[0] if tile_order else (lambda jn: jn)
    nt = T // tt

    def body(a_ref, b_ref, o_ref, acc_ref):
        t = pl.program_id(2)

        @pl.when(t == 0)
        def _():
            acc_ref[...] = jnp.zeros_like(acc_ref)

        acc_ref[...] += _dot_tn(a_ref[...], b_ref[...])

        @pl.when(t == nt - 1)
        def _():
            o_ref[...] = acc_ref[...].astype(o_ref.dtype)

    if slab_cols is None:
        shape = (K, N)
        ospec = pl.BlockSpec((tko, tn), lambda k, jn, t: (k, nat(jn)))
    else:
        nb = slab_cols // tn
        shape = (N // slab_cols, K, slab_cols)
        ospec = pl.BlockSpec((None, tko, tn), lambda k, jn, t: (nat(jn) // nb, k, nat(jn) % nb))
    return _pallas(body, name=name, grid=(K // tko, N // tn, nt),
                   in_specs=[pl.BlockSpec((tt, tko), lambda k, jn, t: (t, k)), pl.BlockSpec((tt, tn), lambda k, jn, t: (t, jn))],
                   out_specs=ospec, out_shape=_sds(shape, BF16), scratch_shapes=[pltpu.VMEM((tko, tn), F32)], jobs=jobs)(a, b)


def _norm_proj(x, g, w, name, jobs=()):
    T, D = x.shape
    nd, cb = w.shape[0], w.shape[2]
    tm = _tile(T, 512)

    def body(x_ref, g_ref, w_ref, p_ref, h_ref):
        @pl.when(pl.program_id(1) == 0)
        def _():
            h_ref[...] = _rms_fwd(x_ref[...], g_ref[...]).astype(BF16)

        p_ref[...] = _dot_nn(h_ref[...], w_ref[...])

    row = pl.BlockSpec((tm, D), lambda i, j: (i, 0))
    return _pallas(
        body, name=name, grid=(T // tm, nd),
        in_specs=[row, pl.BlockSpec((1, D), lambda i, j: (0, 0)),
                  pl.BlockSpec((None, D, cb), lambda i, j: (j, 0, 0))],
        out_specs=[pl.BlockSpec((tm, cb), lambda i, j: (i, j)), row],
        out_shape=[_sds((T, nd * cb), F32), _sds((T, D), BF16)], jobs=jobs,
    )(x, g, w)


def _proj_bwd(dproj, w, x, g, dxo, name, jobs=()):
    T, D = x.shape
    nd, cb = w.shape[0], w.shape[2]
    tm = _tile(T, 512)

    def body(dp_ref, w_ref, x_ref, g_ref, dxo_ref, dx_ref, dg_ref, acc_ref):
        i, j = pl.program_id(0), pl.program_id(1)

        @pl.when((i == 0) & (j == 0))
        def _():
            dg_ref[...] = jnp.zeros_like(dg_ref)

        @pl.when(j == 0)
        def _():
            acc_ref[...] = jnp.zeros_like(acc_ref)

        acc_ref[...] += _dot_nt(dp_ref[...], w_ref[...])

        @pl.when(j == nd - 1)
        def _():
            dxin, dgp = _rms_bwd(x_ref[...], g_ref[...], acc_ref[...])
            dx_ref[...] = dxo_ref[...] + dxin
            dg_ref[...] += dgp

    row = pl.BlockSpec((tm, D), lambda i, j: (i, 0))
    vec = pl.BlockSpec((1, D), lambda i, j: (0, 0))
    return _pallas(
        body, name=name, grid=(T // tm, nd),
        in_specs=[pl.BlockSpec((tm, cb), lambda i, j: (i, j)),
                  pl.BlockSpec((None, D, cb), lambda i, j: (j, 0, 0)), row, vec, row],
        out_specs=[row, vec],
        out_shape=[_sds((T, D), F32), _sds((1, D), F32)],
        scratch_shapes=[pltpu.VMEM((tm, D), F32)], jobs=jobs,
    )(dproj, w, x, g, dxo)


def _rope_tables(pos, invf, sgn, name):
    T = pos.shape[0]
    dk = invf.shape[1]
    tm = _tile(T, 1024)

    def body(p_ref, f_ref, s_ref, c_out, s_out):
        ang = p_ref[...].astype(F32) * f_ref[...]
        c_out[...] = jnp.cos(ang)
        s_out[...] = jnp.sin(ang) * s_ref[...]

    vec = pl.BlockSpec((1, dk), lambda i: (0, 0))
    out = pl.BlockSpec((tm, dk), lambda i: (i, 0))
    return _pallas(body, name=name, grid=(T // tm,), in_specs=[pl.BlockSpec((tm, 1), lambda i: (i, 0)), vec, vec],
                   out_specs=[out, out], out_shape=[_sds((T, dk), F32), _sds((T, dk), F32)])(pos, invf, sgn)


def _decay(lg, r0, tq, S):
    r = r0 + lax.broadcasted_iota(jnp.int32, (tq, S), 0)
    c = lax.broadcasted_iota(jnp.int32, (tq, S), 1)
    rc, cc = r // CHUNK, c // CHUNK
    d = (r - c).astype(F32)
    e = jnp.where(rc == cc, jnp.abs(d), d)
    return jnp.where(cc > rc, 0.0, jnp.exp(lg * e))


def _ret_fwd(proj, cosf, sinf, lgs, B, S, D, name, jobs=()):
    T = B * S
    dk, dv = D // 8, D // 4
    tq = _tile(S, 256)
    nq = S // tq
    scale = dk ** -0.5

    def body(lg_ref, q_ref, k_ref, v_ref, cq_ref, sq_ref, ck_ref, sk_ref, o_ref, kr_ref, vb_ref):
        h, qi = pl.program_id(1), pl.program_id(2)

        @pl.when(qi == 0)
        def _():
            kr_ref[...] = (_rot(k_ref[...], ck_ref[...], sk_ref[...]) * scale).astype(BF16)
            vb_ref[...] = v_ref[...].astype(BF16)

        q = _rot(q_ref[...], cq_ref[...], sq_ref[...]).astype(BF16)
        s = _dot_nt(q, kr_ref[...])
        p = (s * _decay(lg_ref[h], qi * tq, tq, S)).astype(BF16)
        o_ref[...] = _dot_nn(p, vb_ref[...])

    qrow = lambda b, h, qi: b * nq + qi
    return _pallas(
        body, name=name, grid=(B, HEADS, nq),
        in_specs=[pl.BlockSpec(memory_space=pltpu.SMEM),
                  pl.BlockSpec((tq, dk), lambda b, h, qi: (qrow(b, h, qi), h)),
                  pl.BlockSpec((S, dk), lambda b, h, qi: (b, HEADS + h)),
                  pl.BlockSpec((S, dv), lambda b, h, qi: (b, HEADS + h)),
                  pl.BlockSpec((tq, dk), lambda b, h, qi: (qrow(b, h, qi), 0)),
                  pl.BlockSpec((tq, dk), lambda b, h, qi: (qrow(b, h, qi), 0)),
                  pl.BlockSpec((S, dk), lambda b, h, qi: (b, 0)),
                  pl.BlockSpec((S, dk), lambda b, h, qi: (b, 0))],
        out_specs=pl.BlockSpec((tq, dv), lambda b, h, qi: (qrow(b, h, qi), h)),
        out_shape=_sds((T, D), F32),
        scratch_shapes=[pltpu.VMEM((S, dk), BF16), pltpu.VMEM((S, dv), BF16)], jobs=jobs,
    )(lgs, proj, proj, proj, cosf, sinf, cosf, sinf)


def _ret_bwd(do, proj, cosf, sinf, lgs, B, S, D, name, jobs=()):
    T = B * S
    dk, dv = D // 8, D // 4
    tq = _tile(S, 256)
    nq = S // tq
    scale = dk ** -0.5

    def body(lg_ref, do_ref, q_ref, k_ref, v_ref, c_ref, s_ref, dq_ref, dk_ref, dv_ref, kr_ref, vb_ref, dka_ref, dva_ref):
        h = pl.program_id(1)
        lg = lg_ref[h]
        kr_ref[...] = (_rot(k_ref[...], c_ref[...], s_ref[...]) * scale).astype(BF16)
        vb_ref[...] = v_ref[...].astype(BF16)
        dka_ref[...] = jnp.zeros_like(dka_ref)
        dva_ref[...] = jnp.zeros_like(dva_ref)

        def step(qi, carry):
            rows = pl.ds(pl.multiple_of(qi * tq, tq), tq)
            cq, sq = c_ref[rows, :], s_ref[rows, :]
            q = _rot(q_ref[rows, :], cq, sq).astype(BF16)
            dout = do_ref[rows, :].astype(BF16)
            w = _decay(lg, qi * tq, tq, S)
            p = (_dot_nt(q, kr_ref[...]) * w).astype(BF16)
            ds = (_dot_nt(dout, vb_ref[...]) * w).astype(BF16)
            dq_ref[rows, :] = _unrot(_dot_nn(ds, kr_ref[...]), cq, sq).astype(BF16)
            dka_ref[...] += _dot_tn(ds, q)
            dva_ref[...] += _dot_tn(p, dout)
            return carry

        lax.fori_loop(0, nq, step, 0)
        dk_ref[...] = _unrot(dka_ref[...] * scale, c_ref[...], s_ref[...]).astype(BF16)
        dv_ref[...] = dva_ref[...].astype(BF16)

    return _pallas(
        body, name=name, grid=(B, HEADS),
        in_specs=[pl.BlockSpec(memory_space=pltpu.SMEM),
                  pl.BlockSpec((S, dv), lambda b, h: (b, h)),
                  pl.BlockSpec((S, dk), lambda b, h: (b, h)),
                  pl.BlockSpec((S, dk), lambda b, h: (b, HEADS + h)),
                  pl.BlockSpec((S, dv), lambda b, h: (b, HEADS + h)),
                  pl.BlockSpec((S, dk), lambda b, h: (b, 0)),
                  pl.BlockSpec((S, dk), lambda b, h: (b, 0))],
        out_specs=[pl.BlockSpec((S, dk), lambda b, h: (b, h)), pl.BlockSpec((S, dk), lambda b, h: (b, h)),
                   pl.BlockSpec((S, dv), lambda b, h: (b, h))],
        out_shape=[_sds((T, D // 2), BF16), _sds((T, D // 2), BF16), _sds((T, D), BF16)],
        scratch_shapes=[pltpu.VMEM((S, dk), BF16), pltpu.VMEM((S, dv), BF16), pltpu.VMEM((S, dk), F32), pltpu.VMEM((S, dv), F32)], jobs=jobs,
    )(lgs, do, proj, proj, proj, cosf, sinf)


COL_GRET, COL_SCB, COL_SCC, COL_SCX, COL_GLUA, COL_GLUB, COL_GL = 2, 3, 4, 5, 6, 7, 8


def _causal_conv(ext_ref, w_ref, taps, tm):
    acc = None
    for j in range(taps):
        term = ext_ref[pl.ds(HALO - (taps - 1) + j, tm), :] * w_ref[j:j + 1, :]
        acc = term if acc is None else acc + term
    return acc


def _mixer_mid_fwd(proj, o, sc_w, cf_w, cf_b, ln_g, ln_b, S, name, jobs=()):
    T, D = o.shape
    dv = D // HEADS
    tm = _tile(S, 256)
    per_seq = S // tm
    hb = tm // HALO

    def body(gret_ref, scb_ref, scc_ref, scx_ref, ga_ref, gb_ref, scc_h, scx_h, ga_h, gb_h, o_ref,
             scw_ref, cfw_ref, cfb_ref, lng_ref, lnb_ref, ya_ref, yb_ref, yc_ref, cv_ref, u1_ref, ext_ref):
        first = (pl.program_id(0) % per_seq) == 0
        keep = jnp.where(first, 0.0, 1.0)
        gr = gret_ref[...]
        sg = gr * _sigmoid(gr)
        for hh in range(HEADS):
            cols = slice(hh * dv, (hh + 1) * dv)
            hn, _ = _ln_stats(o_ref[:, cols])
            ya_ref[:, cols] = (sg[:, cols] * hn).astype(BF16)
        ext_ref[:HALO, :] = scc_h[...] * scx_h[...] * keep
        ext_ref[HALO:, :] = scc_ref[...] * scx_ref[...]
        cv = _causal_conv(ext_ref, scw_ref, SC_KERNEL, tm)
        cv_ref[...] = cv
        yb_ref[...] = (scb_ref[...] * cv).astype(BF16)
        ext_ref[:HALO, :] = ga_h[...] * _sigmoid(gb_h[...]) * keep
        ext_ref[HALO:, :] = ga_ref[...] * _sigmoid(gb_ref[...])
        u1 = _causal_conv(ext_ref, cfw_ref, CF_KERNEL, tm) + cfb_ref[...]
        u1_ref[...] = u1
        xh, _ = _ln_stats(u1)
        u2 = xh * lng_ref[...] + lnb_ref[...]
        yc_ref[...] = (u2 * _sigmoid(u2)).astype(BF16)

    def colblk(c):
        return pl.BlockSpec((tm, D), lambda i: (i, c))

    def halo(c):
        return pl.BlockSpec((HALO, D), lambda i: (jnp.maximum(i * hb - 1, 0), c))

    row = pl.BlockSpec((tm, D), lambda i: (i, 0))
    vec = pl.BlockSpec((1, D), lambda i: (0, 0))
    return _pallas(
        body, name=name, grid=(T // tm,),
        in_specs=[colblk(COL_GRET), colblk(COL_SCB), colblk(COL_SCC), colblk(COL_SCX), colblk(COL_GLUA), colblk(COL_GLUB),
                  halo(COL_SCC), halo(COL_SCX), halo(COL_GLUA), halo(COL_GLUB), row,
                  pl.BlockSpec((SC_KERNEL, D), lambda i: (0, 0)), pl.BlockSpec((CF_KERNEL, D), lambda i: (0, 0)), vec, vec, vec],
        out_specs=[row, row, row, row, row],
        out_shape=[_sds((T, D), BF16)] * 3 + [_sds((T, D), F32)] * 2,
        scratch_shapes=[pltpu.VMEM((HALO + tm, D), F32)], jobs=jobs,
    )(proj, proj, proj, proj, proj, proj, proj, proj, proj, proj, o, sc_w, cf_w, cf_b, ln_g, ln_b)


def _mixer_merge_fwd(x, proj, ya_in, yb_in, yc_in, g_post, w_ret, w_sc, w_cf, w_o, name, jobs=()):
    T, D = x.shape
    tm = _tile(T, 256)

    def body(x_ref, g0_ref, g1_ref, g2_ref, ya_in_ref, yb_in_ref, yc_in_ref, gp_ref, wr_ref, ws_ref, wc_ref, wo_ref,
             xo_ref, ya_ref, yb_ref, yc_ref, mg_ref, m_ref):
        ya = _dot_nn(ya_in_ref[...], wr_ref[...])
        yb = _dot_nn(yb_in_ref[...], ws_ref[...])
        yc = _dot_nn(yc_in_ref[...], wc_ref[...])
        ya_ref[...] = ya.astype(BF16)
        yb_ref[...] = yb.astype(BF16)
        yc_ref[...] = yc.astype(BF16)
        merged = (_sigmoid(g0_ref[...]) * ya + _sigmoid(g1_ref[...]) * yb + _sigmoid(g2_ref[...]) * yc).astype(BF16)
        mg_ref[...] = merged
        m = _dot_nn(merged, wo_ref[...])
        m_ref[...] = m
        xo_ref[...] = x_ref[...] + _rms_fwd(m, gp_ref[...])

    row = pl.BlockSpec((tm, D), lambda i: (i, 0))
    wsp = pl.BlockSpec((D, D), lambda i: (0, 0))

    def colblk(c):
        return pl.BlockSpec((tm, D), lambda i: (i, c))

    return _pallas(
        body, name=name, grid=(T // tm,),
        in_specs=[row, colblk(COL_GL), colblk(COL_GL + 1), colblk(COL_GL + 2), row, row, row,
                  pl.BlockSpec((1, D), lambda i: (0, 0)), wsp, wsp, wsp, wsp],
        out_specs=[row] * 6,
        out_shape=[_sds((T, D), F32)] + [_sds((T, D), BF16)] * 4 + [_sds((T, D), F32)], jobs=jobs,
    )(x, proj, proj, proj, ya_in, yb_in, yc_in, g_post, w_ret, w_sc, w_cf, w_o)


def _mixer_bwd_a(dxo, m, g_post, ya, yb, yc, proj, o, cv, u1, ln_g, ln_b, w_ret, w_sc, w_cf, w_o, name, jobs=()):
    T, D = m.shape
    dv = D // HEADS
    tm = _tile(T, 128)

    def body(dxo_ref, m_ref, gp_ref, ya_ref, yb_ref, yc_ref, g0_ref, g1_ref, g2_ref, gret_ref, scb_ref, o_ref, cv_ref, u1_ref,
             lng_ref, lnb_ref, wr_ref, ws_ref, wc_ref, wo_ref,
             dm_ref, dya_ref, dyb_ref, dyc_ref, d2_ref, dgl_ref, do_ref, dcv_ref, du1_ref, dgp_ref, dlng_ref, dlnb_ref, dcfb_ref):
        @pl.when(pl.program_id(0) == 0)
        def _():
            for r in (dgp_ref, dlng_ref, dlnb_ref, dcfb_ref):
                r[...] = jnp.zeros_like(r)

        dm, dgp = _rms_bwd(m_ref[...], gp_ref[...], dxo_ref[...])
        dgp_ref[...] += dgp
        dmb = dm.astype(BF16)
        dm_ref[...] = dmb
        dmerged = _dot_nt(dmb, wo_ref[...])
        dys = []
        for k, (g_ref, y_ref, dy_ref) in enumerate(((g0_ref, ya_ref, dya_ref), (g1_ref, yb_ref, dyb_ref), (g2_ref, yc_ref, dyc_ref))):
            sg = _sigmoid(g_ref[...])
            dgl_ref[:, k * D:(k + 1) * D] = (dmerged * y_ref[...].astype(F32) * sg * (1.0 - sg)).astype(BF16)
            dy = (dmerged * sg).astype(BF16)
            dy_ref[...] = dy
            dys.append(dy)
        dya_in = _dot_nt(dys[0], wr_ref[...])
        gr = gret_ref[...]
        sr = _sigmoid(gr)
        for hh in range(HEADS):
            cols = slice(hh * dv, (hh + 1) * dv)
            hn, rstd = _ln_stats(o_ref[:, cols])
            d2_ref[:, cols] = (dya_in[:, cols] * hn * _dsilu(gr[:, cols], sr[:, cols])).astype(BF16)
            do_ref[:, cols] = _ln_bwd(hn, rstd, dya_in[:, cols] * gr[:, cols] * sr[:, cols])
        dyb_in = _dot_nt(dys[1], ws_ref[...])
        d2_ref[:, D:] = (dyb_in * cv_ref[...]).astype(BF16)
        dcv_ref[...] = dyb_in * scb_ref[...]
        dyc_in = _dot_nt(dys[2], wc_ref[...])
        xh, rstd = _ln_stats(u1_ref[...])
        u2 = xh * lng_ref[...] + lnb_ref[...]
        du2 = dyc_in * _dsilu(u2, _sigmoid(u2))
        dlng_ref[...] += jnp.sum(du2 * xh, axis=0, keepdims=True)
        dlnb_ref[...] += jnp.sum(du2, axis=0, keepdims=True)
        du1 = _ln_bwd(xh, rstd, du2 * lng_ref[...])
        du1_ref[...] = du1
        dcfb_ref[...] += jnp.sum(du1, axis=0, keepdims=True)

    row = pl.BlockSpec((tm, D), lambda i: (i, 0))
    vec = pl.BlockSpec((1, D), lambda i: (0, 0))
    wsp = pl.BlockSpec((D, D), lambda i: (0, 0))

    def colblk(c):
        return pl.BlockSpec((tm, D), lambda i: (i, c))

    return _pallas(
        body, name=name, grid=(T // tm,),
        in_specs=[row, row, vec, row, row, row, colblk(COL_GL), colblk(COL_GL + 1), colblk(COL_GL + 2),
                  colblk(COL_GRET), colblk(COL_SCB), row, row, row, vec, vec, wsp, wsp, wsp, wsp],
        out_specs=[row, row, row, row, pl.BlockSpec((tm, 2 * D), lambda i: (i, 0)), pl.BlockSpec((tm, 3 * D), lambda i: (i, 0)),
                   row, row, row, vec, vec, vec, vec],
        out_shape=[_sds((T, D), BF16)] * 4 + [_sds((T, 2 * D), BF16), _sds((T, 3 * D), BF16)] + [_sds((T, D), F32)] * 3
                  + [_sds((1, D), F32)] * 4, jobs=jobs,
    )(dxo, m, g_post, ya, yb, yc, proj, proj, proj, proj, proj, o, cv, u1, ln_g, ln_b, w_ret, w_sc, w_cf, w_o)


def _mixer_bwd_b(dcv, du1, proj, sc_w, cf_w, S, name, jobs=()):
    T, D = dcv.shape
    tm = _tile(S, 256)
    per_seq = S // tm
    hb = tm // HALO
    last_hb = T // HALO - 1

    def anti_conv(dext_ref, w_ref, taps):
        acc = None
        for j in range(taps):
            term = dext_ref[pl.ds(taps - 1 - j, tm), :] * w_ref[j:j + 1, :]
            acc = term if acc is None else acc + term
        return acc

    def body(dcv_ref, du1_ref, dcv_n, du1_n, scc_ref, scx_ref, ga_ref, gb_ref, scc_h, scx_h, ga_h, gb_h, scw_ref, cfw_ref,
             d4_ref, dscw_ref, dcfw_ref, ext_ref, dext_ref):
        i = pl.program_id(0)
        keep_prev = jnp.where((i % per_seq) == 0, 0.0, 1.0)
        keep_next = jnp.where((i % per_seq) == per_seq - 1, 0.0, 1.0)

        @pl.when(i == 0)
        def _():
            dscw_ref[...] = jnp.zeros_like(dscw_ref)
            dcfw_ref[...] = jnp.zeros_like(dcfw_ref)

        dcv = dcv_ref[...]
        dext_ref[:tm, :] = dcv
        dext_ref[tm:, :] = dcv_n[...] * keep_next
        dz = anti_conv(dext_ref, scw_ref, SC_KERNEL)
        scc, scx = scc_ref[...], scx_ref[...]
        d4_ref[:, :D] = (dz * scx).astype(BF16)
        d4_ref[:, D:2 * D] = (dz * scc).astype(BF16)
        ext_ref[:HALO, :] = scc_h[...] * scx_h[...] * keep_prev
        ext_ref[HALO:, :] = scc * scx
        for j in range(SC_KERNEL):
            dscw_ref[j:j + 1, :] += jnp.sum(dcv * ext_ref[pl.ds(HALO - (SC_KERNEL - 1) + j, tm), :], axis=0, keepdims=True)
        du1 = du1_ref[...]
        dext_ref[:tm, :] = du1
        dext_ref[tm:, :] = du1_n[...] * keep_next
        du0 = anti_conv(dext_ref, cfw_ref, CF_KERNEL)
        ga = ga_ref[...]
        sb = _sigmoid(gb_ref[...])
        d4_ref[:, 2 * D:3 * D] = (du0 * sb).astype(BF16)
        d4_ref[:, 3 * D:] = (du0 * ga * sb * (1.0 - sb)).astype(BF16)
        ext_ref[:HALO, :] = ga_h[...] * _sigmoid(gb_h[...]) * keep_prev
        ext_ref[HALO:, :] = ga * sb
        for j in range(CF_KERNEL):
            dcfw_ref[j:j + 1, :] += jnp.sum(du1 * ext_ref[pl.ds(HALO - (CF_KERNEL - 1) + j, tm), :], axis=0, keepdims=True)

    row = pl.BlockSpec((tm, D), lambda i: (i, 0))
    nxt = pl.BlockSpec((HALO, D), lambda i: (jnp.minimum((i + 1) * hb, last_hb), 0))

    def colblk(c):
        return pl.BlockSpec((tm, D), lambda i: (i, c))

    def halo(c):
        return pl.BlockSpec((HALO, D), lambda i: (jnp.maximum(i * hb - 1, 0), c))

    return _pallas(
        body, name=name, grid=(T // tm,),
        in_specs=[row, row, nxt, nxt, colblk(COL_SCC), colblk(COL_SCX), colblk(COL_GLUA), colblk(COL_GLUB),
                  halo(COL_SCC), halo(COL_SCX), halo(COL_GLUA), halo(COL_GLUB),
                  pl.BlockSpec((SC_KERNEL, D), lambda i: (0, 0)), pl.BlockSpec((CF_KERNEL, D), lambda i: (0, 0))],
        out_specs=[pl.BlockSpec((tm, 4 * D), lambda i: (i, 0)), pl.BlockSpec((SC_KERNEL, D), lambda i: (0, 0)),
                   pl.BlockSpec((CF_KERNEL, D), lambda i: (0, 0))],
        out_shape=[_sds((T, 4 * D), BF16), _sds((SC_KERNEL, D), F32), _sds((CF_KERNEL, D), F32)],
        scratch_shapes=[pltpu.VMEM((HALO + tm, D), F32), pltpu.VMEM((tm + HALO, D), F32)], jobs=jobs,
    )(dcv, du1, dcv, du1, proj, proj, proj, proj, proj, proj, proj, proj, sc_w, cf_w)


def _loss_grad(y, tgt, name):
    T, D = y.shape
    tm = _tile(T, 512)

    def body(y_ref, t_ref, dy_ref, loss_ref):
        @pl.when(pl.program_id(0) == 0)
        def _():
            loss_ref[...] = jnp.zeros_like(loss_ref)

        e = y_ref[...] - t_ref[...]
        dy_ref[...] = e * (1.0 / D)
        loss_ref[...] += 0.5 * jnp.sum(jnp.sum(e * e, axis=-1, keepdims=True) * (1.0 / D), axis=0, keepdims=True)

    row = pl.BlockSpec((tm, D), lambda i: (i, 0))
    return _pallas(body, name=name, grid=(T // tm,), in_specs=[row, row],
                   out_specs=[row, pl.BlockSpec((1, 1), lambda i: (0, 0))],
                   out_shape=[_sds((T, D), F32), _sds((1, 1), F32)])(y, tgt)


def _mesh_pos():
    return lax.axis_index("x"), lax.axis_index("y"), lax.axis_index("c")


def _other_chips(x, y):
    return [(1 - x, y), (x, 1 - y), (1 - x, 1 - y)]


def _allgather_job(shards):
    nt = len(shards)

    def parts(ins, outs, send, recv):
        x, y, c = _mesh_pos()

        def slab(t, px, py, pc):
            return outs[t].at[4 * px + 2 * py + pc]

        def copy(t, k, block, to, src=None):
            return pltpu.make_async_remote_copy(
                src_ref=slab(t, *block) if src is None else src, dst_ref=slab(t, *block),
                send_sem=send.at[7 * t + k], recv_sem=recv.at[7 * t + k], device_id=to, device_id_type=MESH)

        return (x, y, c), (x, y, 1 - c), _other_chips(x, y), c, slab, copy

    def start(ins, outs, send, recv, loc):
        me, sibling, chips, c, slab, copy = parts(ins, outs, send, recv)
        for t in range(nt):
            pltpu.make_async_copy(ins[t], slab(t, *me), loc.at[t]).start()
            copy(t, 0, me, sibling, src=ins[t]).start()
            for j, chip in enumerate(chips):
                copy(t, 1 + j, me, (*chip, c), src=ins[t]).start()

    def finish(ins, outs, send, recv, loc):
        me, sibling, chips, c, slab, copy = parts(ins, outs, send, recv)
        for j, chip in enumerate(chips):
            for t in range(nt):
                copy(t, 1 + j, (*chip, c), me).wait_recv()
                copy(t, 4 + j, (*chip, c), sibling).start()
        for t in range(nt):
            copy(t, 0, sibling, me).wait_recv()
            for j, chip in enumerate(chips):
                copy(t, 4 + j, (*chip, 1 - c), me).wait_recv()
        for t in range(nt):
            copy(t, 0, me, sibling, src=ins[t]).wait_send()
            for j, chip in enumerate(chips):
                copy(t, 1 + j, me, (*chip, c), src=ins[t]).wait_send()
                copy(t, 4 + j, (*chip, c), sibling).wait_send()
            pltpu.make_async_copy(ins[t], slab(t, *me), loc.at[t]).wait()

    return _Job(shards, [_sds((N_DEV,) + s.shape, s.dtype) for s in shards], 7 * nt, nt, start, finish)


def _to_sibling_job(grads):
    nt = len(grads)

    def copies(ins, outs, send, recv):
        x, y, c = _mesh_pos()
        return [pltpu.make_async_remote_copy(src_ref=ins[t].at[:, 1 - c], dst_ref=outs[t], send_sem=send.at[t], recv_sem=recv.at[t],
                                             device_id=(x, y, 1 - c), device_id_type=MESH) for t in range(nt)]

    def start(ins, outs, send, recv, loc):
        for cp in copies(ins, outs, send, recv):
            cp.start()

    def finish(ins, outs, send, recv, loc):
        for cp in copies(ins, outs, send, recv):
            cp.wait()

    return _Job(grads, [_sds((N_CHIP,) + g.shape[2:], g.dtype) for g in grads], nt, 0, start, finish)


def _to_chips_job(pairs):
    nt = len(pairs)

    def copies(ins, outs, send, recv, loc):
        x, y, c = _mesh_pos()
        remote = [pltpu.make_async_remote_copy(src_ref=ins[t].at[2 * px + py], dst_ref=outs[t].at[k], send_sem=send.at[3 * t + k],
                                               recv_sem=recv.at[3 * t + k], device_id=(px, py, c), device_id_type=MESH)
                  for t in range(nt) for k, (px, py) in enumerate(_other_chips(x, y))]
        local = [pltpu.make_async_copy(ins[t].at[2 * x + y], outs[t].at[3], loc.at[t]) for t in range(nt)]
        return remote + local

    def start(ins, outs, send, recv, loc):
        for cp in copies(ins, outs, send, recv, loc):
            cp.start()

    def finish(ins, outs, send, recv, loc):
        for cp in copies(ins, outs, send, recv, loc):
            cp.wait()

    return _Job(pairs, [_sds(p.shape, p.dtype) for p in pairs], 3 * nt, nt, start, finish)


def _run_jobs(jobs, name):
    def body(o_ref):
        o_ref[...] = jnp.zeros_like(o_ref)

    _, jres = _pallas(body, name=name, grid=(1,), in_specs=[], out_specs=pl.BlockSpec((8, 128), lambda i: (0, 0)),
                      out_shape=_sds((8, 128), F32), jobs=jobs)()
    return jres


def _allgather_small(v, name):
    R, C = v.shape

    def body(x_ref, out_ref, send_sems, recv_sems, local_sem):
        x, y, c = _mesh_pos()
        me, sibling = (x, y, c), (x, y, 1 - c)
        chips = _other_chips(x, y)

        def slab(px, py, pc):
            return out_ref.at[4 * px + 2 * py + pc]

        def copy(k, block, to, src=None):
            return pltpu.make_async_remote_copy(
                src_ref=slab(*block) if src is None else src, dst_ref=slab(*block),
                send_sem=send_sems.at[k], recv_sem=recv_sems.at[k], device_id=to, device_id_type=MESH)

        mine = pltpu.make_async_copy(x_ref, slab(*me), local_sem)
        mine.start()
        first = [copy(0, me, sibling, src=x_ref)]
        first += [copy(1 + j, me, (*chip, c), src=x_ref) for j, chip in enumerate(chips)]
        for cp in first:
            cp.start()
        passed = [copy(4 + j, (*chip, c), sibling) for j, chip in enumerate(chips)]
        for j, chip in enumerate(chips):
            copy(1 + j, (*chip, c), me).wait_recv()
            passed[j].start()
        copy(0, sibling, me).wait_recv()
        for j, chip in enumerate(chips):
            copy(4 + j, (*chip, 1 - c), me).wait_recv()
        for cp in first + passed:
            cp.wait_send()
        mine.wait()

    return _comm_call(
        body, name=name, in_specs=[pl.BlockSpec(memory_space=pltpu.VMEM)], out_specs=pl.BlockSpec(memory_space=pltpu.VMEM),
        out_shape=_sds((N_DEV, R, C), v.dtype),
        scratch_shapes=[pltpu.SemaphoreType.DMA((7,)), pltpu.SemaphoreType.DMA((7,)), pltpu.SemaphoreType.DMA],
    )(v)


def _pair_sum(core, grad, got, name):
    _, _, R, C = grad.shape
    tr = _tile(R, 512)

    def body(core_ref, mine_ref, got_ref, out_ref):
        out_ref[...] = (mine_ref[...].astype(F32) + got_ref[...].astype(F32)).astype(BF16)

    blk = (None, tr, C)
    return _pallas(
        body, name=name, grid=(N_CHIP, R // tr), nprefetch=1,
        in_specs=[pl.BlockSpec((None, None, tr, C), lambda k, r, core: (k, core[0], r, 0)),
                  pl.BlockSpec(blk, lambda k, r, core: (k, r, 0))],
        out_specs=pl.BlockSpec(blk, lambda k, r, core: (k, r, 0)),
        out_shape=_sds((N_CHIP, R, C), BF16),
    )(core, grad, got)


def _adamw(w, g, m, v):
    m = ADAM_B1 * m + (1.0 - ADAM_B1) * g
    v = ADAM_B2 * v + (1.0 - ADAM_B2) * (g * g)
    m_hat = m / (1.0 - ADAM_B1 ** ADAM_STEP)
    v_hat = v / (1.0 - ADAM_B2 ** ADAM_STEP)
    delta = -ADAM_LR * (m_hat / (jnp.sqrt(v_hat) + ADAM_EPS) + ADAM_WD * w)
    return delta, m, v


def _final_adamw(sums, w, m, v, l, into, name):
    L, R, C = w.shape
    tr = _tile(R, 256)

    def body(s_ref, w_ref, m_ref, v_ref, *rest):
        g_out, d_out, m_out, v_out = rest[-4:]
        g = s_ref[0].astype(F32)
        for k in range(1, N_CHIP):
            g = g + s_ref[k].astype(F32)
        d, mn, vn = _adamw(w_ref[...], g, m_ref[...], v_ref[...])
        g_out[...] = g
        d_out[...] = d
        m_out[...] = mn
        v_out[...] = vn

    loc = pl.BlockSpec((None, tr, C), lambda r: (l, r, 0))
    in_specs = [pl.BlockSpec((N_CHIP, tr, C), lambda r: (0, r, 0)), loc, loc, loc]
    args = [sums, w, m, v]
    aliases = None
    if into is not None:
        in_specs += [ANY] * 4
        args += list(into)
        aliases = {4 + i: i for i in range(4)}
    return _pallas(body, name=name, grid=(R // tr,), in_specs=in_specs, out_specs=[loc] * 4,
                   out_shape=[_sds((L, R, C), F32)] * 4, aliases=aliases)(*args)


def _small_adamw(dev, allg, w, m, v, width, name):
    R = w.shape[0]

    def body(dev_ref, a_ref, w_ref, m_ref, v_ref, g_out, d_out, m_out, v_out):
        g = a_ref[0]
        for k in range(1, N_DEV):
            g = g + a_ref[k]
        d, mn, vn = _adamw(w_ref[...], g, m_ref[...], v_ref[...])
        g_out[...] = g
        d_out[...] = d
        m_out[...] = mn
        v_out[...] = vn

    full = allg.shape[2] == width
    loc = pl.BlockSpec((R, width), lambda i, dev: (0, 0))
    return _pallas(
        body, name=name, grid=(1,), nprefetch=1,
        in_specs=[pl.BlockSpec((N_DEV, R, width), lambda i, dev: (0, 0, 0 if full else dev[0])), loc, loc, loc],
        out_specs=[loc] * 4, out_shape=[_sds((R, width), F32)] * 4,
    )(dev, allg, w, m, v)


BIG = ("ffn1_w_gu", "ffn1_w_down", "w_in", "w_ret_o", "w_sc_o", "w_cf_o", "w_o", "ffn2_w_gu", "ffn2_w_down")
FFN1 = ("ffn1_w_gu", "ffn1_w_down")
MIX_OUT = ("w_ret_o", "w_sc_o", "w_cf_o", "w_o")
ROW_BLOCKS = ("ffn1_w_down", "ffn2_w_down") + MIX_OUT


def kernel(x, positions, norm_g, ffn1_w_gu, ffn1_w_down, w_in, w_ret_o, sc_conv_w, w_sc_o, cf_dw_w, cf_dw_b, cf_ln_g, cf_ln_b, w_cf_o, w_o, ffn2_w_gu, ffn2_w_down, loss_target, m_norm_g, m_ffn1_w_gu, m_ffn1_w_down, m_w_in, m_w_ret_o, m_sc_conv_w, m_w_sc_o, m_cf_dw_w, m_cf_dw_b, m_cf_ln_g, m_cf_ln_b, m_w_cf_o, m_w_o, m_ffn2_w_gu, m_ffn2_w_down, v_norm_g, v_ffn1_w_gu, v_ffn1_w_down, v_w_in, v_w_ret_o, v_sc_conv_w, v_w_sc_o, v_cf_dw_w, v_cf_dw_b, v_cf_ln_g, v_cf_ln_b, v_w_cf_o, v_w_o, v_ffn2_w_gu, v_ffn2_w_down):
    B, S, D = x.shape
    T = B * S
    L = norm_g.shape[0]
    DL = norm_g.shape[2]
    dk = D // 8
    mx, my, mc = _mesh_pos()
    dev_idx = jnp.reshape(4 * mx + 2 * my + mc, (1,)).astype(jnp.int32)
    core_idx = jnp.reshape(mc, (1,)).astype(jnp.int32)

    w32 = dict(ffn1_w_gu=ffn1_w_gu, ffn1_w_down=ffn1_w_down, w_in=w_in, w_ret_o=w_ret_o, w_sc_o=w_sc_o, w_cf_o=w_cf_o,
               w_o=w_o, ffn2_w_gu=ffn2_w_gu, ffn2_w_down=ffn2_w_down)
    m32 = dict(ffn1_w_gu=m_ffn1_w_gu, ffn1_w_down=m_ffn1_w_down, w_in=m_w_in, w_ret_o=m_w_ret_o, w_sc_o=m_w_sc_o,
               w_cf_o=m_w_cf_o, w_o=m_w_o, ffn2_w_gu=m_ffn2_w_gu, ffn2_w_down=m_ffn2_w_down)
    v32 = dict(ffn1_w_gu=v_ffn1_w_gu, ffn1_w_down=v_ffn1_w_down, w_in=v_w_in, w_ret_o=v_w_ret_o, w_sc_o=v_w_sc_o,
               w_cf_o=v_w_cf_o, w_o=v_w_o, ffn2_w_gu=v_ffn2_w_gu, ffn2_w_down=v_ffn2_w_down)

    W = [dict() for _ in range(L)]

    def gather(l, names):
        return _allgather_job([w32[n][l].astype(BF16) for n in names])

    def place(l, names, slabs):
        for n, a in zip(names, slabs):
            W[l][n] = a.reshape(a.shape[0] * a.shape[1], a.shape[2]) if n in ROW_BLOCKS else a

    def carried(fn, *args, jobs, **kw):
        out = fn(*args, jobs=jobs, **kw)
        return out if jobs else (out, [])

    n_sh = 6 + SC_KERNEL + CF_KERNEL
    small_sh = jnp.concatenate([norm_g, sc_conv_w, cf_dw_w], axis=1).reshape(L * n_sh, DL)
    sh_all = _allgather_small(small_sh, "allgather_small_params")
    sh_full = jnp.transpose(sh_all, (1, 0, 2)).reshape(L, n_sh, D)
    norm_full = sh_full[:, :6]
    scw_full = sh_full[:, 6:6 + SC_KERNEL]
    cfw_full = sh_full[:, 6 + SC_KERNEL:]
    place(0, FFN1, _run_jobs([gather(0, FFN1)], "allgather_first")[0])

    half = dk // 2
    inv_freq = ROPE_BASE ** (-jnp.arange(half, dtype=F32) / half)
    invf = jnp.concatenate([inv_freq, inv_freq])[None, :]
    sgn = jnp.concatenate([-jnp.ones((half,), F32), jnp.ones((half,), F32)])[None, :]
    cosf, sinf = _rope_tables(positions.reshape(T, 1), invf, sgn, "rope_tables")
    lgs = jnp.log(1.0 - 2.0 ** (-5.0 - jnp.arange(HEADS, dtype=F32)))

    def vec(a):
        return a.reshape(1, D)

    xc = x.reshape(T, D)
    saved = []
    for l in range(L):
        g = norm_full[l]
        sv = {}
        sv["x0"] = xc
        (xc, sv["h1"], sv["gate1"], sv["up1"], sv["f1"]), jr = _ffn_fwd(
            xc, vec(g[0]), vec(g[1]), W[l]["ffn1_w_gu"], W[l]["ffn1_w_down"], f"ffn1_fwd_{l}", jobs=[gather(l, ("w_in",))])
        place(l, ("w_in",), jr[0])
        sv["x1"] = xc
        (proj, sv["h2"]), jr = _norm_proj(xc, vec(g[2]), W[l]["w_in"], f"proj_fwd_{l}", jobs=[gather(l, MIX_OUT + ("ffn2_w_gu",))])
        place(l, MIX_OUT + ("ffn2_w_gu",), jr[0])
        sv["proj"] = proj
        o, jr = _ret_fwd(proj, cosf, sinf, lgs, B, S, D, f"retention_fwd_{l}", jobs=[gather(l, ("ffn2_w_down",))])
        place(l, ("ffn2_w_down",), jr[0])
        sv["o"] = o
        nxt = l + 1 < L
        (ya_in, yb_in, yc_in, sv["cv"], sv["u1"]), jr = carried(
            _mixer_mid_fwd, proj, o, scw_full[l], cfw_full[l], vec(cf_dw_b[l]), vec(cf_ln_g[l]), vec(cf_ln_b[l]), S,
            f"mixer_mid_fwd_{l}", jobs=[gather(l + 1, ("ffn1_w_gu",))] if nxt else [])
        if nxt:
            place(l + 1, ("ffn1_w_gu",), jr[0])
        sv["ya_in"], sv["yb_in"], sv["yc_in"] = ya_in, yb_in, yc_in
        (xc, sv["ya"], sv["yb"], sv["yc"], sv["merged"], sv["m"]), jr = carried(
            _mixer_merge_fwd, xc, proj, ya_in, yb_in, yc_in, vec(g[3]), W[l]["w_ret_o"], W[l]["w_sc_o"], W[l]["w_cf_o"], W[l]["w_o"],
            f"mixer_merge_fwd_{l}", jobs=[gather(l + 1, ("ffn1_w_down",))] if nxt else [])
        if nxt:
            place(l + 1, ("ffn1_w_down",), jr[0])
        sv["x2"] = xc
        xc, sv["h3"], sv["gate2"], sv["up2"], sv["f2"] = _ffn_fwd(xc, vec(g[4]), vec(g[5]), W[l]["ffn2_w_gu"], W[l]["ffn2_w_down"], f"ffn2_fwd_{l}")
        saved.append(sv)

    dx, loss_part = _loss_grad(xc, loss_target.reshape(T, D), "loss")
    loss = lax.psum(loss_part[0, 0], ("x", "y", "c"))

    to_sibling, to_chips, reduced = [], [], {}

    def grad_ready(n, l, g):
        if g.ndim == 2:
            g = g.reshape(N_DEV, g.shape[0] // N_DEV, g.shape[1])
        to_sibling.append((n, l, g.reshape(N_CHIP, 2, g.shape[1], g.shape[2])))

    def take_jobs():
        a, b = list(to_sibling), list(to_chips)
        to_sibling.clear()
        to_chips.clear()
        jobs = ([_to_chips_job([p for _, _, p in b])] if b else []) + ([_to_sibling_job([g for _, _, g in a])] if a else [])
        return jobs, (a, b)

    def settle(taken, jres):
        a, b = taken
        if b:
            for (n, l, _), got in zip(b, jres[0]):
                reduced[(n, l)] = got
        if a:
            for (n, l, g), got in zip(a, jres[-1]):
                to_chips.append((n, l, _pair_sum(core_idx, g, got, f"pair_sum_{n}_{l}")))

    def carrier(fn, *args, **kw):
        jobs, taken = take_jobs()
        out, jres = carried(fn, *args, jobs=jobs, **kw)
        settle(taken, jres)
        return out

    small_rows = [None] * L
    F = W[0]["ffn1_w_down"].shape[0]
    cb_gu = W[0]["ffn1_w_gu"].shape[2]
    cb_in = W[0]["w_in"].shape[2]
    tf = _ffn_tf(cb_gu)
    gu_order = (lambda jn: (jn % 2) * (F // tf) + jn // 2, tf)

    def ffn_grads(tag, wgu_name, wd_name, dxo, x_in, h, gate, up, f, g_pre, g_post, l):
        dxi, df, dgu, act, dgpre, dgpost = carrier(_ffn_bwd, dxo, x_in, f, gate, up, g_pre, g_post, W[l][wgu_name], W[l][wd_name],
                                                   f"{tag}_bwd_{l}")
        grad_ready(wgu_name, l, carrier(_mm_tn, h, dgu, f"{tag}_dwgu_{l}", slab_cols=cb_gu, tile_order=gu_order))
        grad_ready(wd_name, l, carrier(_mm_tn, act, df, f"{tag}_dwd_{l}"))
        return dxi, dgpre, dgpost

    for l in reversed(range(L)):
        sv = saved[l]
        g = norm_full[l]
        dx, dg4, dg5 = ffn_grads("ffn2", "ffn2_w_gu", "ffn2_w_down", dx, sv["x2"], sv["h3"], sv["gate2"], sv["up2"], sv["f2"],
                                 vec(g[4]), vec(g[5]), l)
        (dm, dya, dyb, dyc, d2, dgl, do, dcv, du1, dg3, dlng, dlnb, dcfb) = carrier(
            _mixer_bwd_a, dx, sv["m"], vec(g[3]), sv["ya"], sv["yb"], sv["yc"], sv["proj"], sv["o"], sv["cv"], sv["u1"],
            vec(cf_ln_g[l]), vec(cf_ln_b[l]), W[l]["w_ret_o"], W[l]["w_sc_o"], W[l]["w_cf_o"], W[l]["w_o"], f"mixer_bwd_a_{l}")
        grad_ready("w_o", l, _mm_tn(sv["merged"], dm, f"dw_o_{l}"))
        grad_ready("w_ret_o", l, _mm_tn(sv["ya_in"], dya, f"dw_ret_o_{l}"))
        grad_ready("w_sc_o", l, _mm_tn(sv["yb_in"], dyb, f"dw_sc_o_{l}"))
        grad_ready("w_cf_o", l, _mm_tn(sv["yc_in"], dyc, f"dw_cf_o_{l}"))
        d4, dscw, dcfw = carrier(_mixer_bwd_b, dcv, du1, sv["proj"], scw_full[l], cfw_full[l], S, f"mixer_bwd_b_{l}")
        dq, dkk, dvv = carrier(_ret_bwd, do, sv["proj"], cosf, sinf, lgs, B, S, D, f"retention_bwd_{l}")
        dproj = jnp.concatenate([dq, dkk, dvv, d2, d4, dgl], axis=1)
        grad_ready("w_in", l, carrier(_mm_tn, sv["h2"], dproj, f"dw_in_{l}", slab_cols=cb_in, tn_pref=cb_in))
        dx, dg2 = carrier(_proj_bwd, dproj, W[l]["w_in"], sv["x1"], vec(g[2]), dx, f"proj_bwd_{l}")
        dx, dg0, dg1 = ffn_grads("ffn1", "ffn1_w_gu", "ffn1_w_down", dx, sv["x0"], sv["h1"], sv["gate1"], sv["up1"], sv["f1"],
                                 vec(g[0]), vec(g[1]), l)
        small_rows[l] = (jnp.concatenate([dg0, dg1, dg2, dg3, dg4, dg5, dscw, dcfw], axis=0),
                         jnp.concatenate([dcfb, dlng, dlnb], axis=0))
    grad_x = dx.reshape(B, S, D)
    flush = 0
    while to_sibling or to_chips:
        jobs, taken = take_jobs()
        settle(taken, _run_jobs(jobs, f"grads_flush_{flush}"))
        flush += 1

    big_out = {}
    for n in BIG:
        res = None
        for l in reversed(range(L)):
            res = _final_adamw(reduced[(n, l)], w32[n], m32[n], v32[n], l, res, f"adamw_{n}_{l}")
        big_out[n] = res

    sh_part = jnp.concatenate([small_rows[l][0] for l in range(L)], axis=0)
    rep_part = jnp.concatenate([small_rows[l][1] for l in range(L)] + [jnp.zeros((8 - 3 * L % 8, D), F32)] * (1 if 3 * L % 8 else 0), axis=0)
    sh_g = _allgather_small(sh_part, "allgather_small_grads")
    rep_g = _allgather_small(rep_part, "allgather_replicated_grads")

    def pack_sh(a, b, c):
        return jnp.concatenate([a, b, c], axis=1).reshape(L * n_sh, DL)

    def pack_rep(a, b, c):
        rows = jnp.stack([a, b, c], axis=1).reshape(3 * L, D)
        return jnp.concatenate([rows, jnp.ones((rep_part.shape[0] - 3 * L, D), F32)], axis=0)

    sh_res = _small_adamw(dev_idx, sh_g, pack_sh(norm_g, sc_conv_w, cf_dw_w), pack_sh(m_norm_g, m_sc_conv_w, m_cf_dw_w),
                          pack_sh(v_norm_g, v_sc_conv_w, v_cf_dw_w), DL, "adamw_small_sharded")
    rep_res = _small_adamw(dev_idx, rep_g, pack_rep(cf_dw_b, cf_ln_g, cf_ln_b), pack_rep(m_cf_dw_b, m_cf_ln_g, m_cf_ln_b),
                           pack_rep(v_cf_dw_b, v_cf_ln_g, v_cf_ln_b), D, "adamw_small_replicated")

    def unpack_sh(a):
        a = a.reshape(L, n_sh, DL)
        return {"norm_g": a[:, :6], "sc_conv_w": a[:, 6:6 + SC_KERNEL], "cf_dw_w": a[:, 6 + SC_KERNEL:]}

    def unpack_rep(a):
        a = a[:3 * L].reshape(L, 3, D)
        return {"cf_dw_b": a[:, 0], "cf_ln_g": a[:, 1], "cf_ln_b": a[:, 2]}

    order = ("norm_g", "ffn1_w_gu", "ffn1_w_down", "w_in", "w_ret_o", "sc_conv_w", "w_sc_o", "cf_dw_w", "cf_dw_b", "cf_ln_g",
             "cf_ln_b", "w_cf_o", "w_o", "ffn2_w_gu", "ffn2_w_down")
    outs = []
    for kind in range(4):
        small = {**unpack_sh(sh_res[kind]), **unpack_rep(rep_res[kind])}
        outs += [big_out[n][kind] if n in big_out else small[n] for n in order]
    return (loss, grad_x, *outs)
```

```python
import functools

import jax
import jax.numpy as jnp
from jax import lax
from jax.experimental import pallas as pl
from jax.experimental.pallas import tpu as pltpu

F32 = jnp.float32
BF16 = jnp.bfloat16
MESH = pl.DeviceIdType.MESH
ANY = pl.BlockSpec(memory_space=pl.ANY)

N_DEV = 8
N_CHIP = 4
CHUNK = 64
HEADS = 4
ROPE_BASE = 10000.0
NORM_EPS = 1e-6
LN_EPS = 1e-5
SC_KERNEL = 3
CF_KERNEL = 31
HALO = 32
ADAM_LR = 0.001
ADAM_B1 = 0.9
ADAM_B2 = 0.999
ADAM_EPS = 1e-08
ADAM_WD = 0.01
ADAM_STEP = 10
VMEM_LIMIT_V7X = 56 * 1024 * 1024


class _Job:
    def __init__(self, ins, out_shapes, n_sems, n_local, start, finish):
        self.ins, self.out_shapes, self.n_sems, self.n_local = list(ins), list(out_shapes), n_sems, n_local
        self.start, self.finish = start, finish


def _pallas(body, *, name, grid, in_specs, out_specs, out_shape, scratch_shapes=(), aliases=None, nprefetch=0, jobs=()):
    extra = {}
    single = not isinstance(out_shape, (list, tuple))
    out_shape = [out_shape] if single else list(out_shape)
    out_specs = [out_specs] if single else list(out_specs)
    in_specs, scratch = list(in_specs), list(scratch_shapes)
    n_in, n_out, n_scr = len(in_specs), len(out_shape), len(scratch)
    for jb in jobs:
        in_specs += [ANY] * len(jb.ins)
        out_specs += [ANY] * len(jb.out_shapes)
        out_shape += jb.out_shapes
        scratch += [pltpu.SemaphoreType.DMA((jb.n_sems,)), pltpu.SemaphoreType.DMA((jb.n_sems,)),
                    pltpu.SemaphoreType.DMA((max(jb.n_local, 1),))]

    def with_jobs(*refs):
        pre, refs = refs[:nprefetch], refs[nprefetch:]
        ins, p = refs[:n_in], n_in
        jins = []
        for jb in jobs:
            jins.append(refs[p:p + len(jb.ins)])
            p += len(jb.ins)
        outs = refs[p:p + n_out]
        p += n_out
        jouts = []
        for jb in jobs:
            jouts.append(refs[p:p + len(jb.out_shapes)])
            p += len(jb.out_shapes)
        scr = refs[p:p + n_scr]
        p += n_scr
        pids = [pl.program_id(d) for d in range(len(grid))]
        first = functools.reduce(jnp.logical_and, [pid == 0 for pid in pids])
        last = functools.reduce(jnp.logical_and, [pid == g - 1 for pid, g in zip(pids, grid)])

        @pl.when(first)
        def _():
            for i, jb in enumerate(jobs):
                jb.start(jins[i], jouts[i], *refs[p + 3 * i:p + 3 * i + 3])

        body(*pre, *ins, *outs, *scr)

        @pl.when(last)
        def _():
            for i, jb in enumerate(jobs):
                jb.finish(jins[i], jouts[i], *refs[p + 3 * i:p + 3 * i + 3])

    params = pltpu.CompilerParams(dimension_semantics=("arbitrary",) * len(grid), vmem_limit_bytes=VMEM_LIMIT_V7X)
    spec = pltpu.PrefetchScalarGridSpec(num_scalar_prefetch=nprefetch, grid=grid, in_specs=in_specs,
                                        out_specs=out_specs, scratch_shapes=scratch)
    call = pl.pallas_call(with_jobs if jobs else body, name=name, grid_spec=spec, out_shape=out_shape, compiler_params=params,
                          input_output_aliases=aliases or {}, **extra)

    def run(*args):
        res = call(*args, *[a for jb in jobs for a in jb.ins])
        own = res[0] if single else list(res[:n_out])
        if not jobs:
            return own
        jres, p = [], n_out
        for jb in jobs:
            jres.append(list(res[p:p + len(jb.out_shapes)]))
            p += len(jb.out_shapes)
        return own, jres

    return run


def _comm_call(body, *, name, in_specs, out_specs, out_shape, scratch_shapes):
    extra = {}
    return pl.pallas_call(body, name=name, in_specs=in_specs, out_specs=out_specs, out_shape=out_shape,
                          scratch_shapes=scratch_shapes, **extra)


def _sds(shape, dtype):
    return jax.ShapeDtypeStruct(shape, dtype)


def _tile(n, pref):
    t = min(n, pref)
    assert n % t == 0, (n, pref)
    return t


def _sigmoid(x):
    return jax.nn.sigmoid(x)


def _rms_fwd(x, g):
    r = lax.rsqrt(jnp.mean(x * x, axis=-1, keepdims=True) + NORM_EPS)
    return x * r * g


def _rms_bwd(x, g, dy):
    r = lax.rsqrt(jnp.mean(x * x, axis=-1, keepdims=True) + NORM_EPS)
    xh = x * r
    dg = jnp.sum(dy * xh, axis=0, keepdims=True)
    dxh = dy * g
    dx = r * (dxh - xh * jnp.mean(dxh * xh, axis=-1, keepdims=True))
    return dx, dg


def _ln_stats(x):
    mu = jnp.mean(x, axis=-1, keepdims=True)
    xc = x - mu
    rstd = lax.rsqrt(jnp.mean(xc * xc, axis=-1, keepdims=True) + LN_EPS)
    return xc * rstd, rstd


def _ln_bwd(xh, rstd, dxh):
    return rstd * (dxh - jnp.mean(dxh, axis=-1, keepdims=True) - xh * jnp.mean(dxh * xh, axis=-1, keepdims=True))


def _dsilu(x, s):
    return s * (1.0 + x * (1.0 - s))


def _rot(x, cosf, sinf):
    return x * cosf + pltpu.roll(x, x.shape[-1] // 2, 1) * sinf


def _unrot(d, cosf, sinf):
    return d * cosf - pltpu.roll(d, d.shape[-1] // 2, 1) * sinf


def _dot_nn(a, b):
    return jnp.dot(a, b, preferred_element_type=F32)


def _dot_nt(a, b):
    return lax.dot_general(a, b, (((1,), (1,)), ((), ())), preferred_element_type=F32)


def _dot_tn(a, b):
    return lax.dot_general(a, b, (((0,), (0,)), ((), ())), preferred_element_type=F32)


def _ffn_tf(cb):
    return _tile(cb, 512)


def _ffn_fwd(x, g_pre, g_post, wgu, wd, name, jobs=()):
    T, D = x.shape
    F = wd.shape[0]
    cb = wgu.shape[2]
    tm, tf = _tile(T, 512), _ffn_tf(cb)
    nj, nb = F // tf, cb // tf

    def body(x_ref, gpre_ref, gpost_ref, wg_ref, wu_ref, wd_ref, xo_ref, ht_ref, gate_ref, up_ref, f_ref, acc_ref, h_ref):
        j = pl.program_id(1)

        @pl.when(j == 0)
        def _():
            h = _rms_fwd(x_ref[...], gpre_ref[...])
            h_ref[...] = h.astype(BF16)
            ht_ref[...] = h.T.astype(BF16)
            acc_ref[...] = jnp.zeros_like(acc_ref)

        h = h_ref[...]
        gate = _dot_nn(h, wg_ref[...])
        up = _dot_nn(h, wu_ref[...])
        gate_ref[...] = gate.astype(BF16)
        up_ref[...] = up.astype(BF16)
        act = (gate * _sigmoid(gate) * up).astype(BF16)
        acc_ref[...] += _dot_nn(act, wd_ref[...])

        @pl.when(j == nj - 1)
        def _():
            f = acc_ref[...]
            f_ref[...] = f
            xo_ref[...] = x_ref[...] + 0.5 * _rms_fwd(f, gpost_ref[...])

    row = pl.BlockSpec((tm, D), lambda i, j: (i, 0))
    vec = pl.BlockSpec((1, D), lambda i, j: (0, 0))
    col = pl.BlockSpec((tm, tf), lambda i, j: (i, j))
    return _pallas(
        body, name=name, grid=(T // tm, nj),
        in_specs=[row, vec, vec,
                  pl.BlockSpec((None, D, tf), lambda i, j: (j // nb, 0, j % nb)),
                  pl.BlockSpec((None, D, tf), lambda i, j: ((nj + j) // nb, 0, j % nb)),
                  pl.BlockSpec((tf, D), lambda i, j: (j, 0))],
        out_specs=[row, pl.BlockSpec((D, tm), lambda i, j: (0, i)), col, col, row],
        out_shape=[_sds((T, D), F32), _sds((D, T), BF16), _sds((T, F), BF16), _sds((T, F), BF16), _sds((T, D), F32)],
        scratch_shapes=[pltpu.VMEM((tm, D), F32), pltpu.VMEM((tm, D), BF16)], jobs=jobs,
    )(x, g_pre, g_post, wgu, wgu, wd)


def _ffn_bwd(dxo, x, f, gate, up, g_pre, g_post, wgu, wd, name, jobs=()):
    T, D = x.shape
    F = wd.shape[0]
    cb = wgu.shape[2]
    tm, tf = _tile(T, 512), _ffn_tf(cb)
    nj, nb = F // tf, cb // tf

    def body(dxo_ref, x_ref, f_ref, gate_ref, up_ref, gpre_ref, gpost_ref, wg_ref, wu_ref, wd_ref,
             dx_ref, df_ref, dgu_ref, act_ref, dgpre_ref, dgpost_ref, acc_ref):
        i, j = pl.program_id(0), pl.program_id(1)

        @pl.when((i == 0) & (j == 0))
        def _():
            dgpre_ref[...] = jnp.zeros_like(dgpre_ref)
            dgpost_ref[...] = jnp.zeros_like(dgpost_ref)

        @pl.when(j == 0)
        def _():
            df, dgp = _rms_bwd(f_ref[...], gpost_ref[...], 0.5 * dxo_ref[...])
            df_ref[...] = df.astype(BF16)
            dgpost_ref[...] += dgp
            acc_ref[...] = jnp.zeros_like(acc_ref)

        dact = _dot_nt(df_ref[...], wd_ref[...])
        g = gate_ref[...].astype(F32)
        u = up_ref[...].astype(F32)
        s = _sigmoid(g)
        silu = g * s
        dgate = (dact * u * _dsilu(g, s)).astype(BF16)
        dup = (dact * silu).astype(BF16)
        act_ref[...] = (silu * u).astype(BF16)
        dgu_ref[:, :tf] = dgate
        dgu_ref[:, tf:] = dup
        acc_ref[...] += _dot_nt(dgate, wg_ref[...]) + _dot_nt(dup, wu_ref[...])

        @pl.when(j == nj - 1)
        def _():
            dxin, dgp = _rms_bwd(x_ref[...], gpre_ref[...], acc_ref[...])
            dx_ref[...] = dxo_ref[...] + dxin
            dgpre_ref[...] += dgp

    row = pl.BlockSpec((tm, D), lambda i, j: (i, 0))
    vec = pl.BlockSpec((1, D), lambda i, j: (0, 0))
    col = pl.BlockSpec((tm, tf), lambda i, j: (i, j))
    return _pallas(
        body, name=name, grid=(T // tm, nj),
        in_specs=[row, row, row, col, col, vec, vec,
                  pl.BlockSpec((None, D, tf), lambda i, j: (j // nb, 0, j % nb)),
                  pl.BlockSpec((None, D, tf), lambda i, j: ((nj + j) // nb, 0, j % nb)),
                  pl.BlockSpec((tf, D), lambda i, j: (j, 0))],
        out_specs=[row, row, pl.BlockSpec((tm, 2 * tf), lambda i, j: (i, j)), col, vec, vec],
        out_shape=[_sds((T, D), F32), _sds((T, D), BF16), _sds((T, 2 * F), BF16), _sds((T, F), BF16),
                   _sds((1, D), F32), _sds((1, D), F32)],
        scratch_shapes=[pltpu.VMEM((tm, D), F32)], jobs=jobs,
    )(dxo, x, f, gate, up, g_pre, g_post, wgu, wgu, wd)


def _mm_tn(a, b, name, slab_cols=None, tile_order=None, tn_pref=1024, a_is_transposed=False, jobs=()):
    T, N = b.shape
    K = a.shape[0] if a_is_transposed else a.shape[1]
    tt = _tile(T, 512)
    tko = _tile(K, 1024)
    tn = tile_order[1] if tile_order else _tile(slab_cols or N, tn_pref)
    nat = tile_order[0] if tile_order else (lambda jn: jn)
    nt = T // tt

    def body(a_ref, b_ref, o_ref, acc_ref):
        t = pl.program_id(2)

        @pl.when(t == 0)
        def _():
            acc_ref[...] = jnp.zeros_like(acc_ref)

        acc_ref[...] += (_dot_nn if a_is_transposed else _dot_tn)(a_ref[...], b_ref[...])

        @pl.when(t == nt - 1)
        def _():
            o_ref[...] = acc_ref[...].astype(o_ref.dtype)

    if slab_cols is None:
        shape = (K, N)
        ospec = pl.BlockSpec((tko, tn), lambda k, jn, t: (k, nat(jn)))
    else:
        nb = slab_cols // tn
        shape = (N // slab_cols, K, slab_cols)
        ospec = pl.BlockSpec((None, tko, tn), lambda k, jn, t: (nat(jn) // nb, k, nat(jn) % nb))
    aspec = pl.BlockSpec((tko, tt), lambda k, jn, t: (k, t)) if a_is_transposed else pl.BlockSpec((tt, tko), lambda k, jn, t: (t, k))
    return _pallas(body, name=name, grid=(K // tko, N // tn, nt),
                   in_specs=[aspec, pl.BlockSpec((tt, tn), lambda k, jn, t: (t, jn))],
                   out_specs=ospec, out_shape=_sds(shape, BF16), scratch_shapes=[pltpu.VMEM((tko, tn), F32)], jobs=jobs)(a, b)


def _norm_proj(x, g, w, name, jobs=()):
    T, D = x.shape
    nd, cb = w.shape[0], w.shape[2]
    tm = _tile(T, 512)

    def body(x_ref, g_ref, w_ref, p_ref, ht_ref, h_ref):
        @pl.when(pl.program_id(1) == 0)
        def _():
            h = _rms_fwd(x_ref[...], g_ref[...])
            h_ref[...] = h.astype(BF16)
            ht_ref[...] = h.T.astype(BF16)

        p_ref[...] = _dot_nn(h_ref[...], w_ref[...])

    row = pl.BlockSpec((tm, D), lambda i, j: (i, 0))
    return _pallas(
        body, name=name, grid=(T // tm, nd),
        in_specs=[row, pl.BlockSpec((1, D), lambda i, j: (0, 0)),
                  pl.BlockSpec((None, D, cb), lambda i, j: (j, 0, 0))],
        out_specs=[pl.BlockSpec((tm, cb), lambda i, j: (i, j)), pl.BlockSpec((D, tm), lambda i, j: (0, i))],
        out_shape=[_sds((T, nd * cb), F32), _sds((D, T), BF16)],
        scratch_shapes=[pltpu.VMEM((tm, D), BF16)], jobs=jobs,
    )(x, g, w)


def _proj_bwd(dproj, w, x, g, dxo, name, jobs=()):
    T, D = x.shape
    nd, cb = w.shape[0], w.shape[2]
    tm = _tile(T, 512)

    def body(dp_ref, w_ref, x_ref, g_ref, dxo_ref, dx_ref, dg_ref, acc_ref):
        i, j = pl.program_id(0), pl.program_id(1)

        @pl.when((i == 0) & (j == 0))
        def _():
            dg_ref[...] = jnp.zeros_like(dg_ref)

        @pl.when(j == 0)
        def _():
            acc_ref[...] = jnp.zeros_like(acc_ref)

        acc_ref[...] += _dot_nt(dp_ref[...], w_ref[...])

        @pl.when(j == nd - 1)
        def _():
            dxin, dgp = _rms_bwd(x_ref[...], g_ref[...], acc_ref[...])
            dx_ref[...] = dxo_ref[...] + dxin
            dg_ref[...] += dgp

    row = pl.BlockSpec((tm, D), lambda i, j: (i, 0))
    vec = pl.BlockSpec((1, D), lambda i, j: (0, 0))
    return _pallas(
        body, name=name, grid=(T // tm, nd),
        in_specs=[pl.BlockSpec((tm, cb), lambda i, j: (i, j)),
                  pl.BlockSpec((None, D, cb), lambda i, j: (j, 0, 0)), row, vec, row],
        out_specs=[row, vec],
        out_shape=[_sds((T, D), F32), _sds((1, D), F32)],
        scratch_shapes=[pltpu.VMEM((tm, D), F32)], jobs=jobs,
    )(dproj, w, x, g, dxo)


def _rope_tables(pos, invf, sgn, name):
    T = pos.shape[0]
    dk = invf.shape[1]
    tm = _tile(T, 1024)

    def body(p_ref, f_ref, s_ref, c_out, s_out):
        ang = p_ref[...].astype(F32) * f_ref[...]
        c_out[...] = jnp.cos(ang)
        s_out[...] = jnp.sin(ang) * s_ref[...]

    vec = pl.BlockSpec((1, dk), lambda i: (0, 0))
    out = pl.BlockSpec((tm, dk), lambda i: (i, 0))
    return _pallas(body, name=name, grid=(T // tm,), in_specs=[pl.BlockSpec((tm, 1), lambda i: (i, 0)), vec, vec],
                   out_specs=[out, out], out_shape=[_sds((T, dk), F32), _sds((T, dk), F32)])(pos, invf, sgn)


def _decay(lg, r0, tq, n):
    r = r0 + lax.broadcasted_iota(jnp.int32, (tq, n), 0)
    c = lax.broadcasted_iota(jnp.int32, (tq, n), 1)
    rc, cc = r // CHUNK, c // CHUNK
    d = (r - c).astype(F32)
    e = jnp.where(rc == cc, jnp.abs(d), d)
    return jnp.where(cc > rc, 0.0, jnp.exp(lg * e))


def _ret_fwd(proj, cosf, sinf, lgs, B, S, D, name, jobs=()):
    T = B * S
    dk, dv = D // 8, D // 4
    tq = _tile(S, 256)
    scale = dk ** -0.5

    def body(lg_ref, q_ref, k_ref, v_ref, c_ref, s_ref, o_ref, kr_ref, vb_ref):
        lg = lg_ref[pl.program_id(1)]
        kr_ref[...] = (_rot(k_ref[...], c_ref[...], s_ref[...]) * scale).astype(BF16)
        vb_ref[...] = v_ref[...].astype(BF16)
        for qi in range(S // tq):
            rows, n = slice(qi * tq, (qi + 1) * tq), (qi + 1) * tq
            q = _rot(q_ref[rows, :], c_ref[rows, :], s_ref[rows, :]).astype(BF16)
            p = (_dot_nt(q, kr_ref[:n, :]) * _decay(lg, qi * tq, tq, n)).astype(BF16)
            o_ref[rows, :] = _dot_nn(p, vb_ref[:n, :])

    return _pallas(
        body, name=name, grid=(B, HEADS),
        in_specs=[pl.BlockSpec(memory_space=pltpu.SMEM),
                  pl.BlockSpec((S, dk), lambda b, h: (b, h)),
                  pl.BlockSpec((S, dk), lambda b, h: (b, HEADS + h)),
                  pl.BlockSpec((S, dv), lambda b, h: (b, HEADS + h)),
                  pl.BlockSpec((S, dk), lambda b, h: (b, 0)),
                  pl.BlockSpec((S, dk), lambda b, h: (b, 0))],
        out_specs=pl.BlockSpec((S, dv), lambda b, h: (b, h)),
        out_shape=_sds((T, D), F32),
        scratch_shapes=[pltpu.VMEM((S, dk), BF16), pltpu.VMEM((S, dv), BF16)], jobs=jobs,
    )(lgs, proj, proj, proj, cosf, sinf)


def _ret_bwd(do, proj, cosf, sinf, lgs, B, S, D, name, jobs=()):
    T = B * S
    dk, dv = D // 8, D // 4
    tq = _tile(S, 256)
    scale = dk ** -0.5

    def body(lg_ref, do_ref, q_ref, k_ref, v_ref, c_ref, s_ref, dq_ref, dk_ref, dv_ref, kr_ref, vb_ref, dka_ref, dva_ref):
        h = pl.program_id(1)
        lg = lg_ref[h]
        kr_ref[...] = (_rot(k_ref[...], c_ref[...], s_ref[...]) * scale).astype(BF16)
        vb_ref[...] = v_ref[...].astype(BF16)
        dka_ref[...] = jnp.zeros_like(dka_ref)
        dva_ref[...] = jnp.zeros_like(dva_ref)

        for qi in range(S // tq):
            rows, n = slice(qi * tq, (qi + 1) * tq), (qi + 1) * tq
            cq, sq = c_ref[rows, :], s_ref[rows, :]
            q = _rot(q_ref[rows, :], cq, sq).astype(BF16)
            dout = do_ref[rows, :].astype(BF16)
            w = _decay(lg, qi * tq, tq, n)
            p = (_dot_nt(q, kr_ref[:n, :]) * w).astype(BF16)
            ds = (_dot_nt(dout, vb_ref[:n, :]) * w).astype(BF16)
            dq_ref[rows, :] = _unrot(_dot_nn(ds, kr_ref[:n, :]), cq, sq).astype(BF16)
            dka_ref[:n, :] += _dot_tn(ds, q)
            dva_ref[:n, :] += _dot_tn(p, dout)

        dk_ref[...] = _unrot(dka_ref[...] * scale, c_ref[...], s_ref[...]).astype(BF16)
        dv_ref[...] = dva_ref[...].astype(BF16)

    return _pallas(
        body, name=name, grid=(B, HEADS),
        in_specs=[pl.BlockSpec(memory_space=pltpu.SMEM),
                  pl.BlockSpec((S, dv), lambda b, h: (b, h)),
                  pl.BlockSpec((S, dk), lambda b, h: (b, h)),
                  pl.BlockSpec((S, dk), lambda b, h: (b, HEADS + h)),
                  pl.BlockSpec((S, dv), lambda b, h: (b, HEADS + h)),
                  pl.BlockSpec((S, dk), lambda b, h: (b, 0)),
                  pl.BlockSpec((S, dk), lambda b, h: (b, 0))],
        out_specs=[pl.BlockSpec((S, dk), lambda b, h: (b, h)), pl.BlockSpec((S, dk), lambda b, h: (b, h)),
                   pl.BlockSpec((S, dv), lambda b, h: (b, h))],
        out_shape=[_sds((T, D // 2), BF16), _sds((T, D // 2), BF16), _sds((T, D), BF16)],
        scratch_shapes=[pltpu.VMEM((S, dk), BF16), pltpu.VMEM((S, dv), BF16), pltpu.VMEM((S, dk), F32), pltpu.VMEM((S, dv), F32)], jobs=jobs,
    )(lgs, do, proj, proj, proj, cosf, sinf)


COL_GRET, COL_SCB, COL_SCC, COL_SCX, COL_GLUA, COL_GLUB, COL_GL = 2, 3, 4, 5, 6, 7, 8


SUBLANES, LANES = 8, 128
CONV_ROWS = 32
SC_FWD = tuple(HALO - (SC_KERNEL - 1) + j for j in range(SC_KERNEL))
CF_FWD = tuple(HALO - (CF_KERNEL - 1) + j for j in range(CF_KERNEL))
SC_BWD = tuple(SC_KERNEL - 1 - j for j in range(SC_KERNEL))
CF_BWD = tuple(CF_KERNEL - 1 - j for j in range(CF_KERNEL))


def _build_shifted(src_ref, e_ref, shifts):
    n = e_ref.shape[1]
    for b in sorted({s % SUBLANES for s in shifts} - {0}):
        e_ref[b - 1] = src_ref[pl.ds(b, n), :]


def _shifted_rows(src_ref, e_ref, s, r0, rows, lanes):
    a, b = divmod(s, SUBLANES)
    at = pl.ds(r0 + SUBLANES * a, rows)
    return src_ref[at, lanes] if b == 0 else e_ref[b - 1, at, lanes]


def _conv_taps(src_ref, e_ref, w_ref, shifts, tm, D, emit):
    for lt in range(D // LANES):
        lanes = pl.ds(lt * LANES, LANES)
        wv = [jnp.broadcast_to(w_ref[j:j + 1, lanes], (CONV_ROWS, LANES)) for j in range(len(shifts))]

        def step(ci, carry):
            r0 = pl.multiple_of(ci * CONV_ROWS, CONV_ROWS)
            acc = _shifted_rows(src_ref, e_ref, shifts[0], r0, CONV_ROWS, lanes) * wv[0]
            for j in range(1, len(shifts)):
                acc = acc + _shifted_rows(src_ref, e_ref, shifts[j], r0, CONV_ROWS, lanes) * wv[j]
            emit(lt, pl.ds(r0, CONV_ROWS), acc)
            return carry

        lax.fori_loop(0, tm // CONV_ROWS, step, 0)


def _conv_weight_grad(d_ref, src_ref, e_ref, shifts, tm, D, dw_ref):
    for lt in range(D // LANES):
        lanes = pl.ds(lt * LANES, LANES)

        def step(ci, accs):
            r0 = pl.multiple_of(ci * SUBLANES, SUBLANES)
            d = d_ref[pl.ds(r0, SUBLANES), lanes]
            return tuple(acc + d * _shifted_rows(src_ref, e_ref, s, r0, SUBLANES, lanes) for acc, s in zip(accs, shifts))

        accs = lax.fori_loop(0, tm // SUBLANES, step, tuple(jnp.zeros((SUBLANES, LANES), F32) for _ in shifts))
        for j, acc in enumerate(accs):
            dw_ref[j:j + 1, lanes] += jnp.sum(acc, axis=0, keepdims=True)


def _mixer_mid_fwd(proj, o, sc_w, cf_w, cf_b, ln_g, ln_b, S, name, jobs=()):
    T, D = o.shape
    dv = D // HEADS
    tm = _tile(S, 256)
    per_seq = S // tm
    hb = tm // HALO

    def body(gret_ref, scb_ref, scc_ref, scx_ref, ga_ref, gb_ref, scc_h, scx_h, ga_h, gb_h, o_ref,
             scw_ref, cfw_ref, cfb_ref, lng_ref, lnb_ref, ya_ref, yb_ref, yc_ref, cv_ref, u1_ref, ext_ref, e_ref):
        first = (pl.program_id(0) % per_seq) == 0
        keep = jnp.where(first, 0.0, 1.0)
        gr = gret_ref[...]
        sg = gr * _sigmoid(gr)
        for hh in range(HEADS):
            cols = slice(hh * dv, (hh + 1) * dv)
            hn, _ = _ln_stats(o_ref[:, cols])
            ya_ref[:, cols] = (sg[:, cols] * hn).astype(BF16)
        ext_ref[:HALO, :] = scc_h[...] * scx_h[...] * keep
        ext_ref[HALO:, :] = scc_ref[...] * scx_ref[...]
        _build_shifted(ext_ref, e_ref, SC_FWD)

        def emit_cv(lt, rows, cv):
            lanes = pl.ds(lt * LANES, LANES)
            cv_ref[rows, lanes] = cv
            yb_ref[rows, lanes] = (scb_ref[rows, lanes] * cv).astype(BF16)

        _conv_taps(ext_ref, e_ref, scw_ref, SC_FWD, tm, D, emit_cv)
        ext_ref[:HALO, :] = ga_h[...] * _sigmoid(gb_h[...]) * keep
        ext_ref[HALO:, :] = ga_ref[...] * _sigmoid(gb_ref[...])
        _build_shifted(ext_ref, e_ref, CF_FWD)

        def emit_u1(lt, rows, acc):
            lanes = pl.ds(lt * LANES, LANES)
            u1_ref[rows, lanes] = acc + cfb_ref[:, lanes]

        _conv_taps(ext_ref, e_ref, cfw_ref, CF_FWD, tm, D, emit_u1)
        xh, _ = _ln_stats(u1_ref[...])
        u2 = xh * lng_ref[...] + lnb_ref[...]
        yc_ref[...] = (u2 * _sigmoid(u2)).astype(BF16)

    def colblk(c):
        return pl.BlockSpec((tm, D), lambda i: (i, c))

    def halo(c):
        return pl.BlockSpec((HALO, D), lambda i: (jnp.maximum(i * hb - 1, 0), c))

    row = pl.BlockSpec((tm, D), lambda i: (i, 0))
    vec = pl.BlockSpec((1, D), lambda i: (0, 0))
    return _pallas(
        body, name=name, grid=(T // tm,),
        in_specs=[colblk(COL_GRET), colblk(COL_SCB), colblk(COL_SCC), colblk(COL_SCX), colblk(COL_GLUA), colblk(COL_GLUB),
                  halo(COL_SCC), halo(COL_SCX), halo(COL_GLUA), halo(COL_GLUB), row,
                  pl.BlockSpec((SC_KERNEL, D), lambda i: (0, 0)), pl.BlockSpec((CF_KERNEL, D), lambda i: (0, 0)), vec, vec, vec],
        out_specs=[row, row, row, row, row],
        out_shape=[_sds((T, D), BF16)] * 3 + [_sds((T, D), F32)] * 2,
        scratch_shapes=[pltpu.VMEM((HALO + tm, D), F32), pltpu.VMEM((SUBLANES - 1, HALO + tm - SUBLANES, D), F32)], jobs=jobs,
    )(proj, proj, proj, proj, proj, proj, proj, proj, proj, proj, o, sc_w, cf_w, cf_b, ln_g, ln_b)


def _mixer_merge_fwd(x, proj, ya_in, yb_in, yc_in, g_post, w_ret, w_sc, w_cf, w_o, name, jobs=()):
    T, D = x.shape
    tm = _tile(T, 256)

    def body(x_ref, g0_ref, g1_ref, g2_ref, ya_in_ref, yb_in_ref, yc_in_ref, gp_ref, wr_ref, ws_ref, wc_ref, wo_ref,
             xo_ref, ya_ref, yb_ref, yc_ref, mg_ref, m_ref):
        ya = _dot_nn(ya_in_ref[...], wr_ref[...])
        yb = _dot_nn(yb_in_ref[...], ws_ref[...])
        yc = _dot_nn(yc_in_ref[...], wc_ref[...])
        ya_ref[...] = ya.astype(BF16)
        yb_ref[...] = yb.astype(BF16)
        yc_ref[...] = yc.astype(BF16)
        merged = (_sigmoid(g0_ref[...]) * ya + _sigmoid(g1_ref[...]) * yb + _sigmoid(g2_ref[...]) * yc).astype(BF16)
        mg_ref[...] = merged
        m = _dot_nn(merged, wo_ref[...])
        m_ref[...] = m
        xo_ref[...] = x_ref[...] + _rms_fwd(m, gp_ref[...])

    row = pl.BlockSpec((tm, D), lambda i: (i, 0))
    wsp = pl.BlockSpec((D, D), lambda i: (0, 0))

    def colblk(c):
        return pl.BlockSpec((tm, D), lambda i: (i, c))

    return _pallas(
        body, name=name, grid=(T // tm,),
        in_specs=[row, colblk(COL_GL), colblk(COL_GL + 1), colblk(COL_GL + 2), row, row, row,
                  pl.BlockSpec((1, D), lambda i: (0, 0)), wsp, wsp, wsp, wsp],
        out_specs=[row] * 6,
        out_shape=[_sds((T, D), F32)] + [_sds((T, D), BF16)] * 4 + [_sds((T, D), F32)], jobs=jobs,
    )(x, proj, proj, proj, ya_in, yb_in, yc_in, g_post, w_ret, w_sc, w_cf, w_o)


def _mixer_bwd_a(dxo, m, g_post, ya, yb, yc, proj, o, cv, u1, ln_g, ln_b, w_ret, w_sc, w_cf, w_o, name, jobs=()):
    T, D = m.shape
    dv = D // HEADS
    tm = _tile(T, 128)

    def body(dxo_ref, m_ref, gp_ref, ya_ref, yb_ref, yc_ref, g0_ref, g1_ref, g2_ref, gret_ref, scb_ref, o_ref, cv_ref, u1_ref,
             lng_ref, lnb_ref, wr_ref, ws_ref, wc_ref, wo_ref,
             dm_ref, dya_ref, dyb_ref, dyc_ref, d2_ref, dgl_ref, do_ref, dcv_ref, du1_ref, dgp_ref, dlng_ref, dlnb_ref, dcfb_ref):
        @pl.when(pl.program_id(0) == 0)
        def _():
            for r in (dgp_ref, dlng_ref, dlnb_ref, dcfb_ref):
                r[...] = jnp.zeros_like(r)

        dm, dgp = _rms_bwd(m_ref[...], gp_ref[...], dxo_ref[...])
        dgp_ref[...] += dgp
        dmb = dm.astype(BF16)
        dm_ref[...] = dmb
        dmerged = _dot_nt(dmb, wo_ref[...])
        dys = []
        for k, (g_ref, y_ref, dy_ref) in enumerate(((g0_ref, ya_ref, dya_ref), (g1_ref, yb_ref, dyb_ref), (g2_ref, yc_ref, dyc_ref))):
            sg = _sigmoid(g_ref[...])
            dgl_ref[:, k * D:(k + 1) * D] = (dmerged * y_ref[...].astype(F32) * sg * (1.0 - sg)).astype(BF16)
            dy = (dmerged * sg).astype(BF16)
            dy_ref[...] = dy
            dys.append(dy)
        dya_in = _dot_nt(dys[0], wr_ref[...])
        gr = gret_ref[...]
        sr = _sigmoid(gr)
        for hh in range(HEADS):
            cols = slice(hh * dv, (hh + 1) * dv)
            hn, rstd = _ln_stats(o_ref[:, cols])
            d2_ref[:, cols] = (dya_in[:, cols] * hn * _dsilu(gr[:, cols], sr[:, cols])).astype(BF16)
            do_ref[:, cols] = _ln_bwd(hn, rstd, dya_in[:, cols] * gr[:, cols] * sr[:, cols])
        dyb_in = _dot_nt(dys[1], ws_ref[...])
        d2_ref[:, D:] = (dyb_in * cv_ref[...]).astype(BF16)
        dcv_ref[...] = dyb_in * scb_ref[...]
        dyc_in = _dot_nt(dys[2], wc_ref[...])
        xh, rstd = _ln_stats(u1_ref[...])
        u2 = xh * lng_ref[...] + lnb_ref[...]
        du2 = dyc_in * _dsilu(u2, _sigmoid(u2))
        dlng_ref[...] += jnp.sum(du2 * xh, axis=0, keepdims=True)
        dlnb_ref[...] += jnp.sum(du2, axis=0, keepdims=True)
        du1 = _ln_bwd(xh, rstd, du2 * lng_ref[...])
        du1_ref[...] = du1
        dcfb_ref[...] += jnp.sum(du1, axis=0, keepdims=True)

    row = pl.BlockSpec((tm, D), lambda i: (i, 0))
    vec = pl.BlockSpec((1, D), lambda i: (0, 0))
    wsp = pl.BlockSpec((D, D), lambda i: (0, 0))

    def colblk(c):
        return pl.BlockSpec((tm, D), lambda i: (i, c))

    return _pallas(
        body, name=name, grid=(T // tm,),
        in_specs=[row, row, vec, row, row, row, colblk(COL_GL), colblk(COL_GL + 1), colblk(COL_GL + 2),
                  colblk(COL_GRET), colblk(COL_SCB), row, row, row, vec, vec, wsp, wsp, wsp, wsp],
        out_specs=[row, row, row, row, pl.BlockSpec((tm, 2 * D), lambda i: (i, 0)), pl.BlockSpec((tm, 3 * D), lambda i: (i, 0)),
                   row, row, row, vec, vec, vec, vec],
        out_shape=[_sds((T, D), BF16)] * 4 + [_sds((T, 2 * D), BF16), _sds((T, 3 * D), BF16)] + [_sds((T, D), F32)] * 3
                  + [_sds((1, D), F32)] * 4, jobs=jobs,
    )(dxo, m, g_post, ya, yb, yc, proj, proj, proj, proj, proj, o, cv, u1, ln_g, ln_b, w_ret, w_sc, w_cf, w_o)


def _mixer_bwd_b(dcv, du1, proj, sc_w, cf_w, S, name, jobs=()):
    T, D = dcv.shape
    tm = _tile(S, 256)
    per_seq = S // tm
    hb = tm // HALO
    last_hb = T // HALO - 1

    def body(dcv_ref, du1_ref, dcv_n, du1_n, scc_ref, scx_ref, ga_ref, gb_ref, scc_h, scx_h, ga_h, gb_h, scw_ref, cfw_ref,
             d4_ref, dscw_ref, dcfw_ref, ext_ref, dext_ref, e_ref, sb_ref):
        i = pl.program_id(0)
        keep_prev = jnp.where((i % per_seq) == 0, 0.0, 1.0)
        keep_next = jnp.where((i % per_seq) == per_seq - 1, 0.0, 1.0)

        @pl.when(i == 0)
        def _():
            dscw_ref[...] = jnp.zeros_like(dscw_ref)
            dcfw_ref[...] = jnp.zeros_like(dcfw_ref)

        dext_ref[:tm, :] = dcv_ref[...]
        dext_ref[tm:, :] = dcv_n[...] * keep_next
        _build_shifted(dext_ref, e_ref, SC_BWD)

        def emit_dz(lt, rows, dz):
            lanes = pl.ds(lt * LANES, LANES)
            d4_ref[rows, pl.ds(lt * LANES, LANES)] = (dz * scx_ref[rows, lanes]).astype(BF16)
            d4_ref[rows, pl.ds(D + lt * LANES, LANES)] = (dz * scc_ref[rows, lanes]).astype(BF16)

        _conv_taps(dext_ref, e_ref, scw_ref, SC_BWD, tm, D, emit_dz)
        ext_ref[:HALO, :] = scc_h[...] * scx_h[...] * keep_prev
        ext_ref[HALO:, :] = scc_ref[...] * scx_ref[...]
        _build_shifted(ext_ref, e_ref, SC_FWD)
        _conv_weight_grad(dcv_ref, ext_ref, e_ref, SC_FWD, tm, D, dscw_ref)
        dext_ref[:tm, :] = du1_ref[...]
        dext_ref[tm:, :] = du1_n[...] * keep_next
        _build_shifted(dext_ref, e_ref, CF_BWD)
        sb_ref[...] = _sigmoid(gb_ref[...])

        def emit_du0(lt, rows, du0):
            lanes = pl.ds(lt * LANES, LANES)
            sb = sb_ref[rows, lanes]
            d4_ref[rows, pl.ds(2 * D + lt * LANES, LANES)] = (du0 * sb).astype(BF16)
            d4_ref[rows, pl.ds(3 * D + lt * LANES, LANES)] = (du0 * ga_ref[rows, lanes] * sb * (1.0 - sb)).astype(BF16)

        _conv_taps(dext_ref, e_ref, cfw_ref, CF_BWD, tm, D, emit_du0)
        ext_ref[:HALO, :] = ga_h[...] * _sigmoid(gb_h[...]) * keep_prev
        ext_ref[HALO:, :] = ga_ref[...] * sb_ref[...]
        _build_shifted(ext_ref, e_ref, CF_FWD)
        _conv_weight_grad(du1_ref, ext_ref, e_ref, CF_FWD, tm, D, dcfw_ref)

    row = pl.BlockSpec((tm, D), lambda i: (i, 0))
    nxt = pl.BlockSpec((HALO, D), lambda i: (jnp.minimum((i + 1) * hb, last_hb), 0))

    def colblk(c):
        return pl.BlockSpec((tm, D), lambda i: (i, c))

    def halo(c):
        return pl.BlockSpec((HALO, D), lambda i: (jnp.maximum(i * hb - 1, 0), c))

    return _pallas(
        body, name=name, grid=(T // tm,),
        in_specs=[row, row, nxt, nxt, colblk(COL_SCC), colblk(COL_SCX), colblk(COL_GLUA), colblk(COL_GLUB),
                  halo(COL_SCC), halo(COL_SCX), halo(COL_GLUA), halo(COL_GLUB),
                  pl.BlockSpec((SC_KERNEL, D), lambda i: (0, 0)), pl.BlockSpec((CF_KERNEL, D), lambda i: (0, 0))],
        out_specs=[pl.BlockSpec((tm, 4 * D), lambda i: (i, 0)), pl.BlockSpec((SC_KERNEL, D), lambda i: (0, 0)),
                   pl.BlockSpec((CF_KERNEL, D), lambda i: (0, 0))],
        out_shape=[_sds((T, 4 * D), BF16), _sds((SC_KERNEL, D), F32), _sds((CF_KERNEL, D), F32)],
        scratch_shapes=[pltpu.VMEM((HALO + tm, D), F32), pltpu.VMEM((tm + HALO, D), F32),
                        pltpu.VMEM((SUBLANES - 1, HALO + tm - SUBLANES, D), F32), pltpu.VMEM((tm, D), F32)], jobs=jobs,
    )(dcv, du1, dcv, du1, proj, proj, proj, proj, proj, proj, proj, proj, sc_w, cf_w)


def _loss_grad(y, tgt, name):
    T, D = y.shape
    tm = _tile(T, 512)

    def body(y_ref, t_ref, dy_ref, loss_ref):
        @pl.when(pl.program_id(0) == 0)
        def _():
            loss_ref[...] = jnp.zeros_like(loss_ref)

        e = y_ref[...] - t_ref[...]
        dy_ref[...] = e * (1.0 / D)
        loss_ref[...] += 0.5 * jnp.sum(jnp.sum(e * e, axis=-1, keepdims=True) * (1.0 / D), axis=0, keepdims=True)

    row = pl.BlockSpec((tm, D), lambda i: (i, 0))
    return _pallas(body, name=name, grid=(T // tm,), in_specs=[row, row],
                   out_specs=[row, pl.BlockSpec((1, 1), lambda i: (0, 0))],
                   out_shape=[_sds((T, D), F32), _sds((1, 1), F32)])(y, tgt)


def _mesh_pos():
    return lax.axis_index("x"), lax.axis_index("y"), lax.axis_index("c")


def _other_chips(x, y):
    return [(1 - x, y), (x, 1 - y), (1 - x, 1 - y)]


def _allgather_job(shards):
    nt = len(shards)

    def parts(ins, outs, send, recv):
        x, y, c = _mesh_pos()

        def slab(t, px, py, pc):
            return outs[t].at[4 * px + 2 * py + pc]

        def copy(t, k, block, to, src=None):
            return pltpu.make_async_remote_copy(
                src_ref=slab(t, *block) if src is None else src, dst_ref=slab(t, *block),
                send_sem=send.at[7 * t + k], recv_sem=recv.at[7 * t + k], device_id=to, device_id_type=MESH)

        return (x, y, c), (x, y, 1 - c), _other_chips(x, y), c, slab, copy

    def start(ins, outs, send, recv, loc):
        me, sibling, chips, c, slab, copy = parts(ins, outs, send, recv)
        for t in range(nt):
            pltpu.make_async_copy(ins[t], slab(t, *me), loc.at[t]).start()
            copy(t, 0, me, sibling, src=ins[t]).start()
            for j, chip in enumerate(chips):
                copy(t, 1 + j, me, (*chip, c), src=ins[t]).start()

    def finish(ins, outs, send, recv, loc):
        me, sibling, chips, c, slab, copy = parts(ins, outs, send, recv)
        for j, chip in enumerate(chips):
            for t in range(nt):
                copy(t, 1 + j, (*chip, c), me).wait_recv()
                copy(t, 4 + j, (*chip, c), sibling).start()
        for t in range(nt):
            copy(t, 0, sibling, me).wait_recv()
            for j, chip in enumerate(chips):
                copy(t, 4 + j, (*chip, 1 - c), me).wait_recv()
        for t in range(nt):
            copy(t, 0, me, sibling, src=ins[t]).wait_send()
            for j, chip in enumerate(chips):
                copy(t, 1 + j, me, (*chip, c), src=ins[t]).wait_send()
                copy(t, 4 + j, (*chip, c), sibling).wait_send()
            pltpu.make_async_copy(ins[t], slab(t, *me), loc.at[t]).wait()

    return _Job(shards, [_sds((N_DEV,) + s.shape, s.dtype) for s in shards], 7 * nt, nt, start, finish)


def _to_sibling_job(grads):
    nt = len(grads)

    def copies(ins, outs, send, recv):
        x, y, c = _mesh_pos()
        return [pltpu.make_async_remote_copy(src_ref=ins[t].at[:, 1 - c], dst_ref=outs[t], send_sem=send.at[t], recv_sem=recv.at[t],
                                             device_id=(x, y, 1 - c), device_id_type=MESH) for t in range(nt)]

    def start(ins, outs, send, recv, loc):
        for cp in copies(ins, outs, send, recv):
            cp.start()

    def finish(ins, outs, send, recv, loc):
        for cp in copies(ins, outs, send, recv):
            cp.wait()

    return _Job(grads, [_sds((N_CHIP,) + g.shape[2:], g.dtype) for g in grads], nt, 0, start, finish)


def _to_chips_job(pairs):
    nt = len(pairs)

    def copies(ins, outs, send, recv, loc):
        x, y, c = _mesh_pos()
        remote = [pltpu.make_async_remote_copy(src_ref=ins[t].at[2 * px + py], dst_ref=outs[t].at[k], send_sem=send.at[3 * t + k],
                                               recv_sem=recv.at[3 * t + k], device_id=(px, py, c), device_id_type=MESH)
                  for t in range(nt) for k, (px, py) in enumerate(_other_chips(x, y))]
        local = [pltpu.make_async_copy(ins[t].at[2 * x + y], outs[t].at[3], loc.at[t]) for t in range(nt)]
        return remote + local

    def start(ins, outs, send, recv, loc):
        for cp in copies(ins, outs, send, recv, loc):
            cp.start()

    def finish(ins, outs, send, recv, loc):
        for cp in copies(ins, outs, send, recv, loc):
            cp.wait()

    return _Job(pairs, [_sds(p.shape, p.dtype) for p in pairs], 3 * nt, nt, start, finish)


def _run_jobs(jobs, name):
    def body(o_ref):
        o_ref[...] = jnp.zeros_like(o_ref)

    _, jres = _pallas(body, name=name, grid=(1,), in_specs=[], out_specs=pl.BlockSpec((8, 128), lambda i: (0, 0)),
                      out_shape=_sds((8, 128), F32), jobs=jobs)()
    return jres


def _allgather_small(v, name):
    R, C = v.shape

    def body(x_ref, out_ref, send_sems, recv_sems, local_sem):
        x, y, c = _mesh_pos()
        me, sibling = (x, y, c), (x, y, 1 - c)
        chips = _other_chips(x, y)

        def slab(px, py, pc):
            return out_ref.at[4 * px + 2 * py + pc]

        def copy(k, block, to, src=None):
            return pltpu.make_async_remote_copy(
                src_ref=slab(*block) if src is None else src, dst_ref=slab(*block),
                send_sem=send_sems.at[k], recv_sem=recv_sems.at[k], device_id=to, device_id_type=MESH)

        mine = pltpu.make_async_copy(x_ref, slab(*me), local_sem)
        mine.start()
        first = [copy(0, me, sibling, src=x_ref)]
        first += [copy(1 + j, me, (*chip, c), src=x_ref) for j, chip in enumerate(chips)]
        for cp in first:
            cp.start()
        passed = [copy(4 + j, (*chip, c), sibling) for j, chip in enumerate(chips)]
        for j, chip in enumerate(chips):
            copy(1 + j, (*chip, c), me).wait_recv()
            passed[j].start()
        copy(0, sibling, me).wait_recv()
        for j, chip in enumerate(chips):
            copy(4 + j, (*chip, 1 - c), me).wait_recv()
        for cp in first + passed:
            cp.wait_send()
        mine.wait()

    return _comm_call(
        body, name=name, in_specs=[pl.BlockSpec(memory_space=pltpu.VMEM)], out_specs=pl.BlockSpec(memory_space=pltpu.VMEM),
        out_shape=_sds((N_DEV, R, C), v.dtype),
        scratch_shapes=[pltpu.SemaphoreType.DMA((7,)), pltpu.SemaphoreType.DMA((7,)), pltpu.SemaphoreType.DMA],
    )(v)


def _pair_sum(core, grad, got, name):
    _, _, R, C = grad.shape
    tr = _tile(R, 512)

    def body(core_ref, mine_ref, got_ref, out_ref):
        out_ref[...] = (mine_ref[...].astype(F32) + got_ref[...].astype(F32)).astype(BF16)

    blk = (None, tr, C)
    return _pallas(
        body, name=name, grid=(N_CHIP, R // tr), nprefetch=1,
        in_specs=[pl.BlockSpec((None, None, tr, C), lambda k, r, core: (k, core[0], r, 0)),
                  pl.BlockSpec(blk, lambda k, r, core: (k, r, 0))],
        out_specs=pl.BlockSpec(blk, lambda k, r, core: (k, r, 0)),
        out_shape=_sds((N_CHIP, R, C), BF16),
    )(core, grad, got)


def _adamw(w, g, m, v):
    m = ADAM_B1 * m + (1.0 - ADAM_B1) * g
    v = ADAM_B2 * v + (1.0 - ADAM_B2) * (g * g)
    m_hat = m / (1.0 - ADAM_B1 ** ADAM_STEP)
    v_hat = v / (1.0 - ADAM_B2 ** ADAM_STEP)
    delta = -ADAM_LR * (m_hat / (jnp.sqrt(v_hat) + ADAM_EPS) + ADAM_WD * w)
    return delta, m, v


def _final_adamw(sums, w, m, v, l, into, name):
    L, R, C = w.shape
    tr = _tile(R, 256)

    def body(s_ref, w_ref, m_ref, v_ref, *rest):
        g_out, d_out, m_out, v_out = rest[-4:]
        g = s_ref[0].astype(F32)
        for k in range(1, N_CHIP):
            g = g + s_ref[k].astype(F32)
        d, mn, vn = _adamw(w_ref[...], g, m_ref[...], v_ref[...])
        g_out[...] = g
        d_out[...] = d
        m_out[...] = mn
        v_out[...] = vn

    loc = pl.BlockSpec((None, tr, C), lambda r: (l, r, 0))
    in_specs = [pl.BlockSpec((N_CHIP, tr, C), lambda r: (0, r, 0)), loc, loc, loc]
    args = [sums, w, m, v]
    aliases = None
    if into is not None:
        in_specs += [ANY] * 4
        args += list(into)
        aliases = {4 + i: i for i in range(4)}
    return _pallas(body, name=name, grid=(R // tr,), in_specs=in_specs, out_specs=[loc] * 4,
                   out_shape=[_sds((L, R, C), F32)] * 4, aliases=aliases)(*args)


def _small_adamw(dev, allg, w, m, v, width, name):
    R = w.shape[0]

    def body(dev_ref, a_ref, w_ref, m_ref, v_ref, g_out, d_out, m_out, v_out):
        g = a_ref[0]
        for k in range(1, N_DEV):
            g = g + a_ref[k]
        d, mn, vn = _adamw(w_ref[...], g, m_ref[...], v_ref[...])
        g_out[...] = g
        d_out[...] = d
        m_out[...] = mn
        v_out[...] = vn

    full = allg.shape[2] == width
    loc = pl.BlockSpec((R, width), lambda i, dev: (0, 0))
    return _pallas(
        body, name=name, grid=(1,), nprefetch=1,
        in_specs=[pl.BlockSpec((N_DEV, R, width), lambda i, dev: (0, 0, 0 if full else dev[0])), loc, loc, loc],
        out_specs=[loc] * 4, out_shape=[_sds((R, width), F32)] * 4,
    )(dev, allg, w, m, v)


BIG = ("ffn1_w_gu", "ffn1_w_down", "w_in", "w_ret_o", "w_sc_o", "w_cf_o", "w_o", "ffn2_w_gu", "ffn2_w_down")
FFN1 = ("ffn1_w_gu", "ffn1_w_down")
MIX_OUT = ("w_ret_o", "w_sc_o", "w_cf_o", "w_o")
ROW_BLOCKS = ("ffn1_w_down", "ffn2_w_down") + MIX_OUT


def kernel(x, positions, norm_g, ffn1_w_gu, ffn1_w_down, w_in, w_ret_o, sc_conv_w, w_sc_o, cf_dw_w, cf_dw_b, cf_ln_g, cf_ln_b, w_cf_o, w_o, ffn2_w_gu, ffn2_w_down, loss_target, m_norm_g, m_ffn1_w_gu, m_ffn1_w_down, m_w_in, m_w_ret_o, m_sc_conv_w, m_w_sc_o, m_cf_dw_w, m_cf_dw_b, m_cf_ln_g, m_cf_ln_b, m_w_cf_o, m_w_o, m_ffn2_w_gu, m_ffn2_w_down, v_norm_g, v_ffn1_w_gu, v_ffn1_w_down, v_w_in, v_w_ret_o, v_sc_conv_w, v_w_sc_o, v_cf_dw_w, v_cf_dw_b, v_cf_ln_g, v_cf_ln_b, v_w_cf_o, v_w_o, v_ffn2_w_gu, v_ffn2_w_down):
    B, S, D = x.shape
    T = B * S
    L = norm_g.shape[0]
    DL = norm_g.shape[2]
    dk = D // 8
    mx, my, mc = _mesh_pos()
    dev_idx = jnp.reshape(4 * mx + 2 * my + mc, (1,)).astype(jnp.int32)
    core_idx = jnp.reshape(mc, (1,)).astype(jnp.int32)

    w32 = dict(ffn1_w_gu=ffn1_w_gu, ffn1_w_down=ffn1_w_down, w_in=w_in, w_ret_o=w_ret_o, w_sc_o=w_sc_o, w_cf_o=w_cf_o,
               w_o=w_o, ffn2_w_gu=ffn2_w_gu, ffn2_w_down=ffn2_w_down)
    m32 = dict(ffn1_w_gu=m_ffn1_w_gu, ffn1_w_down=m_ffn1_w_down, w_in=m_w_in, w_ret_o=m_w_ret_o, w_sc_o=m_w_sc_o,
               w_cf_o=m_w_cf_o, w_o=m_w_o, ffn2_w_gu=m_ffn2_w_gu, ffn2_w_down=m_ffn2_w_down)
    v32 = dict(ffn1_w_gu=v_ffn1_w_gu, ffn1_w_down=v_ffn1_w_down, w_in=v_w_in, w_ret_o=v_w_ret_o, w_sc_o=v_w_sc_o,
               w_cf_o=v_w_cf_o, w_o=v_w_o, ffn2_w_gu=v_ffn2_w_gu, ffn2_w_down=v_ffn2_w_down)

    W = [dict() for _ in range(L)]

    def gather(l, names):
        return _allgather_job([w32[n][l].astype(BF16) for n in names])

    def place(l, names, slabs):
        for n, a in zip(names, slabs):
            W[l][n] = a.reshape(a.shape[0] * a.shape[1], a.shape[2]) if n in ROW_BLOCKS else a

    def carried(fn, *args, jobs, **kw):
        out = fn(*args, jobs=jobs, **kw)
        return out if jobs else (out, [])

    n_sh = 6 + SC_KERNEL + CF_KERNEL
    small_sh = jnp.concatenate([norm_g, sc_conv_w, cf_dw_w], axis=1).reshape(L * n_sh, DL)
    sh_all = _allgather_small(small_sh, "allgather_small_params")
    sh_full = jnp.transpose(sh_all, (1, 0, 2)).reshape(L, n_sh, D)
    norm_full = sh_full[:, :6]
    scw_full = sh_full[:, 6:6 + SC_KERNEL]
    cfw_full = sh_full[:, 6 + SC_KERNEL:]
    place(0, FFN1, _run_jobs([gather(0, FFN1)], "allgather_first")[0])

    half = dk // 2
    inv_freq = ROPE_BASE ** (-jnp.arange(half, dtype=F32) / half)
    invf = jnp.concatenate([inv_freq, inv_freq])[None, :]
    sgn = jnp.concatenate([-jnp.ones((half,), F32), jnp.ones((half,), F32)])[None, :]
    cosf, sinf = _rope_tables(positions.reshape(T, 1), invf, sgn, "rope_tables")
    lgs = jnp.log(1.0 - 2.0 ** (-5.0 - jnp.arange(HEADS, dtype=F32)))

    def vec(a):
        return a.reshape(1, D)

    xc = x.reshape(T, D)
    saved = []
    for l in range(L):
        g = norm_full[l]
        sv = {}
        sv["x0"] = xc
        (xc, sv["h1"], sv["gate1"], sv["up1"], sv["f1"]), jr = _ffn_fwd(
            xc, vec(g[0]), vec(g[1]), W[l]["ffn1_w_gu"], W[l]["ffn1_w_down"], f"ffn1_fwd_{l}", jobs=[gather(l, ("w_in",))])
        place(l, ("w_in",), jr[0])
        sv["x1"] = xc
        (proj, sv["h2"]), jr = _norm_proj(xc, vec(g[2]), W[l]["w_in"], f"proj_fwd_{l}", jobs=[gather(l, MIX_OUT + ("ffn2_w_gu",))])
        place(l, MIX_OUT + ("ffn2_w_gu",), jr[0])
        sv["proj"] = proj
        o, jr = _ret_fwd(proj, cosf, sinf, lgs, B, S, D, f"retention_fwd_{l}", jobs=[gather(l, ("ffn2_w_down",))])
        place(l, ("ffn2_w_down",), jr[0])
        sv["o"] = o
        nxt = l + 1 < L
        (ya_in, yb_in, yc_in, sv["cv"], sv["u1"]), jr = carried(
            _mixer_mid_fwd, proj, o, scw_full[l], cfw_full[l], vec(cf_dw_b[l]), vec(cf_ln_g[l]), vec(cf_ln_b[l]), S,
            f"mixer_mid_fwd_{l}", jobs=[gather(l + 1, ("ffn1_w_gu",))] if nxt else [])
        if nxt:
            place(l + 1, ("ffn1_w_gu",), jr[0])
        sv["ya_in"], sv["yb_in"], sv["yc_in"] = ya_in, yb_in, yc_in
        (xc, sv["ya"], sv["yb"], sv["yc"], sv["merged"], sv["m"]), jr = carried(
            _mixer_merge_fwd, xc, proj, ya_in, yb_in, yc_in, vec(g[3]), W[l]["w_ret_o"], W[l]["w_sc_o"], W[l]["w_cf_o"], W[l]["w_o"],
            f"mixer_merge_fwd_{l}", jobs=[gather(l + 1, ("ffn1_w_down",))] if nxt else [])
        if nxt:
            place(l + 1, ("ffn1_w_down",), jr[0])
        sv["x2"] = xc
        xc, sv["h3"], sv["gate2"], sv["up2"], sv["f2"] = _ffn_fwd(xc, vec(g[4]), vec(g[5]), W[l]["ffn2_w_gu"], W[l]["ffn2_w_down"], f"ffn2_fwd_{l}")
        saved.append(sv)

    dx, loss_part = _loss_grad(xc, loss_target.reshape(T, D), "loss")
    loss = lax.psum(loss_part[0, 0], ("x", "y", "c"))

    to_sibling, to_chips, reduced = [], [], {}

    def grad_ready(n, l, g):
        if g.ndim == 2:
            g = g.reshape(N_DEV, g.shape[0] // N_DEV, g.shape[1])
        to_sibling.append((n, l, g.reshape(N_CHIP, 2, g.shape[1], g.shape[2])))

    def take_jobs():
        a, b = list(to_sibling), list(to_chips)
        to_sibling.clear()
        to_chips.clear()
        jobs = ([_to_chips_job([p for _, _, p in b])] if b else []) + ([_to_sibling_job([g for _, _, g in a])] if a else [])
        return jobs, (a, b)

    def settle(taken, jres):
        a, b = taken
        if b:
            for (n, l, _), got in zip(b, jres[0]):
                reduced[(n, l)] = got
        if a:
            for (n, l, g), got in zip(a, jres[-1]):
                to_chips.append((n, l, _pair_sum(core_idx, g, got, f"pair_sum_{n}_{l}")))

    def carrier(fn, *args, **kw):
        jobs, taken = take_jobs()
        out, jres = carried(fn, *args, jobs=jobs, **kw)
        settle(taken, jres)
        return out

    small_rows = [None] * L
    F = W[0]["ffn1_w_down"].shape[0]
    cb_gu = W[0]["ffn1_w_gu"].shape[2]
    cb_in = W[0]["w_in"].shape[2]
    tf = _ffn_tf(cb_gu)
    gu_order = (lambda jn: (jn % 2) * (F // tf) + jn // 2, tf)

    def ffn_grads(tag, wgu_name, wd_name, dxo, x_in, h, gate, up, f, g_pre, g_post, l):
        dxi, df, dgu, act, dgpre, dgpost = carrier(_ffn_bwd, dxo, x_in, f, gate, up, g_pre, g_post, W[l][wgu_name], W[l][wd_name],
                                                   f"{tag}_bwd_{l}")
        grad_ready(wgu_name, l, carrier(_mm_tn, h, dgu, f"{tag}_dwgu_{l}", slab_cols=cb_gu, tile_order=gu_order, a_is_transposed=True))
        grad_ready(wd_name, l, carrier(_mm_tn, act, df, f"{tag}_dwd_{l}"))
        return dxi, dgpre, dgpost

    for l in reversed(range(L)):
        sv = saved[l]
        g = norm_full[l]
        dx, dg4, dg5 = ffn_grads("ffn2", "ffn2_w_gu", "ffn2_w_down", dx, sv["x2"], sv["h3"], sv["gate2"], sv["up2"], sv["f2"],
                                 vec(g[4]), vec(g[5]), l)
        (dm, dya, dyb, dyc, d2, dgl, do, dcv, du1, dg3, dlng, dlnb, dcfb) = carrier(
            _mixer_bwd_a, dx, sv["m"], vec(g[3]), sv["ya"], sv["yb"], sv["yc"], sv["proj"], sv["o"], sv["cv"], sv["u1"],
            vec(cf_ln_g[l]), vec(cf_ln_b[l]), W[l]["w_ret_o"], W[l]["w_sc_o"], W[l]["w_cf_o"], W[l]["w_o"], f"mixer_bwd_a_{l}")
        grad_ready("w_o", l, _mm_tn(sv["merged"], dm, f"dw_o_{l}"))
        grad_ready("w_ret_o", l, _mm_tn(sv["ya_in"], dya, f"dw_ret_o_{l}"))
        grad_ready("w_sc_o", l, _mm_tn(sv["yb_in"], dyb, f"dw_sc_o_{l}"))
        grad_ready("w_cf_o", l, _mm_tn(sv["yc_in"], dyc, f"dw_cf_o_{l}"))
        d4, dscw, dcfw = carrier(_mixer_bwd_b, dcv, du1, sv["proj"], scw_full[l], cfw_full[l], S, f"mixer_bwd_b_{l}")
        dq, dkk, dvv = carrier(_ret_bwd, do, sv["proj"], cosf, sinf, lgs, B, S, D, f"retention_bwd_{l}")
        dproj = jnp.concatenate([dq, dkk, dvv, d2, d4, dgl], axis=1)
        grad_ready("w_in", l, carrier(_mm_tn, sv["h2"], dproj, f"dw_in_{l}", slab_cols=cb_in, tn_pref=cb_in, a_is_transposed=True))
        dx, dg2 = carrier(_proj_bwd, dproj, W[l]["w_in"], sv["x1"], vec(g[2]), dx, f"proj_bwd_{l}")
        dx, dg0, dg1 = ffn_grads("ffn1", "ffn1_w_gu", "ffn1_w_down", dx, sv["x0"], sv["h1"], sv["gate1"], sv["up1"], sv["f1"],
                                 vec(g[0]), vec(g[1]), l)
        small_rows[l] = (jnp.concatenate([dg0, dg1, dg2, dg3, dg4, dg5, dscw, dcfw], axis=0),
                         jnp.concatenate([dcfb, dlng, dlnb], axis=0))
    grad_x = dx.reshape(B, S, D)
    flush = 0
    while to_sibling or to_chips:
        jobs, taken = take_jobs()
        settle(taken, _run_jobs(jobs, f"grads_flush_{flush}"))
        flush += 1

    big_out = {}
    for n in BIG:
        res = None
        for l in reversed(range(L)):
            res = _final_adamw(reduced[(n, l)], w32[n], m32[n], v32[n], l, res, f"adamw_{n}_{l}")
        big_out[n] = res

    sh_part = jnp.concatenate([small_rows[l][0] for l in range(L)], axis=0)
    rep_part = jnp.concatenate([small_rows[l][1] for l in range(L)] + [jnp.zeros((8 - 3 * L % 8, D), F32)] * (1 if 3 * L % 8 else 0), axis=0)
    sh_g = _allgather_small(sh_part, "allgather_small_grads")
    rep_g = _allgather_small(rep_part, "allgather_replicated_grads")

    def pack_sh(a, b, c):
        return jnp.concatenate([a, b, c], axis=1).reshape(L * n_sh, DL)

    def pack_rep(a, b, c):
        rows = jnp.stack([a, b, c], axis=1).reshape(3 * L, D)
        return jnp.concatenate([rows, jnp.ones((rep_part.shape[0] - 3 * L, D), F32)], axis=0)

    sh_res = _small_adamw(dev_idx, sh_g, pack_sh(norm_g, sc_conv_w, cf_dw_w), pack_sh(m_norm_g, m_sc_conv_w, m_cf_dw_w),
                          pack_sh(v_norm_g, v_sc_conv_w, v_cf_dw_w), DL, "adamw_small_sharded")
    rep_res = _small_adamw(dev_idx, rep_g, pack_rep(cf_dw_b, cf_ln_g, cf_ln_b), pack_rep(m_cf_dw_b, m_cf_ln_g, m_cf_ln_b),
                           pack_rep(v_cf_dw_b, v_cf_ln_g, v_cf_ln_b), D, "adamw_small_replicated")

    def unpack_sh(a):
        a = a.reshape(L, n_sh, DL)
        return {"norm_g": a[:, :6], "sc_conv_w": a[:, 6:6 + SC_KERNEL], "cf_dw_w": a[:, 6 + SC_KERNEL:]}

    def unpack_rep(a):
        a = a[:3 * L].reshape(L, 3, D)
        return {"cf_dw_b": a[:, 0], "cf_ln_g": a[:, 1], "cf_ln_b": a[:, 2]}

    order = ("norm_g", "ffn1_w_gu", "ffn1_w_down", "w_in", "w_ret_o", "sc_conv_w", "w_sc_o", "cf_dw_w", "cf_dw_b", "cf_ln_g",
             "cf_ln_b", "w_cf_o", "w_o", "ffn2_w_gu", "ffn2_w_down")
    outs = []
    for kind in range(4):
        small = {**unpack_sh(sh_res[kind]), **unpack_rep(rep_res[kind])}
        outs += [big_out[n][kind] if n in big_out else small[n] for n in order]
    return (loss, grad_x, *outs)
```

```python
import functools

import jax
import jax.numpy as jnp
from jax import lax
from jax.experimental import pallas as pl
from jax.experimental.pallas import tpu as pltpu

F32 = jnp.float32
BF16 = jnp.bfloat16
MESH = pl.DeviceIdType.MESH
ANY = pl.BlockSpec(memory_space=pl.ANY)

N_DEV = 8
N_CHIP = 4
CHUNK = 64
HEADS = 4
ROPE_BASE = 10000.0
NORM_EPS = 1e-6
LN_EPS = 1e-5
SC_KERNEL = 3
CF_KERNEL = 31
HALO = 32
ADAM_LR = 0.001
ADAM_B1 = 0.9
ADAM_B2 = 0.999
ADAM_EPS = 1e-08
ADAM_WD = 0.01
ADAM_STEP = 10
VMEM_LIMIT_V7X = 56 * 1024 * 1024


class _Job:
    def __init__(self, ins, out_shapes, n_sems, n_local, start, finish):
        self.ins, self.out_shapes, self.n_sems, self.n_local = list(ins), list(out_shapes), n_sems, n_local
        self.start, self.finish = start, finish


def _pallas(body, *, name, grid, in_specs, out_specs, out_shape, scratch_shapes=(), aliases=None, nprefetch=0, jobs=()):
    extra = {}
    single = not isinstance(out_shape, (list, tuple))
    out_shape = [out_shape] if single else list(out_shape)
    out_specs = [out_specs] if single else list(out_specs)
    in_specs, scratch = list(in_specs), list(scratch_shapes)
    n_in, n_out, n_scr = len(in_specs), len(out_shape), len(scratch)
    for jb in jobs:
        in_specs += [ANY] * len(jb.ins)
        out_specs += [ANY] * len(jb.out_shapes)
        out_shape += jb.out_shapes
        scratch += [pltpu.SemaphoreType.DMA((jb.n_sems,)), pltpu.SemaphoreType.DMA((jb.n_sems,)),
                    pltpu.SemaphoreType.DMA((max(jb.n_local, 1),))]

    def with_jobs(*refs):
        pre, refs = refs[:nprefetch], refs[nprefetch:]
        ins, p = refs[:n_in], n_in
        jins = []
        for jb in jobs:
            jins.append(refs[p:p + len(jb.ins)])
            p += len(jb.ins)
        outs = refs[p:p + n_out]
        p += n_out
        jouts = []
        for jb in jobs:
            jouts.append(refs[p:p + len(jb.out_shapes)])
            p += len(jb.out_shapes)
        scr = refs[p:p + n_scr]
        p += n_scr
        pids = [pl.program_id(d) for d in range(len(grid))]
        first = functools.reduce(jnp.logical_and, [pid == 0 for pid in pids])
        last = functools.reduce(jnp.logical_and, [pid == g - 1 for pid, g in zip(pids, grid)])

        @pl.when(first)
        def _():
            for i, jb in enumerate(jobs):
                jb.start(jins[i], jouts[i], *refs[p + 3 * i:p + 3 * i + 3])

        body(*pre, *ins, *outs, *scr)

        @pl.when(last)
        def _():
            for i, jb in enumerate(jobs):
                jb.finish(jins[i], jouts[i], *refs[p + 3 * i:p + 3 * i + 3])

    params = pltpu.CompilerParams(dimension_semantics=("arbitrary",) * len(grid), vmem_limit_bytes=VMEM_LIMIT_V7X)
    spec = pltpu.PrefetchScalarGridSpec(num_scalar_prefetch=nprefetch, grid=grid, in_specs=in_specs,
                                        out_specs=out_specs, scratch_shapes=scratch)
    call = pl.pallas_call(with_jobs if jobs else body, name=name, grid_spec=spec, out_shape=out_shape, compiler_params=params,
                          input_output_aliases=aliases or {}, **extra)

    def run(*args):
        res = call(*args, *[a for jb in jobs for a in jb.ins])
        own = res[0] if single else list(res[:n_out])
        if not jobs:
            return own
        jres, p = [], n_out
        for jb in jobs:
            jres.append(list(res[p:p + len(jb.out_shapes)]))
            p += len(jb.out_shapes)
        return own, jres

    return run


def _comm_call(body, *, name, in_specs, out_specs, out_shape, scratch_shapes):
    extra = {}
    return pl.pallas_call(body, name=name, in_specs=in_specs, out_specs=out_specs, out_shape=out_shape,
                          scratch_shapes=scratch_shapes, **extra)


def _sds(shape, dtype):
    return jax.ShapeDtypeStruct(shape, dtype)


def _tile(n, pref):
    t = min(n, pref)
    assert n % t == 0, (n, pref)
    return t


def _sigmoid(x):
    return jax.nn.sigmoid(x)


def _rms_fwd(x, g):
    r = lax.rsqrt(jnp.mean(x * x, axis=-1, keepdims=True) + NORM_EPS)
    return x * r * g


def _rms_bwd(x, g, dy):
    r = lax.rsqrt(jnp.mean(x * x, axis=-1, keepdims=True) + NORM_EPS)
    xh = x * r
    dg = jnp.sum(dy * xh, axis=0, keepdims=True)
    dxh = dy * g
    dx = r * (dxh - xh * jnp.mean(dxh * xh, axis=-1, keepdims=True))
    return dx, dg


def _ln_stats(x):
    mu = jnp.mean(x, axis=-1, keepdims=True)
    xc = x - mu
    rstd = lax.rsqrt(jnp.mean(xc * xc, axis=-1, keepdims=True) + LN_EPS)
    return xc * rstd, rstd


def _ln_bwd(xh, rstd, dxh):
    return rstd * (dxh - jnp.mean(dxh, axis=-1, keepdims=True) - xh * jnp.mean(dxh * xh, axis=-1, keepdims=True))


def _dsilu(x, s):
    return s * (1.0 + x * (1.0 - s))


def _rot(x, cosf, sinf):
    return x * cosf + pltpu.roll(x, x.shape[-1] // 2, 1) * sinf


def _unrot(d, cosf, sinf):
    return d * cosf - pltpu.roll(d, d.shape[-1] // 2, 1) * sinf


def _dot_nn(a, b):
    return jnp.dot(a, b, preferred_element_type=F32)


def _dot_nt(a, b):
    return lax.dot_general(a, b, (((1,), (1,)), ((), ())), preferred_element_type=F32)


def _dot_tn(a, b):
    return lax.dot_general(a, b, (((0,), (0,)), ((), ())), preferred_element_type=F32)


def _ffn_tf(cb):
    return _tile(cb, 1024)


def _ffn_fwd(x, g_pre, g_post, wgu, wd, name, jobs=()):
    T, D = x.shape
    F = wd.shape[0]
    cb = wgu.shape[2]
    tm, tf = _tile(T, 512), _ffn_tf(cb)
    nj, nb = F // tf, cb // tf

    def body(x_ref, gpre_ref, gpost_ref, wg_ref, wu_ref, wd_ref, xo_ref, ht_ref, gate_ref, up_ref, f_ref, acc_ref, h_ref):
        j = pl.program_id(1)

        @pl.when(j == 0)
        def _():
            h = _rms_fwd(x_ref[...], gpre_ref[...])
            h_ref[...] = h.astype(BF16)
            ht_ref[...] = h.T.astype(BF16)
            acc_ref[...] = jnp.zeros_like(acc_ref)

        h = h_ref[...]
        gate = _dot_nn(h, wg_ref[...])
        up = _dot_nn(h, wu_ref[...])
        gate_ref[...] = gate.astype(BF16)
        up_ref[...] = up.astype(BF16)
        act = (gate * _sigmoid(gate) * up).astype(BF16)
        acc_ref[...] += _dot_nn(act, wd_ref[...])

        @pl.when(j == nj - 1)
        def _():
            f = acc_ref[...]
            f_ref[...] = f
            xo_ref[...] = x_ref[...] + 0.5 * _rms_fwd(f, gpost_ref[...])

    row = pl.BlockSpec((tm, D), lambda i, j: (i, 0))
    vec = pl.BlockSpec((1, D), lambda i, j: (0, 0))
    col = pl.BlockSpec((tm, tf), lambda i, j: (i, j))
    return _pallas(
        body, name=name, grid=(T // tm, nj),
        in_specs=[row, vec, vec,
                  pl.BlockSpec((None, D, tf), lambda i, j: (j // nb, 0, j % nb)),
                  pl.BlockSpec((None, D, tf), lambda i, j: ((nj + j) // nb, 0, j % nb)),
                  pl.BlockSpec((tf, D), lambda i, j: (j, 0))],
        out_specs=[row, pl.BlockSpec((D, tm), lambda i, j: (0, i)), col, col, row],
        out_shape=[_sds((T, D), F32), _sds((D, T), BF16), _sds((T, F), BF16), _sds((T, F), BF16), _sds((T, D), F32)],
        scratch_shapes=[pltpu.VMEM((tm, D), F32), pltpu.VMEM((tm, D), BF16)], jobs=jobs,
    )(x, g_pre, g_post, wgu, wgu, wd)


def _ffn_bwd(dxo, x, f, gate, up, g_pre, g_post, wgu, wd, name, jobs=()):
    T, D = x.shape
    F = wd.shape[0]
    cb = wgu.shape[2]
    tm, tf = _tile(T, 512), _ffn_tf(cb)
    nj, nb = F // tf, cb // tf

    def body(dxo_ref, x_ref, f_ref, gate_ref, up_ref, gpre_ref, gpost_ref, wg_ref, wu_ref, wd_ref,
             dx_ref, df_ref, dgate_ref, dup_ref, act_ref, dgpre_ref, dgpost_ref, acc_ref):
        i, j = pl.program_id(0), pl.program_id(1)

        @pl.when((i == 0) & (j == 0))
        def _():
            dgpre_ref[...] = jnp.zeros_like(dgpre_ref)
            dgpost_ref[...] = jnp.zeros_like(dgpost_ref)

        @pl.when(j == 0)
        def _():
            df, dgp = _rms_bwd(f_ref[...], gpost_ref[...], 0.5 * dxo_ref[...])
            df_ref[...] = df.astype(BF16)
            dgpost_ref[...] += dgp
            acc_ref[...] = jnp.zeros_like(acc_ref)

        dact = _dot_nt(df_ref[...], wd_ref[...])
        g = gate_ref[...].astype(F32)
        u = up_ref[...].astype(F32)
        s = _sigmoid(g)
        silu = g * s
        dgate = (dact * u * _dsilu(g, s)).astype(BF16)
        dup = (dact * silu).astype(BF16)
        act_ref[...] = (silu * u).astype(BF16)
        dgate_ref[...] = dgate
        dup_ref[...] = dup
        acc_ref[...] += _dot_nt(dgate, wg_ref[...]) + _dot_nt(dup, wu_ref[...])

        @pl.when(j == nj - 1)
        def _():
            dxin, dgp = _rms_bwd(x_ref[...], gpre_ref[...], acc_ref[...])
            dx_ref[...] = dxo_ref[...] + dxin
            dgpre_ref[...] += dgp

    row = pl.BlockSpec((tm, D), lambda i, j: (i, 0))
    vec = pl.BlockSpec((1, D), lambda i, j: (0, 0))
    col = pl.BlockSpec((tm, tf), lambda i, j: (i, j))
    return _pallas(
        body, name=name, grid=(T // tm, nj),
        in_specs=[row, row, row, col, col, vec, vec,
                  pl.BlockSpec((None, D, tf), lambda i, j: (j // nb, 0, j % nb)),
                  pl.BlockSpec((None, D, tf), lambda i, j: ((nj + j) // nb, 0, j % nb)),
                  pl.BlockSpec((tf, D), lambda i, j: (j, 0))],
        out_specs=[row, row, col, col, col, vec, vec],
        out_shape=[_sds((T, D), F32), _sds((T, D), BF16), _sds((T, F), BF16), _sds((T, F), BF16), _sds((T, F), BF16),
                   _sds((1, D), F32), _sds((1, D), F32)],
        scratch_shapes=[pltpu.VMEM((tm, D), F32)], jobs=jobs,
    )(dxo, x, f, gate, up, g_pre, g_post, wgu, wgu, wd)


def _mm_tn(a, b, name, slab_cols=None, n_slabs=None, first_slab=0, into=None, tn_pref=1024, tt_pref=1024,
           a_is_transposed=False, jobs=()):
    T, N = b.shape
    K = a.shape[0] if a_is_transposed else a.shape[1]
    tt = _tile(T, tt_pref)
    tko = _tile(K, 1024)
    tn = _tile(slab_cols or N, tn_pref)
    nt = T // tt

    def body(a_ref, b_ref, *rest):
        o_ref, acc_ref = rest[-2:]
        t = pl.program_id(2)

        @pl.when(t == 0)
        def _():
            acc_ref[...] = jnp.zeros_like(acc_ref)

        acc_ref[...] += (_dot_nn if a_is_transposed else _dot_tn)(a_ref[...], b_ref[...])

        @pl.when(t == nt - 1)
        def _():
            o_ref[...] = acc_ref[...].astype(o_ref.dtype)

    if slab_cols is None:
        shape = (K, N)
        ospec = pl.BlockSpec((tko, tn), lambda k, jn, t: (k, jn))
    else:
        nb = slab_cols // tn
        shape = (n_slabs or N // slab_cols, K, slab_cols)
        ospec = pl.BlockSpec((None, tko, tn), lambda k, jn, t: (first_slab + jn // nb, k, jn % nb))
    aspec = pl.BlockSpec((tko, tt), lambda k, jn, t: (k, t)) if a_is_transposed else pl.BlockSpec((tt, tko), lambda k, jn, t: (t, k))
    in_specs, args = [aspec, pl.BlockSpec((tt, tn), lambda k, jn, t: (t, jn))], [a, b]
    if into is not None:
        in_specs.append(ANY)
        args.append(into)
    return _pallas(body, name=name, grid=(K // tko, N // tn, nt), in_specs=in_specs, out_specs=ospec, out_shape=_sds(shape, BF16),
                   scratch_shapes=[pltpu.VMEM((tko, tn), F32)], aliases={2: 0} if into is not None else None, jobs=jobs)(*args)


def _norm_proj(x, g, w, name, jobs=()):
    T, D = x.shape
    nd, cb = w.shape[0], w.shape[2]
    tm = _tile(T, 1024)

    def body(x_ref, g_ref, w_ref, p_ref, ht_ref, h_ref):
        @pl.when(pl.program_id(1) == 0)
        def _():
            h = _rms_fwd(x_ref[...], g_ref[...])
            h_ref[...] = h.astype(BF16)
            ht_ref[...] = h.T.astype(BF16)

        p_ref[...] = _dot_nn(h_ref[...], w_ref[...])

    row = pl.BlockSpec((tm, D), lambda i, j: (i, 0))
    return _pallas(
        body, name=name, grid=(T // tm, nd),
        in_specs=[row, pl.BlockSpec((1, D), lambda i, j: (0, 0)),
                  pl.BlockSpec((None, D, cb), lambda i, j: (j, 0, 0))],
        out_specs=[pl.BlockSpec((tm, cb), lambda i, j: (i, j)), pl.BlockSpec((D, tm), lambda i, j: (0, i))],
        out_shape=[_sds((T, nd * cb), F32), _sds((D, T), BF16)],
        scratch_shapes=[pltpu.VMEM((tm, D), BF16)], jobs=jobs,
    )(x, g, w)


def _proj_bwd(dproj, w, x, g, dxo, name, jobs=()):
    T, D = x.shape
    nd, cb = w.shape[0], w.shape[2]
    tm = _tile(T, 1024)

    def body(dp_ref, w_ref, x_ref, g_ref, dxo_ref, dx_ref, dg_ref, acc_ref):
        i, j = pl.program_id(0), pl.program_id(1)

        @pl.when((i == 0) & (j == 0))
        def _():
            dg_ref[...] = jnp.zeros_like(dg_ref)

        @pl.when(j == 0)
        def _():
            acc_ref[...] = jnp.zeros_like(acc_ref)

        acc_ref[...] += _dot_nt(dp_ref[...], w_ref[...])

        @pl.when(j == nd - 1)
        def _():
            dxin, dgp = _rms_bwd(x_ref[...], g_ref[...], acc_ref[...])
            dx_ref[...] = dxo_ref[...] + dxin
            dg_ref[...] += dgp

    row = pl.BlockSpec((tm, D), lambda i, j: (i, 0))
    vec = pl.BlockSpec((1, D), lambda i, j: (0, 0))
    return _pallas(
        body, name=name, grid=(T // tm, nd),
        in_specs=[pl.BlockSpec((tm, cb), lambda i, j: (i, j)),
                  pl.BlockSpec((None, D, cb), lambda i, j: (j, 0, 0)), row, vec, row],
        out_specs=[row, vec],
        out_shape=[_sds((T, D), F32), _sds((1, D), F32)],
        scratch_shapes=[pltpu.VMEM((tm, D), F32)], jobs=jobs,
    )(dproj, w, x, g, dxo)


def _rope_tables(pos, invf, sgn, name):
    T = pos.shape[0]
    dk = invf.shape[1]
    tm = _tile(T, 1024)

    def body(p_ref, f_ref, s_ref, c_out, s_out):
        ang = p_ref[...].astype(F32) * f_ref[...]
        c_out[...] = jnp.cos(ang)
        s_out[...] = jnp.sin(ang) * s_ref[...]

    vec = pl.BlockSpec((1, dk), lambda i: (0, 0))
    out = pl.BlockSpec((tm, dk), lambda i: (i, 0))
    return _pallas(body, name=name, grid=(T // tm,), in_specs=[pl.BlockSpec((tm, 1), lambda i: (i, 0)), vec, vec],
                   out_specs=[out, out], out_shape=[_sds((T, dk), F32), _sds((T, dk), F32)])(pos, invf, sgn)


def _decay(lg, r0, tq, n):
    r = r0 + lax.broadcasted_iota(jnp.int32, (tq, n), 0)
    c = lax.broadcasted_iota(jnp.int32, (tq, n), 1)
    rc, cc = r // CHUNK, c // CHUNK
    d = (r - c).astype(F32)
    e = jnp.where(rc == cc, jnp.abs(d), d)
    return jnp.where(cc > rc, 0.0, jnp.exp(lg * e))


def _ret_fwd(proj, cosf, sinf, lgs, B, S, D, name, jobs=()):
    T = B * S
    dk, dv = D // 8, D // 4
    tq = _tile(S, 256)
    scale = dk ** -0.5

    def body(lg_ref, q_ref, k_ref, v_ref, c_ref, s_ref, o_ref, kr_ref, vb_ref):
        lg = lg_ref[pl.program_id(1)]
        kr_ref[...] = (_rot(k_ref[...], c_ref[...], s_ref[...]) * scale).astype(BF16)
        vb_ref[...] = v_ref[...].astype(BF16)
        for qi in range(S // tq):
            rows, n = slice(qi * tq, (qi + 1) * tq), (qi + 1) * tq
            q = _rot(q_ref[rows, :], c_ref[rows, :], s_ref[rows, :]).astype(BF16)
            p = (_dot_nt(q, kr_ref[:n, :]) * _decay(lg, qi * tq, tq, n)).astype(BF16)
            o_ref[rows, :] = _dot_nn(p, vb_ref[:n, :])

    return _pallas(
        body, name=name, grid=(B, HEADS),
        in_specs=[pl.BlockSpec(memory_space=pltpu.SMEM),
                  pl.BlockSpec((S, dk), lambda b, h: (b, h)),
                  pl.BlockSpec((S, dk), lambda b, h: (b, HEADS + h)),
                  pl.BlockSpec((S, dv), lambda b, h: (b, HEADS + h)),
                  pl.BlockSpec((S, dk), lambda b, h: (b, 0)),
                  pl.BlockSpec((S, dk), lambda b, h: (b, 0))],
        out_specs=pl.BlockSpec((S, dv), lambda b, h: (b, h)),
        out_shape=_sds((T, D), F32),
        scratch_shapes=[pltpu.VMEM((S, dk), BF16), pltpu.VMEM((S, dv), BF16)], jobs=jobs,
    )(lgs, proj, proj, proj, cosf, sinf)


def _ret_bwd(do, proj, cosf, sinf, lgs, B, S, D, name, jobs=()):
    T = B * S
    dk, dv = D // 8, D // 4
    tq = _tile(S, 256)
    scale = dk ** -0.5

    def body(lg_ref, do_ref, q_ref, k_ref, v_ref, c_ref, s_ref, dq_ref, dk_ref, dv_ref, kr_ref, vb_ref, dka_ref, dva_ref):
        h = pl.program_id(1)
        lg = lg_ref[h]
        kr_ref[...] = (_rot(k_ref[...], c_ref[...], s_ref[...]) * scale).astype(BF16)
        vb_ref[...] = v_ref[...].astype(BF16)
        dka_ref[...] = jnp.zeros_like(dka_ref)
        dva_ref[...] = jnp.zeros_like(dva_ref)

        for qi in range(S // tq):
            rows, n = slice(qi * tq, (qi + 1) * tq), (qi + 1) * tq
            cq, sq = c_ref[rows, :], s_ref[rows, :]
            q = _rot(q_ref[rows, :], cq, sq).astype(BF16)
            dout = do_ref[rows, :].astype(BF16)
            w = _decay(lg, qi * tq, tq, n)
            p = (_dot_nt(q, kr_ref[:n, :]) * w).astype(BF16)
            ds = (_dot_nt(dout, vb_ref[:n, :]) * w).astype(BF16)
            dq_ref[rows, :] = _unrot(_dot_nn(ds, kr_ref[:n, :]), cq, sq).astype(BF16)
            dka_ref[:n, :] += _dot_tn(ds, q)
            dva_ref[:n, :] += _dot_tn(p, dout)

        dk_ref[...] = _unrot(dka_ref[...] * scale, c_ref[...], s_ref[...]).astype(BF16)
        dv_ref[...] = dva_ref[...].astype(BF16)

    return _pallas(
        body, name=name, grid=(B, HEADS),
        in_specs=[pl.BlockSpec(memory_space=pltpu.SMEM),
                  pl.BlockSpec((S, dv), lambda b, h: (b, h)),
                  pl.BlockSpec((S, dk), lambda b, h: (b, h)),
                  pl.BlockSpec((S, dk), lambda b, h: (b, HEADS + h)),
                  pl.BlockSpec((S, dv), lambda b, h: (b, HEADS + h)),
                  pl.BlockSpec((S, dk), lambda b, h: (b, 0)),
                  pl.BlockSpec((S, dk), lambda b, h: (b, 0))],
        out_specs=[pl.BlockSpec((S, dk), lambda b, h: (b, h)), pl.BlockSpec((S, dk), lambda b, h: (b, h)),
                   pl.BlockSpec((S, dv), lambda b, h: (b, h))],
        out_shape=[_sds((T, D // 2), BF16), _sds((T, D // 2), BF16), _sds((T, D), BF16)],
        scratch_shapes=[pltpu.VMEM((S, dk), BF16), pltpu.VMEM((S, dv), BF16), pltpu.VMEM((S, dk), F32), pltpu.VMEM((S, dv), F32)], jobs=jobs,
    )(lgs, do, proj, proj, proj, cosf, sinf)


COL_GRET, COL_SCB, COL_SCC, COL_SCX, COL_GLUA, COL_GLUB, COL_GL = 2, 3, 4, 5, 6, 7, 8


SUBLANES, LANES = 8, 128
CONV_ROWS = 16
CONV_CHAINS = 4
WGRAD_ROWS = 32
SC_FWD = tuple(HALO - (SC_KERNEL - 1) + j for j in range(SC_KERNEL))
CF_FWD = tuple(HALO - (CF_KERNEL - 1) + j for j in range(CF_KERNEL))
SC_BWD = tuple(SC_KERNEL - 1 - j for j in range(SC_KERNEL))
CF_BWD = tuple(CF_KERNEL - 1 - j for j in range(CF_KERNEL))


def _build_shifted(src_ref, e_ref, shifts):
    n = e_ref.shape[1]
    for b in sorted({s % SUBLANES for s in shifts} - {0}):
        e_ref[b - 1] = src_ref[pl.ds(b, n), :]


def _shifted_rows(src_ref, e_ref, s, r0, rows, lanes):
    a, b = divmod(s, SUBLANES)
    at = pl.ds(r0 + SUBLANES * a, rows)
    return src_ref[at, lanes] if b == 0 else e_ref[b - 1, at, lanes]


def _conv_taps(src_ref, e_ref, w_ref, shifts, tm, D, emit):
    for lt in range(D // LANES):
        lanes = pl.ds(lt * LANES, LANES)
        wv = [jnp.broadcast_to(w_ref[j:j + 1, lanes], (CONV_ROWS, LANES)) for j in range(len(shifts))]

        def step(ci, carry):
            r0 = pl.multiple_of(ci * CONV_ROWS, CONV_ROWS)
            parts = [None] * min(CONV_CHAINS, len(shifts))
            for j, s in enumerate(shifts):
                term = _shifted_rows(src_ref, e_ref, s, r0, CONV_ROWS, lanes) * wv[j]
                k = j % len(parts)
                parts[k] = term if parts[k] is None else parts[k] + term
            while len(parts) > 1:
                parts = [a + b for a, b in zip(parts[::2], parts[1::2])] + ([parts[-1]] if len(parts) % 2 else [])
            emit(lt, pl.ds(r0, CONV_ROWS), parts[0])
            return carry

        lax.fori_loop(0, tm // CONV_ROWS, step, 0)


def _conv_weight_grad(d_ref, src_ref, e_ref, shifts, tm, D, dw_ref):
    groups = WGRAD_ROWS // SUBLANES
    for lt in range(D // LANES):
        lanes = pl.ds(lt * LANES, LANES)

        def step(ci, accs):
            r0 = pl.multiple_of(ci * WGRAD_ROWS, WGRAD_ROWS)
            d = d_ref[pl.ds(r0, WGRAD_ROWS), lanes]
            out = []
            for acc, s in zip(accs, shifts):
                prod = d * _shifted_rows(src_ref, e_ref, s, r0, WGRAD_ROWS, lanes)
                terms = [prod[g * SUBLANES:(g + 1) * SUBLANES, :] for g in range(groups)]
                while len(terms) > 1:
                    terms = [a + b for a, b in zip(terms[::2], terms[1::2])]
                out.append(acc + terms[0])
            return tuple(out)

        accs = lax.fori_loop(0, tm // WGRAD_ROWS, step, tuple(jnp.zeros((SUBLANES, LANES), F32) for _ in shifts))
        for j, acc in enumerate(accs):
            dw_ref[j:j + 1, lanes] += jnp.sum(acc, axis=0, keepdims=True)


def _mixer_mid_fwd(proj, o, sc_w, cf_w, cf_b, ln_g, ln_b, S, name, jobs=()):
    T, D = o.shape
    dv = D // HEADS
    tm = _tile(S, 256)
    per_seq = S // tm
    hb = tm // HALO

    def body(gret_ref, scb_ref, scc_ref, scx_ref, ga_ref, gb_ref, scc_h, scx_h, ga_h, gb_h, o_ref,
             scw_ref, cfw_ref, cfb_ref, lng_ref, lnb_ref, ya_ref, yb_ref, yc_ref, cv_ref, u1_ref, ext_ref, e_ref):
        first = (pl.program_id(0) % per_seq) == 0
        keep = jnp.where(first, 0.0, 1.0)
        gr = gret_ref[...]
        sg = gr * _sigmoid(gr)
        for hh in range(HEADS):
            cols = slice(hh * dv, (hh + 1) * dv)
            hn, _ = _ln_stats(o_ref[:, cols])
            ya_ref[:, cols] = (sg[:, cols] * hn).astype(BF16)
        ext_ref[:HALO, :] = scc_h[...] * scx_h[...] * keep
        ext_ref[HALO:, :] = scc_ref[...] * scx_ref[...]
        _build_shifted(ext_ref, e_ref, SC_FWD)

        def emit_cv(lt, rows, cv):
            lanes = pl.ds(lt * LANES, LANES)
            cv_ref[rows, lanes] = cv
            yb_ref[rows, lanes] = (scb_ref[rows, lanes] * cv).astype(BF16)

        _conv_taps(ext_ref, e_ref, scw_ref, SC_FWD, tm, D, emit_cv)
        ext_ref[:HALO, :] = ga_h[...] * _sigmoid(gb_h[...]) * keep
        ext_ref[HALO:, :] = ga_ref[...] * _sigmoid(gb_ref[...])
        _build_shifted(ext_ref, e_ref, CF_FWD)

        def emit_u1(lt, rows, acc):
            lanes = pl.ds(lt * LANES, LANES)
            u1_ref[rows, lanes] = acc + cfb_ref[:, lanes]

        _conv_taps(ext_ref, e_ref, cfw_ref, CF_FWD, tm, D, emit_u1)
        xh, _ = _ln_stats(u1_ref[...])
        u2 = xh * lng_ref[...] + lnb_ref[...]
        yc_ref[...] = (u2 * _sigmoid(u2)).astype(BF16)

    def colblk(c):
        return pl.BlockSpec((tm, D), lambda i: (i, c))

    def halo(c):
        return pl.BlockSpec((HALO, D), lambda i: (jnp.maximum(i * hb - 1, 0), c))

    row = pl.BlockSpec((tm, D), lambda i: (i, 0))
    vec = pl.BlockSpec((1, D), lambda i: (0, 0))
    return _pallas(
        body, name=name, grid=(T // tm,),
        in_specs=[colblk(COL_GRET), colblk(COL_SCB), colblk(COL_SCC), colblk(COL_SCX), colblk(COL_GLUA), colblk(COL_GLUB),
                  halo(COL_SCC), halo(COL_SCX), halo(COL_GLUA), halo(COL_GLUB), row,
                  pl.BlockSpec((SC_KERNEL, D), lambda i: (0, 0)), pl.BlockSpec((CF_KERNEL, D), lambda i: (0, 0)), vec, vec, vec],
        out_specs=[row, row, row, row, row],
        out_shape=[_sds((T, D), BF16)] * 3 + [_sds((T, D), F32)] * 2,
        scratch_shapes=[pltpu.VMEM((HALO + tm, D), F32), pltpu.VMEM((SUBLANES - 1, HALO + tm - SUBLANES, D), F32)], jobs=jobs,
    )(proj, proj, proj, proj, proj, proj, proj, proj, proj, proj, o, sc_w, cf_w, cf_b, ln_g, ln_b)


def _mixer_merge_fwd(x, proj, ya_in, yb_in, yc_in, g_post, w_ret, w_sc, w_cf, w_o, name, jobs=()):
    T, D = x.shape
    tm = _tile(T, 256)

    def body(x_ref, g0_ref, g1_ref, g2_ref, ya_in_ref, yb_in_ref, yc_in_ref, gp_ref, wr_ref, ws_ref, wc_ref, wo_ref,
             xo_ref, ya_ref, yb_ref, yc_ref, mg_ref, m_ref):
        ya = _dot_nn(ya_in_ref[...], wr_ref[...])
        yb = _dot_nn(yb_in_ref[...], ws_ref[...])
        yc = _dot_nn(yc_in_ref[...], wc_ref[...])
        ya_ref[...] = ya.astype(BF16)
        yb_ref[...] = yb.astype(BF16)
        yc_ref[...] = yc.astype(BF16)
        merged = (_sigmoid(g0_ref[...]) * ya + _sigmoid(g1_ref[...]) * yb + _sigmoid(g2_ref[...]) * yc).astype(BF16)
        mg_ref[...] = merged
        m = _dot_nn(merged, wo_ref[...])
        m_ref[...] = m
        xo_ref[...] = x_ref[...] + _rms_fwd(m, gp_ref[...])

    row = pl.BlockSpec((tm, D), lambda i: (i, 0))
    wsp = pl.BlockSpec((D, D), lambda i: (0, 0))

    def colblk(c):
        return pl.BlockSpec((tm, D), lambda i: (i, c))

    return _pallas(
        body, name=name, grid=(T // tm,),
        in_specs=[row, colblk(COL_GL), colblk(COL_GL + 1), colblk(COL_GL + 2), row, row, row,
                  pl.BlockSpec((1, D), lambda i: (0, 0)), wsp, wsp, wsp, wsp],
        out_specs=[row] * 6,
        out_shape=[_sds((T, D), F32)] + [_sds((T, D), BF16)] * 4 + [_sds((T, D), F32)], jobs=jobs,
    )(x, proj, proj, proj, ya_in, yb_in, yc_in, g_post, w_ret, w_sc, w_cf, w_o)


def _mixer_bwd_a(dxo, m, g_post, ya, yb, yc, proj, o, cv, u1, ln_g, ln_b, w_ret, w_sc, w_cf, w_o, name, jobs=()):
    T, D = m.shape
    dv = D // HEADS
    tm = _tile(T, 128)

    def body(dxo_ref, m_ref, gp_ref, ya_ref, yb_ref, yc_ref, g0_ref, g1_ref, g2_ref, gret_ref, scb_ref, o_ref, cv_ref, u1_ref,
             lng_ref, lnb_ref, wr_ref, ws_ref, wc_ref, wo_ref,
             dm_ref, dya_ref, dyb_ref, dyc_ref, d2_ref, dgl_ref, do_ref, dcv_ref, du1_ref, dgp_ref, dlng_ref, dlnb_ref, dcfb_ref):
        @pl.when(pl.program_id(0) == 0)
        def _():
            for r in (dgp_ref, dlng_ref, dlnb_ref, dcfb_ref):
                r[...] = jnp.zeros_like(r)

        dm, dgp = _rms_bwd(m_ref[...], gp_ref[...], dxo_ref[...])
        dgp_ref[...] += dgp
        dmb = dm.astype(BF16)
        dm_ref[...] = dmb
        dmerged = _dot_nt(dmb, wo_ref[...])
        dys = []
        for k, (g_ref, y_ref, dy_ref) in enumerate(((g0_ref, ya_ref, dya_ref), (g1_ref, yb_ref, dyb_ref), (g2_ref, yc_ref, dyc_ref))):
            sg = _sigmoid(g_ref[...])
            dgl_ref[:, k * D:(k + 1) * D] = (dmerged * y_ref[...].astype(F32) * sg * (1.0 - sg)).astype(BF16)
            dy = (dmerged * sg).astype(BF16)
            dy_ref[...] = dy
            dys.append(dy)
        dya_in = _dot_nt(dys[0], wr_ref[...])
        gr = gret_ref[...]
        sr = _sigmoid(gr)
        for hh in range(HEADS):
            cols = slice(hh * dv, (hh + 1) * dv)
            hn, rstd = _ln_stats(o_ref[:, cols])
            d2_ref[:, cols] = (dya_in[:, cols] * hn * _dsilu(gr[:, cols], sr[:, cols])).astype(BF16)
            do_ref[:, cols] = _ln_bwd(hn, rstd, dya_in[:, cols] * gr[:, cols] * sr[:, cols])
        dyb_in = _dot_nt(dys[1], ws_ref[...])
        d2_ref[:, D:] = (dyb_in * cv_ref[...]).astype(BF16)
        dcv_ref[...] = dyb_in * scb_ref[...]
        dyc_in = _dot_nt(dys[2], wc_ref[...])
        xh, rstd = _ln_stats(u1_ref[...])
        u2 = xh * lng_ref[...] + lnb_ref[...]
        du2 = dyc_in * _dsilu(u2, _sigmoid(u2))
        dlng_ref[...] += jnp.sum(du2 * xh, axis=0, keepdims=True)
        dlnb_ref[...] += jnp.sum(du2, axis=0, keepdims=True)
        du1 = _ln_bwd(xh, rstd, du2 * lng_ref[...])
        du1_ref[...] = du1
        dcfb_ref[...] += jnp.sum(du1, axis=0, keepdims=True)

    row = pl.BlockSpec((tm, D), lambda i: (i, 0))
    vec = pl.BlockSpec((1, D), lambda i: (0, 0))
    wsp = pl.BlockSpec((D, D), lambda i: (0, 0))

    def colblk(c):
        return pl.BlockSpec((tm, D), lambda i: (i, c))

    return _pallas(
        body, name=name, grid=(T // tm,),
        in_specs=[row, row, vec, row, row, row, colblk(COL_GL), colblk(COL_GL + 1), colblk(COL_GL + 2),
                  colblk(COL_GRET), colblk(COL_SCB), row, row, row, vec, vec, wsp, wsp, wsp, wsp],
        out_specs=[row, row, row, row, pl.BlockSpec((tm, 2 * D), lambda i: (i, 0)), pl.BlockSpec((tm, 3 * D), lambda i: (i, 0)),
                   row, row, row, vec, vec, vec, vec],
        out_shape=[_sds((T, D), BF16)] * 4 + [_sds((T, 2 * D), BF16), _sds((T, 3 * D), BF16)] + [_sds((T, D), F32)] * 3
                  + [_sds((1, D), F32)] * 4, jobs=jobs,
    )(dxo, m, g_post, ya, yb, yc, proj, proj, proj, proj, proj, o, cv, u1, ln_g, ln_b, w_ret, w_sc, w_cf, w_o)


def _mixer_bwd_b(dcv, du1, proj, sc_w, cf_w, S, name, jobs=()):
    T, D = dcv.shape
    tm = _tile(S, 256)
    per_seq = S // tm
    hb = tm // HALO
    last_hb = T // HALO - 1

    def body(dcv_ref, du1_ref, dcv_n, du1_n, scc_ref, scx_ref, ga_ref, gb_ref, scc_h, scx_h, ga_h, gb_h, scw_ref, cfw_ref,
             d4_ref, dscw_ref, dcfw_ref, ext_ref, dext_ref, e_ref, sb_ref):
        i = pl.program_id(0)
        keep_prev = jnp.where((i % per_seq) == 0, 0.0, 1.0)
        keep_next = jnp.where((i % per_seq) == per_seq - 1, 0.0, 1.0)

        @pl.when(i == 0)
        def _():
            dscw_ref[...] = jnp.zeros_like(dscw_ref)
            dcfw_ref[...] = jnp.zeros_like(dcfw_ref)

        dext_ref[:tm, :] = dcv_ref[...]
        dext_ref[tm:, :] = dcv_n[...] * keep_next
        _build_shifted(dext_ref, e_ref, SC_BWD)

        def emit_dz(lt, rows, dz):
            lanes = pl.ds(lt * LANES, LANES)
            d4_ref[rows, pl.ds(lt * LANES, LANES)] = (dz * scx_ref[rows, lanes]).astype(BF16)
            d4_ref[rows, pl.ds(D + lt * LANES, LANES)] = (dz * scc_ref[rows, lanes]).astype(BF16)

        _conv_taps(dext_ref, e_ref, scw_ref, SC_BWD, tm, D, emit_dz)
        ext_ref[:HALO, :] = scc_h[...] * scx_h[...] * keep_prev
        ext_ref[HALO:, :] = scc_ref[...] * scx_ref[...]
        _build_shifted(ext_ref, e_ref, SC_FWD)
        _conv_weight_grad(dcv_ref, ext_ref, e_ref, SC_FWD, tm, D, dscw_ref)
        dext_ref[:tm, :] = du1_ref[...]
        dext_ref[tm:, :] = du1_n[...] * keep_next
        _build_shifted(dext_ref, e_ref, CF_BWD)
        sb_ref[...] = _sigmoid(gb_ref[...])

        def emit_du0(lt, rows, du0):
            lanes = pl.ds(lt * LANES, LANES)
            sb = sb_ref[rows, lanes]
            d4_ref[rows, pl.ds(2 * D + lt * LANES, LANES)] = (du0 * sb).astype(BF16)
            d4_ref[rows, pl.ds(3 * D + lt * LANES, LANES)] = (du0 * ga_ref[rows, lanes] * sb * (1.0 - sb)).astype(BF16)

        _conv_taps(dext_ref, e_ref, cfw_ref, CF_BWD, tm, D, emit_du0)
        ext_ref[:HALO, :] = ga_h[...] * _sigmoid(gb_h[...]) * keep_prev
        ext_ref[HALO:, :] = ga_ref[...] * sb_ref[...]
        _build_shifted(ext_ref, e_ref, CF_FWD)
        _conv_weight_grad(du1_ref, ext_ref, e_ref, CF_FWD, tm, D, dcfw_ref)

    row = pl.BlockSpec((tm, D), lambda i: (i, 0))
    nxt = pl.BlockSpec((HALO, D), lambda i: (jnp.minimum((i + 1) * hb, last_hb), 0))

    def colblk(c):
        return pl.BlockSpec((tm, D), lambda i: (i, c))

    def halo(c):
        return pl.BlockSpec((HALO, D), lambda i: (jnp.maximum(i * hb - 1, 0), c))

    return _pallas(
        body, name=name, grid=(T // tm,),
        in_specs=[row, row, nxt, nxt, colblk(COL_SCC), colblk(COL_SCX), colblk(COL_GLUA), colblk(COL_GLUB),
                  halo(COL_SCC), halo(COL_SCX), halo(COL_GLUA), halo(COL_GLUB),
                  pl.BlockSpec((SC_KERNEL, D), lambda i: (0, 0)), pl.BlockSpec((CF_KERNEL, D), lambda i: (0, 0))],
        out_specs=[pl.BlockSpec((tm, 4 * D), lambda i: (i, 0)), pl.BlockSpec((SC_KERNEL, D), lambda i: (0, 0)),
                   pl.BlockSpec((CF_KERNEL, D), lambda i: (0, 0))],
        out_shape=[_sds((T, 4 * D), BF16), _sds((SC_KERNEL, D), F32), _sds((CF_KERNEL, D), F32)],
        scratch_shapes=[pltpu.VMEM((HALO + tm, D), F32), pltpu.VMEM((tm + HALO, D), F32),
                        pltpu.VMEM((SUBLANES - 1, HALO + tm - SUBLANES, D), F32), pltpu.VMEM((tm, D), F32)], jobs=jobs,
    )(dcv, du1, dcv, du1, proj, proj, proj, proj, proj, proj, proj, proj, sc_w, cf_w)


def _loss_grad(y, tgt, name):
    T, D = y.shape
    tm = _tile(T, 512)

    def body(y_ref, t_ref, dy_ref, loss_ref):
        @pl.when(pl.program_id(0) == 0)
        def _():
            loss_ref[...] = jnp.zeros_like(loss_ref)

        e = y_ref[...] - t_ref[...]
        dy_ref[...] = e * (1.0 / D)
        loss_ref[...] += 0.5 * jnp.sum(jnp.sum(e * e, axis=-1, keepdims=True) * (1.0 / D), axis=0, keepdims=True)

    row = pl.BlockSpec((tm, D), lambda i: (i, 0))
    return _pallas(body, name=name, grid=(T // tm,), in_specs=[row, row],
                   out_specs=[row, pl.BlockSpec((1, 1), lambda i: (0, 0))],
                   out_shape=[_sds((T, D), F32), _sds((1, 1), F32)])(y, tgt)


def _mesh_pos():
    return lax.axis_index("x"), lax.axis_index("y"), lax.axis_index("c")


def _other_chips(x, y):
    return [(1 - x, y), (x, 1 - y), (1 - x, 1 - y)]


def _allgather_job(shards):
    nt = len(shards)

    def parts(ins, outs, send, recv):
        x, y, c = _mesh_pos()

        def slab(t, px, py, pc):
            return outs[t].at[4 * px + 2 * py + pc]

        def copy(t, k, block, to, src=None):
            return pltpu.make_async_remote_copy(
                src_ref=slab(t, *block) if src is None else src, dst_ref=slab(t, *block),
                send_sem=send.at[7 * t + k], recv_sem=recv.at[7 * t + k], device_id=to, device_id_type=MESH)

        return (x, y, c), (x, y, 1 - c), _other_chips(x, y), c, slab, copy

    def start(ins, outs, send, recv, loc):
        me, sibling, chips, c, slab, copy = parts(ins, outs, send, recv)
        for t in range(nt):
            pltpu.make_async_copy(ins[t], slab(t, *me), loc.at[t]).start()
            copy(t, 0, me, sibling, src=ins[t]).start()
            for j, chip in enumerate(chips):
                copy(t, 1 + j, me, (*chip, c), src=ins[t]).start()

    def finish(ins, outs, send, recv, loc):
        me, sibling, chips, c, slab, copy = parts(ins, outs, send, recv)
        for j, chip in enumerate(chips):
            for t in range(nt):
                copy(t, 1 + j, (*chip, c), me).wait_recv()
                copy(t, 4 + j, (*chip, c), sibling).start()
        for t in range(nt):
            copy(t, 0, sibling, me).wait_recv()
            for j, chip in enumerate(chips):
                copy(t, 4 + j, (*chip, 1 - c), me).wait_recv()
        for t in range(nt):
            copy(t, 0, me, sibling, src=ins[t]).wait_send()
            for j, chip in enumerate(chips):
                copy(t, 1 + j, me, (*chip, c), src=ins[t]).wait_send()
                copy(t, 4 + j, (*chip, c), sibling).wait_send()
            pltpu.make_async_copy(ins[t], slab(t, *me), loc.at[t]).wait()

    return _Job(shards, [_sds((N_DEV,) + s.shape, s.dtype) for s in shards], 7 * nt, nt, start, finish)


def _to_sibling_job(grads):
    nt = len(grads)

    def copies(ins, outs, send, recv):
        x, y, c = _mesh_pos()
        return [pltpu.make_async_remote_copy(src_ref=ins[t].at[:, 1 - c], dst_ref=outs[t], send_sem=send.at[t], recv_sem=recv.at[t],
                                             device_id=(x, y, 1 - c), device_id_type=MESH) for t in range(nt)]

    def start(ins, outs, send, recv, loc):
        for cp in copies(ins, outs, send, recv):
            cp.start()

    def finish(ins, outs, send, recv, loc):
        for cp in copies(ins, outs, send, recv):
            cp.wait()

    return _Job(grads, [_sds((N_CHIP,) + g.shape[2:], g.dtype) for g in grads], nt, 0, start, finish)


def _to_chips_job(pairs):
    nt = len(pairs)

    def copies(ins, outs, send, recv, loc):
        x, y, c = _mesh_pos()
        remote = [pltpu.make_async_remote_copy(src_ref=ins[t].at[2 * px + py], dst_ref=outs[t].at[k], send_sem=send.at[3 * t + k],
                                               recv_sem=recv.at[3 * t + k], device_id=(px, py, c), device_id_type=MESH)
                  for t in range(nt) for k, (px, py) in enumerate(_other_chips(x, y))]
        local = [pltpu.make_async_copy(ins[t].at[2 * x + y], outs[t].at[3], loc.at[t]) for t in range(nt)]
        return remote + local

    def start(ins, outs, send, recv, loc):
        for cp in copies(ins, outs, send, recv, loc):
            cp.start()

    def finish(ins, outs, send, recv, loc):
        for cp in copies(ins, outs, send, recv, loc):
            cp.wait()

    return _Job(pairs, [_sds(p.shape, p.dtype) for p in pairs], 3 * nt, nt, start, finish)


def _run_jobs(jobs, name):
    def body(o_ref):
        o_ref[...] = jnp.zeros_like(o_ref)

    _, jres = _pallas(body, name=name, grid=(1,), in_specs=[], out_specs=pl.BlockSpec((8, 128), lambda i: (0, 0)),
                      out_shape=_sds((8, 128), F32), jobs=jobs)()
    return jres


def _allgather_small(v, name):
    R, C = v.shape

    def body(x_ref, out_ref, send_sems, recv_sems, local_sem):
        x, y, c = _mesh_pos()
        me, sibling = (x, y, c), (x, y, 1 - c)
        chips = _other_chips(x, y)

        def slab(px, py, pc):
            return out_ref.at[4 * px + 2 * py + pc]

        def copy(k, block, to, src=None):
            return pltpu.make_async_remote_copy(
                src_ref=slab(*block) if src is None else src, dst_ref=slab(*block),
                send_sem=send_sems.at[k], recv_sem=recv_sems.at[k], device_id=to, device_id_type=MESH)

        mine = pltpu.make_async_copy(x_ref, slab(*me), local_sem)
        mine.start()
        first = [copy(0, me, sibling, src=x_ref)]
        first += [copy(1 + j, me, (*chip, c), src=x_ref) for j, chip in enumerate(chips)]
        for cp in first:
            cp.start()
        passed = [copy(4 + j, (*chip, c), sibling) for j, chip in enumerate(chips)]
        for j, chip in enumerate(chips):
            copy(1 + j, (*chip, c), me).wait_recv()
            passed[j].start()
        copy(0, sibling, me).wait_recv()
        for j, chip in enumerate(chips):
            copy(4 + j, (*chip, 1 - c), me).wait_recv()
        for cp in first + passed:
            cp.wait_send()
        mine.wait()

    return _comm_call(
        body, name=name, in_specs=[pl.BlockSpec(memory_space=pltpu.VMEM)], out_specs=pl.BlockSpec(memory_space=pltpu.VMEM),
        out_shape=_sds((N_DEV, R, C), v.dtype),
        scratch_shapes=[pltpu.SemaphoreType.DMA((7,)), pltpu.SemaphoreType.DMA((7,)), pltpu.SemaphoreType.DMA],
    )(v)


def _pair_sum(core, grad, got, name):
    _, _, R, C = grad.shape
    tr = _tile(R, 512)

    def body(core_ref, mine_ref, got_ref, out_ref):
        out_ref[...] = (mine_ref[...].astype(F32) + got_ref[...].astype(F32)).astype(BF16)

    blk = (None, tr, C)
    return _pallas(
        body, name=name, grid=(N_CHIP, R // tr), nprefetch=1,
        in_specs=[pl.BlockSpec((None, None, tr, C), lambda k, r, core: (k, core[0], r, 0)),
                  pl.BlockSpec(blk, lambda k, r, core: (k, r, 0))],
        out_specs=pl.BlockSpec(blk, lambda k, r, core: (k, r, 0)),
        out_shape=_sds((N_CHIP, R, C), BF16),
    )(core, grad, got)


def _adamw(w, g, m, v):
    m = ADAM_B1 * m + (1.0 - ADAM_B1) * g
    v = ADAM_B2 * v + (1.0 - ADAM_B2) * (g * g)
    m_hat = m / (1.0 - ADAM_B1 ** ADAM_STEP)
    v_hat = v / (1.0 - ADAM_B2 ** ADAM_STEP)
    delta = -ADAM_LR * (m_hat / (jnp.sqrt(v_hat) + ADAM_EPS) + ADAM_WD * w)
    return delta, m, v


def _final_adamw(sums, w, m, v, l, into, name):
    L, R, C = w.shape
    tr = _tile(R, 256)

    def body(s_ref, w_ref, m_ref, v_ref, *rest):
        g_out, d_out, m_out, v_out = rest[-4:]
        g = s_ref[0].astype(F32)
        for k in range(1, N_CHIP):
            g = g + s_ref[k].astype(F32)
        d, mn, vn = _adamw(w_ref[...], g, m_ref[...], v_ref[...])
        g_out[...] = g
        d_out[...] = d
        m_out[...] = mn
        v_out[...] = vn

    loc = pl.BlockSpec((None, tr, C), lambda r: (l, r, 0))
    in_specs = [pl.BlockSpec((N_CHIP, tr, C), lambda r: (0, r, 0)), loc, loc, loc]
    args = [sums, w, m, v]
    aliases = None
    if into is not None:
        in_specs += [ANY] * 4
        args += list(into)
        aliases = {4 + i: i for i in range(4)}
    return _pallas(body, name=name, grid=(R // tr,), in_specs=in_specs, out_specs=[loc] * 4,
                   out_shape=[_sds((L, R, C), F32)] * 4, aliases=aliases)(*args)


def _small_adamw(dev, allg, w, m, v, width, name):
    R = w.shape[0]

    def body(dev_ref, a_ref, w_ref, m_ref, v_ref, g_out, d_out, m_out, v_out):
        g = a_ref[0]
        for k in range(1, N_DEV):
            g = g + a_ref[k]
        d, mn, vn = _adamw(w_ref[...], g, m_ref[...], v_ref[...])
        g_out[...] = g
        d_out[...] = d
        m_out[...] = mn
        v_out[...] = vn

    full = allg.shape[2] == width
    loc = pl.BlockSpec((R, width), lambda i, dev: (0, 0))
    return _pallas(
        body, name=name, grid=(1,), nprefetch=1,
        in_specs=[pl.BlockSpec((N_DEV, R, width), lambda i, dev: (0, 0, 0 if full else dev[0])), loc, loc, loc],
        out_specs=[loc] * 4, out_shape=[_sds((R, width), F32)] * 4,
    )(dev, allg, w, m, v)


BIG = ("ffn1_w_gu", "ffn1_w_down", "w_in", "w_ret_o", "w_sc_o", "w_cf_o", "w_o", "ffn2_w_gu", "ffn2_w_down")
FFN1 = ("ffn1_w_gu", "ffn1_w_down")
MIX_OUT = ("w_ret_o", "w_sc_o", "w_cf_o", "w_o")
ROW_BLOCKS = ("ffn1_w_down", "ffn2_w_down") + MIX_OUT


def kernel(x, positions, norm_g, ffn1_w_gu, ffn1_w_down, w_in, w_ret_o, sc_conv_w, w_sc_o, cf_dw_w, cf_dw_b, cf_ln_g, cf_ln_b, w_cf_o, w_o, ffn2_w_gu, ffn2_w_down, loss_target, m_norm_g, m_ffn1_w_gu, m_ffn1_w_down, m_w_in, m_w_ret_o, m_sc_conv_w, m_w_sc_o, m_cf_dw_w, m_cf_dw_b, m_cf_ln_g, m_cf_ln_b, m_w_cf_o, m_w_o, m_ffn2_w_gu, m_ffn2_w_down, v_norm_g, v_ffn1_w_gu, v_ffn1_w_down, v_w_in, v_w_ret_o, v_sc_conv_w, v_w_sc_o, v_cf_dw_w, v_cf_dw_b, v_cf_ln_g, v_cf_ln_b, v_w_cf_o, v_w_o, v_ffn2_w_gu, v_ffn2_w_down):
    B, S, D = x.shape
    T = B * S
    L = norm_g.shape[0]
    DL = norm_g.shape[2]
    dk = D // 8
    mx, my, mc = _mesh_pos()
    dev_idx = jnp.reshape(4 * mx + 2 * my + mc, (1,)).astype(jnp.int32)
    core_idx = jnp.reshape(mc, (1,)).astype(jnp.int32)

    w32 = dict(ffn1_w_gu=ffn1_w_gu, ffn1_w_down=ffn1_w_down, w_in=w_in, w_ret_o=w_ret_o, w_sc_o=w_sc_o, w_cf_o=w_cf_o,
               w_o=w_o, ffn2_w_gu=ffn2_w_gu, ffn2_w_down=ffn2_w_down)
    m32 = dict(ffn1_w_gu=m_ffn1_w_gu, ffn1_w_down=m_ffn1_w_down, w_in=m_w_in, w_ret_o=m_w_ret_o, w_sc_o=m_w_sc_o,
               w_cf_o=m_w_cf_o, w_o=m_w_o, ffn2_w_gu=m_ffn2_w_gu, ffn2_w_down=m_ffn2_w_down)
    v32 = dict(ffn1_w_gu=v_ffn1_w_gu, ffn1_w_down=v_ffn1_w_down, w_in=v_w_in, w_ret_o=v_w_ret_o, w_sc_o=v_w_sc_o,
               w_cf_o=v_w_cf_o, w_o=v_w_o, ffn2_w_gu=v_ffn2_w_gu, ffn2_w_down=v_ffn2_w_down)

    W = [dict() for _ in range(L)]

    def gather(l, names):
        return _allgather_job([w32[n][l].astype(BF16) for n in names])

    def place(l, names, slabs):
        for n, a in zip(names, slabs):
            W[l][n] = a.reshape(a.shape[0] * a.shape[1], a.shape[2]) if n in ROW_BLOCKS else a

    def carried(fn, *args, jobs, **kw):
        out = fn(*args, jobs=jobs, **kw)
        return out if jobs else (out, [])

    n_sh = 6 + SC_KERNEL + CF_KERNEL
    small_sh = jnp.concatenate([norm_g, sc_conv_w, cf_dw_w], axis=1).reshape(L * n_sh, DL)
    sh_all = _allgather_small(small_sh, "allgather_small_params")
    sh_full = jnp.transpose(sh_all, (1, 0, 2)).reshape(L, n_sh, D)
    norm_full = sh_full[:, :6]
    scw_full = sh_full[:, 6:6 + SC_KERNEL]
    cfw_full = sh_full[:, 6 + SC_KERNEL:]
    place(0, FFN1, _run_jobs([gather(0, FFN1)], "allgather_first")[0])

    half = dk // 2
    inv_freq = ROPE_BASE ** (-jnp.arange(half, dtype=F32) / half)
    invf = jnp.concatenate([inv_freq, inv_freq])[None, :]
    sgn = jnp.concatenate([-jnp.ones((half,), F32), jnp.ones((half,), F32)])[None, :]
    cosf, sinf = _rope_tables(positions.reshape(T, 1), invf, sgn, "rope_tables")
    lgs = jnp.log(1.0 - 2.0 ** (-5.0 - jnp.arange(HEADS, dtype=F32)))

    def vec(a):
        return a.reshape(1, D)

    xc = x.reshape(T, D)
    saved = []
    for l in range(L):
        g = norm_full[l]
        sv = {}
        sv["x0"] = xc
        (xc, sv["h1"], sv["gate1"], sv["up1"], sv["f1"]), jr = _ffn_fwd(
            xc, vec(g[0]), vec(g[1]), W[l]["ffn1_w_gu"], W[l]["ffn1_w_down"], f"ffn1_fwd_{l}", jobs=[gather(l, ("w_in",))])
        place(l, ("w_in",), jr[0])
        sv["x1"] = xc
        (proj, sv["h2"]), jr = _norm_proj(xc, vec(g[2]), W[l]["w_in"], f"proj_fwd_{l}", jobs=[gather(l, MIX_OUT + ("ffn2_w_gu",))])
        place(l, MIX_OUT + ("ffn2_w_gu",), jr[0])
        sv["proj"] = proj
        o, jr = _ret_fwd(proj, cosf, sinf, lgs, B, S, D, f"retention_fwd_{l}", jobs=[gather(l, ("ffn2_w_down",))])
        place(l, ("ffn2_w_down",), jr[0])
        sv["o"] = o
        nxt = l + 1 < L
        (ya_in, yb_in, yc_in, sv["cv"], sv["u1"]), jr = carried(
            _mixer_mid_fwd, proj, o, scw_full[l], cfw_full[l], vec(cf_dw_b[l]), vec(cf_ln_g[l]), vec(cf_ln_b[l]), S,
            f"mixer_mid_fwd_{l}", jobs=[gather(l + 1, ("ffn1_w_gu",))] if nxt else [])
        if nxt:
            place(l + 1, ("ffn1_w_gu",), jr[0])
        sv["ya_in"], sv["yb_in"], sv["yc_in"] = ya_in, yb_in, yc_in
        (xc, sv["ya"], sv["yb"], sv["yc"], sv["merged"], sv["m"]), jr = carried(
            _mixer_merge_fwd, xc, proj, ya_in, yb_in, yc_in, vec(g[3]), W[l]["w_ret_o"], W[l]["w_sc_o"], W[l]["w_cf_o"], W[l]["w_o"],
            f"mixer_merge_fwd_{l}", jobs=[gather(l + 1, ("ffn1_w_down",))] if nxt else [])
        if nxt:
            place(l + 1, ("ffn1_w_down",), jr[0])
        sv["x2"] = xc
        xc, sv["h3"], sv["gate2"], sv["up2"], sv["f2"] = _ffn_fwd(xc, vec(g[4]), vec(g[5]), W[l]["ffn2_w_gu"], W[l]["ffn2_w_down"], f"ffn2_fwd_{l}")
        saved.append(sv)

    dx, loss_part = _loss_grad(xc, loss_target.reshape(T, D), "loss")
    loss = lax.psum(loss_part[0, 0], ("x", "y", "c"))

    to_sibling, to_chips, reduced = [], [], {}

    def grad_ready(n, l, g):
        if g.ndim == 2:
            g = g.reshape(N_DEV, g.shape[0] // N_DEV, g.shape[1])
        to_sibling.append((n, l, g.reshape(N_CHIP, 2, g.shape[1], g.shape[2])))

    def take_jobs():
        a, b = list(to_sibling), list(to_chips)
        to_sibling.clear()
        to_chips.clear()
        jobs = ([_to_chips_job([p for _, _, p in b])] if b else []) + ([_to_sibling_job([g for _, _, g in a])] if a else [])
        return jobs, (a, b)

    def settle(taken, jres):
        a, b = taken
        if b:
            for (n, l, _), got in zip(b, jres[0]):
                reduced[(n, l)] = got
        if a:
            for (n, l, g), got in zip(a, jres[-1]):
                to_chips.append((n, l, _pair_sum(core_idx, g, got, f"pair_sum_{n}_{l}")))

    def carrier(fn, *args, **kw):
        jobs, taken = take_jobs()
        out, jres = carried(fn, *args, jobs=jobs, **kw)
        settle(taken, jres)
        return out

    small_rows = [None] * L
    cb_gu = W[0]["ffn1_w_gu"].shape[2]
    cb_in = W[0]["w_in"].shape[2]

    def ffn_grads(tag, wgu_name, wd_name, dxo, x_in, ht, gate, up, f, g_pre, g_post, l):
        dxi, df, dgate, dup, act, dgpre, dgpost = carrier(_ffn_bwd, dxo, x_in, f, gate, up, g_pre, g_post, W[l][wgu_name],
                                                          W[l][wd_name], f"{tag}_bwd_{l}")
        half = carrier(_mm_tn, ht, dgate, f"{tag}_dwg_{l}", slab_cols=cb_gu, n_slabs=N_DEV, a_is_transposed=True)
        grad_ready(wgu_name, l, carrier(_mm_tn, ht, dup, f"{tag}_dwu_{l}", slab_cols=cb_gu, n_slabs=N_DEV, first_slab=N_DEV // 2,
                                        into=half, a_is_transposed=True))
        grad_ready(wd_name, l, carrier(_mm_tn, act, df, f"{tag}_dwd_{l}"))
        return dxi, dgpre, dgpost

    for l in reversed(range(L)):
        sv = saved[l]
        g = norm_full[l]
        dx, dg4, dg5 = ffn_grads("ffn2", "ffn2_w_gu", "ffn2_w_down", dx, sv["x2"], sv["h3"], sv["gate2"], sv["up2"], sv["f2"],
                                 vec(g[4]), vec(g[5]), l)
        (dm, dya, dyb, dyc, d2, dgl, do, dcv, du1, dg3, dlng, dlnb, dcfb) = carrier(
            _mixer_bwd_a, dx, sv["m"], vec(g[3]), sv["ya"], sv["yb"], sv["yc"], sv["proj"], sv["o"], sv["cv"], sv["u1"],
            vec(cf_ln_g[l]), vec(cf_ln_b[l]), W[l]["w_ret_o"], W[l]["w_sc_o"], W[l]["w_cf_o"], W[l]["w_o"], f"mixer_bwd_a_{l}")
        grad_ready("w_o", l, _mm_tn(sv["merged"], dm, f"dw_o_{l}", tt_pref=2048))
        grad_ready("w_ret_o", l, _mm_tn(sv["ya_in"], dya, f"dw_ret_o_{l}", tt_pref=2048))
        grad_ready("w_sc_o", l, _mm_tn(sv["yb_in"], dyb, f"dw_sc_o_{l}", tt_pref=2048))
        grad_ready("w_cf_o", l, _mm_tn(sv["yc_in"], dyc, f"dw_cf_o_{l}", tt_pref=2048))
        d4, dscw, dcfw = carrier(_mixer_bwd_b, dcv, du1, sv["proj"], scw_full[l], cfw_full[l], S, f"mixer_bwd_b_{l}")
        dq, dkk, dvv = carrier(_ret_bwd, do, sv["proj"], cosf, sinf, lgs, B, S, D, f"retention_bwd_{l}")
        dproj = jnp.concatenate([dq, dkk, dvv, d2, d4, dgl], axis=1)
        grad_ready("w_in", l, carrier(_mm_tn, sv["h2"], dproj, f"dw_in_{l}", slab_cols=cb_in, tn_pref=cb_in, a_is_transposed=True))
        dx, dg2 = carrier(_proj_bwd, dproj, W[l]["w_in"], sv["x1"], vec(g[2]), dx, f"proj_bwd_{l}")
        dx, dg0, dg1 = ffn_grads("ffn1", "ffn1_w_gu", "ffn1_w_down", dx, sv["x0"], sv["h1"], sv["gate1"], sv["up1"], sv["f1"],
                                 vec(g[0]), vec(g[1]), l)
        small_rows[l] = (jnp.concatenate([dg0, dg1, dg2, dg3, dg4, dg5, dscw, dcfw], axis=0),
                         jnp.concatenate([dcfb, dlng, dlnb], axis=0))
    grad_x = dx.reshape(B, S, D)
    flush = 0
    while to_sibling or to_chips:
        jobs, taken = take_jobs()
        settle(taken, _run_jobs(jobs, f"grads_flush_{flush}"))
        flush += 1

    big_out = {}
    for n in BIG:
        res = None
        for l in reversed(range(L)):
            res = _final_adamw(reduced[(n, l)], w32[n], m32[n], v32[n], l, res, f"adamw_{n}_{l}")
        big_out[n] = res

    sh_part = jnp.concatenate([small_rows[l][0] for l in range(L)], axis=0)
    rep_part = jnp.concatenate([small_rows[l][1] for l in range(L)] + [jnp.zeros((8 - 3 * L % 8, D), F32)] * (1 if 3 * L % 8 else 0), axis=0)
    sh_g = _allgather_small(sh_part, "allgather_small_grads")
    rep_g = _allgather_small(rep_part, "allgather_replicated_grads")

    def pack_sh(a, b, c):
        return jnp.concatenate([a, b, c], axis=1).reshape(L * n_sh, DL)

    def pack_rep(a, b, c):
        rows = jnp.stack([a, b, c], axis=1).reshape(3 * L, D)
        return jnp.concatenate([rows, jnp.ones((rep_part.shape[0] - 3 * L, D), F32)], axis=0)

    sh_res = _small_adamw(dev_idx, sh_g, pack_sh(norm_g, sc_conv_w, cf_dw_w), pack_sh(m_norm_g, m_sc_conv_w, m_cf_dw_w),
                          pack_sh(v_norm_g, v_sc_conv_w, v_cf_dw_w), DL, "adamw_small_sharded")
    rep_res = _small_adamw(dev_idx, rep_g, pack_rep(cf_dw_b, cf_ln_g, cf_ln_b), pack_rep(m_cf_dw_b, m_cf_ln_g, m_cf_ln_b),
                           pack_rep(v_cf_dw_b, v_cf_ln_g, v_cf_ln_b), D, "adamw_small_replicated")

    def unpack_sh(a):
        a = a.reshape(L, n_sh, DL)
        return {"norm_g": a[:, :6], "sc_conv_w": a[:, 6:6 + SC_KERNEL], "cf_dw_w": a[:, 6 + SC_KERNEL:]}

    def unpack_rep(a):
        a = a[:3 * L].reshape(L, 3, D)
        return {"cf_dw_b": a[:, 0], "cf_ln_g": a[:, 1], "cf_ln_b": a[:, 2]}

    order = ("norm_g", "ffn1_w_gu", "ffn1_w_down", "w_in", "w_ret_o", "sc_conv_w", "w_sc_o", "cf_dw_w", "cf_dw_b", "cf_ln_g",
             "cf_ln_b", "w_cf_o", "w_o", "ffn2_w_gu", "ffn2_w_down")
    outs = []
    for kind in range(4):
        small = {**unpack_sh(sh_res[kind]), **unpack_rep(rep_res[kind])}
        outs += [big_out[n][kind] if n in big_out else small[n] for n in order]
    return (loss, grad_x, *outs)
```

```python
import functools

import jax
import jax.numpy as jnp
from jax import lax
from jax.experimental import pallas as pl
from jax.experimental.pallas import tpu as pltpu

F32 = jnp.float32
BF16 = jnp.bfloat16
MESH = pl.DeviceIdType.MESH
ANY = pl.BlockSpec(memory_space=pl.ANY)

N_DEV = 8
N_CHIP = 4
CHUNK = 64
HEADS = 4
ROPE_BASE = 10000.0
NORM_EPS = 1e-6
LN_EPS = 1e-5
SC_KERNEL = 3
CF_KERNEL = 31
HALO = 32
ADAM_LR = 0.001
ADAM_B1 = 0.9
ADAM_B2 = 0.999
ADAM_EPS = 1e-08
ADAM_WD = 0.01
ADAM_STEP = 10
VMEM_LIMIT_V7X = 56 * 1024 * 1024


class _Job:
    def __init__(self, ins, out_shapes, n_sems, n_local, start, finish):
        self.ins, self.out_shapes, self.n_sems, self.n_local = list(ins), list(out_shapes), n_sems, n_local
        self.start, self.finish = start, finish


def _pallas(body, *, name, grid, in_specs, out_specs, out_shape, scratch_shapes=(), aliases=None, nprefetch=0, jobs=()):
    extra = {}
    single = not isinstance(out_shape, (list, tuple))
    out_shape = [out_shape] if single else list(out_shape)
    out_specs = [out_specs] if single else list(out_specs)
    in_specs, scratch = list(in_specs), list(scratch_shapes)
    n_in, n_out, n_scr = len(in_specs), len(out_shape), len(scratch)
    for jb in jobs:
        in_specs += [ANY] * len(jb.ins)
        out_specs += [ANY] * len(jb.out_shapes)
        out_shape += jb.out_shapes
        scratch += [pltpu.SemaphoreType.DMA((jb.n_sems,)), pltpu.SemaphoreType.DMA((jb.n_sems,)),
                    pltpu.SemaphoreType.DMA((max(jb.n_local, 1),))]

    def with_jobs(*refs):
        pre, refs = refs[:nprefetch], refs[nprefetch:]
        ins, p = refs[:n_in], n_in
        jins = []
        for jb in jobs:
            jins.append(refs[p:p + len(jb.ins)])
            p += len(jb.ins)
        outs = refs[p:p + n_out]
        p += n_out
        jouts = []
        for jb in jobs:
            jouts.append(refs[p:p + len(jb.out_shapes)])
            p += len(jb.out_shapes)
        scr = refs[p:p + n_scr]
        p += n_scr
        pids = [pl.program_id(d) for d in range(len(grid))]
        first = functools.reduce(jnp.logical_and, [pid == 0 for pid in pids])
        last = functools.reduce(jnp.logical_and, [pid == g - 1 for pid, g in zip(pids, grid)])

        @pl.when(first)
        def _():
            for i, jb in enumerate(jobs):
                jb.start(jins[i], jouts[i], *refs[p + 3 * i:p + 3 * i + 3])

        body(*pre, *ins, *outs, *scr)

        @pl.when(last)
        def _():
            for i, jb in enumerate(jobs):
                jb.finish(jins[i], jouts[i], *refs[p + 3 * i:p + 3 * i + 3])

    params = pltpu.CompilerParams(dimension_semantics=("arbitrary",) * len(grid), vmem_limit_bytes=VMEM_LIMIT_V7X)
    spec = pltpu.PrefetchScalarGridSpec(num_scalar_prefetch=nprefetch, grid=grid, in_specs=in_specs,
                                        out_specs=out_specs, scratch_shapes=scratch)
    call = pl.pallas_call(with_jobs if jobs else body, name=name, grid_spec=spec, out_shape=out_shape, compiler_params=params,
                          input_output_aliases=aliases or {}, **extra)

    def run(*args):
        res = call(*args, *[a for jb in jobs for a in jb.ins])
        own = res[0] if single else list(res[:n_out])
        if not jobs:
            return own
        jres, p = [], n_out
        for jb in jobs:
            jres.append(list(res[p:p + len(jb.out_shapes)]))
            p += len(jb.out_shapes)
        return own, jres

    return run


def _comm_call(body, *, name, in_specs, out_specs, out_shape, scratch_shapes):
    extra = {}
    return pl.pallas_call(body, name=name, in_specs=in_specs, out_specs=out_specs, out_shape=out_shape,
                          scratch_shapes=scratch_shapes, **extra)


def _sds(shape, dtype):
    return jax.ShapeDtypeStruct(shape, dtype)


def _tile(n, pref):
    t = min(n, pref)
    assert n % t == 0, (n, pref)
    return t


def _sigmoid(x):
    return jax.nn.sigmoid(x)


def _rms_fwd(x, g):
    r = lax.rsqrt(jnp.mean(x * x, axis=-1, keepdims=True) + NORM_EPS)
    return x * r * g


def _rms_bwd(x, g, dy):
    r = lax.rsqrt(jnp.mean(x * x, axis=-1, keepdims=True) + NORM_EPS)
    xh = x * r
    dg = jnp.sum(dy * xh, axis=0, keepdims=True)
    dxh = dy * g
    dx = r * (dxh - xh * jnp.mean(dxh * xh, axis=-1, keepdims=True))
    return dx, dg


def _ln_stats(x):
    mu = jnp.mean(x, axis=-1, keepdims=True)
    xc = x - mu
    rstd = lax.rsqrt(jnp.mean(xc * xc, axis=-1, keepdims=True) + LN_EPS)
    return xc * rstd, rstd


def _ln_bwd(xh, rstd, dxh):
    return rstd * (dxh - jnp.mean(dxh, axis=-1, keepdims=True) - xh * jnp.mean(dxh * xh, axis=-1, keepdims=True))


def _dsilu(x, s):
    return s * (1.0 + x * (1.0 - s))


def _rot(x, cosf, sinf):
    return x * cosf + pltpu.roll(x, x.shape[-1] // 2, 1) * sinf


def _unrot(d, cosf, sinf):
    return d * cosf - pltpu.roll(d, d.shape[-1] // 2, 1) * sinf


def _dot_nn(a, b):
    return jnp.dot(a, b, preferred_element_type=F32)


def _dot_nt(a, b):
    return lax.dot_general(a, b, (((1,), (1,)), ((), ())), preferred_element_type=F32)


def _dot_tn(a, b):
    return lax.dot_general(a, b, (((0,), (0,)), ((), ())), preferred_element_type=F32)


FFN_CHUNK = 256


def _ffn_tf(cb):
    return _tile(cb, 1024)


def _ffn_fwd(x, g_pre, g_post, wgu, wd, name, jobs=()):
    T, D = x.shape
    F = wd.shape[0]
    cb = wgu.shape[2]
    tm, tf = _tile(T, 512), _ffn_tf(cb)
    nj, nb, sub = F // tf, cb // tf, _tile(tf, FFN_CHUNK)

    def body(x_ref, gpre_ref, gpost_ref, wg_ref, wu_ref, wd_ref, xo_ref, ht_ref, gate_ref, up_ref, f_ref, acc_ref, h_ref):
        j = pl.program_id(1)

        @pl.when(j == 0)
        def _():
            h = _rms_fwd(x_ref[...], gpre_ref[...])
            h_ref[...] = h.astype(BF16)
            ht_ref[...] = h.T.astype(BF16)
            acc_ref[...] = jnp.zeros_like(acc_ref)

        h = h_ref[...]
        part = None
        for c in range(tf // sub):
            cols = slice(c * sub, (c + 1) * sub)
            gate = _dot_nn(h, wg_ref[:, cols])
            up = _dot_nn(h, wu_ref[:, cols])
            gate_ref[:, cols] = gate.astype(BF16)
            up_ref[:, cols] = up.astype(BF16)
            act = (gate * _sigmoid(gate) * up).astype(BF16)
            down = _dot_nn(act, wd_ref[cols, :])
            part = down if part is None else part + down
        acc_ref[...] += part

        @pl.when(j == nj - 1)
        def _():
            f = acc_ref[...]
            f_ref[...] = f
            xo_ref[...] = x_ref[...] + 0.5 * _rms_fwd(f, gpost_ref[...])

    row = pl.BlockSpec((tm, D), lambda i, j: (i, 0))
    vec = pl.BlockSpec((1, D), lambda i, j: (0, 0))
    col = pl.BlockSpec((tm, tf), lambda i, j: (i, j))
    return _pallas(
        body, name=name, grid=(T // tm, nj),
        in_specs=[row, vec, vec,
                  pl.BlockSpec((None, D, tf), lambda i, j: (j // nb, 0, j % nb)),
                  pl.BlockSpec((None, D, tf), lambda i, j: ((nj + j) // nb, 0, j % nb)),
                  pl.BlockSpec((tf, D), lambda i, j: (j, 0))],
        out_specs=[row, pl.BlockSpec((D, tm), lambda i, j: (0, i)), col, col, row],
        out_shape=[_sds((T, D), F32), _sds((D, T), BF16), _sds((T, F), BF16), _sds((T, F), BF16), _sds((T, D), F32)],
        scratch_shapes=[pltpu.VMEM((tm, D), F32), pltpu.VMEM((tm, D), BF16)], jobs=jobs,
    )(x, g_pre, g_post, wgu, wgu, wd)


def _ffn_bwd(dxo, x, f, gate, up, g_pre, g_post, wgu, wd, name, jobs=()):
    T, D = x.shape
    F = wd.shape[0]
    cb = wgu.shape[2]
    tm, tf = _tile(T, 512), _ffn_tf(cb)
    nj, nb, sub = F // tf, cb // tf, _tile(tf, FFN_CHUNK)

    def body(dxo_ref, x_ref, f_ref, gate_ref, up_ref, gpre_ref, gpost_ref, wg_ref, wu_ref, wd_ref,
             dx_ref, df_ref, dgate_ref, dup_ref, act_ref, dgpre_ref, dgpost_ref, acc_ref):
        i, j = pl.program_id(0), pl.program_id(1)

        @pl.when((i == 0) & (j == 0))
        def _():
            dgpre_ref[...] = jnp.zeros_like(dgpre_ref)
            dgpost_ref[...] = jnp.zeros_like(dgpost_ref)

        @pl.when(j == 0)
        def _():
            df, dgp = _rms_bwd(f_ref[...], gpost_ref[...], 0.5 * dxo_ref[...])
            df_ref[...] = df.astype(BF16)
            dgpost_ref[...] += dgp
            acc_ref[...] = jnp.zeros_like(acc_ref)

        df = df_ref[...]
        part = None
        for c in range(tf // sub):
            cols = slice(c * sub, (c + 1) * sub)
            dact = _dot_nt(df, wd_ref[cols, :])
            g = gate_ref[:, cols].astype(F32)
            u = up_ref[:, cols].astype(F32)
            s = _sigmoid(g)
            silu = g * s
            dgate = (dact * u * _dsilu(g, s)).astype(BF16)
            dup = (dact * silu).astype(BF16)
            act_ref[:, cols] = (silu * u).astype(BF16)
            dgate_ref[:, cols] = dgate
            dup_ref[:, cols] = dup
            dh = _dot_nt(dgate, wg_ref[:, cols]) + _dot_nt(dup, wu_ref[:, cols])
            part = dh if part is None else part + dh
        acc_ref[...] += part

        @pl.when(j == nj - 1)
        def _():
            dxin, dgp = _rms_bwd(x_ref[...], gpre_ref[...], acc_ref[...])
            dx_ref[...] = dxo_ref[...] + dxin
            dgpre_ref[...] += dgp

    row = pl.BlockSpec((tm, D), lambda i, j: (i, 0))
    vec = pl.BlockSpec((1, D), lambda i, j: (0, 0))
    col = pl.BlockSpec((tm, tf), lambda i, j: (i, j))
    return _pallas(
        body, name=name, grid=(T // tm, nj),
        in_specs=[row, row, row, col, col, vec, vec,
                  pl.BlockSpec((None, D, tf), lambda i, j: (j // nb, 0, j % nb)),
                  pl.BlockSpec((None, D, tf), lambda i, j: ((nj + j) // nb, 0, j % nb)),
                  pl.BlockSpec((tf, D), lambda i, j: (j, 0))],
        out_specs=[row, row, col, col, col, vec, vec],
        out_shape=[_sds((T, D), F32), _sds((T, D), BF16), _sds((T, F), BF16), _sds((T, F), BF16), _sds((T, F), BF16),
                   _sds((1, D), F32), _sds((1, D), F32)],
        scratch_shapes=[pltpu.VMEM((tm, D), F32)], jobs=jobs,
    )(dxo, x, f, gate, up, g_pre, g_post, wgu, wgu, wd)


def _mm_tn(a, b, name, slab_cols=None, n_slabs=None, first_slab=0, into=None, tn_pref=1024, tt_pref=1024,
           a_is_transposed=False, jobs=()):
    T, N = b.shape
    K = a.shape[0] if a_is_transposed else a.shape[1]
    tt = _tile(T, tt_pref)
    tko = _tile(K, 1024)
    tn = _tile(slab_cols or N, tn_pref)
    nt = T // tt

    def body(a_ref, b_ref, *rest):
        o_ref, acc_ref = rest[-2:]
        t = pl.program_id(2)

        @pl.when(t == 0)
        def _():
            acc_ref[...] = jnp.zeros_like(acc_ref)

        acc_ref[...] += (_dot_nn if a_is_transposed else _dot_tn)(a_ref[...], b_ref[...])

        @pl.when(t == nt - 1)
        def _():
            o_ref[...] = acc_ref[...].astype(o_ref.dtype)

    if slab_cols is None:
        shape = (K, N)
        ospec = pl.BlockSpec((tko, tn), lambda k, jn, t: (k, jn))
    else:
        nb = slab_cols // tn
        shape = (n_slabs or N // slab_cols, K, slab_cols)
        ospec = pl.BlockSpec((None, tko, tn), lambda k, jn, t: (first_slab + jn // nb, k, jn % nb))
    aspec = pl.BlockSpec((tko, tt), lambda k, jn, t: (k, t)) if a_is_transposed else pl.BlockSpec((tt, tko), lambda k, jn, t: (t, k))
    in_specs, args = [aspec, pl.BlockSpec((tt, tn), lambda k, jn, t: (t, jn))], [a, b]
    if into is not None:
        in_specs.append(ANY)
        args.append(into)
    return _pallas(body, name=name, grid=(K // tko, N // tn, nt), in_specs=in_specs, out_specs=ospec, out_shape=_sds(shape, BF16),
                   scratch_shapes=[pltpu.VMEM((tko, tn), F32)], aliases={2: 0} if into is not None else None, jobs=jobs)(*args)


def _norm_proj(x, g, w, name, jobs=()):
    T, D = x.shape
    nd, cb = w.shape[0], w.shape[2]
    tm = _tile(T, 1024)

    def body(x_ref, g_ref, w_ref, p_ref, ht_ref, h_ref):
        @pl.when(pl.program_id(1) == 0)
        def _():
            h = _rms_fwd(x_ref[...], g_ref[...])
            h_ref[...] = h.astype(BF16)
            ht_ref[...] = h.T.astype(BF16)

        p_ref[...] = _dot_nn(h_ref[...], w_ref[...])

    row = pl.BlockSpec((tm, D), lambda i, j: (i, 0))
    return _pallas(
        body, name=name, grid=(T // tm, nd),
        in_specs=[row, pl.BlockSpec((1, D), lambda i, j: (0, 0)),
                  pl.BlockSpec((None, D, cb), lambda i, j: (j, 0, 0))],
        out_specs=[pl.BlockSpec((tm, cb), lambda i, j: (i, j)), pl.BlockSpec((D, tm), lambda i, j: (0, i))],
        out_shape=[_sds((T, nd * cb), F32), _sds((D, T), BF16)],
        scratch_shapes=[pltpu.VMEM((tm, D), BF16)], jobs=jobs,
    )(x, g, w)


def _proj_bwd(dproj, w, x, g, dxo, name, jobs=()):
    T, D = x.shape
    nd, cb = w.shape[0], w.shape[2]
    tm = _tile(T, 1024)

    def body(dp_ref, w_ref, x_ref, g_ref, dxo_ref, dx_ref, dg_ref, acc_ref):
        i, j = pl.program_id(0), pl.program_id(1)

        @pl.when((i == 0) & (j == 0))
        def _():
            dg_ref[...] = jnp.zeros_like(dg_ref)

        @pl.when(j == 0)
        def _():
            acc_ref[...] = jnp.zeros_like(acc_ref)

        acc_ref[...] += _dot_nt(dp_ref[...], w_ref[...])

        @pl.when(j == nd - 1)
        def _():
            dxin, dgp = _rms_bwd(x_ref[...], g_ref[...], acc_ref[...])
            dx_ref[...] = dxo_ref[...] + dxin
            dg_ref[...] += dgp

    row = pl.BlockSpec((tm, D), lambda i, j: (i, 0))
    vec = pl.BlockSpec((1, D), lambda i, j: (0, 0))
    return _pallas(
        body, name=name, grid=(T // tm, nd),
        in_specs=[pl.BlockSpec((tm, cb), lambda i, j: (i, j)),
                  pl.BlockSpec((None, D, cb), lambda i, j: (j, 0, 0)), row, vec, row],
        out_specs=[row, vec],
        out_shape=[_sds((T, D), F32), _sds((1, D), F32)],
        scratch_shapes=[pltpu.VMEM((tm, D), F32)], jobs=jobs,
    )(dproj, w, x, g, dxo)


def _rope_tables(pos, invf, sgn, name):
    T = pos.shape[0]
    dk = invf.shape[1]
    tm = _tile(T, 1024)

    def body(p_ref, f_ref, s_ref, c_out, s_out):
        ang = p_ref[...].astype(F32) * f_ref[...]
        c_out[...] = jnp.cos(ang)
        s_out[...] = jnp.sin(ang) * s_ref[...]

    vec = pl.BlockSpec((1, dk), lambda i: (0, 0))
    out = pl.BlockSpec((tm, dk), lambda i: (i, 0))
    return _pallas(body, name=name, grid=(T // tm,), in_specs=[pl.BlockSpec((tm, 1), lambda i: (i, 0)), vec, vec],
                   out_specs=[out, out], out_shape=[_sds((T, dk), F32), _sds((T, dk), F32)])(pos, invf, sgn)


def _decay(lg, r0, tq, n):
    r = r0 + lax.broadcasted_iota(jnp.int32, (tq, n), 0)
    c = lax.broadcasted_iota(jnp.int32, (tq, n), 1)
    rc, cc = r // CHUNK, c // CHUNK
    d = (r - c).astype(F32)
    e = jnp.where(rc == cc, jnp.abs(d), d)
    return jnp.where(cc > rc, 0.0, jnp.exp(lg * e))


def _ret_fwd(proj, cosf, sinf, lgs, B, S, D, name, jobs=()):
    T = B * S
    dk, dv = D // 8, D // 4
    tq = _tile(S, 256)
    scale = dk ** -0.5

    def body(lg_ref, q_ref, k_ref, v_ref, c_ref, s_ref, o_ref, kr_ref, vb_ref):
        lg = lg_ref[pl.program_id(1)]
        kr_ref[...] = (_rot(k_ref[...], c_ref[...], s_ref[...]) * scale).astype(BF16)
        vb_ref[...] = v_ref[...].astype(BF16)
        for qi in range(S // tq):
            rows, n = slice(qi * tq, (qi + 1) * tq), (qi + 1) * tq
            q = _rot(q_ref[rows, :], c_ref[rows, :], s_ref[rows, :]).astype(BF16)
            p = (_dot_nt(q, kr_ref[:n, :]) * _decay(lg, qi * tq, tq, n)).astype(BF16)
            o_ref[rows, :] = _dot_nn(p, vb_ref[:n, :])

    return _pallas(
        body, name=name, grid=(B, HEADS),
        in_specs=[pl.BlockSpec(memory_space=pltpu.SMEM),
                  pl.BlockSpec((S, dk), lambda b, h: (b, h)),
                  pl.BlockSpec((S, dk), lambda b, h: (b, HEADS + h)),
                  pl.BlockSpec((S, dv), lambda b, h: (b, HEADS + h)),
                  pl.BlockSpec((S, dk), lambda b, h: (b, 0)),
                  pl.BlockSpec((S, dk), lambda b, h: (b, 0))],
        out_specs=pl.BlockSpec((S, dv), lambda b, h: (b, h)),
        out_shape=_sds((T, D), F32),
        scratch_shapes=[pltpu.VMEM((S, dk), BF16), pltpu.VMEM((S, dv), BF16)], jobs=jobs,
    )(lgs, proj, proj, proj, cosf, sinf)


def _ret_bwd(do, proj, cosf, sinf, lgs, B, S, D, name, jobs=()):
    T = B * S
    dk, dv = D // 8, D // 4
    tq = _tile(S, 256)
    scale = dk ** -0.5

    def body(lg_ref, do_ref, q_ref, k_ref, v_ref, c_ref, s_ref, dq_ref, dk_ref, dv_ref, kr_ref, vb_ref, dka_ref, dva_ref):
        h = pl.program_id(1)
        lg = lg_ref[h]
        kr_ref[...] = (_rot(k_ref[...], c_ref[...], s_ref[...]) * scale).astype(BF16)
        vb_ref[...] = v_ref[...].astype(BF16)
        dka_ref[...] = jnp.zeros_like(dka_ref)
        dva_ref[...] = jnp.zeros_like(dva_ref)

        for qi in range(S // tq):
            rows, n = slice(qi * tq, (qi + 1) * tq), (qi + 1) * tq
            cq, sq = c_ref[rows, :], s_ref[rows, :]
            q = _rot(q_ref[rows, :], cq, sq).astype(BF16)
            dout = do_ref[rows, :].astype(BF16)
            w = _decay(lg, qi * tq, tq, n)
            p = (_dot_nt(q, kr_ref[:n, :]) * w).astype(BF16)
            ds = (_dot_nt(dout, vb_ref[:n, :]) * w).astype(BF16)
            dq_ref[rows, :] = _unrot(_dot_nn(ds, kr_ref[:n, :]), cq, sq).astype(BF16)
            dka_ref[:n, :] += _dot_tn(ds, q)
            dva_ref[:n, :] += _dot_tn(p, dout)

        dk_ref[...] = _unrot(dka_ref[...] * scale, c_ref[...], s_ref[...]).astype(BF16)
        dv_ref[...] = dva_ref[...].astype(BF16)

    return _pallas(
        body, name=name, grid=(B, HEADS),
        in_specs=[pl.BlockSpec(memory_space=pltpu.SMEM),
                  pl.BlockSpec((S, dv), lambda b, h: (b, h)),
                  pl.BlockSpec((S, dk), lambda b, h: (b, h)),
                  pl.BlockSpec((S, dk), lambda b, h: (b, HEADS + h)),
                  pl.BlockSpec((S, dv), lambda b, h: (b, HEADS + h)),
                  pl.BlockSpec((S, dk), lambda b, h: (b, 0)),
                  pl.BlockSpec((S, dk), lambda b, h: (b, 0))],
        out_specs=[pl.BlockSpec((S, dk), lambda b, h: (b, h)), pl.BlockSpec((S, dk), lambda b, h: (b, h)),
                   pl.BlockSpec((S, dv), lambda b, h: (b, h))],
        out_shape=[_sds((T, D // 2), BF16), _sds((T, D // 2), BF16), _sds((T, D), BF16)],
        scratch_shapes=[pltpu.VMEM((S, dk), BF16), pltpu.VMEM((S, dv), BF16), pltpu.VMEM((S, dk), F32), pltpu.VMEM((S, dv), F32)], jobs=jobs,
    )(lgs, do, proj, proj, proj, cosf, sinf)


COL_GRET, COL_SCB, COL_SCC, COL_SCX, COL_GLUA, COL_GLUB, COL_GL = 2, 3, 4, 5, 6, 7, 8


SUBLANES, LANES = 8, 128
CONV_BLOCK = 256
SC_FWD = tuple(HALO - (SC_KERNEL - 1) + j for j in range(SC_KERNEL))
CF_FWD = tuple(HALO - (CF_KERNEL - 1) + j for j in range(CF_KERNEL))
SC_BWD = tuple(SC_KERNEL - 1 - j for j in range(SC_KERNEL))
CF_BWD = tuple(CF_KERNEL - 1 - j for j in range(CF_KERNEL))


def _build_shifted(src_ref, e_ref, shifts):
    n = e_ref.shape[1]
    for b in sorted({s % SUBLANES for s in shifts} - {0}):
        e_ref[b - 1] = src_ref[pl.ds(b, n), :]


def _shifted_rows(src_ref, e_ref, s, r0, rows, lanes):
    a, b = divmod(s, SUBLANES)
    at = pl.ds(r0 + SUBLANES * a, rows)
    return src_ref[at, lanes] if b == 0 else e_ref[b - 1, at, lanes]


def _tree_sum(terms):
    while len(terms) > 1:
        terms = [a + b for a, b in zip(terms[::2], terms[1::2])] + ([terms[-1]] if len(terms) % 2 else [])
    return terms[0]


def _conv_taps(src_ref, e_ref, w_ref, shifts, tm, D, emit, u_ref=None, dw_ref=None):
    rows = min(tm, CONV_BLOCK if u_ref is None else CONV_BLOCK // 2)
    for lt in range(D // LANES):
        lanes = pl.ds(lt * LANES, LANES)
        for r0 in range(0, tm, rows):
            u = None if u_ref is None else u_ref[pl.ds(r0, rows), lanes]
            acc = None
            for j, s in enumerate(shifts):
                v = _shifted_rows(src_ref, e_ref, s, r0, rows, lanes)
                term = v * w_ref[j:j + 1, lanes]
                acc = term if acc is None else acc + term
                if u is not None:
                    prod = u * v
                    part = _tree_sum([prod[g * SUBLANES:(g + 1) * SUBLANES, :] for g in range(rows // SUBLANES)])
                    dw_ref[j:j + 1, lanes] += jnp.sum(part, axis=0, keepdims=True)
            emit(lt, pl.ds(r0, rows), acc)


def _mixer_mid_fwd(proj, o, sc_w, cf_w, cf_b, ln_g, ln_b, S, name, jobs=()):
    T, D = o.shape
    dv = D // HEADS
    tm = _tile(S, 256)
    per_seq = S // tm
    hb = tm // HALO

    def body(gret_ref, scb_ref, scc_ref, scx_ref, ga_ref, gb_ref, scc_h, scx_h, ga_h, gb_h, o_ref,
             scw_ref, cfw_ref, cfb_ref, lng_ref, lnb_ref, ya_ref, yb_ref, yc_ref, cv_ref, u1_ref, ext_ref, e_ref):
        first = (pl.program_id(0) % per_seq) == 0
        keep = jnp.where(first, 0.0, 1.0)
        gr = gret_ref[...]
        sg = gr * _sigmoid(gr)
        for hh in range(HEADS):
            cols = slice(hh * dv, (hh + 1) * dv)
            hn, _ = _ln_stats(o_ref[:, cols])
            ya_ref[:, cols] = (sg[:, cols] * hn).astype(BF16)
        ext_ref[:HALO, :] = scc_h[...] * scx_h[...] * keep
        ext_ref[HALO:, :] = scc_ref[...] * scx_ref[...]
        _build_shifted(ext_ref, e_ref, SC_FWD)

        def emit_cv(lt, rows, cv):
            lanes = pl.ds(lt * LANES, LANES)
            cv_ref[rows, lanes] = cv
            yb_ref[rows, lanes] = (scb_ref[rows, lanes] * cv).astype(BF16)

        _conv_taps(ext_ref, e_ref, scw_ref, SC_FWD, tm, D, emit_cv)
        ext_ref[:HALO, :] = ga_h[...] * _sigmoid(gb_h[...]) * keep
        ext_ref[HALO:, :] = ga_ref[...] * _sigmoid(gb_ref[...])
        _build_shifted(ext_ref, e_ref, CF_FWD)

        def emit_u1(lt, rows, acc):
            lanes = pl.ds(lt * LANES, LANES)
            u1_ref[rows, lanes] = acc + cfb_ref[:, lanes]

        _conv_taps(ext_ref, e_ref, cfw_ref, CF_FWD, tm, D, emit_u1)
        xh, _ = _ln_stats(u1_ref[...])
        u2 = xh * lng_ref[...] + lnb_ref[...]
        yc_ref[...] = (u2 * _sigmoid(u2)).astype(BF16)

    def colblk(c):
        return pl.BlockSpec((tm, D), lambda i: (i, c))

    def halo(c):
        return pl.BlockSpec((HALO, D), lambda i: (jnp.maximum(i * hb - 1, 0), c))

    row = pl.BlockSpec((tm, D), lambda i: (i, 0))
    vec = pl.BlockSpec((1, D), lambda i: (0, 0))
    return _pallas(
        body, name=name, grid=(T // tm,),
        in_specs=[colblk(COL_GRET), colblk(COL_SCB), colblk(COL_SCC), colblk(COL_SCX), colblk(COL_GLUA), colblk(COL_GLUB),
                  halo(COL_SCC), halo(COL_SCX), halo(COL_GLUA), halo(COL_GLUB), row,
                  pl.BlockSpec((SC_KERNEL, D), lambda i: (0, 0)), pl.BlockSpec((CF_KERNEL, D), lambda i: (0, 0)), vec, vec, vec],
        out_specs=[row, row, row, row, row],
        out_shape=[_sds((T, D), BF16)] * 3 + [_sds((T, D), F32)] * 2,
        scratch_shapes=[pltpu.VMEM((HALO + tm, D), F32), pltpu.VMEM((SUBLANES - 1, HALO + tm - SUBLANES, D), F32)], jobs=jobs,
    )(proj, proj, proj, proj, proj, proj, proj, proj, proj, proj, o, sc_w, cf_w, cf_b, ln_g, ln_b)


def _mixer_merge_fwd(x, proj, ya_in, yb_in, yc_in, g_post, w_ret, w_sc, w_cf, w_o, name, jobs=()):
    T, D = x.shape
    tm = _tile(T, 256)

    def body(x_ref, g0_ref, g1_ref, g2_ref, ya_in_ref, yb_in_ref, yc_in_ref, gp_ref, wr_ref, ws_ref, wc_ref, wo_ref,
             xo_ref, ya_ref, yb_ref, yc_ref, mg_ref, m_ref):
        ya = _dot_nn(ya_in_ref[...], wr_ref[...])
        yb = _dot_nn(yb_in_ref[...], ws_ref[...])
        yc = _dot_nn(yc_in_ref[...], wc_ref[...])
        ya_ref[...] = ya.astype(BF16)
        yb_ref[...] = yb.astype(BF16)
        yc_ref[...] = yc.astype(BF16)
        merged = (_sigmoid(g0_ref[...]) * ya + _sigmoid(g1_ref[...]) * yb + _sigmoid(g2_ref[...]) * yc).astype(BF16)
        mg_ref[...] = merged
        m = _dot_nn(merged, wo_ref[...])
        m_ref[...] = m
        xo_ref[...] = x_ref[...] + _rms_fwd(m, gp_ref[...])

    row = pl.BlockSpec((tm, D), lambda i: (i, 0))
    wsp = pl.BlockSpec((D, D), lambda i: (0, 0), pipeline_mode=pl.Buffered(1))

    def colblk(c):
        return pl.BlockSpec((tm, D), lambda i: (i, c))

    return _pallas(
        body, name=name, grid=(T // tm,),
        in_specs=[row, colblk(COL_GL), colblk(COL_GL + 1), colblk(COL_GL + 2), row, row, row,
                  pl.BlockSpec((1, D), lambda i: (0, 0)), wsp, wsp, wsp, wsp],
        out_specs=[row] * 6,
        out_shape=[_sds((T, D), F32)] + [_sds((T, D), BF16)] * 4 + [_sds((T, D), F32)], jobs=jobs,
    )(x, proj, proj, proj, ya_in, yb_in, yc_in, g_post, w_ret, w_sc, w_cf, w_o)


def _mixer_bwd_a(dxo, m, g_post, ya, yb, yc, proj, o, cv, u1, ln_g, ln_b, w_ret, w_sc, w_cf, w_o, name, jobs=()):
    T, D = m.shape
    dv = D // HEADS
    tm = _tile(T, 256)

    def body(dxo_ref, m_ref, gp_ref, ya_ref, yb_ref, yc_ref, g0_ref, g1_ref, g2_ref, gret_ref, scb_ref, o_ref, cv_ref, u1_ref,
             lng_ref, lnb_ref, wr_ref, ws_ref, wc_ref, wo_ref,
             dm_ref, dya_ref, dyb_ref, dyc_ref, d2_ref, dgl_ref, do_ref, dcv_ref, du1_ref, dgp_ref, dlng_ref, dlnb_ref, dcfb_ref):
        @pl.when(pl.program_id(0) == 0)
        def _():
            for r in (dgp_ref, dlng_ref, dlnb_ref, dcfb_ref):
                r[...] = jnp.zeros_like(r)

        dm, dgp = _rms_bwd(m_ref[...], gp_ref[...], dxo_ref[...])
        dgp_ref[...] += dgp
        dmb = dm.astype(BF16)
        dm_ref[...] = dmb
        dmerged = _dot_nt(dmb, wo_ref[...])
        dys = []
        for k, (g_ref, y_ref, dy_ref) in enumerate(((g0_ref, ya_ref, dya_ref), (g1_ref, yb_ref, dyb_ref), (g2_ref, yc_ref, dyc_ref))):
            sg = _sigmoid(g_ref[...])
            dgl_ref[:, k * D:(k + 1) * D] = (dmerged * y_ref[...].astype(F32) * sg * (1.0 - sg)).astype(BF16)
            dy = (dmerged * sg).astype(BF16)
            dy_ref[...] = dy
            dys.append(dy)
        dya_in = _dot_nt(dys[0], wr_ref[...])
        gr = gret_ref[...]
        sr = _sigmoid(gr)
        for hh in range(HEADS):
            cols = slice(hh * dv, (hh + 1) * dv)
            hn, rstd = _ln_stats(o_ref[:, cols])
            d2_ref[:, cols] = (dya_in[:, cols] * hn * _dsilu(gr[:, cols], sr[:, cols])).astype(BF16)
            do_ref[:, cols] = _ln_bwd(hn, rstd, dya_in[:, cols] * gr[:, cols] * sr[:, cols])
        dyb_in = _dot_nt(dys[1], ws_ref[...])
        d2_ref[:, D:] = (dyb_in * cv_ref[...]).astype(BF16)
        dcv_ref[...] = dyb_in * scb_ref[...]
        dyc_in = _dot_nt(dys[2], wc_ref[...])
        xh, rstd = _ln_stats(u1_ref[...])
        u2 = xh * lng_ref[...] + lnb_ref[...]
        du2 = dyc_in * _dsilu(u2, _sigmoid(u2))
        dlng_ref[...] += jnp.sum(du2 * xh, axis=0, keepdims=True)
        dlnb_ref[...] += jnp.sum(du2, axis=0, keepdims=True)
        du1 = _ln_bwd(xh, rstd, du2 * lng_ref[...])
        du1_ref[...] = du1
        dcfb_ref[...] += jnp.sum(du1, axis=0, keepdims=True)

    row = pl.BlockSpec((tm, D), lambda i: (i, 0))
    vec = pl.BlockSpec((1, D), lambda i: (0, 0))
    wsp = pl.BlockSpec((D, D), lambda i: (0, 0), pipeline_mode=pl.Buffered(1))

    def colblk(c):
        return pl.BlockSpec((tm, D), lambda i: (i, c))

    return _pallas(
        body, name=name, grid=(T // tm,),
        in_specs=[row, row, vec, row, row, row, colblk(COL_GL), colblk(COL_GL + 1), colblk(COL_GL + 2),
                  colblk(COL_GRET), colblk(COL_SCB), row, row, row, vec, vec, wsp, wsp, wsp, wsp],
        out_specs=[row, row, row, row, pl.BlockSpec((tm, 2 * D), lambda i: (i, 0)), pl.BlockSpec((tm, 3 * D), lambda i: (i, 0)),
                   row, row, row, vec, vec, vec, vec],
        out_shape=[_sds((T, D), BF16)] * 4 + [_sds((T, 2 * D), BF16), _sds((T, 3 * D), BF16)] + [_sds((T, D), F32)] * 3
                  + [_sds((1, D), F32)] * 4, jobs=jobs,
    )(dxo, m, g_post, ya, yb, yc, proj, proj, proj, proj, proj, o, cv, u1, ln_g, ln_b, w_ret, w_sc, w_cf, w_o)


def _mixer_bwd_b(dcv, du1, proj, sc_w, cf_w, S, name, jobs=()):
    T, D = dcv.shape
    tm = _tile(S, 256)
    per_seq = S // tm
    hb = tm // HALO
    last_hb = T // HALO - 1

    def body(dcv_ref, du1_ref, dcv_n, du1_n, scc_ref, scx_ref, ga_ref, gb_ref, scw_ref, cfw_ref,
             d4_ref, dscw_ref, dcfw_ref, u_ref, dext_ref, e_ref, sb_ref):
        i = pl.program_id(0)
        keep_next = jnp.where((i % per_seq) == per_seq - 1, 0.0, 1.0)

        @pl.when(i == 0)
        def _():
            dscw_ref[...] = jnp.zeros_like(dscw_ref)
            dcfw_ref[...] = jnp.zeros_like(dcfw_ref)

        dext_ref[:tm, :] = dcv_ref[...]
        dext_ref[tm:, :] = dcv_n[...] * keep_next
        _build_shifted(dext_ref, e_ref, SC_BWD)
        u_ref[...] = scc_ref[...] * scx_ref[...]

        def emit_dz(lt, rows, dz):
            lanes = pl.ds(lt * LANES, LANES)
            d4_ref[rows, pl.ds(lt * LANES, LANES)] = (dz * scx_ref[rows, lanes]).astype(BF16)
            d4_ref[rows, pl.ds(D + lt * LANES, LANES)] = (dz * scc_ref[rows, lanes]).astype(BF16)

        _conv_taps(dext_ref, e_ref, scw_ref, SC_BWD, tm, D, emit_dz, u_ref=u_ref, dw_ref=dscw_ref)
        dext_ref[:tm, :] = du1_ref[...]
        dext_ref[tm:, :] = du1_n[...] * keep_next
        _build_shifted(dext_ref, e_ref, CF_BWD)
        sb_ref[...] = _sigmoid(gb_ref[...])
        u_ref[...] = ga_ref[...] * sb_ref[...]

        def emit_du0(lt, rows, du0):
            lanes = pl.ds(lt * LANES, LANES)
            sb = sb_ref[rows, lanes]
            d4_ref[rows, pl.ds(2 * D + lt * LANES, LANES)] = (du0 * sb).astype(BF16)
            d4_ref[rows, pl.ds(3 * D + lt * LANES, LANES)] = (du0 * ga_ref[rows, lanes] * sb * (1.0 - sb)).astype(BF16)

        _conv_taps(dext_ref, e_ref, cfw_ref, CF_BWD, tm, D, emit_du0, u_ref=u_ref, dw_ref=dcfw_ref)

    row = pl.BlockSpec((tm, D), lambda i: (i, 0))
    nxt = pl.BlockSpec((HALO, D), lambda i: (jnp.minimum((i + 1) * hb, last_hb), 0))

    def colblk(c):
        return pl.BlockSpec((tm, D), lambda i: (i, c))

    return _pallas(
        body, name=name, grid=(T // tm,),
        in_specs=[row, row, nxt, nxt, colblk(COL_SCC), colblk(COL_SCX), colblk(COL_GLUA), colblk(COL_GLUB),
                  pl.BlockSpec((SC_KERNEL, D), lambda i: (0, 0)), pl.BlockSpec((CF_KERNEL, D), lambda i: (0, 0))],
        out_specs=[pl.BlockSpec((tm, 4 * D), lambda i: (i, 0)), pl.BlockSpec((SC_KERNEL, D), lambda i: (0, 0)),
                   pl.BlockSpec((CF_KERNEL, D), lambda i: (0, 0))],
        out_shape=[_sds((T, 4 * D), BF16), _sds((SC_KERNEL, D), F32), _sds((CF_KERNEL, D), F32)],
        scratch_shapes=[pltpu.VMEM((tm, D), F32), pltpu.VMEM((tm + HALO, D), F32),
                        pltpu.VMEM((SUBLANES - 1, HALO + tm - SUBLANES, D), F32), pltpu.VMEM((tm, D), F32)], jobs=jobs,
    )(dcv, du1, dcv, du1, proj, proj, proj, proj, sc_w, cf_w)


def _loss_grad(y, tgt, name):
    T, D = y.shape
    tm = _tile(T, 512)

    def body(y_ref, t_ref, dy_ref, loss_ref):
        @pl.when(pl.program_id(0) == 0)
        def _():
            loss_ref[...] = jnp.zeros_like(loss_ref)

        e = y_ref[...] - t_ref[...]
        dy_ref[...] = e * (1.0 / D)
        loss_ref[...] += 0.5 * jnp.sum(jnp.sum(e * e, axis=-1, keepdims=True) * (1.0 / D), axis=0, keepdims=True)

    row = pl.BlockSpec((tm, D), lambda i: (i, 0))
    return _pallas(body, name=name, grid=(T // tm,), in_specs=[row, row],
                   out_specs=[row, pl.BlockSpec((1, 1), lambda i: (0, 0))],
                   out_shape=[_sds((T, D), F32), _sds((1, 1), F32)])(y, tgt)


def _mesh_pos():
    return lax.axis_index("x"), lax.axis_index("y"), lax.axis_index("c")


def _other_chips(x, y):
    return [(1 - x, y), (x, 1 - y), (1 - x, 1 - y)]


def _allgather_job(shards):
    nt = len(shards)

    def parts(ins, outs, send, recv):
        x, y, c = _mesh_pos()

        def slab(t, px, py, pc):
            return outs[t].at[4 * px + 2 * py + pc]

        def copy(t, k, block, to, src=None):
            return pltpu.make_async_remote_copy(
                src_ref=slab(t, *block) if src is None else src, dst_ref=slab(t, *block),
                send_sem=send.at[7 * t + k], recv_sem=recv.at[7 * t + k], device_id=to, device_id_type=MESH)

        return (x, y, c), (x, y, 1 - c), _other_chips(x, y), c, slab, copy

    def start(ins, outs, send, recv, loc):
        me, sibling, chips, c, slab, copy = parts(ins, outs, send, recv)
        for t in range(nt):
            pltpu.make_async_copy(ins[t], slab(t, *me), loc.at[t]).start()
            copy(t, 0, me, sibling, src=ins[t]).start()
            for j, chip in enumerate(chips):
                copy(t, 1 + j, me, (*chip, c), src=ins[t]).start()

    def finish(ins, outs, send, recv, loc):
        me, sibling, chips, c, slab, copy = parts(ins, outs, send, recv)
        for j, chip in enumerate(chips):
            for t in range(nt):
                copy(t, 1 + j, (*chip, c), me).wait_recv()
                copy(t, 4 + j, (*chip, c), sibling).start()
        for t in range(nt):
            copy(t, 0, sibling, me).wait_recv()
            for j, chip in enumerate(chips):
                copy(t, 4 + j, (*chip, 1 - c), me).wait_recv()
        for t in range(nt):
            copy(t, 0, me, sibling, src=ins[t]).wait_send()
            for j, chip in enumerate(chips):
                copy(t, 1 + j, me, (*chip, c), src=ins[t]).wait_send()
                copy(t, 4 + j, (*chip, c), sibling).wait_send()
            pltpu.make_async_copy(ins[t], slab(t, *me), loc.at[t]).wait()

    return _Job(shards, [_sds((N_DEV,) + s.shape, s.dtype) for s in shards], 7 * nt, nt, start, finish)


def _to_sibling_job(grads):
    nt = len(grads)

    def copies(ins, outs, send, recv):
        x, y, c = _mesh_pos()
        return [pltpu.make_async_remote_copy(src_ref=ins[t].at[:, 1 - c], dst_ref=outs[t], send_sem=send.at[t], recv_sem=recv.at[t],
                                             device_id=(x, y, 1 - c), device_id_type=MESH) for t in range(nt)]

    def start(ins, outs, send, recv, loc):
        for cp in copies(ins, outs, send, recv):
            cp.start()

    def finish(ins, outs, send, recv, loc):
        for cp in copies(ins, outs, send, recv):
            cp.wait()

    return _Job(grads, [_sds((N_CHIP,) + g.shape[2:], g.dtype) for g in grads], nt, 0, start, finish)


def _to_chips_job(pairs):
    nt = len(pairs)

    def copies(ins, outs, send, recv, loc):
        x, y, c = _mesh_pos()
        remote = [pltpu.make_async_remote_copy(src_ref=ins[t].at[2 * px + py], dst_ref=outs[t].at[k], send_sem=send.at[3 * t + k],
                                               recv_sem=recv.at[3 * t + k], device_id=(px, py, c), device_id_type=MESH)
                  for t in range(nt) for k, (px, py) in enumerate(_other_chips(x, y))]
        local = [pltpu.make_async_copy(ins[t].at[2 * x + y], outs[t].at[3], loc.at[t]) for t in range(nt)]
        return remote + local

    def start(ins, outs, send, recv, loc):
        for cp in copies(ins, outs, send, recv, loc):
            cp.start()

    def finish(ins, outs, send, recv, loc):
        for cp in copies(ins, outs, send, recv, loc):
            cp.wait()

    return _Job(pairs, [_sds(p.shape, p.dtype) for p in pairs], 3 * nt, nt, start, finish)


def _run_jobs(jobs, name):
    def body(o_ref):
        o_ref[...] = jnp.zeros_like(o_ref)

    _, jres = _pallas(body, name=name, grid=(1,), in_specs=[], out_specs=pl.BlockSpec((8, 128), lambda i: (0, 0)),
                      out_shape=_sds((8, 128), F32), jobs=jobs)()
    return jres


def _allgather_small(v, name):
    R, C = v.shape

    def body(x_ref, out_ref, send_sems, recv_sems, local_sem):
        x, y, c = _mesh_pos()
        me, sibling = (x, y, c), (x, y, 1 - c)
        chips = _other_chips(x, y)

        def slab(px, py, pc):
            return out_ref.at[4 * px + 2 * py + pc]

        def copy(k, block, to, src=None):
            return pltpu.make_async_remote_copy(
                src_ref=slab(*block) if src is None else src, dst_ref=slab(*block),
                send_sem=send_sems.at[k], recv_sem=recv_sems.at[k], device_id=to, device_id_type=MESH)

        mine = pltpu.make_async_copy(x_ref, slab(*me), local_sem)
        mine.start()
        first = [copy(0, me, sibling, src=x_ref)]
        first += [copy(1 + j, me, (*chip, c), src=x_ref) for j, chip in enumerate(chips)]
        for cp in first:
            cp.start()
        passed = [copy(4 + j, (*chip, c), sibling) for j, chip in enumerate(chips)]
        for j, chip in enumerate(chips):
            copy(1 + j, (*chip, c), me).wait_recv()
            passed[j].start()
        copy(0, sibling, me).wait_recv()
        for j, chip in enumerate(chips):
            copy(4 + j, (*chip, 1 - c), me).wait_recv()
        for cp in first + passed:
            cp.wait_send()
        mine.wait()

    return _comm_call(
        body, name=name, in_specs=[pl.BlockSpec(memory_space=pltpu.VMEM)], out_specs=pl.BlockSpec(memory_space=pltpu.VMEM),
        out_shape=_sds((N_DEV, R, C), v.dtype),
        scratch_shapes=[pltpu.SemaphoreType.DMA((7,)), pltpu.SemaphoreType.DMA((7,)), pltpu.SemaphoreType.DMA],
    )(v)


def _pair_sum(core, grad, got, name):
    _, _, R, C = grad.shape
    tr = _tile(R, 512)

    def body(core_ref, mine_ref, got_ref, out_ref):
        out_ref[...] = (mine_ref[...].astype(F32) + got_ref[...].astype(F32)).astype(BF16)

    blk = (None, tr, C)
    return _pallas(
        body, name=name, grid=(N_CHIP, R // tr), nprefetch=1,
        in_specs=[pl.BlockSpec((None, None, tr, C), lambda k, r, core: (k, core[0], r, 0)),
                  pl.BlockSpec(blk, lambda k, r, core: (k, r, 0))],
        out_specs=pl.BlockSpec(blk, lambda k, r, core: (k, r, 0)),
        out_shape=_sds((N_CHIP, R, C), BF16),
    )(core, grad, got)


def _adamw(w, g, m, v):
    m = ADAM_B1 * m + (1.0 - ADAM_B1) * g
    v = ADAM_B2 * v + (1.0 - ADAM_B2) * (g * g)
    m_hat = m / (1.0 - ADAM_B1 ** ADAM_STEP)
    v_hat = v / (1.0 - ADAM_B2 ** ADAM_STEP)
    delta = -ADAM_LR * (m_hat / (jnp.sqrt(v_hat) + ADAM_EPS) + ADAM_WD * w)
    return delta, m, v


def _final_adamw(sums, w, m, v, l, into, name):
    L, R, C = w.shape
    tr = _tile(R, 256)

    def body(s_ref, w_ref, m_ref, v_ref, *rest):
        g_out, d_out, m_out, v_out = rest[-4:]
        g = s_ref[0].astype(F32)
        for k in range(1, N_CHIP):
            g = g + s_ref[k].astype(F32)
        d, mn, vn = _adamw(w_ref[...], g, m_ref[...], v_ref[...])
        g_out[...] = g
        d_out[...] = d
        m_out[...] = mn
        v_out[...] = vn

    loc = pl.BlockSpec((None, tr, C), lambda r: (l, r, 0))
    in_specs = [pl.BlockSpec((N_CHIP, tr, C), lambda r: (0, r, 0)), loc, loc, loc]
    args = [sums, w, m, v]
    aliases = None
    if into is not None:
        in_specs += [ANY] * 4
        args += list(into)
        aliases = {4 + i: i for i in range(4)}
    return _pallas(body, name=name, grid=(R // tr,), in_specs=in_specs, out_specs=[loc] * 4,
                   out_shape=[_sds((L, R, C), F32)] * 4, aliases=aliases)(*args)


def _small_adamw(dev, allg, w, m, v, width, name):
    R = w.shape[0]

    def body(dev_ref, a_ref, w_ref, m_ref, v_ref, g_out, d_out, m_out, v_out):
        g = a_ref[0]
        for k in range(1, N_DEV):
            g = g + a_ref[k]
        d, mn, vn = _adamw(w_ref[...], g, m_ref[...], v_ref[...])
        g_out[...] = g
        d_out[...] = d
        m_out[...] = mn
        v_out[...] = vn

    full = allg.shape[2] == width
    loc = pl.BlockSpec((R, width), lambda i, dev: (0, 0))
    return _pallas(
        body, name=name, grid=(1,), nprefetch=1,
        in_specs=[pl.BlockSpec((N_DEV, R, width), lambda i, dev: (0, 0, 0 if full else dev[0])), loc, loc, loc],
        out_specs=[loc] * 4, out_shape=[_sds((R, width), F32)] * 4,
    )(dev, allg, w, m, v)


BIG = ("ffn1_w_gu", "ffn1_w_down", "w_in", "w_ret_o", "w_sc_o", "w_cf_o", "w_o", "ffn2_w_gu", "ffn2_w_down")
FFN1 = ("ffn1_w_gu", "ffn1_w_down")
MIX_OUT = ("w_ret_o", "w_sc_o", "w_cf_o", "w_o")
ROW_BLOCKS = ("ffn1_w_down", "ffn2_w_down") + MIX_OUT


def kernel(x, positions, norm_g, ffn1_w_gu, ffn1_w_down, w_in, w_ret_o, sc_conv_w, w_sc_o, cf_dw_w, cf_dw_b, cf_ln_g, cf_ln_b, w_cf_o, w_o, ffn2_w_gu, ffn2_w_down, loss_target, m_norm_g, m_ffn1_w_gu, m_ffn1_w_down, m_w_in, m_w_ret_o, m_sc_conv_w, m_w_sc_o, m_cf_dw_w, m_cf_dw_b, m_cf_ln_g, m_cf_ln_b, m_w_cf_o, m_w_o, m_ffn2_w_gu, m_ffn2_w_down, v_norm_g, v_ffn1_w_gu, v_ffn1_w_down, v_w_in, v_w_ret_o, v_sc_conv_w, v_w_sc_o, v_cf_dw_w, v_cf_dw_b, v_cf_ln_g, v_cf_ln_b, v_w_cf_o, v_w_o, v_ffn2_w_gu, v_ffn2_w_down):
    B, S, D = x.shape
    T = B * S
    L = norm_g.shape[0]
    DL = norm_g.shape[2]
    dk = D // 8
    mx, my, mc = _mesh_pos()
    dev_idx = jnp.reshape(4 * mx + 2 * my + mc, (1,)).astype(jnp.int32)
    core_idx = jnp.reshape(mc, (1,)).astype(jnp.int32)

    w32 = dict(ffn1_w_gu=ffn1_w_gu, ffn1_w_down=ffn1_w_down, w_in=w_in, w_ret_o=w_ret_o, w_sc_o=w_sc_o, w_cf_o=w_cf_o,
               w_o=w_o, ffn2_w_gu=ffn2_w_gu, ffn2_w_down=ffn2_w_down)
    m32 = dict(ffn1_w_gu=m_ffn1_w_gu, ffn1_w_down=m_ffn1_w_down, w_in=m_w_in, w_ret_o=m_w_ret_o, w_sc_o=m_w_sc_o,
               w_cf_o=m_w_cf_o, w_o=m_w_o, ffn2_w_gu=m_ffn2_w_gu, ffn2_w_down=m_ffn2_w_down)
    v32 = dict(ffn1_w_gu=v_ffn1_w_gu, ffn1_w_down=v_ffn1_w_down, w_in=v_w_in, w_ret_o=v_w_ret_o, w_sc_o=v_w_sc_o,
               w_cf_o=v_w_cf_o, w_o=v_w_o, ffn2_w_gu=v_ffn2_w_gu, ffn2_w_down=v_ffn2_w_down)

    W = [dict() for _ in range(L)]

    def gather(l, names):
        return _allgather_job([w32[n][l].astype(BF16) for n in names])

    def place(l, names, slabs):
        for n, a in zip(names, slabs):
            W[l][n] = a.reshape(a.shape[0] * a.shape[1], a.shape[2]) if n in ROW_BLOCKS else a

    def carried(fn, *args, jobs, **kw):
        out = fn(*args, jobs=jobs, **kw)
        return out if jobs else (out, [])

    n_sh = 6 + SC_KERNEL + CF_KERNEL
    small_sh = jnp.concatenate([norm_g, sc_conv_w, cf_dw_w], axis=1).reshape(L * n_sh, DL)
    sh_all = _allgather_small(small_sh, "allgather_small_params")
    sh_full = jnp.transpose(sh_all, (1, 0, 2)).reshape(L, n_sh, D)
    norm_full = sh_full[:, :6]
    scw_full = sh_full[:, 6:6 + SC_KERNEL]
    cfw_full = sh_full[:, 6 + SC_KERNEL:]
    place(0, FFN1, _run_jobs([gather(0, FFN1)], "allgather_first")[0])

    half = dk // 2
    inv_freq = ROPE_BASE ** (-jnp.arange(half, dtype=F32) / half)
    invf = jnp.concatenate([inv_freq, inv_freq])[None, :]
    sgn = jnp.concatenate([-jnp.ones((half,), F32), jnp.ones((half,), F32)])[None, :]
    cosf, sinf = _rope_tables(positions.reshape(T, 1), invf, sgn, "rope_tables")
    lgs = jnp.log(1.0 - 2.0 ** (-5.0 - jnp.arange(HEADS, dtype=F32)))

    def vec(a):
        return a.reshape(1, D)

    xc = x.reshape(T, D)
    saved = []
    for l in range(L):
        g = norm_full[l]
        sv = {}
        sv["x0"] = xc
        (xc, sv["h1"], sv["gate1"], sv["up1"], sv["f1"]), jr = _ffn_fwd(
            xc, vec(g[0]), vec(g[1]), W[l]["ffn1_w_gu"], W[l]["ffn1_w_down"], f"ffn1_fwd_{l}", jobs=[gather(l, ("w_in",))])
        place(l, ("w_in",), jr[0])
        sv["x1"] = xc
        (proj, sv["h2"]), jr = _norm_proj(xc, vec(g[2]), W[l]["w_in"], f"proj_fwd_{l}", jobs=[gather(l, MIX_OUT + ("ffn2_w_gu",))])
        place(l, MIX_OUT + ("ffn2_w_gu",), jr[0])
        sv["proj"] = proj
        o, jr = _ret_fwd(proj, cosf, sinf, lgs, B, S, D, f"retention_fwd_{l}", jobs=[gather(l, ("ffn2_w_down",))])
        place(l, ("ffn2_w_down",), jr[0])
        sv["o"] = o
        nxt = l + 1 < L
        (ya_in, yb_in, yc_in, sv["cv"], sv["u1"]), jr = carried(
            _mixer_mid_fwd, proj, o, scw_full[l], cfw_full[l], vec(cf_dw_b[l]), vec(cf_ln_g[l]), vec(cf_ln_b[l]), S,
            f"mixer_mid_fwd_{l}", jobs=[gather(l + 1, ("ffn1_w_gu",))] if nxt else [])
        if nxt:
            place(l + 1, ("ffn1_w_gu",), jr[0])
        sv["ya_in"], sv["yb_in"], sv["yc_in"] = ya_in, yb_in, yc_in
        (xc, sv["ya"], sv["yb"], sv["yc"], sv["merged"], sv["m"]), jr = carried(
            _mixer_merge_fwd, xc, proj, ya_in, yb_in, yc_in, vec(g[3]), W[l]["w_ret_o"], W[l]["w_sc_o"], W[l]["w_cf_o"], W[l]["w_o"],
            f"mixer_merge_fwd_{l}", jobs=[gather(l + 1, ("ffn1_w_down",))] if nxt else [])
        if nxt:
            place(l + 1, ("ffn1_w_down",), jr[0])
        sv["x2"] = xc
        xc, sv["h3"], sv["gate2"], sv["up2"], sv["f2"] = _ffn_fwd(xc, vec(g[4]), vec(g[5]), W[l]["ffn2_w_gu"], W[l]["ffn2_w_down"], f"ffn2_fwd_{l}")
        saved.append(sv)

    dx, loss_part = _loss_grad(xc, loss_target.reshape(T, D), "loss")
    loss = lax.psum(loss_part[0, 0], ("x", "y", "c"))

    to_sibling, to_chips, reduced = [], [], {}

    def grad_ready(n, l, g):
        if g.ndim == 2:
            g = g.reshape(N_DEV, g.shape[0] // N_DEV, g.shape[1])
        to_sibling.append((n, l, g.reshape(N_CHIP, 2, g.shape[1], g.shape[2])))

    def take_jobs():
        a, b = list(to_sibling), list(to_chips)
        to_sibling.clear()
        to_chips.clear()
        jobs = ([_to_chips_job([p for _, _, p in b])] if b else []) + ([_to_sibling_job([g for _, _, g in a])] if a else [])
        return jobs, (a, b)

    def settle(taken, jres):
        a, b = taken
        if b:
            for (n, l, _), got in zip(b, jres[0]):
                reduced[(n, l)] = got
        if a:
            for (n, l, g), got in zip(a, jres[-1]):
                to_chips.append((n, l, _pair_sum(core_idx, g, got, f"pair_sum_{n}_{l}")))

    def carrier(fn, *args, **kw):
        jobs, taken = take_jobs()
        out, jres = carried(fn, *args, jobs=jobs, **kw)
        settle(taken, jres)
        return out

    small_rows = [None] * L
    cb_gu = W[0]["ffn1_w_gu"].shape[2]
    cb_in = W[0]["w_in"].shape[2]

    def ffn_grads(tag, wgu_name, wd_name, dxo, x_in, ht, gate, up, f, g_pre, g_post, l):
        dxi, df, dgate, dup, act, dgpre, dgpost = carrier(_ffn_bwd, dxo, x_in, f, gate, up, g_pre, g_post, W[l][wgu_name],
                                                          W[l][wd_name], f"{tag}_bwd_{l}")
        grad_ready(wd_name, l, carrier(_mm_tn, act, df, f"{tag}_dwd_{l}"))
        half = carrier(_mm_tn, ht, dgate, f"{tag}_dwg_{l}", slab_cols=cb_gu, n_slabs=N_DEV, a_is_transposed=True)
        grad_ready(wgu_name, l, carrier(_mm_tn, ht, dup, f"{tag}_dwu_{l}", slab_cols=cb_gu, n_slabs=N_DEV, first_slab=N_DEV // 2,
                                        into=half, a_is_transposed=True))
        return dxi, dgpre, dgpost

    for l in reversed(range(L)):
        sv = saved[l]
        g = norm_full[l]
        dx, dg4, dg5 = ffn_grads("ffn2", "ffn2_w_gu", "ffn2_w_down", dx, sv["x2"], sv["h3"], sv["gate2"], sv["up2"], sv["f2"],
                                 vec(g[4]), vec(g[5]), l)
        (dm, dya, dyb, dyc, d2, dgl, do, dcv, du1, dg3, dlng, dlnb, dcfb) = carrier(
            _mixer_bwd_a, dx, sv["m"], vec(g[3]), sv["ya"], sv["yb"], sv["yc"], sv["proj"], sv["o"], sv["cv"], sv["u1"],
            vec(cf_ln_g[l]), vec(cf_ln_b[l]), W[l]["w_ret_o"], W[l]["w_sc_o"], W[l]["w_cf_o"], W[l]["w_o"], f"mixer_bwd_a_{l}")
        grad_ready("w_o", l, _mm_tn(sv["merged"], dm, f"dw_o_{l}", tt_pref=2048))
        grad_ready("w_ret_o", l, _mm_tn(sv["ya_in"], dya, f"dw_ret_o_{l}", tt_pref=2048))
        grad_ready("w_sc_o", l, _mm_tn(sv["yb_in"], dyb, f"dw_sc_o_{l}", tt_pref=2048))
        grad_ready("w_cf_o", l, _mm_tn(sv["yc_in"], dyc, f"dw_cf_o_{l}", tt_pref=2048))
        d4, dscw, dcfw = carrier(_mixer_bwd_b, dcv, du1, sv["proj"], scw_full[l], cfw_full[l], S, f"mixer_bwd_b_{l}")
        dq, dkk, dvv = carrier(_ret_bwd, do, sv["proj"], cosf, sinf, lgs, B, S, D, f"retention_bwd_{l}")
        dproj = jnp.concatenate([dq, dkk, dvv, d2, d4, dgl], axis=1)
        grad_ready("w_in", l, carrier(_mm_tn, sv["h2"], dproj, f"dw_in_{l}", slab_cols=cb_in, tn_pref=cb_in, a_is_transposed=True))
        dx, dg2 = carrier(_proj_bwd, dproj, W[l]["w_in"], sv["x1"], vec(g[2]), dx, f"proj_bwd_{l}")
        dx, dg0, dg1 = ffn_grads("ffn1", "ffn1_w_gu", "ffn1_w_down", dx, sv["x0"], sv["h1"], sv["gate1"], sv["up1"], sv["f1"],
                                 vec(g[0]), vec(g[1]), l)
        small_rows[l] = (jnp.concatenate([dg0, dg1, dg2, dg3, dg4, dg5, dscw, dcfw], axis=0),
                         jnp.concatenate([dcfb, dlng, dlnb], axis=0))
    grad_x = dx.reshape(B, S, D)
    flush = 0
    while to_sibling or to_chips:
        jobs, taken = take_jobs()
        settle(taken, _run_jobs(jobs, f"grads_flush_{flush}"))
        flush += 1

    big_out = {}
    for n in BIG:
        res = None
        for l in reversed(range(L)):
            res = _final_adamw(reduced[(n, l)], w32[n], m32[n], v32[n], l, res, f"adamw_{n}_{l}")
        big_out[n] = res

    sh_part = jnp.concatenate([small_rows[l][0] for l in range(L)], axis=0)
    rep_part = jnp.concatenate([small_rows[l][1] for l in range(L)] + [jnp.zeros((8 - 3 * L % 8, D), F32)] * (1 if 3 * L % 8 else 0), axis=0)
    sh_g = _allgather_small(sh_part, "allgather_small_grads")
    rep_g = _allgather_small(rep_part, "allgather_replicated_grads")

    def pack_sh(a, b, c):
        return jnp.concatenate([a, b, c], axis=1).reshape(L * n_sh, DL)

    def pack_rep(a, b, c):
        rows = jnp.stack([a, b, c], axis=1).reshape(3 * L, D)
        return jnp.concatenate([rows, jnp.ones((rep_part.shape[0] - 3 * L, D), F32)], axis=0)

    sh_res = _small_adamw(dev_idx, sh_g, pack_sh(norm_g, sc_conv_w, cf_dw_w), pack_sh(m_norm_g, m_sc_conv_w, m_cf_dw_w),
                          pack_sh(v_norm_g, v_sc_conv_w, v_cf_dw_w), DL, "adamw_small_sharded")
    rep_res = _small_adamw(dev_idx, rep_g, pack_rep(cf_dw_b, cf_ln_g, cf_ln_b), pack_rep(m_cf_dw_b, m_cf_ln_g, m_cf_ln_b),
                           pack_rep(v_cf_dw_b, v_cf_ln_g, v_cf_ln_b), D, "adamw_small_replicated")

    def unpack_sh(a):
        a = a.reshape(L, n_sh, DL)
        return {"norm_g": a[:, :6], "sc_conv_w": a[:, 6:6 + SC_KERNEL], "cf_dw_w": a[:, 6 + SC_KERNEL:]}

    def unpack_rep(a):
        a = a[:3 * L].reshape(L, 3, D)
        return {"cf_dw_b": a[:, 0], "cf_ln_g": a[:, 1], "cf_ln_b": a[:, 2]}

    order = ("norm_g", "ffn1_w_gu", "ffn1_w_down", "w_in", "w_ret_o", "sc_conv_w", "w_sc_o", "cf_dw_w", "cf_dw_b", "cf_ln_g",
             "cf_ln_b", "w_cf_o", "w_o", "ffn2_w_gu", "ffn2_w_down")
    outs = []
    for kind in range(4):
        small = {**unpack_sh(sh_res[kind]), **unpack_rep(rep_res[kind])}
        outs += [big_out[n][kind] if n in big_out else small[n] for n in order]
    return (loss, grad_x, *outs)
```

```python
import functools

import jax
import jax.numpy as jnp
from jax import lax
from jax.experimental import pallas as pl
from jax.experimental.pallas import tpu as pltpu

F32 = jnp.float32
BF16 = jnp.bfloat16
MESH = pl.DeviceIdType.MESH
ANY = pl.BlockSpec(memory_space=pl.ANY)

N_DEV = 8
N_CHIP = 4
CHUNK = 64
HEADS = 4
ROPE_BASE = 10000.0
NORM_EPS = 1e-6
LN_EPS = 1e-5
SC_KERNEL = 3
CF_KERNEL = 31
HALO = 32
ADAM_LR = 0.001
ADAM_B1 = 0.9
ADAM_B2 = 0.999
ADAM_EPS = 1e-08
ADAM_WD = 0.01
ADAM_STEP = 10
VMEM_LIMIT_V7X = 56 * 1024 * 1024


class _Job:
    def __init__(self, ins, out_shapes, n_sems, n_local, start, finish):
        self.ins, self.out_shapes, self.n_sems, self.n_local = list(ins), list(out_shapes), n_sems, n_local
        self.start, self.finish = start, finish


def _pallas(body, *, name, grid, in_specs, out_specs, out_shape, scratch_shapes=(), aliases=None, nprefetch=0, jobs=()):
    extra = {}
    single = not isinstance(out_shape, (list, tuple))
    out_shape = [out_shape] if single else list(out_shape)
    out_specs = [out_specs] if single else list(out_specs)
    in_specs, scratch = list(in_specs), list(scratch_shapes)
    n_in, n_out, n_scr = len(in_specs), len(out_shape), len(scratch)
    for jb in jobs:
        in_specs += [ANY] * len(jb.ins)
        out_specs += [ANY] * len(jb.out_shapes)
        out_shape += jb.out_shapes
        scratch += [pltpu.SemaphoreType.DMA((jb.n_sems,)), pltpu.SemaphoreType.DMA((jb.n_sems,)),
                    pltpu.SemaphoreType.DMA((max(jb.n_local, 1),))]

    def with_jobs(*refs):
        pre, refs = refs[:nprefetch], refs[nprefetch:]
        ins, p = refs[:n_in], n_in
        jins = []
        for jb in jobs:
            jins.append(refs[p:p + len(jb.ins)])
            p += len(jb.ins)
        outs = refs[p:p + n_out]
        p += n_out
        jouts = []
        for jb in jobs:
            jouts.append(refs[p:p + len(jb.out_shapes)])
            p += len(jb.out_shapes)
        scr = refs[p:p + n_scr]
        p += n_scr
        pids = [pl.program_id(d) for d in range(len(grid))]
        first = functools.reduce(jnp.logical_and, [pid == 0 for pid in pids])
        last = functools.reduce(jnp.logical_and, [pid == g - 1 for pid, g in zip(pids, grid)])

        @pl.when(first)
        def _():
            for i, jb in enumerate(jobs):
                jb.start(jins[i], jouts[i], *refs[p + 3 * i:p + 3 * i + 3])

        body(*pre, *ins, *outs, *scr)

        @pl.when(last)
        def _():
            for i, jb in enumerate(jobs):
                jb.finish(jins[i], jouts[i], *refs[p + 3 * i:p + 3 * i + 3])

    params = pltpu.CompilerParams(dimension_semantics=("arbitrary",) * len(grid), vmem_limit_bytes=VMEM_LIMIT_V7X)
    spec = pltpu.PrefetchScalarGridSpec(num_scalar_prefetch=nprefetch, grid=grid, in_specs=in_specs,
                                        out_specs=out_specs, scratch_shapes=scratch)
    call = pl.pallas_call(with_jobs if jobs else body, name=name, grid_spec=spec, out_shape=out_shape, compiler_params=params,
                          input_output_aliases=aliases or {}, **extra)

    def run(*args):
        res = call(*args, *[a for jb in jobs for a in jb.ins])
        own = res[0] if single else list(res[:n_out])
        if not jobs:
            return own
        jres, p = [], n_out
        for jb in jobs:
            jres.append(list(res[p:p + len(jb.out_shapes)]))
            p += len(jb.out_shapes)
        return own, jres

    return run


def _comm_call(body, *, name, in_specs, out_specs, out_shape, scratch_shapes):
    extra = {}
    return pl.pallas_call(body, name=name, in_specs=in_specs, out_specs=out_specs, out_shape=out_shape,
                          scratch_shapes=scratch_shapes, **extra)


def _sds(shape, dtype):
    return jax.ShapeDtypeStruct(shape, dtype)


def _tile(n, pref):
    t = min(n, pref)
    assert n % t == 0, (n, pref)
    return t


def _sigmoid(x):
    return jax.nn.sigmoid(x)


def _rms_fwd(x, g):
    r = lax.rsqrt(jnp.mean(x * x, axis=-1, keepdims=True) + NORM_EPS)
    return x * r * g


def _rms_bwd(x, g, dy):
    r = lax.rsqrt(jnp.mean(x * x, axis=-1, keepdims=True) + NORM_EPS)
    xh = x * r
    dg = jnp.sum(dy * xh, axis=0, keepdims=True)
    dxh = dy * g
    dx = r * (dxh - xh * jnp.mean(dxh * xh, axis=-1, keepdims=True))
    return dx, dg


def _ln_stats(x):
    mu = jnp.mean(x, axis=-1, keepdims=True)
    xc = x - mu
    rstd = lax.rsqrt(jnp.mean(xc * xc, axis=-1, keepdims=True) + LN_EPS)
    return xc * rstd, rstd


def _ln_bwd(xh, rstd, dxh):
    return rstd * (dxh - jnp.mean(dxh, axis=-1, keepdims=True) - xh * jnp.mean(dxh * xh, axis=-1, keepdims=True))


def _dsilu(x, s):
    return s * (1.0 + x * (1.0 - s))


def _rot(x, cosf, sinf):
    return x * cosf + pltpu.roll(x, x.shape[-1] // 2, 1) * sinf


def _unrot(d, cosf, sinf):
    return d * cosf - pltpu.roll(d, d.shape[-1] // 2, 1) * sinf


def _dot_nn(a, b):
    return jnp.dot(a, b, preferred_element_type=F32)


def _dot_nt(a, b):
    return lax.dot_general(a, b, (((1,), (1,)), ((), ())), preferred_element_type=F32)


def _dot_tn(a, b):
    return lax.dot_general(a, b, (((0,), (0,)), ((), ())), preferred_element_type=F32)


def _ffn_tf(cb):
    return _tile(cb, 1024)


def _ffn_fwd(x, g_pre, g_post, wgu, wd, name, jobs=()):
    T, D = x.shape
    F = wd.shape[0]
    cb = wgu.shape[2]
    tm, tf = _tile(T, 512), _ffn_tf(cb)
    nj, nb = F // tf, cb // tf

    def body(x_ref, gpre_ref, gpost_ref, wg_ref, wu_ref, wd_ref, xo_ref, ht_ref, gate_ref, up_ref, f_ref, acc_ref, h_ref):
        j = pl.program_id(1)

        @pl.when(j == 0)
        def _():
            h = _rms_fwd(x_ref[...], gpre_ref[...])
            h_ref[...] = h.astype(BF16)
            ht_ref[...] = h.T.astype(BF16)
            acc_ref[...] = jnp.zeros_like(acc_ref)

        h = h_ref[...]
        gate = _dot_nn(h, wg_ref[...])
        up = _dot_nn(h, wu_ref[...])
        gate_ref[...] = gate.astype(BF16)
        up_ref[...] = up.astype(BF16)
        act = (gate * _sigmoid(gate) * up).astype(BF16)
        acc_ref[...] += _dot_nn(act, wd_ref[...])

        @pl.when(j == nj - 1)
        def _():
            f = acc_ref[...]
            f_ref[...] = f
            xo_ref[...] = x_ref[...] + 0.5 * _rms_fwd(f, gpost_ref[...])

    row = pl.BlockSpec((tm, D), lambda i, j: (i, 0))
    vec = pl.BlockSpec((1, D), lambda i, j: (0, 0))
    col = pl.BlockSpec((tm, tf), lambda i, j: (i, j))
    return _pallas(
        body, name=name, grid=(T // tm, nj),
        in_specs=[row, vec, vec,
                  pl.BlockSpec((None, D, tf), lambda i, j: (j // nb, 0, j % nb)),
                  pl.BlockSpec((None, D, tf), lambda i, j: ((nj + j) // nb, 0, j % nb)),
                  pl.BlockSpec((tf, D), lambda i, j: (j, 0))],
        out_specs=[row, pl.BlockSpec((D, tm), lambda i, j: (0, i)), col, col, row],
        out_shape=[_sds((T, D), F32), _sds((D, T), BF16), _sds((T, F), BF16), _sds((T, F), BF16), _sds((T, D), F32)],
        scratch_shapes=[pltpu.VMEM((tm, D), F32), pltpu.VMEM((tm, D), BF16)], jobs=jobs,
    )(x, g_pre, g_post, wgu, wgu, wd)


def _ffn_bwd(dxo, x, f, gate, up, g_pre, g_post, wgu, wd, name, jobs=()):
    T, D = x.shape
    F = wd.shape[0]
    cb = wgu.shape[2]
    tm, tf = _tile(T, 512), _ffn_tf(cb)
    nj, nb = F // tf, cb // tf

    def body(dxo_ref, x_ref, f_ref, gate_ref, up_ref, gpre_ref, gpost_ref, wg_ref, wu_ref, wd_ref,
             dx_ref, df_ref, dgate_ref, dup_ref, act_ref, dgpre_ref, dgpost_ref, acc_ref):
        i, j = pl.program_id(0), pl.program_id(1)

        @pl.when((i == 0) & (j == 0))
        def _():
            dgpre_ref[...] = jnp.zeros_like(dgpre_ref)
            dgpost_ref[...] = jnp.zeros_like(dgpost_ref)

        @pl.when(j == 0)
        def _():
            df, dgp = _rms_bwd(f_ref[...], gpost_ref[...], 0.5 * dxo_ref[...])
            df_ref[...] = df.astype(BF16)
            dgpost_ref[...] += dgp
            acc_ref[...] = jnp.zeros_like(acc_ref)

        dact = _dot_nt(df_ref[...], wd_ref[...])
        g = gate_ref[...].astype(F32)
        u = up_ref[...].astype(F32)
        s = _sigmoid(g)
        silu = g * s
        dgate = (dact * u * _dsilu(g, s)).astype(BF16)
        dup = (dact * silu).astype(BF16)
        act_ref[...] = (silu * u).astype(BF16)
        dgate_ref[...] = dgate
        dup_ref[...] = dup
        acc_ref[...] += _dot_nt(dgate, wg_ref[...]) + _dot_nt(dup, wu_ref[...])

        @pl.when(j == nj - 1)
        def _():
            dxin, dgp = _rms_bwd(x_ref[...], gpre_ref[...], acc_ref[...])
            dx_ref[...] = dxo_ref[...] + dxin
            dgpre_ref[...] += dgp

    row = pl.BlockSpec((tm, D), lambda i, j: (i, 0))
    vec = pl.BlockSpec((1, D), lambda i, j: (0, 0))
    col = pl.BlockSpec((tm, tf), lambda i, j: (i, j))
    return _pallas(
        body, name=name, grid=(T // tm, nj),
        in_specs=[row, row, row, col, col, vec, vec,
                  pl.BlockSpec((None, D, tf), lambda i, j: (j // nb, 0, j % nb)),
                  pl.BlockSpec((None, D, tf), lambda i, j: ((nj + j) // nb, 0, j % nb)),
                  pl.BlockSpec((tf, D), lambda i, j: (j, 0))],
        out_specs=[row, row, col, col, col, vec, vec],
        out_shape=[_sds((T, D), F32), _sds((T, D), BF16), _sds((T, F), BF16), _sds((T, F), BF16), _sds((T, F), BF16),
                   _sds((1, D), F32), _sds((1, D), F32)],
        scratch_shapes=[pltpu.VMEM((tm, D), F32)], jobs=jobs,
    )(dxo, x, f, gate, up, g_pre, g_post, wgu, wgu, wd)


def _mm_tn(a, b, name, slab_cols=None, n_slabs=None, first_slab=0, into=None, tn_pref=1024, tt_pref=2048,
           a_is_transposed=False, jobs=()):
    T, N = b.shape
    K = a.shape[0] if a_is_transposed else a.shape[1]
    tt = _tile(T, tt_pref)
    tko = _tile(K, 1024)
    tn = _tile(slab_cols or N, tn_pref)
    nt = T // tt

    def body(a_ref, b_ref, *rest):
        o_ref, acc_ref = rest[-2:]
        t = pl.program_id(2)

        @pl.when(t == 0)
        def _():
            acc_ref[...] = jnp.zeros_like(acc_ref)

        acc_ref[...] += (_dot_nn if a_is_transposed else _dot_tn)(a_ref[...], b_ref[...])

        @pl.when(t == nt - 1)
        def _():
            o_ref[...] = acc_ref[...].astype(o_ref.dtype)

    if slab_cols is None:
        shape = (K, N)
        ospec = pl.BlockSpec((tko, tn), lambda k, jn, t: (k, jn))
    else:
        nb = slab_cols // tn
        shape = (n_slabs or N // slab_cols, K, slab_cols)
        ospec = pl.BlockSpec((None, tko, tn), lambda k, jn, t: (first_slab + jn // nb, k, jn % nb))
    aspec = pl.BlockSpec((tko, tt), lambda k, jn, t: (k, t)) if a_is_transposed else pl.BlockSpec((tt, tko), lambda k, jn, t: (t, k))
    in_specs, args = [aspec, pl.BlockSpec((tt, tn), lambda k, jn, t: (t, jn))], [a, b]
    if into is not None:
        in_specs.append(ANY)
        args.append(into)
    return _pallas(body, name=name, grid=(K // tko, N // tn, nt), in_specs=in_specs, out_specs=ospec, out_shape=_sds(shape, BF16),
                   scratch_shapes=[pltpu.VMEM((tko, tn), F32)], aliases={2: 0} if into is not None else None, jobs=jobs)(*args)


def _norm_proj(x, g, w, name, jobs=()):
    T, D = x.shape
    nd, cb = w.shape[0], w.shape[2]
    tm = _tile(T, 1024)

    def body(x_ref, g_ref, w_ref, p_ref, ht_ref, h_ref):
        @pl.when(pl.program_id(1) == 0)
        def _():
            h = _rms_fwd(x_ref[...], g_ref[...])
            h_ref[...] = h.astype(BF16)
            ht_ref[...] = h.T.astype(BF16)

        p_ref[...] = _dot_nn(h_ref[...], w_ref[...])

    row = pl.BlockSpec((tm, D), lambda i, j: (i, 0))
    return _pallas(
        body, name=name, grid=(T // tm, nd),
        in_specs=[row, pl.BlockSpec((1, D), lambda i, j: (0, 0)),
                  pl.BlockSpec((None, D, cb), lambda i, j: (j, 0, 0))],
        out_specs=[pl.BlockSpec((tm, cb), lambda i, j: (i, j)), pl.BlockSpec((D, tm), lambda i, j: (0, i))],
        out_shape=[_sds((T, nd * cb), F32), _sds((D, T), BF16)],
        scratch_shapes=[pltpu.VMEM((tm, D), BF16)], jobs=jobs,
    )(x, g, w)


def _proj_bwd(dproj, w, x, g, dxo, name, jobs=()):
    T, D = x.shape
    nd, cb = w.shape[0], w.shape[2]
    tm = _tile(T, 1024)

    def body(dp_ref, w_ref, x_ref, g_ref, dxo_ref, dx_ref, dg_ref, acc_ref):
        i, j = pl.program_id(0), pl.program_id(1)

        @pl.when((i == 0) & (j == 0))
        def _():
            dg_ref[...] = jnp.zeros_like(dg_ref)

        @pl.when(j == 0)
        def _():
            acc_ref[...] = jnp.zeros_like(acc_ref)

        acc_ref[...] += _dot_nt(dp_ref[...], w_ref[...])

        @pl.when(j == nd - 1)
        def _():
            dxin, dgp = _rms_bwd(x_ref[...], g_ref[...], acc_ref[...])
            dx_ref[...] = dxo_ref[...] + dxin
            dg_ref[...] += dgp

    row = pl.BlockSpec((tm, D), lambda i, j: (i, 0))
    vec = pl.BlockSpec((1, D), lambda i, j: (0, 0))
    return _pallas(
        body, name=name, grid=(T // tm, nd),
        in_specs=[pl.BlockSpec((tm, cb), lambda i, j: (i, j)),
                  pl.BlockSpec((None, D, cb), lambda i, j: (j, 0, 0)), row, vec, row],
        out_specs=[row, vec],
        out_shape=[_sds((T, D), F32), _sds((1, D), F32)],
        scratch_shapes=[pltpu.VMEM((tm, D), F32)], jobs=jobs,
    )(dproj, w, x, g, dxo)


def _rope_tables(pos, invf, sgn, name):
    T = pos.shape[0]
    dk = invf.shape[1]
    tm = _tile(T, 1024)

    def body(p_ref, f_ref, s_ref, c_out, s_out):
        ang = p_ref[...].astype(F32) * f_ref[...]
        c_out[...] = jnp.cos(ang)
        s_out[...] = jnp.sin(ang) * s_ref[...]

    vec = pl.BlockSpec((1, dk), lambda i: (0, 0))
    out = pl.BlockSpec((tm, dk), lambda i: (i, 0))
    return _pallas(body, name=name, grid=(T // tm,), in_specs=[pl.BlockSpec((tm, 1), lambda i: (i, 0)), vec, vec],
                   out_specs=[out, out], out_shape=[_sds((T, dk), F32), _sds((T, dk), F32)])(pos, invf, sgn)


def _decay(lg, r0, tq, n):
    r = r0 + lax.broadcasted_iota(jnp.int32, (tq, n), 0)
    c = lax.broadcasted_iota(jnp.int32, (tq, n), 1)
    rc, cc = r // CHUNK, c // CHUNK
    d = (r - c).astype(F32)
    e = jnp.where(rc == cc, jnp.abs(d), d)
    return jnp.where(cc > rc, 0.0, jnp.exp(lg * e))


def _ret_fwd(proj, cosf, sinf, lgs, B, S, D, name, jobs=()):
    T = B * S
    dk, dv = D // 8, D // 4
    tq = _tile(S, 256)
    scale = dk ** -0.5

    def body(lg_ref, q_ref, k_ref, v_ref, c_ref, s_ref, o_ref, kr_ref, vb_ref):
        lg = lg_ref[pl.program_id(1)]
        kr_ref[...] = (_rot(k_ref[...], c_ref[...], s_ref[...]) * scale).astype(BF16)
        vb_ref[...] = v_ref[...].astype(BF16)
        for qi in range(S // tq):
            rows, n = slice(qi * tq, (qi + 1) * tq), (qi + 1) * tq
            q = _rot(q_ref[rows, :], c_ref[rows, :], s_ref[rows, :]).astype(BF16)
            p = (_dot_nt(q, kr_ref[:n, :]) * _decay(lg, qi * tq, tq, n)).astype(BF16)
            o_ref[rows, :] = _dot_nn(p, vb_ref[:n, :])

    return _pallas(
        body, name=name, grid=(B, HEADS),
        in_specs=[pl.BlockSpec(memory_space=pltpu.SMEM),
                  pl.BlockSpec((S, dk), lambda b, h: (b, h)),
                  pl.BlockSpec((S, dk), lambda b, h: (b, HEADS + h)),
                  pl.BlockSpec((S, dv), lambda b, h: (b, HEADS + h)),
                  pl.BlockSpec((S, dk), lambda b, h: (b, 0)),
                  pl.BlockSpec((S, dk), lambda b, h: (b, 0))],
        out_specs=pl.BlockSpec((S, dv), lambda b, h: (b, h)),
        out_shape=_sds((T, D), F32),
        scratch_shapes=[pltpu.VMEM((S, dk), BF16), pltpu.VMEM((S, dv), BF16)], jobs=jobs,
    )(lgs, proj, proj, proj, cosf, sinf)


def _ret_bwd(do, proj, cosf, sinf, lgs, B, S, D, name, jobs=()):
    T = B * S
    dk, dv = D // 8, D // 4
    tq = _tile(S, 256)
    scale = dk ** -0.5

    def body(lg_ref, do_ref, q_ref, k_ref, v_ref, c_ref, s_ref, dq_ref, dk_ref, dv_ref, kr_ref, vb_ref, dka_ref, dva_ref):
        h = pl.program_id(1)
        lg = lg_ref[h]
        kr_ref[...] = (_rot(k_ref[...], c_ref[...], s_ref[...]) * scale).astype(BF16)
        vb_ref[...] = v_ref[...].astype(BF16)
        dka_ref[...] = jnp.zeros_like(dka_ref)
        dva_ref[...] = jnp.zeros_like(dva_ref)

        for qi in range(S // tq):
            rows, n = slice(qi * tq, (qi + 1) * tq), (qi + 1) * tq
            cq, sq = c_ref[rows, :], s_ref[rows, :]
            q = _rot(q_ref[rows, :], cq, sq).astype(BF16)
            dout = do_ref[rows, :].astype(BF16)
            w = _decay(lg, qi * tq, tq, n)
            p = (_dot_nt(q, kr_ref[:n, :]) * w).astype(BF16)
            ds = (_dot_nt(dout, vb_ref[:n, :]) * w).astype(BF16)
            dq_ref[rows, :] = _unrot(_dot_nn(ds, kr_ref[:n, :]), cq, sq).astype(BF16)
            dka_ref[:n, :] += _dot_tn(ds, q)
            dva_ref[:n, :] += _dot_tn(p, dout)

        dk_ref[...] = _unrot(dka_ref[...] * scale, c_ref[...], s_ref[...]).astype(BF16)
        dv_ref[...] = dva_ref[...].astype(BF16)

    return _pallas(
        body, name=name, grid=(B, HEADS),
        in_specs=[pl.BlockSpec(memory_space=pltpu.SMEM),
                  pl.BlockSpec((S, dv), lambda b, h: (b, h)),
                  pl.BlockSpec((S, dk), lambda b, h: (b, h)),
                  pl.BlockSpec((S, dk), lambda b, h: (b, HEADS + h)),
                  pl.BlockSpec((S, dv), lambda b, h: (b, HEADS + h)),
                  pl.BlockSpec((S, dk), lambda b, h: (b, 0)),
                  pl.BlockSpec((S, dk), lambda b, h: (b, 0))],
        out_specs=[pl.BlockSpec((S, dk), lambda b, h: (b, h)), pl.BlockSpec((S, dk), lambda b, h: (b, h)),
                   pl.BlockSpec((S, dv), lambda b, h: (b, h))],
        out_shape=[_sds((T, D // 2), BF16), _sds((T, D // 2), BF16), _sds((T, D), BF16)],
        scratch_shapes=[pltpu.VMEM((S, dk), BF16), pltpu.VMEM((S, dv), BF16), pltpu.VMEM((S, dk), F32), pltpu.VMEM((S, dv), F32)], jobs=jobs,
    )(lgs, do, proj, proj, proj, cosf, sinf)


COL_GRET, COL_SCB, COL_SCC, COL_SCX, COL_GLUA, COL_GLUB, COL_GL = 2, 3, 4, 5, 6, 7, 8


SUBLANES, LANES = 8, 128
CONV_BLOCK = 256
SC_FWD = tuple(HALO - (SC_KERNEL - 1) + j for j in range(SC_KERNEL))
CF_FWD = tuple(HALO - (CF_KERNEL - 1) + j for j in range(CF_KERNEL))
SC_BWD = tuple(SC_KERNEL - 1 - j for j in range(SC_KERNEL))
CF_BWD = tuple(CF_KERNEL - 1 - j for j in range(CF_KERNEL))


def _build_shifted(src_ref, e_ref, shifts):
    n = e_ref.shape[1]
    for b in sorted({s % SUBLANES for s in shifts} - {0}):
        e_ref[b - 1] = src_ref[pl.ds(b, n), :]


def _shifted_rows(src_ref, e_ref, s, r0, rows, lanes):
    a, b = divmod(s, SUBLANES)
    at = pl.ds(r0 + SUBLANES * a, rows)
    return src_ref[at, lanes] if b == 0 else e_ref[b - 1, at, lanes]


def _tree_sum(terms):
    while len(terms) > 1:
        terms = [a + b for a, b in zip(terms[::2], terms[1::2])] + ([terms[-1]] if len(terms) % 2 else [])
    return terms[0]


def _conv_taps(src_ref, e_ref, w_ref, shifts, tm, D, emit, u_ref=None, dw_ref=None):
    rows = min(tm, CONV_BLOCK if u_ref is None else CONV_BLOCK // 2)
    for lt in range(D // LANES):
        lanes = pl.ds(lt * LANES, LANES)
        for r0 in range(0, tm, rows):
            u = None if u_ref is None else u_ref[pl.ds(r0, rows), lanes]
            acc = None
            for j, s in enumerate(shifts):
                v = _shifted_rows(src_ref, e_ref, s, r0, rows, lanes)
                term = v * w_ref[j:j + 1, lanes]
                acc = term if acc is None else acc + term
                if u is not None:
                    prod = u * v
                    part = _tree_sum([prod[g * SUBLANES:(g + 1) * SUBLANES, :] for g in range(rows // SUBLANES)])
                    dw_ref[j:j + 1, lanes] += jnp.sum(part, axis=0, keepdims=True)
            emit(lt, pl.ds(r0, rows), acc)


def _mixer_mid_fwd(proj, o, sc_w, cf_w, cf_b, ln_g, ln_b, S, name, jobs=()):
    T, D = o.shape
    dv = D // HEADS
    tm = _tile(S, 256)
    per_seq = S // tm
    hb = tm // HALO

    def body(gret_ref, scb_ref, scc_ref, scx_ref, ga_ref, gb_ref, scc_h, scx_h, ga_h, gb_h, o_ref,
             scw_ref, cfw_ref, cfb_ref, lng_ref, lnb_ref, ya_ref, yb_ref, yc_ref, cv_ref, u1_ref, ext_ref, e_ref):
        first = (pl.program_id(0) % per_seq) == 0
        keep = jnp.where(first, 0.0, 1.0)
        gr = gret_ref[...]
        sg = gr * _sigmoid(gr)
        for hh in range(HEADS):
            cols = slice(hh * dv, (hh + 1) * dv)
            hn, _ = _ln_stats(o_ref[:, cols])
            ya_ref[:, cols] = (sg[:, cols] * hn).astype(BF16)
        ext_ref[:HALO, :] = scc_h[...] * scx_h[...] * keep
        ext_ref[HALO:, :] = scc_ref[...] * scx_ref[...]
        _build_shifted(ext_ref, e_ref, SC_FWD)

        def emit_cv(lt, rows, cv):
            lanes = pl.ds(lt * LANES, LANES)
            cv_ref[rows, lanes] = cv
            yb_ref[rows, lanes] = (scb_ref[rows, lanes] * cv).astype(BF16)

        _conv_taps(ext_ref, e_ref, scw_ref, SC_FWD, tm, D, emit_cv)
        ext_ref[:HALO, :] = ga_h[...] * _sigmoid(gb_h[...]) * keep
        ext_ref[HALO:, :] = ga_ref[...] * _sigmoid(gb_ref[...])
        _build_shifted(ext_ref, e_ref, CF_FWD)

        def emit_u1(lt, rows, acc):
            lanes = pl.ds(lt * LANES, LANES)
            u1_ref[rows, lanes] = acc + cfb_ref[:, lanes]

        _conv_taps(ext_ref, e_ref, cfw_ref, CF_FWD, tm, D, emit_u1)
        xh, _ = _ln_stats(u1_ref[...])
        u2 = xh * lng_ref[...] + lnb_ref[...]
        yc_ref[...] = (u2 * _sigmoid(u2)).astype(BF16)

    def colblk(c):
        return pl.BlockSpec((tm, D), lambda i: (i, c))

    def halo(c):
        return pl.BlockSpec((HALO, D), lambda i: (jnp.maximum(i * hb - 1, 0), c))

    row = pl.BlockSpec((tm, D), lambda i: (i, 0))
    vec = pl.BlockSpec((1, D), lambda i: (0, 0))
    return _pallas(
        body, name=name, grid=(T // tm,),
        in_specs=[colblk(COL_GRET), colblk(COL_SCB), colblk(COL_SCC), colblk(COL_SCX), colblk(COL_GLUA), colblk(COL_GLUB),
                  halo(COL_SCC), halo(COL_SCX), halo(COL_GLUA), halo(COL_GLUB), row,
                  pl.BlockSpec((SC_KERNEL, D), lambda i: (0, 0)), pl.BlockSpec((CF_KERNEL, D), lambda i: (0, 0)), vec, vec, vec],
        out_specs=[row, row, row, row, row],
        out_shape=[_sds((T, D), BF16)] * 3 + [_sds((T, D), F32)] * 2,
        scratch_shapes=[pltpu.VMEM((HALO + tm, D), F32), pltpu.VMEM((SUBLANES - 1, HALO + tm - SUBLANES, D), F32)], jobs=jobs,
    )(proj, proj, proj, proj, proj, proj, proj, proj, proj, proj, o, sc_w, cf_w, cf_b, ln_g, ln_b)


def _mixer_merge_fwd(x, proj, ya_in, yb_in, yc_in, g_post, w_ret, w_sc, w_cf, w_o, name, jobs=()):
    T, D = x.shape
    tm = _tile(T, 512)

    def body(x_ref, g0_ref, g1_ref, g2_ref, ya_in_ref, yb_in_ref, yc_in_ref, gp_ref, wr_ref, ws_ref, wc_ref, wo_ref,
             xo_ref, ya_ref, yb_ref, yc_ref, mg_ref, m_ref):
        ya = _dot_nn(ya_in_ref[...], wr_ref[...])
        yb = _dot_nn(yb_in_ref[...], ws_ref[...])
        yc = _dot_nn(yc_in_ref[...], wc_ref[...])
        ya_ref[...] = ya.astype(BF16)
        yb_ref[...] = yb.astype(BF16)
        yc_ref[...] = yc.astype(BF16)
        merged = (_sigmoid(g0_ref[...]) * ya + _sigmoid(g1_ref[...]) * yb + _sigmoid(g2_ref[...]) * yc).astype(BF16)
        mg_ref[...] = merged
        m = _dot_nn(merged, wo_ref[...])
        m_ref[...] = m
        xo_ref[...] = x_ref[...] + _rms_fwd(m, gp_ref[...])

    row = pl.BlockSpec((tm, D), lambda i: (i, 0))
    wsp = pl.BlockSpec((D, D), lambda i: (0, 0), pipeline_mode=pl.Buffered(1))

    def colblk(c):
        return pl.BlockSpec((tm, D), lambda i: (i, c))

    return _pallas(
        body, name=name, grid=(T // tm,),
        in_specs=[row, colblk(COL_GL), colblk(COL_GL + 1), colblk(COL_GL + 2), row, row, row,
                  pl.BlockSpec((1, D), lambda i: (0, 0)), wsp, wsp, wsp, wsp],
        out_specs=[row] * 6,
        out_shape=[_sds((T, D), F32)] + [_sds((T, D), BF16)] * 4 + [_sds((T, D), F32)], jobs=jobs,
    )(x, proj, proj, proj, ya_in, yb_in, yc_in, g_post, w_ret, w_sc, w_cf, w_o)


def _mixer_bwd_a(dxo, m, g_post, ya, yb, yc, proj, o, cv, u1, ln_g, ln_b, w_ret, w_sc, w_cf, w_o, name, jobs=()):
    T, D = m.shape
    dv = D // HEADS
    tm = _tile(T, 256)

    def body(dxo_ref, m_ref, gp_ref, ya_ref, yb_ref, yc_ref, g0_ref, g1_ref, g2_ref, gret_ref, scb_ref, o_ref, cv_ref, u1_ref,
             lng_ref, lnb_ref, wr_ref, ws_ref, wc_ref, wo_ref,
             dm_ref, dya_ref, dyb_ref, dyc_ref, d2_ref, dgl_ref, do_ref, dcv_ref, du1_ref, dgp_ref, dlng_ref, dlnb_ref, dcfb_ref):
        @pl.when(pl.program_id(0) == 0)
        def _():
            for r in (dgp_ref, dlng_ref, dlnb_ref, dcfb_ref):
                r[...] = jnp.zeros_like(r)

        dm, dgp = _rms_bwd(m_ref[...], gp_ref[...], dxo_ref[...])
        dgp_ref[...] += dgp
        dmb = dm.astype(BF16)
        dm_ref[...] = dmb
        dmerged = _dot_nt(dmb, wo_ref[...])
        dys = []
        for k, (g_ref, y_ref, dy_ref) in enumerate(((g0_ref, ya_ref, dya_ref), (g1_ref, yb_ref, dyb_ref), (g2_ref, yc_ref, dyc_ref))):
            sg = _sigmoid(g_ref[...])
            dgl_ref[:, k * D:(k + 1) * D] = (dmerged * y_ref[...].astype(F32) * sg * (1.0 - sg)).astype(BF16)
            dy = (dmerged * sg).astype(BF16)
            dy_ref[...] = dy
            dys.append(dy)
        dya_in = _dot_nt(dys[0], wr_ref[...])
        gr = gret_ref[...]
        sr = _sigmoid(gr)
        for hh in range(HEADS):
            cols = slice(hh * dv, (hh + 1) * dv)
            hn, rstd = _ln_stats(o_ref[:, cols])
            d2_ref[:, cols] = (dya_in[:, cols] * hn * _dsilu(gr[:, cols], sr[:, cols])).astype(BF16)
            do_ref[:, cols] = _ln_bwd(hn, rstd, dya_in[:, cols] * gr[:, cols] * sr[:, cols])
        dyb_in = _dot_nt(dys[1], ws_ref[...])
        d2_ref[:, D:] = (dyb_in * cv_ref[...]).astype(BF16)
        dcv_ref[...] = dyb_in * scb_ref[...]
        dyc_in = _dot_nt(dys[2], wc_ref[...])
        xh, rstd = _ln_stats(u1_ref[...])
        u2 = xh * lng_ref[...] + lnb_ref[...]
        du2 = dyc_in * _dsilu(u2, _sigmoid(u2))
        dlng_ref[...] += jnp.sum(du2 * xh, axis=0, keepdims=True)
        dlnb_ref[...] += jnp.sum(du2, axis=0, keepdims=True)
        du1 = _ln_bwd(xh, rstd, du2 * lng_ref[...])
        du1_ref[...] = du1
        dcfb_ref[...] += jnp.sum(du1, axis=0, keepdims=True)

    row = pl.BlockSpec((tm, D), lambda i: (i, 0))
    vec = pl.BlockSpec((1, D), lambda i: (0, 0))
    wsp = pl.BlockSpec((D, D), lambda i: (0, 0), pipeline_mode=pl.Buffered(1))

    def colblk(c):
        return pl.BlockSpec((tm, D), lambda i: (i, c))

    return _pallas(
        body, name=name, grid=(T // tm,),
        in_specs=[row, row, vec, row, row, row, colblk(COL_GL), colblk(COL_GL + 1), colblk(COL_GL + 2),
                  colblk(COL_GRET), colblk(COL_SCB), row, row, row, vec, vec, wsp, wsp, wsp, wsp],
        out_specs=[row, row, row, row, pl.BlockSpec((tm, 2 * D), lambda i: (i, 0)), pl.BlockSpec((tm, 3 * D), lambda i: (i, 0)),
                   row, row, row, vec, vec, vec, vec],
        out_shape=[_sds((T, D), BF16)] * 4 + [_sds((T, 2 * D), BF16), _sds((T, 3 * D), BF16)] + [_sds((T, D), F32)] * 3
                  + [_sds((1, D), F32)] * 4, jobs=jobs,
    )(dxo, m, g_post, ya, yb, yc, proj, proj, proj, proj, proj, o, cv, u1, ln_g, ln_b, w_ret, w_sc, w_cf, w_o)


def _mixer_bwd_b(dcv, du1, proj, sc_w, cf_w, S, name, jobs=()):
    T, D = dcv.shape
    tm = _tile(S, 256)
    per_seq = S // tm
    hb = tm // HALO
    last_hb = T // HALO - 1

    def body(dcv_ref, du1_ref, dcv_n, du1_n, scc_ref, scx_ref, ga_ref, gb_ref, scw_ref, cfw_ref,
             d4_ref, dscw_ref, dcfw_ref, u_ref, dext_ref, e_ref, sb_ref):
        i = pl.program_id(0)
        keep_next = jnp.where((i % per_seq) == per_seq - 1, 0.0, 1.0)

        @pl.when(i == 0)
        def _():
            dscw_ref[...] = jnp.zeros_like(dscw_ref)
            dcfw_ref[...] = jnp.zeros_like(dcfw_ref)

        dext_ref[:tm, :] = dcv_ref[...]
        dext_ref[tm:, :] = dcv_n[...] * keep_next
        _build_shifted(dext_ref, e_ref, SC_BWD)
        u_ref[...] = scc_ref[...] * scx_ref[...]

        def emit_dz(lt, rows, dz):
            lanes = pl.ds(lt * LANES, LANES)
            d4_ref[rows, pl.ds(lt * LANES, LANES)] = (dz * scx_ref[rows, lanes]).astype(BF16)
            d4_ref[rows, pl.ds(D + lt * LANES, LANES)] = (dz * scc_ref[rows, lanes]).astype(BF16)

        _conv_taps(dext_ref, e_ref, scw_ref, SC_BWD, tm, D, emit_dz, u_ref=u_ref, dw_ref=dscw_ref)
        dext_ref[:tm, :] = du1_ref[...]
        dext_ref[tm:, :] = du1_n[...] * keep_next
        _build_shifted(dext_ref, e_ref, CF_BWD)
        sb_ref[...] = _sigmoid(gb_ref[...])
        u_ref[...] = ga_ref[...] * sb_ref[...]

        def emit_du0(lt, rows, du0):
            lanes = pl.ds(lt * LANES, LANES)
            sb = sb_ref[rows, lanes]
            d4_ref[rows, pl.ds(2 * D + lt * LANES, LANES)] = (du0 * sb).astype(BF16)
            d4_ref[rows, pl.ds(3 * D + lt * LANES, LANES)] = (du0 * ga_ref[rows, lanes] * sb * (1.0 - sb)).astype(BF16)

        _conv_taps(dext_ref, e_ref, cfw_ref, CF_BWD, tm, D, emit_du0, u_ref=u_ref, dw_ref=dcfw_ref)

    row = pl.BlockSpec((tm, D), lambda i: (i, 0))
    nxt = pl.BlockSpec((HALO, D), lambda i: (jnp.minimum((i + 1) * hb, last_hb), 0))

    def colblk(c):
        return pl.BlockSpec((tm, D), lambda i: (i, c))

    return _pallas(
        body, name=name, grid=(T // tm,),
        in_specs=[row, row, nxt, nxt, colblk(COL_SCC), colblk(COL_SCX), colblk(COL_GLUA), colblk(COL_GLUB),
                  pl.BlockSpec((SC_KERNEL, D), lambda i: (0, 0)), pl.BlockSpec((CF_KERNEL, D), lambda i: (0, 0))],
        out_specs=[pl.BlockSpec((tm, 4 * D), lambda i: (i, 0)), pl.BlockSpec((SC_KERNEL, D), lambda i: (0, 0)),
                   pl.BlockSpec((CF_KERNEL, D), lambda i: (0, 0))],
        out_shape=[_sds((T, 4 * D), BF16), _sds((SC_KERNEL, D), F32), _sds((CF_KERNEL, D), F32)],
        scratch_shapes=[pltpu.VMEM((tm, D), F32), pltpu.VMEM((tm + HALO, D), F32),
                        pltpu.VMEM((SUBLANES - 1, HALO + tm - SUBLANES, D), F32), pltpu.VMEM((tm, D), F32)], jobs=jobs,
    )(dcv, du1, dcv, du1, proj, proj, proj, proj, sc_w, cf_w)


def _loss_grad(y, tgt, name):
    T, D = y.shape
    tm = _tile(T, 512)

    def body(y_ref, t_ref, dy_ref, loss_ref):
        @pl.when(pl.program_id(0) == 0)
        def _():
            loss_ref[...] = jnp.zeros_like(loss_ref)

        e = y_ref[...] - t_ref[...]
        dy_ref[...] = e * (1.0 / D)
        loss_ref[...] += 0.5 * jnp.sum(jnp.sum(e * e, axis=-1, keepdims=True) * (1.0 / D), axis=0, keepdims=True)

    row = pl.BlockSpec((tm, D), lambda i: (i, 0))
    return _pallas(body, name=name, grid=(T // tm,), in_specs=[row, row],
                   out_specs=[row, pl.BlockSpec((1, 1), lambda i: (0, 0))],
                   out_shape=[_sds((T, D), F32), _sds((1, 1), F32)])(y, tgt)


def _mesh_pos():
    return lax.axis_index("x"), lax.axis_index("y"), lax.axis_index("c")


def _other_chips(x, y):
    return [(1 - x, y), (x, 1 - y), (1 - x, 1 - y)]


def _allgather_job(shards):
    nt = len(shards)

    def parts(ins, outs, send, recv):
        x, y, c = _mesh_pos()

        def slab(t, px, py, pc):
            return outs[t].at[4 * px + 2 * py + pc]

        def copy(t, k, block, to, src=None):
            return pltpu.make_async_remote_copy(
                src_ref=slab(t, *block) if src is None else src, dst_ref=slab(t, *block),
                send_sem=send.at[7 * t + k], recv_sem=recv.at[7 * t + k], device_id=to, device_id_type=MESH)

        return (x, y, c), (x, y, 1 - c), _other_chips(x, y), c, slab, copy

    def start(ins, outs, send, recv, loc):
        me, sibling, chips, c, slab, copy = parts(ins, outs, send, recv)
        for t in range(nt):
            pltpu.make_async_copy(ins[t], slab(t, *me), loc.at[t]).start()
            copy(t, 0, me, sibling, src=ins[t]).start()
            for j, chip in enumerate(chips):
                copy(t, 1 + j, me, (*chip, c), src=ins[t]).start()

    def finish(ins, outs, send, recv, loc):
        me, sibling, chips, c, slab, copy = parts(ins, outs, send, recv)
        for j, chip in enumerate(chips):
            for t in range(nt):
                copy(t, 1 + j, (*chip, c), me).wait_recv()
                copy(t, 4 + j, (*chip, c), sibling).start()
        for t in range(nt):
            copy(t, 0, sibling, me).wait_recv()
            for j, chip in enumerate(chips):
                copy(t, 4 + j, (*chip, 1 - c), me).wait_recv()
        for t in range(nt):
            copy(t, 0, me, sibling, src=ins[t]).wait_send()
            for j, chip in enumerate(chips):
                copy(t, 1 + j, me, (*chip, c), src=ins[t]).wait_send()
                copy(t, 4 + j, (*chip, c), sibling).wait_send()
            pltpu.make_async_copy(ins[t], slab(t, *me), loc.at[t]).wait()

    return _Job(shards, [_sds((N_DEV,) + s.shape, s.dtype) for s in shards], 7 * nt, nt, start, finish)


def _to_sibling_job(grads):
    nt = len(grads)

    def copies(ins, outs, send, recv):
        x, y, c = _mesh_pos()
        return [pltpu.make_async_remote_copy(src_ref=ins[t].at[:, 1 - c], dst_ref=outs[t], send_sem=send.at[t], recv_sem=recv.at[t],
                                             device_id=(x, y, 1 - c), device_id_type=MESH) for t in range(nt)]

    def start(ins, outs, send, recv, loc):
        for cp in copies(ins, outs, send, recv):
            cp.start()

    def finish(ins, outs, send, recv, loc):
        for cp in copies(ins, outs, send, recv):
            cp.wait()

    return _Job(grads, [_sds((N_CHIP,) + g.shape[2:], g.dtype) for g in grads], nt, 0, start, finish)


def _to_chips_job(pairs):
    nt = len(pairs)

    def copies(ins, outs, send, recv, loc):
        x, y, c = _mesh_pos()
        remote = [pltpu.make_async_remote_copy(src_ref=ins[t].at[2 * px + py], dst_ref=outs[t].at[k], send_sem=send.at[3 * t + k],
                                               recv_sem=recv.at[3 * t + k], device_id=(px, py, c), device_id_type=MESH)
                  for t in range(nt) for k, (px, py) in enumerate(_other_chips(x, y))]
        local = [pltpu.make_async_copy(ins[t].at[2 * x + y], outs[t].at[3], loc.at[t]) for t in range(nt)]
        return remote + local

    def start(ins, outs, send, recv, loc):
        for cp in copies(ins, outs, send, recv, loc):
            cp.start()

    def finish(ins, outs, send, recv, loc):
        for cp in copies(ins, outs, send, recv, loc):
            cp.wait()

    return _Job(pairs, [_sds(p.shape, p.dtype) for p in pairs], 3 * nt, nt, start, finish)


def _run_jobs(jobs, name):
    def body(o_ref):
        o_ref[...] = jnp.zeros_like(o_ref)

    _, jres = _pallas(body, name=name, grid=(1,), in_specs=[], out_specs=pl.BlockSpec((8, 128), lambda i: (0, 0)),
                      out_shape=_sds((8, 128), F32), jobs=jobs)()
    return jres


def _allgather_small(v, name):
    R, C = v.shape

    def body(x_ref, out_ref, send_sems, recv_sems, local_sem):
        x, y, c = _mesh_pos()
        me, sibling = (x, y, c), (x, y, 1 - c)
        chips = _other_chips(x, y)

        def slab(px, py, pc):
            return out_ref.at[4 * px + 2 * py + pc]

        def copy(k, block, to, src=None):
            return pltpu.make_async_remote_copy(
                src_ref=slab(*block) if src is None else src, dst_ref=slab(*block),
                send_sem=send_sems.at[k], recv_sem=recv_sems.at[k], device_id=to, device_id_type=MESH)

        mine = pltpu.make_async_copy(x_ref, slab(*me), local_sem)
        mine.start()
        first = [copy(0, me, sibling, src=x_ref)]
        first += [copy(1 + j, me, (*chip, c), src=x_ref) for j, chip in enumerate(chips)]
        for cp in first:
            cp.start()
        passed = [copy(4 + j, (*chip, c), sibling) for j, chip in enumerate(chips)]
        for j, chip in enumerate(chips):
            copy(1 + j, (*chip, c), me).wait_recv()
            passed[j].start()
        copy(0, sibling, me).wait_recv()
        for j, chip in enumerate(chips):
            copy(4 + j, (*chip, 1 - c), me).wait_recv()
        for cp in first + passed:
            cp.wait_send()
        mine.wait()

    return _comm_call(
        body, name=name, in_specs=[pl.BlockSpec(memory_space=pltpu.VMEM)], out_specs=pl.BlockSpec(memory_space=pltpu.VMEM),
        out_shape=_sds((N_DEV, R, C), v.dtype),
        scratch_shapes=[pltpu.SemaphoreType.DMA((7,)), pltpu.SemaphoreType.DMA((7,)), pltpu.SemaphoreType.DMA],
    )(v)


def _pair_sum(core, grad, got, name):
    _, _, R, C = grad.shape
    tr = _tile(R, 1024)

    def body(core_ref, mine_ref, got_ref, out_ref):
        out_ref[...] = (mine_ref[...].astype(F32) + got_ref[...].astype(F32)).astype(BF16)

    blk = (None, tr, C)
    return _pallas(
        body, name=name, grid=(N_CHIP, R // tr), nprefetch=1,
        in_specs=[pl.BlockSpec((None, None, tr, C), lambda k, r, core: (k, core[0], r, 0)),
                  pl.BlockSpec(blk, lambda k, r, core: (k, r, 0))],
        out_specs=pl.BlockSpec(blk, lambda k, r, core: (k, r, 0)),
        out_shape=_sds((N_CHIP, R, C), BF16),
    )(core, grad, got)


def _adamw(w, g, m, v):
    m = ADAM_B1 * m + (1.0 - ADAM_B1) * g
    v = ADAM_B2 * v + (1.0 - ADAM_B2) * (g * g)
    m_hat = m / (1.0 - ADAM_B1 ** ADAM_STEP)
    v_hat = v / (1.0 - ADAM_B2 ** ADAM_STEP)
    delta = -ADAM_LR * (m_hat / (jnp.sqrt(v_hat) + ADAM_EPS) + ADAM_WD * w)
    return delta, m, v


def _final_adamw(sums, w, m, v, l, into, name):
    L, R, C = w.shape
    tr = _tile(R, 512 if C <= 1024 else 256)

    def body(s_ref, w_ref, m_ref, v_ref, *rest):
        g_out, d_out, m_out, v_out = rest[-4:]
        g = s_ref[0].astype(F32)
        for k in range(1, N_CHIP):
            g = g + s_ref[k].astype(F32)
        d, mn, vn = _adamw(w_ref[...], g, m_ref[...], v_ref[...])
        g_out[...] = g
        d_out[...] = d
        m_out[...] = mn
        v_out[...] = vn

    loc = pl.BlockSpec((None, tr, C), lambda r: (l, r, 0))
    in_specs = [pl.BlockSpec((N_CHIP, tr, C), lambda r: (0, r, 0)), loc, loc, loc]
    args = [sums, w, m, v]
    aliases = None
    if into is not None:
        in_specs += [ANY] * 4
        args += list(into)
        aliases = {4 + i: i for i in range(4)}
    return _pallas(body, name=name, grid=(R // tr,), in_specs=in_specs, out_specs=[loc] * 4,
                   out_shape=[_sds((L, R, C), F32)] * 4, aliases=aliases)(*args)


def _small_adamw(dev, allg, w, m, v, width, name):
    R = w.shape[0]

    def body(dev_ref, a_ref, w_ref, m_ref, v_ref, g_out, d_out, m_out, v_out):
        g = a_ref[0]
        for k in range(1, N_DEV):
            g = g + a_ref[k]
        d, mn, vn = _adamw(w_ref[...], g, m_ref[...], v_ref[...])
        g_out[...] = g
        d_out[...] = d
        m_out[...] = mn
        v_out[...] = vn

    full = allg.shape[2] == width
    loc = pl.BlockSpec((R, width), lambda i, dev: (0, 0))
    return _pallas(
        body, name=name, grid=(1,), nprefetch=1,
        in_specs=[pl.BlockSpec((N_DEV, R, width), lambda i, dev: (0, 0, 0 if full else dev[0])), loc, loc, loc],
        out_specs=[loc] * 4, out_shape=[_sds((R, width), F32)] * 4,
    )(dev, allg, w, m, v)


BIG = ("ffn1_w_gu", "ffn1_w_down", "w_in", "w_ret_o", "w_sc_o", "w_cf_o", "w_o", "ffn2_w_gu", "ffn2_w_down")
FFN1 = ("ffn1_w_gu", "ffn1_w_down")
MIX_OUT = ("w_ret_o", "w_sc_o", "w_cf_o", "w_o")
ROW_BLOCKS = ("ffn1_w_down", "ffn2_w_down") + MIX_OUT


def kernel(x, positions, norm_g, ffn1_w_gu, ffn1_w_down, w_in, w_ret_o, sc_conv_w, w_sc_o, cf_dw_w, cf_dw_b, cf_ln_g, cf_ln_b, w_cf_o, w_o, ffn2_w_gu, ffn2_w_down, loss_target, m_norm_g, m_ffn1_w_gu, m_ffn1_w_down, m_w_in, m_w_ret_o, m_sc_conv_w, m_w_sc_o, m_cf_dw_w, m_cf_dw_b, m_cf_ln_g, m_cf_ln_b, m_w_cf_o, m_w_o, m_ffn2_w_gu, m_ffn2_w_down, v_norm_g, v_ffn1_w_gu, v_ffn1_w_down, v_w_in, v_w_ret_o, v_sc_conv_w, v_w_sc_o, v_cf_dw_w, v_cf_dw_b, v_cf_ln_g, v_cf_ln_b, v_w_cf_o, v_w_o, v_ffn2_w_gu, v_ffn2_w_down):
    B, S, D = x.shape
    T = B * S
    L = norm_g.shape[0]
    DL = norm_g.shape[2]
    dk = D // 8
    mx, my, mc = _mesh_pos()
    dev_idx = jnp.reshape(4 * mx + 2 * my + mc, (1,)).astype(jnp.int32)
    core_idx = jnp.reshape(mc, (1,)).astype(jnp.int32)

    w32 = dict(ffn1_w_gu=ffn1_w_gu, ffn1_w_down=ffn1_w_down, w_in=w_in, w_ret_o=w_ret_o, w_sc_o=w_sc_o, w_cf_o=w_cf_o,
               w_o=w_o, ffn2_w_gu=ffn2_w_gu, ffn2_w_down=ffn2_w_down)
    m32 = dict(ffn1_w_gu=m_ffn1_w_gu, ffn1_w_down=m_ffn1_w_down, w_in=m_w_in, w_ret_o=m_w_ret_o, w_sc_o=m_w_sc_o,
               w_cf_o=m_w_cf_o, w_o=m_w_o, ffn2_w_gu=m_ffn2_w_gu, ffn2_w_down=m_ffn2_w_down)
    v32 = dict(ffn1_w_gu=v_ffn1_w_gu, ffn1_w_down=v_ffn1_w_down, w_in=v_w_in, w_ret_o=v_w_ret_o, w_sc_o=v_w_sc_o,
               w_cf_o=v_w_cf_o, w_o=v_w_o, ffn2_w_gu=v_ffn2_w_gu, ffn2_w_down=v_ffn2_w_down)

    W = [dict() for _ in range(L)]

    def gather(l, names):
        return _allgather_job([w32[n][l].astype(BF16) for n in names])

    def place(l, names, slabs):
        for n, a in zip(names, slabs):
            W[l][n] = a.reshape(a.shape[0] * a.shape[1], a.shape[2]) if n in ROW_BLOCKS else a

    def carried(fn, *args, jobs, **kw):
        out = fn(*args, jobs=jobs, **kw)
        return out if jobs else (out, [])

    n_sh = 6 + SC_KERNEL + CF_KERNEL
    small_sh = jnp.concatenate([norm_g, sc_conv_w, cf_dw_w], axis=1).reshape(L * n_sh, DL)
    sh_all = _allgather_small(small_sh, "allgather_small_params")
    sh_full = jnp.transpose(sh_all, (1, 0, 2)).reshape(L, n_sh, D)
    norm_full = sh_full[:, :6]
    scw_full = sh_full[:, 6:6 + SC_KERNEL]
    cfw_full = sh_full[:, 6 + SC_KERNEL:]
    place(0, FFN1, _run_jobs([gather(0, FFN1)], "allgather_first")[0])

    half = dk // 2
    inv_freq = ROPE_BASE ** (-jnp.arange(half, dtype=F32) / half)
    invf = jnp.concatenate([inv_freq, inv_freq])[None, :]
    sgn = jnp.concatenate([-jnp.ones((half,), F32), jnp.ones((half,), F32)])[None, :]
    cosf, sinf = _rope_tables(positions.reshape(T, 1), invf, sgn, "rope_tables")
    lgs = jnp.log(1.0 - 2.0 ** (-5.0 - jnp.arange(HEADS, dtype=F32)))

    def vec(a):
        return a.reshape(1, D)

    xc = x.reshape(T, D)
    saved = []
    for l in range(L):
        g = norm_full[l]
        sv = {}
        sv["x0"] = xc
        (xc, sv["h1"], sv["gate1"], sv["up1"], sv["f1"]), jr = _ffn_fwd(
            xc, vec(g[0]), vec(g[1]), W[l]["ffn1_w_gu"], W[l]["ffn1_w_down"], f"ffn1_fwd_{l}", jobs=[gather(l, ("w_in",))])
        place(l, ("w_in",), jr[0])
        sv["x1"] = xc
        (proj, sv["h2"]), jr = _norm_proj(xc, vec(g[2]), W[l]["w_in"], f"proj_fwd_{l}", jobs=[gather(l, MIX_OUT + ("ffn2_w_gu",))])
        place(l, MIX_OUT + ("ffn2_w_gu",), jr[0])
        sv["proj"] = proj
        o, jr = _ret_fwd(proj, cosf, sinf, lgs, B, S, D, f"retention_fwd_{l}", jobs=[gather(l, ("ffn2_w_down",))])
        place(l, ("ffn2_w_down",), jr[0])
        sv["o"] = o
        nxt = l + 1 < L
        (ya_in, yb_in, yc_in, sv["cv"], sv["u1"]), jr = carried(
            _mixer_mid_fwd, proj, o, scw_full[l], cfw_full[l], vec(cf_dw_b[l]), vec(cf_ln_g[l]), vec(cf_ln_b[l]), S,
            f"mixer_mid_fwd_{l}", jobs=[gather(l + 1, ("ffn1_w_gu",))] if nxt else [])
        if nxt:
            place(l + 1, ("ffn1_w_gu",), jr[0])
        sv["ya_in"], sv["yb_in"], sv["yc_in"] = ya_in, yb_in, yc_in
        (xc, sv["ya"], sv["yb"], sv["yc"], sv["merged"], sv["m"]), jr = carried(
            _mixer_merge_fwd, xc, proj, ya_in, yb_in, yc_in, vec(g[3]), W[l]["w_ret_o"], W[l]["w_sc_o"], W[l]["w_cf_o"], W[l]["w_o"],
            f"mixer_merge_fwd_{l}", jobs=[gather(l + 1, ("ffn1_w_down",))] if nxt else [])
        if nxt:
            place(l + 1, ("ffn1_w_down",), jr[0])
        sv["x2"] = xc
        xc, sv["h3"], sv["gate2"], sv["up2"], sv["f2"] = _ffn_fwd(xc, vec(g[4]), vec(g[5]), W[l]["ffn2_w_gu"], W[l]["ffn2_w_down"], f"ffn2_fwd_{l}")
        saved.append(sv)

    dx, loss_part = _loss_grad(xc, loss_target.reshape(T, D), "loss")
    loss = lax.psum(loss_part[0, 0], ("x", "y", "c"))

    to_sibling, to_chips, reduced = [], [], {}

    def grad_ready(n, l, g):
        if g.ndim == 2:
            g = g.reshape(N_DEV, g.shape[0] // N_DEV, g.shape[1])
        to_sibling.append((n, l, g.reshape(N_CHIP, 2, g.shape[1], g.shape[2])))

    def take_jobs():
        a, b = list(to_sibling), list(to_chips)
        to_sibling.clear()
        to_chips.clear()
        jobs = ([_to_chips_job([p for _, _, p in b])] if b else []) + ([_to_sibling_job([g for _, _, g in a])] if a else [])
        return jobs, (a, b)

    def settle(taken, jres):
        a, b = taken
        if b:
            for (n, l, _), got in zip(b, jres[0]):
                reduced[(n, l)] = got
        if a:
            for (n, l, g), got in zip(a, jres[-1]):
                to_chips.append((n, l, _pair_sum(core_idx, g, got, f"pair_sum_{n}_{l}")))

    def carrier(fn, *args, **kw):
        jobs, taken = take_jobs()
        out, jres = carried(fn, *args, jobs=jobs, **kw)
        settle(taken, jres)
        return out

    small_rows = [None] * L
    cb_gu = W[0]["ffn1_w_gu"].shape[2]
    cb_in = W[0]["w_in"].shape[2]

    def ffn_grads(tag, wgu_name, wd_name, dxo, x_in, ht, gate, up, f, g_pre, g_post, l):
        dxi, df, dgate, dup, act, dgpre, dgpost = carrier(_ffn_bwd, dxo, x_in, f, gate, up, g_pre, g_post, W[l][wgu_name],
                                                          W[l][wd_name], f"{tag}_bwd_{l}")
        grad_ready(wd_name, l, carrier(_mm_tn, act, df, f"{tag}_dwd_{l}"))
        half = carrier(_mm_tn, ht, dgate, f"{tag}_dwg_{l}", slab_cols=cb_gu, n_slabs=N_DEV, a_is_transposed=True)
        grad_ready(wgu_name, l, carrier(_mm_tn, ht, dup, f"{tag}_dwu_{l}", slab_cols=cb_gu, n_slabs=N_DEV, first_slab=N_DEV // 2,
                                        into=half, a_is_transposed=True))
        return dxi, dgpre, dgpost

    for l in reversed(range(L)):
        sv = saved[l]
        g = norm_full[l]
        dx, dg4, dg5 = ffn_grads("ffn2", "ffn2_w_gu", "ffn2_w_down", dx, sv["x2"], sv["h3"], sv["gate2"], sv["up2"], sv["f2"],
                                 vec(g[4]), vec(g[5]), l)
        (dm, dya, dyb, dyc, d2, dgl, do, dcv, du1, dg3, dlng, dlnb, dcfb) = carrier(
            _mixer_bwd_a, dx, sv["m"], vec(g[3]), sv["ya"], sv["yb"], sv["yc"], sv["proj"], sv["o"], sv["cv"], sv["u1"],
            vec(cf_ln_g[l]), vec(cf_ln_b[l]), W[l]["w_ret_o"], W[l]["w_sc_o"], W[l]["w_cf_o"], W[l]["w_o"], f"mixer_bwd_a_{l}")
        grad_ready("w_o", l, _mm_tn(sv["merged"], dm, f"dw_o_{l}", tt_pref=2048))
        grad_ready("w_ret_o", l, _mm_tn(sv["ya_in"], dya, f"dw_ret_o_{l}", tt_pref=2048))
        grad_ready("w_sc_o", l, _mm_tn(sv["yb_in"], dyb, f"dw_sc_o_{l}", tt_pref=2048))
        grad_ready("w_cf_o", l, _mm_tn(sv["yc_in"], dyc, f"dw_cf_o_{l}", tt_pref=2048))
        d4, dscw, dcfw = carrier(_mixer_bwd_b, dcv, du1, sv["proj"], scw_full[l], cfw_full[l], S, f"mixer_bwd_b_{l}")
        dq, dkk, dvv = carrier(_ret_bwd, do, sv["proj"], cosf, sinf, lgs, B, S, D, f"retention_bwd_{l}")
        dproj = jnp.concatenate([dq, dkk, dvv, d2, d4, dgl], axis=1)
        grad_ready("w_in", l, carrier(_mm_tn, sv["h2"], dproj, f"dw_in_{l}", slab_cols=cb_in, tn_pref=cb_in, a_is_transposed=True))
        dx, dg2 = carrier(_proj_bwd, dproj, W[l]["w_in"], sv["x1"], vec(g[2]), dx, f"proj_bwd_{l}")
        dx, dg0, dg1 = ffn_grads("ffn1", "ffn1_w_gu", "ffn1_w_down", dx, sv["x0"], sv["h1"], sv["gate1"], sv["up1"], sv["f1"],
                                 vec(g[0]), vec(g[1]), l)
        small_rows[l] = (jnp.concatenate([dg0, dg1, dg2, dg3, dg4, dg5, dscw, dcfw], axis=0),
                         jnp.concatenate([dcfb, dlng, dlnb], axis=0))
    grad_x = dx.reshape(B, S, D)
    flush = 0
    while to_sibling or to_chips:
        jobs, taken = take_jobs()
        settle(taken, _run_jobs(jobs, f"grads_flush_{flush}"))
        flush += 1

    big_out = {}
    for n in BIG:
        res = None
        for l in reversed(range(L)):
            res = _final_adamw(reduced[(n, l)], w32[n], m32[n], v32[n], l, res, f"adamw_{n}_{l}")
        big_out[n] = res

    sh_part = jnp.concatenate([small_rows[l][0] for l in range(L)], axis=0)
    rep_part = jnp.concatenate([small_rows[l][1] for l in range(L)] + [jnp.zeros((8 - 3 * L % 8, D), F32)] * (1 if 3 * L % 8 else 0), axis=0)
    sh_g = _allgather_small(sh_part, "allgather_small_grads")
    rep_g = _allgather_small(rep_part, "allgather_replicated_grads")

    def pack_sh(a, b, c):
        return jnp.concatenate([a, b, c], axis=1).reshape(L * n_sh, DL)

    def pack_rep(a, b, c):
        rows = jnp.stack([a, b, c], axis=1).reshape(3 * L, D)
        return jnp.concatenate([rows, jnp.ones((rep_part.shape[0] - 3 * L, D), F32)], axis=0)

    sh_res = _small_adamw(dev_idx, sh_g, pack_sh(norm_g, sc_conv_w, cf_dw_w), pack_sh(m_norm_g, m_sc_conv_w, m_cf_dw_w),
                          pack_sh(v_norm_g, v_sc_conv_w, v_cf_dw_w), DL, "adamw_small_sharded")
    rep_res = _small_adamw(dev_idx, rep_g, pack_rep(cf_dw_b, cf_ln_g, cf_ln_b), pack_rep(m_cf_dw_b, m_cf_ln_g, m_cf_ln_b),
                           pack_rep(v_cf_dw_b, v_cf_ln_g, v_cf_ln_b), D, "adamw_small_replicated")

    def unpack_sh(a):
        a = a.reshape(L, n_sh, DL)
        return {"norm_g": a[:, :6], "sc_conv_w": a[:, 6:6 + SC_KERNEL], "cf_dw_w": a[:, 6 + SC_KERNEL:]}

    def unpack_rep(a):
        a = a[:3 * L].reshape(L, 3, D)
        return {"cf_dw_b": a[:, 0], "cf_ln_g": a[:, 1], "cf_ln_b": a[:, 2]}

    order = ("norm_g", "ffn1_w_gu", "ffn1_w_down", "w_in", "w_ret_o", "sc_conv_w", "w_sc_o", "cf_dw_w", "cf_dw_b", "cf_ln_g",
             "cf_ln_b", "w_cf_o", "w_o", "ffn2_w_gu", "ffn2_w_down")
    outs = []
    for kind in range(4):
        small = {**unpack_sh(sh_res[kind]), **unpack_rep(rep_res[kind])}
        outs += [big_out[n][kind] if n in big_out else small[n] for n in order]
    return (loss, grad_x, *outs)
```

```python
import functools

import jax
import jax.numpy as jnp
from jax import lax
from jax.experimental import pallas as pl
from jax.experimental.pallas import tpu as pltpu

F32 = jnp.float32
BF16 = jnp.bfloat16
MESH = pl.DeviceIdType.MESH
ANY = pl.BlockSpec(memory_space=pl.ANY)

N_DEV = 8
N_CHIP = 4
CHUNK = 64
HEADS = 4
ROPE_BASE = 10000.0
NORM_EPS = 1e-6
LN_EPS = 1e-5
SC_KERNEL = 3
CF_KERNEL = 31
HALO = 32
ADAM_LR = 0.001
ADAM_B1 = 0.9
ADAM_B2 = 0.999
ADAM_EPS = 1e-08
ADAM_WD = 0.01
ADAM_STEP = 10
VMEM_LIMIT_V7X = 56 * 1024 * 1024
LIGHT_CARRIER_ELEMS = 3 * 1024 * 1024


class _Job:
    def __init__(self, ins, out_shapes, n_sems, n_local, start, finish):
        self.ins, self.out_shapes, self.n_sems, self.n_local = list(ins), list(out_shapes), n_sems, n_local
        self.start, self.finish = start, finish


def _pallas(body, *, name, grid, in_specs, out_specs, out_shape, scratch_shapes=(), aliases=None, nprefetch=0, jobs=()):
    extra = {}
    single = not isinstance(out_shape, (list, tuple))
    out_shape = [out_shape] if single else list(out_shape)
    out_specs = [out_specs] if single else list(out_specs)
    in_specs, scratch = list(in_specs), list(scratch_shapes)
    n_in, n_out, n_scr = len(in_specs), len(out_shape), len(scratch)
    for jb in jobs:
        in_specs += [ANY] * len(jb.ins)
        out_specs += [ANY] * len(jb.out_shapes)
        out_shape += jb.out_shapes
        scratch += [pltpu.SemaphoreType.DMA((jb.n_sems,)), pltpu.SemaphoreType.DMA((jb.n_sems,)),
                    pltpu.SemaphoreType.DMA((max(jb.n_local, 1),))]

    def with_jobs(*refs):
        pre, refs = refs[:nprefetch], refs[nprefetch:]
        ins, p = refs[:n_in], n_in
        jins = []
        for jb in jobs:
            jins.append(refs[p:p + len(jb.ins)])
            p += len(jb.ins)
        outs = refs[p:p + n_out]
        p += n_out
        jouts = []
        for jb in jobs:
            jouts.append(refs[p:p + len(jb.out_shapes)])
            p += len(jb.out_shapes)
        scr = refs[p:p + n_scr]
        p += n_scr
        pids = [pl.program_id(d) for d in range(len(grid))]
        first = functools.reduce(jnp.logical_and, [pid == 0 for pid in pids])
        last = functools.reduce(jnp.logical_and, [pid == g - 1 for pid, g in zip(pids, grid)])

        @pl.when(first)
        def _():
            for i, jb in enumerate(jobs):
                jb.start(jins[i], jouts[i], *refs[p + 3 * i:p + 3 * i + 3])

        body(*pre, *ins, *outs, *scr)

        @pl.when(last)
        def _():
            for i, jb in enumerate(jobs):
                jb.finish(jins[i], jouts[i], *refs[p + 3 * i:p + 3 * i + 3])

    params = pltpu.CompilerParams(dimension_semantics=("arbitrary",) * len(grid), vmem_limit_bytes=VMEM_LIMIT_V7X)
    spec = pltpu.PrefetchScalarGridSpec(num_scalar_prefetch=nprefetch, grid=grid, in_specs=in_specs,
                                        out_specs=out_specs, scratch_shapes=scratch)
    call = pl.pallas_call(with_jobs if jobs else body, name=name, grid_spec=spec, out_shape=out_shape, compiler_params=params,
                          input_output_aliases=aliases or {}, **extra)

    def run(*args):
        res = call(*args, *[a for jb in jobs for a in jb.ins])
        own = res[0] if single else list(res[:n_out])
        if not jobs:
            return own
        jres, p = [], n_out
        for jb in jobs:
            jres.append(list(res[p:p + len(jb.out_shapes)]))
            p += len(jb.out_shapes)
        return own, jres

    return run


def _comm_call(body, *, name, in_specs, out_specs, out_shape, scratch_shapes):
    extra = {}
    return pl.pallas_call(body, name=name, in_specs=in_specs, out_specs=out_specs, out_shape=out_shape,
                          scratch_shapes=scratch_shapes, **extra)


def _sds(shape, dtype):
    return jax.ShapeDtypeStruct(shape, dtype)


def _tile(n, pref):
    t = min(n, pref)
    assert n % t == 0, (n, pref)
    return t


def _sigmoid(x):
    return jax.nn.sigmoid(x)


def _rms_fwd(x, g):
    r = lax.rsqrt(jnp.mean(x * x, axis=-1, keepdims=True) + NORM_EPS)
    return x * r * g


def _rms_bwd(x, g, dy):
    r = lax.rsqrt(jnp.mean(x * x, axis=-1, keepdims=True) + NORM_EPS)
    xh = x * r
    dg = jnp.sum(dy * xh, axis=0, keepdims=True)
    dxh = dy * g
    dx = r * (dxh - xh * jnp.mean(dxh * xh, axis=-1, keepdims=True))
    return dx, dg


def _ln_stats(x):
    mu = jnp.mean(x, axis=-1, keepdims=True)
    xc = x - mu
    rstd = lax.rsqrt(jnp.mean(xc * xc, axis=-1, keepdims=True) + LN_EPS)
    return xc * rstd, rstd


def _ln_bwd(xh, rstd, dxh):
    return rstd * (dxh - jnp.mean(dxh, axis=-1, keepdims=True) - xh * jnp.mean(dxh * xh, axis=-1, keepdims=True))


def _dsilu(x, s):
    return s * (1.0 + x * (1.0 - s))


def _rot(x, cosf, sinf):
    return x * cosf + pltpu.roll(x, x.shape[-1] // 2, 1) * sinf


def _unrot(d, cosf, sinf):
    return d * cosf - pltpu.roll(d, d.shape[-1] // 2, 1) * sinf


def _dot_nn(a, b):
    return jnp.dot(a, b, preferred_element_type=F32)


def _dot_nt(a, b):
    return lax.dot_general(a, b, (((1,), (1,)), ((), ())), preferred_element_type=F32)


def _dot_tn(a, b):
    return lax.dot_general(a, b, (((0,), (0,)), ((), ())), preferred_element_type=F32)


def _ffn_tf(cb):
    return _tile(cb, 1024)


def _ffn_fwd(x, g_pre, g_post, wgu, wd, name, jobs=()):
    T, D = x.shape
    F = wd.shape[0]
    cb = wgu.shape[2]
    tm, tf = _tile(T, 512), _ffn_tf(cb)
    nj, nb = F // tf, cb // tf

    def body(x_ref, gpre_ref, gpost_ref, wg_ref, wu_ref, wd_ref, xo_ref, ht_ref, gate_ref, up_ref, f_ref, acc_ref, h_ref):
        j = pl.program_id(1)

        @pl.when(j == 0)
        def _():
            h = _rms_fwd(x_ref[...], gpre_ref[...])
            h_ref[...] = h.astype(BF16)
            ht_ref[...] = h.T.astype(BF16)
            acc_ref[...] = jnp.zeros_like(acc_ref)

        h = h_ref[...]
        gate = _dot_nn(h, wg_ref[...])
        up = _dot_nn(h, wu_ref[...])
        gate_ref[...] = gate.astype(BF16)
        up_ref[...] = up.astype(BF16)
        act = (gate * _sigmoid(gate) * up).astype(BF16)
        acc_ref[...] += _dot_nn(act, wd_ref[...])

        @pl.when(j == nj - 1)
        def _():
            f = acc_ref[...]
            f_ref[...] = f
            xo_ref[...] = x_ref[...] + 0.5 * _rms_fwd(f, gpost_ref[...])

    row = pl.BlockSpec((tm, D), lambda i, j: (i, 0))
    vec = pl.BlockSpec((1, D), lambda i, j: (0, 0))
    col = pl.BlockSpec((tm, tf), lambda i, j: (i, j))
    return _pallas(
        body, name=name, grid=(T // tm, nj),
        in_specs=[row, vec, vec,
                  pl.BlockSpec((None, D, tf), lambda i, j: (j // nb, 0, j % nb)),
                  pl.BlockSpec((None, D, tf), lambda i, j: ((nj + j) // nb, 0, j % nb)),
                  pl.BlockSpec((tf, D), lambda i, j: (j, 0))],
        out_specs=[row, pl.BlockSpec((D, tm), lambda i, j: (0, i)), col, col, row],
        out_shape=[_sds((T, D), F32), _sds((D, T), BF16), _sds((T, F), BF16), _sds((T, F), BF16), _sds((T, D), F32)],
        scratch_shapes=[pltpu.VMEM((tm, D), F32), pltpu.VMEM((tm, D), BF16)], jobs=jobs,
    )(x, g_pre, g_post, wgu, wgu, wd)


def _ffn_bwd(dxo, x, f, gate, up, g_pre, g_post, wgu, wd, name, jobs=()):
    T, D = x.shape
    F = wd.shape[0]
    cb = wgu.shape[2]
    tm, tf = _tile(T, 512), _ffn_tf(cb)
    nj, nb = F // tf, cb // tf

    def body(dxo_ref, x_ref, f_ref, gate_ref, up_ref, gpre_ref, gpost_ref, wg_ref, wu_ref, wd_ref,
             dx_ref, df_ref, dgate_ref, dup_ref, act_ref, dgpre_ref, dgpost_ref, acc_ref):
        i, j = pl.program_id(0), pl.program_id(1)

        @pl.when((i == 0) & (j == 0))
        def _():
            dgpre_ref[...] = jnp.zeros_like(dgpre_ref)
            dgpost_ref[...] = jnp.zeros_like(dgpost_ref)

        @pl.when(j == 0)
        def _():
            df, dgp = _rms_bwd(f_ref[...], gpost_ref[...], 0.5 * dxo_ref[...])
            df_ref[...] = df.astype(BF16)
            dgpost_ref[...] += dgp
            acc_ref[...] = jnp.zeros_like(acc_ref)

        dact = _dot_nt(df_ref[...], wd_ref[...])
        g = gate_ref[...].astype(F32)
        u = up_ref[...].astype(F32)
        s = _sigmoid(g)
        silu = g * s
        dgate = (dact * u * _dsilu(g, s)).astype(BF16)
        dup = (dact * silu).astype(BF16)
        act_ref[...] = (silu * u).astype(BF16)
        dgate_ref[...] = dgate
        dup_ref[...] = dup
        acc_ref[...] += _dot_nt(dgate, wg_ref[...]) + _dot_nt(dup, wu_ref[...])

        @pl.when(j == nj - 1)
        def _():
            dxin, dgp = _rms_bwd(x_ref[...], gpre_ref[...], acc_ref[...])
            dx_ref[...] = dxo_ref[...] + dxin
            dgpre_ref[...] += dgp

    row = pl.BlockSpec((tm, D), lambda i, j: (i, 0))
    vec = pl.BlockSpec((1, D), lambda i, j: (0, 0))
    col = pl.BlockSpec((tm, tf), lambda i, j: (i, j))
    return _pallas(
        body, name=name, grid=(T // tm, nj),
        in_specs=[row, row, row, col, col, vec, vec,
                  pl.BlockSpec((None, D, tf), lambda i, j: (j // nb, 0, j % nb)),
                  pl.BlockSpec((None, D, tf), lambda i, j: ((nj + j) // nb, 0, j % nb)),
                  pl.BlockSpec((tf, D), lambda i, j: (j, 0))],
        out_specs=[row, row, col, col, col, vec, vec],
        out_shape=[_sds((T, D), F32), _sds((T, D), BF16), _sds((T, F), BF16), _sds((T, F), BF16), _sds((T, F), BF16),
                   _sds((1, D), F32), _sds((1, D), F32)],
        scratch_shapes=[pltpu.VMEM((tm, D), F32)], jobs=jobs,
    )(dxo, x, f, gate, up, g_pre, g_post, wgu, wgu, wd)


def _mm_tn(a, b, name, slab_cols=None, n_slabs=None, first_slab=0, into=None, tn_pref=1024, tt_pref=2048,
           a_is_transposed=False, jobs=()):
    T, N = b.shape
    K = a.shape[0] if a_is_transposed else a.shape[1]
    tt = _tile(T, tt_pref)
    tko = _tile(K, 1024)
    tn = _tile(slab_cols or N, tn_pref)
    nt = T // tt

    def body(a_ref, b_ref, *rest):
        o_ref, acc_ref = rest[-2:]
        t = pl.program_id(2)

        @pl.when(t == 0)
        def _():
            acc_ref[...] = jnp.zeros_like(acc_ref)

        acc_ref[...] += (_dot_nn if a_is_transposed else _dot_tn)(a_ref[...], b_ref[...])

        @pl.when(t == nt - 1)
        def _():
            o_ref[...] = acc_ref[...].astype(o_ref.dtype)

    if slab_cols is None:
        shape = (K, N)
        ospec = pl.BlockSpec((tko, tn), lambda k, jn, t: (k, jn))
    else:
        nb = slab_cols // tn
        shape = (n_slabs or N // slab_cols, K, slab_cols)
        ospec = pl.BlockSpec((None, tko, tn), lambda k, jn, t: (first_slab + jn // nb, k, jn % nb))
    aspec = pl.BlockSpec((tko, tt), lambda k, jn, t: (k, t)) if a_is_transposed else pl.BlockSpec((tt, tko), lambda k, jn, t: (t, k))
    in_specs, args = [aspec, pl.BlockSpec((tt, tn), lambda k, jn, t: (t, jn))], [a, b]
    if into is not None:
        in_specs.append(ANY)
        args.append(into)
    return _pallas(body, name=name, grid=(K // tko, N // tn, nt), in_specs=in_specs, out_specs=ospec, out_shape=_sds(shape, BF16),
                   scratch_shapes=[pltpu.VMEM((tko, tn), F32)], aliases={2: 0} if into is not None else None, jobs=jobs)(*args)


def _norm_proj(x, g, w, name, jobs=()):
    T, D = x.shape
    nd, cb = w.shape[0], w.shape[2]
    tm = _tile(T, 1024)

    def body(x_ref, g_ref, w_ref, p_ref, ht_ref, h_ref):
        @pl.when(pl.program_id(1) == 0)
        def _():
            h = _rms_fwd(x_ref[...], g_ref[...])
            h_ref[...] = h.astype(BF16)
            ht_ref[...] = h.T.astype(BF16)

        p_ref[...] = _dot_nn(h_ref[...], w_ref[...])

    row = pl.BlockSpec((tm, D), lambda i, j: (i, 0))
    return _pallas(
        body, name=name, grid=(T // tm, nd),
        in_specs=[row, pl.BlockSpec((1, D), lambda i, j: (0, 0)),
                  pl.BlockSpec((None, D, cb), lambda i, j: (j, 0, 0))],
        out_specs=[pl.BlockSpec((tm, cb), lambda i, j: (i, j)), pl.BlockSpec((D, tm), lambda i, j: (0, i))],
        out_shape=[_sds((T, nd * cb), F32), _sds((D, T), BF16)],
        scratch_shapes=[pltpu.VMEM((tm, D), BF16)], jobs=jobs,
    )(x, g, w)


def _proj_bwd(dproj, w, x, g, dxo, name, jobs=()):
    T, D = x.shape
    nd, cb = w.shape[0], w.shape[2]
    tm = _tile(T, 1024)

    def body(dp_ref, w_ref, x_ref, g_ref, dxo_ref, dx_ref, dg_ref, acc_ref):
        i, j = pl.program_id(0), pl.program_id(1)

        @pl.when((i == 0) & (j == 0))
        def _():
            dg_ref[...] = jnp.zeros_like(dg_ref)

        @pl.when(j == 0)
        def _():
            acc_ref[...] = jnp.zeros_like(acc_ref)

        acc_ref[...] += _dot_nt(dp_ref[...], w_ref[...])

        @pl.when(j == nd - 1)
        def _():
            dxin, dgp = _rms_bwd(x_ref[...], g_ref[...], acc_ref[...])
            dx_ref[...] = dxo_ref[...] + dxin
            dg_ref[...] += dgp

    row = pl.BlockSpec((tm, D), lambda i, j: (i, 0))
    vec = pl.BlockSpec((1, D), lambda i, j: (0, 0))
    return _pallas(
        body, name=name, grid=(T // tm, nd),
        in_specs=[pl.BlockSpec((tm, cb), lambda i, j: (i, j)),
                  pl.BlockSpec((None, D, cb), lambda i, j: (j, 0, 0)), row, vec, row],
        out_specs=[row, vec],
        out_shape=[_sds((T, D), F32), _sds((1, D), F32)],
        scratch_shapes=[pltpu.VMEM((tm, D), F32)], jobs=jobs,
    )(dproj, w, x, g, dxo)


def _rope_tables(pos, invf, sgn, name):
    T = pos.shape[0]
    dk = invf.shape[1]
    tm = _tile(T, 1024)

    def body(p_ref, f_ref, s_ref, c_out, s_out):
        ang = p_ref[...].astype(F32) * f_ref[...]
        c_out[...] = jnp.cos(ang)
        s_out[...] = jnp.sin(ang) * s_ref[...]

    vec = pl.BlockSpec((1, dk), lambda i: (0, 0))
    out = pl.BlockSpec((tm, dk), lambda i: (i, 0))
    return _pallas(body, name=name, grid=(T // tm,), in_specs=[pl.BlockSpec((tm, 1), lambda i: (i, 0)), vec, vec],
                   out_specs=[out, out], out_shape=[_sds((T, dk), F32), _sds((T, dk), F32)])(pos, invf, sgn)


def _decay(lg, r0, tq, n):
    r = r0 + lax.broadcasted_iota(jnp.int32, (tq, n), 0)
    c = lax.broadcasted_iota(jnp.int32, (tq, n), 1)
    rc, cc = r // CHUNK, c // CHUNK
    d = (r - c).astype(F32)
    e = jnp.where(rc == cc, jnp.abs(d), d)
    return jnp.where(cc > rc, 0.0, jnp.exp(lg * e))


def _ret_fwd(proj, cosf, sinf, lgs, B, S, D, name, jobs=()):
    T = B * S
    dk, dv = D // 8, D // 4
    tq = _tile(S, 256)
    scale = dk ** -0.5

    def body(lg_ref, q_ref, k_ref, v_ref, c_ref, s_ref, o_ref, kr_ref, vb_ref):
        lg = lg_ref[pl.program_id(1)]
        kr_ref[...] = (_rot(k_ref[...], c_ref[...], s_ref[...]) * scale).astype(BF16)
        vb_ref[...] = v_ref[...].astype(BF16)
        for qi in range(S // tq):
            rows, n = slice(qi * tq, (qi + 1) * tq), (qi + 1) * tq
            q = _rot(q_ref[rows, :], c_ref[rows, :], s_ref[rows, :]).astype(BF16)
            p = (_dot_nt(q, kr_ref[:n, :]) * _decay(lg, qi * tq, tq, n)).astype(BF16)
            o_ref[rows, :] = _dot_nn(p, vb_ref[:n, :])

    return _pallas(
        body, name=name, grid=(B, HEADS),
        in_specs=[pl.BlockSpec(memory_space=pltpu.SMEM),
                  pl.BlockSpec((S, dk), lambda b, h: (b, h)),
                  pl.BlockSpec((S, dk), lambda b, h: (b, HEADS + h)),
                  pl.BlockSpec((S, dv), lambda b, h: (b, HEADS + h)),
                  pl.BlockSpec((S, dk), lambda b, h: (b, 0)),
                  pl.BlockSpec((S, dk), lambda b, h: (b, 0))],
        out_specs=pl.BlockSpec((S, dv), lambda b, h: (b, h)),
        out_shape=_sds((T, D), F32),
        scratch_shapes=[pltpu.VMEM((S, dk), BF16), pltpu.VMEM((S, dv), BF16)], jobs=jobs,
    )(lgs, proj, proj, proj, cosf, sinf)


def _ret_bwd(do, proj, cosf, sinf, lgs, dproj, B, S, D, name, jobs=()):
    dk, dv = D // 8, D // 4
    tq = _tile(S, 256)
    scale = dk ** -0.5

    def body(lg_ref, do_ref, q_ref, k_ref, v_ref, c_ref, s_ref, dproj_in, dproj_ref, kr_ref, vb_ref, dka_ref, dva_ref,
             dq_ref, dk_ref, dv_ref, copy_sems):
        b, h = pl.program_id(0), pl.program_id(1)
        lg = lg_ref[h]
        kr_ref[...] = (_rot(k_ref[...], c_ref[...], s_ref[...]) * scale).astype(BF16)
        vb_ref[...] = v_ref[...].astype(BF16)
        dka_ref[...] = jnp.zeros_like(dka_ref)
        dva_ref[...] = jnp.zeros_like(dva_ref)

        for qi in range(S // tq):
            rows, n = slice(qi * tq, (qi + 1) * tq), (qi + 1) * tq
            cq, sq = c_ref[rows, :], s_ref[rows, :]
            q = _rot(q_ref[rows, :], cq, sq).astype(BF16)
            dout = do_ref[rows, :].astype(BF16)
            w = _decay(lg, qi * tq, tq, n)
            p = (_dot_nt(q, kr_ref[:n, :]) * w).astype(BF16)
            ds = (_dot_nt(dout, vb_ref[:n, :]) * w).astype(BF16)
            dq_ref[rows, :] = _unrot(_dot_nn(ds, kr_ref[:n, :]), cq, sq).astype(BF16)
            dka_ref[:n, :] += _dot_tn(ds, q)
            dva_ref[:n, :] += _dot_tn(p, dout)

        dk_ref[...] = _unrot(dka_ref[...] * scale, c_ref[...], s_ref[...]).astype(BF16)
        dv_ref[...] = dva_ref[...].astype(BF16)
        rows = pl.ds(pl.multiple_of(b * S, S), S)
        copies = [pltpu.make_async_copy(dq_ref, dproj_ref.at[rows, pl.ds(pl.multiple_of(h * dk, dk), dk)], copy_sems.at[0]),
                  pltpu.make_async_copy(dk_ref, dproj_ref.at[rows, pl.ds(pl.multiple_of((HEADS + h) * dk, dk), dk)], copy_sems.at[1]),
                  pltpu.make_async_copy(dv_ref, dproj_ref.at[rows, pl.ds(pl.multiple_of((HEADS + h) * dv, dv), dv)], copy_sems.at[2])]
        for cp in copies:
            cp.start()
        for cp in copies:
            cp.wait()

    return _pallas(
        body, name=name, grid=(B, HEADS),
        in_specs=[pl.BlockSpec(memory_space=pltpu.SMEM),
                  pl.BlockSpec((S, dv), lambda b, h: (b, h)),
                  pl.BlockSpec((S, dk), lambda b, h: (b, h)),
                  pl.BlockSpec((S, dk), lambda b, h: (b, HEADS + h)),
                  pl.BlockSpec((S, dv), lambda b, h: (b, HEADS + h)),
                  pl.BlockSpec((S, dk), lambda b, h: (b, 0)),
                  pl.BlockSpec((S, dk), lambda b, h: (b, 0)), ANY],
        out_specs=ANY, out_shape=_sds(dproj.shape, BF16),
        scratch_shapes=[pltpu.VMEM((S, dk), BF16), pltpu.VMEM((S, dv), BF16), pltpu.VMEM((S, dk), F32), pltpu.VMEM((S, dv), F32),
                        pltpu.VMEM((S, dk), BF16), pltpu.VMEM((S, dk), BF16), pltpu.VMEM((S, dv), BF16), pltpu.SemaphoreType.DMA((3,))],
        aliases={7: 0}, jobs=jobs,
    )(lgs, do, proj, proj, proj, cosf, sinf, dproj)


COL_GRET, COL_SCB, COL_SCC, COL_SCX, COL_GLUA, COL_GLUB, COL_GL = 2, 3, 4, 5, 6, 7, 8


SUBLANES, LANES = 8, 128
CONV_BLOCK = 256
SC_FWD = tuple(HALO - (SC_KERNEL - 1) + j for j in range(SC_KERNEL))
CF_FWD = tuple(HALO - (CF_KERNEL - 1) + j for j in range(CF_KERNEL))
SC_BWD = tuple(SC_KERNEL - 1 - j for j in range(SC_KERNEL))
CF_BWD = tuple(CF_KERNEL - 1 - j for j in range(CF_KERNEL))


def _build_shifted(src_ref, e_ref, shifts):
    n = e_ref.shape[1]
    for b in sorted({s % SUBLANES for s in shifts} - {0}):
        e_ref[b - 1] = src_ref[pl.ds(b, n), :]


def _shifted_rows(src_ref, e_ref, s, r0, rows, lanes):
    a, b = divmod(s, SUBLANES)
    at = pl.ds(r0 + SUBLANES * a, rows)
    return src_ref[at, lanes] if b == 0 else e_ref[b - 1, at, lanes]


def _tree_sum(terms):
    while len(terms) > 1:
        terms = [a + b for a, b in zip(terms[::2], terms[1::2])] + ([terms[-1]] if len(terms) % 2 else [])
    return terms[0]


def _conv_taps(src_ref, e_ref, w_ref, shifts, tm, D, emit, u_ref=None, dw_ref=None):
    rows = min(tm, CONV_BLOCK if u_ref is None else CONV_BLOCK // 2)
    for lt in range(D // LANES):
        lanes = pl.ds(lt * LANES, LANES)
        for r0 in range(0, tm, rows):
            u = None if u_ref is None else u_ref[pl.ds(r0, rows), lanes]
            acc = None
            for j, s in enumerate(shifts):
                v = _shifted_rows(src_ref, e_ref, s, r0, rows, lanes)
                term = v * w_ref[j:j + 1, lanes]
                acc = term if acc is None else acc + term
                if u is not None:
                    prod = u * v
                    part = _tree_sum([prod[g * SUBLANES:(g + 1) * SUBLANES, :] for g in range(rows // SUBLANES)])
                    dw_ref[j:j + 1, lanes] += jnp.sum(part, axis=0, keepdims=True)
            emit(lt, pl.ds(r0, rows), acc)


def _mixer_mid_fwd(proj, o, sc_w, cf_w, cf_b, ln_g, ln_b, S, name, jobs=()):
    T, D = o.shape
    dv = D // HEADS
    tm = _tile(S, 256)
    per_seq = S // tm
    hb = tm // HALO

    def body(gret_ref, scb_ref, scc_ref, scx_ref, ga_ref, gb_ref, scc_h, scx_h, ga_h, gb_h, o_ref,
             scw_ref, cfw_ref, cfb_ref, lng_ref, lnb_ref, ya_ref, yb_ref, yc_ref, cv_ref, u1_ref, ext_ref, e_ref):
        first = (pl.program_id(0) % per_seq) == 0
        keep = jnp.where(first, 0.0, 1.0)
        gr = gret_ref[...]
        sg = gr * _sigmoid(gr)
        for hh in range(HEADS):
            cols = slice(hh * dv, (hh + 1) * dv)
            hn, _ = _ln_stats(o_ref[:, cols])
            ya_ref[:, cols] = (sg[:, cols] * hn).astype(BF16)
        ext_ref[:HALO, :] = scc_h[...] * scx_h[...] * keep
        ext_ref[HALO:, :] = scc_ref[...] * scx_ref[...]
        _build_shifted(ext_ref, e_ref, SC_FWD)

        def emit_cv(lt, rows, cv):
            lanes = pl.ds(lt * LANES, LANES)
            cv_ref[rows, lanes] = cv
            yb_ref[rows, lanes] = (scb_ref[rows, lanes] * cv).astype(BF16)

        _conv_taps(ext_ref, e_ref, scw_ref, SC_FWD, tm, D, emit_cv)
        ext_ref[:HALO, :] = ga_h[...] * _sigmoid(gb_h[...]) * keep
        ext_ref[HALO:, :] = ga_ref[...] * _sigmoid(gb_ref[...])
        _build_shifted(ext_ref, e_ref, CF_FWD)

        def emit_u1(lt, rows, acc):
            lanes = pl.ds(lt * LANES, LANES)
            u1_ref[rows, lanes] = acc + cfb_ref[:, lanes]

        _conv_taps(ext_ref, e_ref, cfw_ref, CF_FWD, tm, D, emit_u1)
        xh, _ = _ln_stats(u1_ref[...])
        u2 = xh * lng_ref[...] + lnb_ref[...]
        yc_ref[...] = (u2 * _sigmoid(u2)).astype(BF16)

    def colblk(c):
        return pl.BlockSpec((tm, D), lambda i: (i, c))

    def halo(c):
        return pl.BlockSpec((HALO, D), lambda i: (jnp.maximum(i * hb - 1, 0), c))

    row = pl.BlockSpec((tm, D), lambda i: (i, 0))
    vec = pl.BlockSpec((1, D), lambda i: (0, 0))
    return _pallas(
        body, name=name, grid=(T // tm,),
        in_specs=[colblk(COL_GRET), colblk(COL_SCB), colblk(COL_SCC), colblk(COL_SCX), colblk(COL_GLUA), colblk(COL_GLUB),
                  halo(COL_SCC), halo(COL_SCX), halo(COL_GLUA), halo(COL_GLUB), row,
                  pl.BlockSpec((SC_KERNEL, D), lambda i: (0, 0)), pl.BlockSpec((CF_KERNEL, D), lambda i: (0, 0)), vec, vec, vec],
        out_specs=[row, row, row, row, row],
        out_shape=[_sds((T, D), BF16)] * 3 + [_sds((T, D), F32)] * 2,
        scratch_shapes=[pltpu.VMEM((HALO + tm, D), F32), pltpu.VMEM((SUBLANES - 1, HALO + tm - SUBLANES, D), F32)], jobs=jobs,
    )(proj, proj, proj, proj, proj, proj, proj, proj, proj, proj, o, sc_w, cf_w, cf_b, ln_g, ln_b)


def _mixer_merge_fwd(x, proj, ya_in, yb_in, yc_in, g_post, w_ret, w_sc, w_cf, w_o, name, jobs=()):
    T, D = x.shape
    tm = _tile(T, 512)

    def body(x_ref, g0_ref, g1_ref, g2_ref, ya_in_ref, yb_in_ref, yc_in_ref, gp_ref, wr_ref, ws_ref, wc_ref, wo_ref,
             xo_ref, ya_ref, yb_ref, yc_ref, mg_ref, m_ref):
        ya = _dot_nn(ya_in_ref[...], wr_ref[...])
        yb = _dot_nn(yb_in_ref[...], ws_ref[...])
        yc = _dot_nn(yc_in_ref[...], wc_ref[...])
        ya_ref[...] = ya.astype(BF16)
        yb_ref[...] = yb.astype(BF16)
        yc_ref[...] = yc.astype(BF16)
        merged = (_sigmoid(g0_ref[...]) * ya + _sigmoid(g1_ref[...]) * yb + _sigmoid(g2_ref[...]) * yc).astype(BF16)
        mg_ref[...] = merged
        m = _dot_nn(merged, wo_ref[...])
        m_ref[...] = m
        xo_ref[...] = x_ref[...] + _rms_fwd(m, gp_ref[...])

    row = pl.BlockSpec((tm, D), lambda i: (i, 0))
    wsp = pl.BlockSpec((D, D), lambda i: (0, 0), pipeline_mode=pl.Buffered(1))

    def colblk(c):
        return pl.BlockSpec((tm, D), lambda i: (i, c))

    return _pallas(
        body, name=name, grid=(T // tm,),
        in_specs=[row, colblk(COL_GL), colblk(COL_GL + 1), colblk(COL_GL + 2), row, row, row,
                  pl.BlockSpec((1, D), lambda i: (0, 0)), wsp, wsp, wsp, wsp],
        out_specs=[row] * 6,
        out_shape=[_sds((T, D), F32)] + [_sds((T, D), BF16)] * 4 + [_sds((T, D), F32)], jobs=jobs,
    )(x, proj, proj, proj, ya_in, yb_in, yc_in, g_post, w_ret, w_sc, w_cf, w_o)


def _mixer_bwd_a(dxo, m, g_post, ya, yb, yc, proj, o, cv, u1, ln_g, ln_b, w_ret, w_sc, w_cf, w_o, name, jobs=()):
    T, D = m.shape
    dv = D // HEADS
    tm = _tile(T, 256)

    def body(dxo_ref, m_ref, gp_ref, ya_ref, yb_ref, yc_ref, g0_ref, g1_ref, g2_ref, gret_ref, scb_ref, o_ref, cv_ref, u1_ref,
             lng_ref, lnb_ref, wr_ref, ws_ref, wc_ref, wo_ref,
             dm_ref, dya_ref, dyb_ref, dyc_ref, dproj_ref, do_ref, dcv_ref, du1_ref, dgp_ref, dlng_ref, dlnb_ref, dcfb_ref,
             d2_ref, dgl_ref, copy_sems):
        @pl.when(pl.program_id(0) == 0)
        def _():
            for r in (dgp_ref, dlng_ref, dlnb_ref, dcfb_ref):
                r[...] = jnp.zeros_like(r)

        dm, dgp = _rms_bwd(m_ref[...], gp_ref[...], dxo_ref[...])
        dgp_ref[...] += dgp
        dmb = dm.astype(BF16)
        dm_ref[...] = dmb
        dmerged = _dot_nt(dmb, wo_ref[...])
        dys = []
        for k, (g_ref, y_ref, dy_ref) in enumerate(((g0_ref, ya_ref, dya_ref), (g1_ref, yb_ref, dyb_ref), (g2_ref, yc_ref, dyc_ref))):
            sg = _sigmoid(g_ref[...])
            dgl_ref[:, k * D:(k + 1) * D] = (dmerged * y_ref[...].astype(F32) * sg * (1.0 - sg)).astype(BF16)
            dy = (dmerged * sg).astype(BF16)
            dy_ref[...] = dy
            dys.append(dy)
        dya_in = _dot_nt(dys[0], wr_ref[...])
        gr = gret_ref[...]
        sr = _sigmoid(gr)
        for hh in range(HEADS):
            cols = slice(hh * dv, (hh + 1) * dv)
            hn, rstd = _ln_stats(o_ref[:, cols])
            d2_ref[:, cols] = (dya_in[:, cols] * hn * _dsilu(gr[:, cols], sr[:, cols])).astype(BF16)
            do_ref[:, cols] = _ln_bwd(hn, rstd, dya_in[:, cols] * gr[:, cols] * sr[:, cols])
        dyb_in = _dot_nt(dys[1], ws_ref[...])
        d2_ref[:, D:] = (dyb_in * cv_ref[...]).astype(BF16)
        dcv_ref[...] = dyb_in * scb_ref[...]
        dyc_in = _dot_nt(dys[2], wc_ref[...])
        xh, rstd = _ln_stats(u1_ref[...])
        u2 = xh * lng_ref[...] + lnb_ref[...]
        du2 = dyc_in * _dsilu(u2, _sigmoid(u2))
        dlng_ref[...] += jnp.sum(du2 * xh, axis=0, keepdims=True)
        dlnb_ref[...] += jnp.sum(du2, axis=0, keepdims=True)
        du1 = _ln_bwd(xh, rstd, du2 * lng_ref[...])
        du1_ref[...] = du1
        dcfb_ref[...] += jnp.sum(du1, axis=0, keepdims=True)
        rows = pl.ds(pl.multiple_of(pl.program_id(0) * tm, tm), tm)
        copies = [pltpu.make_async_copy(d2_ref, dproj_ref.at[rows, pl.ds(COL_GRET * D, 2 * D)], copy_sems.at[0]),
                  pltpu.make_async_copy(dgl_ref, dproj_ref.at[rows, pl.ds(COL_GL * D, 3 * D)], copy_sems.at[1])]
        for cp in copies:
            cp.start()
        for cp in copies:
            cp.wait()

    row = pl.BlockSpec((tm, D), lambda i: (i, 0))
    vec = pl.BlockSpec((1, D), lambda i: (0, 0))
    wsp = pl.BlockSpec((D, D), lambda i: (0, 0), pipeline_mode=pl.Buffered(1))

    def colblk(c):
        return pl.BlockSpec((tm, D), lambda i: (i, c))

    return _pallas(
        body, name=name, grid=(T // tm,),
        in_specs=[row, row, vec, row, row, row, colblk(COL_GL), colblk(COL_GL + 1), colblk(COL_GL + 2),
                  colblk(COL_GRET), colblk(COL_SCB), row, row, row, vec, vec, wsp, wsp, wsp, wsp],
        out_specs=[row, row, row, row, ANY, row, row, row, vec, vec, vec, vec],
        out_shape=[_sds((T, D), BF16)] * 4 + [_sds((T, proj.shape[1]), BF16)] + [_sds((T, D), F32)] * 3 + [_sds((1, D), F32)] * 4,
        scratch_shapes=[pltpu.VMEM((tm, 2 * D), BF16), pltpu.VMEM((tm, 3 * D), BF16), pltpu.SemaphoreType.DMA((2,))], jobs=jobs,
    )(dxo, m, g_post, ya, yb, yc, proj, proj, proj, proj, proj, o, cv, u1, ln_g, ln_b, w_ret, w_sc, w_cf, w_o)


def _mixer_bwd_b(dcv, du1, proj, sc_w, cf_w, dproj, S, name, jobs=()):
    T, D = dcv.shape
    tm = _tile(S, 256)
    per_seq = S // tm
    hb = tm // HALO
    last_hb = T // HALO - 1

    def body(dcv_ref, du1_ref, dcv_n, du1_n, scc_ref, scx_ref, ga_ref, gb_ref, scw_ref, cfw_ref, dproj_in,
             d4_ref, dscw_ref, dcfw_ref, u_ref, dext_ref, e_ref, sb_ref):
        i = pl.program_id(0)
        keep_next = jnp.where((i % per_seq) == per_seq - 1, 0.0, 1.0)

        @pl.when(i == 0)
        def _():
            dscw_ref[...] = jnp.zeros_like(dscw_ref)
            dcfw_ref[...] = jnp.zeros_like(dcfw_ref)

        dext_ref[:tm, :] = dcv_ref[...]
        dext_ref[tm:, :] = dcv_n[...] * keep_next
        _build_shifted(dext_ref, e_ref, SC_BWD)
        u_ref[...] = scc_ref[...] * scx_ref[...]

        def emit_dz(lt, rows, dz):
            lanes = pl.ds(lt * LANES, LANES)
            d4_ref[rows, pl.ds(lt * LANES, LANES)] = (dz * scx_ref[rows, lanes]).astype(BF16)
            d4_ref[rows, pl.ds(D + lt * LANES, LANES)] = (dz * scc_ref[rows, lanes]).astype(BF16)

        _conv_taps(dext_ref, e_ref, scw_ref, SC_BWD, tm, D, emit_dz, u_ref=u_ref, dw_ref=dscw_ref)
        dext_ref[:tm, :] = du1_ref[...]
        dext_ref[tm:, :] = du1_n[...] * keep_next
        _build_shifted(dext_ref, e_ref, CF_BWD)
        sb_ref[...] = _sigmoid(gb_ref[...])
        u_ref[...] = ga_ref[...] * sb_ref[...]

        def emit_du0(lt, rows, du0):
            lanes = pl.ds(lt * LANES, LANES)
            sb = sb_ref[rows, lanes]
            d4_ref[rows, pl.ds(2 * D + lt * LANES, LANES)] = (du0 * sb).astype(BF16)
            d4_ref[rows, pl.ds(3 * D + lt * LANES, LANES)] = (du0 * ga_ref[rows, lanes] * sb * (1.0 - sb)).astype(BF16)

        _conv_taps(dext_ref, e_ref, cfw_ref, CF_BWD, tm, D, emit_du0, u_ref=u_ref, dw_ref=dcfw_ref)

    row = pl.BlockSpec((tm, D), lambda i: (i, 0))
    nxt = pl.BlockSpec((HALO, D), lambda i: (jnp.minimum((i + 1) * hb, last_hb), 0))

    def colblk(c):
        return pl.BlockSpec((tm, D), lambda i: (i, c))

    return _pallas(
        body, name=name, grid=(T // tm,),
        in_specs=[row, row, nxt, nxt, colblk(COL_SCC), colblk(COL_SCX), colblk(COL_GLUA), colblk(COL_GLUB),
                  pl.BlockSpec((SC_KERNEL, D), lambda i: (0, 0)), pl.BlockSpec((CF_KERNEL, D), lambda i: (0, 0)), ANY],
        out_specs=[pl.BlockSpec((tm, 4 * D), lambda i: (i, COL_SCC // 4)), pl.BlockSpec((SC_KERNEL, D), lambda i: (0, 0)),
                   pl.BlockSpec((CF_KERNEL, D), lambda i: (0, 0))],
        out_shape=[_sds(dproj.shape, BF16), _sds((SC_KERNEL, D), F32), _sds((CF_KERNEL, D), F32)],
        scratch_shapes=[pltpu.VMEM((tm, D), F32), pltpu.VMEM((tm + HALO, D), F32),
                        pltpu.VMEM((SUBLANES - 1, HALO + tm - SUBLANES, D), F32), pltpu.VMEM((tm, D), F32)],
        aliases={10: 0}, jobs=jobs,
    )(dcv, du1, dcv, du1, proj, proj, proj, proj, sc_w, cf_w, dproj)


def _loss_grad(y, tgt, name):
    T, D = y.shape
    tm = _tile(T, 512)

    def body(y_ref, t_ref, dy_ref, loss_ref):
        @pl.when(pl.program_id(0) == 0)
        def _():
            loss_ref[...] = jnp.zeros_like(loss_ref)

        e = y_ref[...] - t_ref[...]
        dy_ref[...] = e * (1.0 / D)
        loss_ref[...] += 0.5 * jnp.sum(jnp.sum(e * e, axis=-1, keepdims=True) * (1.0 / D), axis=0, keepdims=True)

    row = pl.BlockSpec((tm, D), lambda i: (i, 0))
    return _pallas(body, name=name, grid=(T // tm,), in_specs=[row, row],
                   out_specs=[row, pl.BlockSpec((1, 1), lambda i: (0, 0))],
                   out_shape=[_sds((T, D), F32), _sds((1, 1), F32)])(y, tgt)


def _mesh_pos():
    return lax.axis_index("x"), lax.axis_index("y"), lax.axis_index("c")


def _other_chips(x, y):
    return [(1 - x, y), (x, 1 - y), (1 - x, 1 - y)]


def _allgather_job(shards):
    nt = len(shards)

    def parts(ins, outs, send, recv):
        x, y, c = _mesh_pos()

        def slab(t, px, py, pc):
            return outs[t].at[4 * px + 2 * py + pc]

        def copy(t, k, block, to, src=None):
            return pltpu.make_async_remote_copy(
                src_ref=slab(t, *block) if src is None else src, dst_ref=slab(t, *block),
                send_sem=send.at[7 * t + k], recv_sem=recv.at[7 * t + k], device_id=to, device_id_type=MESH)

        return (x, y, c), (x, y, 1 - c), _other_chips(x, y), c, slab, copy

    def start(ins, outs, send, recv, loc):
        me, sibling, chips, c, slab, copy = parts(ins, outs, send, recv)
        for t in range(nt):
            pltpu.make_async_copy(ins[t], slab(t, *me), loc.at[t]).start()
            copy(t, 0, me, sibling, src=ins[t]).start()
            for j, chip in enumerate(chips):
                copy(t, 1 + j, me, (*chip, c), src=ins[t]).start()

    def finish(ins, outs, send, recv, loc):
        me, sibling, chips, c, slab, copy = parts(ins, outs, send, recv)
        for j, chip in enumerate(chips):
            for t in range(nt):
                copy(t, 1 + j, (*chip, c), me).wait_recv()
                copy(t, 4 + j, (*chip, c), sibling).start()
        for t in range(nt):
            copy(t, 0, sibling, me).wait_recv()
            for j, chip in enumerate(chips):
                copy(t, 4 + j, (*chip, 1 - c), me).wait_recv()
        for t in range(nt):
            copy(t, 0, me, sibling, src=ins[t]).wait_send()
            for j, chip in enumerate(chips):
                copy(t, 1 + j, me, (*chip, c), src=ins[t]).wait_send()
                copy(t, 4 + j, (*chip, c), sibling).wait_send()
            pltpu.make_async_copy(ins[t], slab(t, *me), loc.at[t]).wait()

    return _Job(shards, [_sds((N_DEV,) + s.shape, s.dtype) for s in shards], 7 * nt, nt, start, finish)


def _to_sibling_job(grads):
    nt = len(grads)

    def copies(ins, outs, send, recv):
        x, y, c = _mesh_pos()
        return [pltpu.make_async_remote_copy(src_ref=ins[t].at[:, 1 - c], dst_ref=outs[t], send_sem=send.at[t], recv_sem=recv.at[t],
                                             device_id=(x, y, 1 - c), device_id_type=MESH) for t in range(nt)]

    def start(ins, outs, send, recv, loc):
        for cp in copies(ins, outs, send, recv):
            cp.start()

    def finish(ins, outs, send, recv, loc):
        for cp in copies(ins, outs, send, recv):
            cp.wait()

    return _Job(grads, [_sds((N_CHIP,) + g.shape[2:], g.dtype) for g in grads], nt, 0, start, finish)


def _to_chips_job(pairs):
    nt = len(pairs)

    def copies(ins, outs, send, recv, loc):
        x, y, c = _mesh_pos()
        remote = [pltpu.make_async_remote_copy(src_ref=ins[t].at[2 * px + py], dst_ref=outs[t].at[k], send_sem=send.at[3 * t + k],
                                               recv_sem=recv.at[3 * t + k], device_id=(px, py, c), device_id_type=MESH)
                  for t in range(nt) for k, (px, py) in enumerate(_other_chips(x, y))]
        local = [pltpu.make_async_copy(ins[t].at[2 * x + y], outs[t].at[3], loc.at[t]) for t in range(nt)]
        return remote + local

    def start(ins, outs, send, recv, loc):
        for cp in copies(ins, outs, send, recv, loc):
            cp.start()

    def finish(ins, outs, send, recv, loc):
        for cp in copies(ins, outs, send, recv, loc):
            cp.wait()

    return _Job(pairs, [_sds(p.shape, p.dtype) for p in pairs], 3 * nt, nt, start, finish)


def _run_jobs(jobs, name):
    def body(o_ref):
        o_ref[...] = jnp.zeros_like(o_ref)

    _, jres = _pallas(body, name=name, grid=(1,), in_specs=[], out_specs=pl.BlockSpec((8, 128), lambda i: (0, 0)),
                      out_shape=_sds((8, 128), F32), jobs=jobs)()
    return jres


def _allgather_small(v, name):
    R, C = v.shape

    def body(x_ref, out_ref, send_sems, recv_sems, local_sem):
        x, y, c = _mesh_pos()
        me, sibling = (x, y, c), (x, y, 1 - c)
        chips = _other_chips(x, y)

        def slab(px, py, pc):
            return out_ref.at[4 * px + 2 * py + pc]

        def copy(k, block, to, src=None):
            return pltpu.make_async_remote_copy(
                src_ref=slab(*block) if src is None else src, dst_ref=slab(*block),
                send_sem=send_sems.at[k], recv_sem=recv_sems.at[k], device_id=to, device_id_type=MESH)

        mine = pltpu.make_async_copy(x_ref, slab(*me), local_sem)
        mine.start()
        first = [copy(0, me, sibling, src=x_ref)]
        first += [copy(1 + j, me, (*chip, c), src=x_ref) for j, chip in enumerate(chips)]
        for cp in first:
            cp.start()
        passed = [copy(4 + j, (*chip, c), sibling) for j, chip in enumerate(chips)]
        for j, chip in enumerate(chips):
            copy(1 + j, (*chip, c), me).wait_recv()
            passed[j].start()
        copy(0, sibling, me).wait_recv()
        for j, chip in enumerate(chips):
            copy(4 + j, (*chip, 1 - c), me).wait_recv()
        for cp in first + passed:
            cp.wait_send()
        mine.wait()

    return _comm_call(
        body, name=name, in_specs=[pl.BlockSpec(memory_space=pltpu.VMEM)], out_specs=pl.BlockSpec(memory_space=pltpu.VMEM),
        out_shape=_sds((N_DEV, R, C), v.dtype),
        scratch_shapes=[pltpu.SemaphoreType.DMA((7,)), pltpu.SemaphoreType.DMA((7,)), pltpu.SemaphoreType.DMA],
    )(v)


def _pair_sum(core, grad, got, name):
    _, _, R, C = grad.shape
    tr = _tile(R, 1024)

    def body(core_ref, mine_ref, got_ref, out_ref):
        out_ref[...] = (mine_ref[...].astype(F32) + got_ref[...].astype(F32)).astype(BF16)

    blk = (None, tr, C)
    return _pallas(
        body, name=name, grid=(N_CHIP, R // tr), nprefetch=1,
        in_specs=[pl.BlockSpec((None, None, tr, C), lambda k, r, core: (k, core[0], r, 0)),
                  pl.BlockSpec(blk, lambda k, r, core: (k, r, 0))],
        out_specs=pl.BlockSpec(blk, lambda k, r, core: (k, r, 0)),
        out_shape=_sds((N_CHIP, R, C), BF16),
    )(core, grad, got)


def _adamw(w, g, m, v):
    m = ADAM_B1 * m + (1.0 - ADAM_B1) * g
    v = ADAM_B2 * v + (1.0 - ADAM_B2) * (g * g)
    m_hat = m / (1.0 - ADAM_B1 ** ADAM_STEP)
    v_hat = v / (1.0 - ADAM_B2 ** ADAM_STEP)
    delta = -ADAM_LR * (m_hat / (jnp.sqrt(v_hat) + ADAM_EPS) + ADAM_WD * w)
    return delta, m, v


def _final_adamw(sums, w, m, v, l, into, name):
    L, R, C = w.shape
    tr = _tile(R, 512 if C <= 1024 else 256)

    def body(s_ref, w_ref, m_ref, v_ref, *rest):
        g_out, d_out, m_out, v_out = rest[-4:]
        g = s_ref[0].astype(F32)
        for k in range(1, N_CHIP):
            g = g + s_ref[k].astype(F32)
        d, mn, vn = _adamw(w_ref[...], g, m_ref[...], v_ref[...])
        g_out[...] = g
        d_out[...] = d
        m_out[...] = mn
        v_out[...] = vn

    loc = pl.BlockSpec((None, tr, C), lambda r: (l, r, 0))
    in_specs = [pl.BlockSpec((N_CHIP, tr, C), lambda r: (0, r, 0)), loc, loc, loc]
    args = [sums, w, m, v]
    aliases = None
    if into is not None:
        in_specs += [ANY] * 4
        args += list(into)
        aliases = {4 + i: i for i in range(4)}
    return _pallas(body, name=name, grid=(R // tr,), in_specs=in_specs, out_specs=[loc] * 4,
                   out_shape=[_sds((L, R, C), F32)] * 4, aliases=aliases)(*args)


def _small_adamw(dev, allg, w, m, v, width, name):
    R = w.shape[0]

    def body(dev_ref, a_ref, w_ref, m_ref, v_ref, g_out, d_out, m_out, v_out):
        g = a_ref[0]
        for k in range(1, N_DEV):
            g = g + a_ref[k]
        d, mn, vn = _adamw(w_ref[...], g, m_ref[...], v_ref[...])
        g_out[...] = g
        d_out[...] = d
        m_out[...] = mn
        v_out[...] = vn

    full = allg.shape[2] == width
    loc = pl.BlockSpec((R, width), lambda i, dev: (0, 0))
    return _pallas(
        body, name=name, grid=(1,), nprefetch=1,
        in_specs=[pl.BlockSpec((N_DEV, R, width), lambda i, dev: (0, 0, 0 if full else dev[0])), loc, loc, loc],
        out_specs=[loc] * 4, out_shape=[_sds((R, width), F32)] * 4,
    )(dev, allg, w, m, v)


BIG = ("ffn1_w_gu", "ffn1_w_down", "w_in", "w_ret_o", "w_sc_o", "w_cf_o", "w_o", "ffn2_w_gu", "ffn2_w_down")
FFN1 = ("ffn1_w_gu", "ffn1_w_down")
MIX_OUT = ("w_ret_o", "w_sc_o", "w_cf_o", "w_o")
ROW_BLOCKS = ("ffn1_w_down", "ffn2_w_down") + MIX_OUT


def kernel(x, positions, norm_g, ffn1_w_gu, ffn1_w_down, w_in, w_ret_o, sc_conv_w, w_sc_o, cf_dw_w, cf_dw_b, cf_ln_g, cf_ln_b, w_cf_o, w_o, ffn2_w_gu, ffn2_w_down, loss_target, m_norm_g, m_ffn1_w_gu, m_ffn1_w_down, m_w_in, m_w_ret_o, m_sc_conv_w, m_w_sc_o, m_cf_dw_w, m_cf_dw_b, m_cf_ln_g, m_cf_ln_b, m_w_cf_o, m_w_o, m_ffn2_w_gu, m_ffn2_w_down, v_norm_g, v_ffn1_w_gu, v_ffn1_w_down, v_w_in, v_w_ret_o, v_sc_conv_w, v_w_sc_o, v_cf_dw_w, v_cf_dw_b, v_cf_ln_g, v_cf_ln_b, v_w_cf_o, v_w_o, v_ffn2_w_gu, v_ffn2_w_down):
    B, S, D = x.shape
    T = B * S
    L = norm_g.shape[0]
    DL = norm_g.shape[2]
    dk = D // 8
    mx, my, mc = _mesh_pos()
    dev_idx = jnp.reshape(4 * mx + 2 * my + mc, (1,)).astype(jnp.int32)
    core_idx = jnp.reshape(mc, (1,)).astype(jnp.int32)

    w32 = dict(ffn1_w_gu=ffn1_w_gu, ffn1_w_down=ffn1_w_down, w_in=w_in, w_ret_o=w_ret_o, w_sc_o=w_sc_o, w_cf_o=w_cf_o,
               w_o=w_o, ffn2_w_gu=ffn2_w_gu, ffn2_w_down=ffn2_w_down)
    m32 = dict(ffn1_w_gu=m_ffn1_w_gu, ffn1_w_down=m_ffn1_w_down, w_in=m_w_in, w_ret_o=m_w_ret_o, w_sc_o=m_w_sc_o,
               w_cf_o=m_w_cf_o, w_o=m_w_o, ffn2_w_gu=m_ffn2_w_gu, ffn2_w_down=m_ffn2_w_down)
    v32 = dict(ffn1_w_gu=v_ffn1_w_gu, ffn1_w_down=v_ffn1_w_down, w_in=v_w_in, w_ret_o=v_w_ret_o, w_sc_o=v_w_sc_o,
               w_cf_o=v_w_cf_o, w_o=v_w_o, ffn2_w_gu=v_ffn2_w_gu, ffn2_w_down=v_ffn2_w_down)

    W = [dict() for _ in range(L)]

    def gather(l, names):
        return _allgather_job([w32[n][l].astype(BF16) for n in names])

    def place(l, names, slabs):
        for n, a in zip(names, slabs):
            W[l][n] = a.reshape(a.shape[0] * a.shape[1], a.shape[2]) if n in ROW_BLOCKS else a

    def carried(fn, *args, jobs, **kw):
        out = fn(*args, jobs=jobs, **kw)
        return out if jobs else (out, [])

    n_sh = 6 + SC_KERNEL + CF_KERNEL
    small_sh = jnp.concatenate([norm_g, sc_conv_w, cf_dw_w], axis=1).reshape(L * n_sh, DL)
    sh_all = _allgather_small(small_sh, "allgather_small_params")
    sh_full = jnp.transpose(sh_all, (1, 0, 2)).reshape(L, n_sh, D)
    norm_full = sh_full[:, :6]
    scw_full = sh_full[:, 6:6 + SC_KERNEL]
    cfw_full = sh_full[:, 6 + SC_KERNEL:]
    place(0, FFN1, _run_jobs([gather(0, FFN1)], "allgather_first")[0])

    half = dk // 2
    inv_freq = ROPE_BASE ** (-jnp.arange(half, dtype=F32) / half)
    invf = jnp.concatenate([inv_freq, inv_freq])[None, :]
    sgn = jnp.concatenate([-jnp.ones((half,), F32), jnp.ones((half,), F32)])[None, :]
    cosf, sinf = _rope_tables(positions.reshape(T, 1), invf, sgn, "rope_tables")
    lgs = jnp.log(1.0 - 2.0 ** (-5.0 - jnp.arange(HEADS, dtype=F32)))

    def vec(a):
        return a.reshape(1, D)

    xc = x.reshape(T, D)
    saved = []
    for l in range(L):
        g = norm_full[l]
        sv = {}
        sv["x0"] = xc
        (xc, sv["h1"], sv["gate1"], sv["up1"], sv["f1"]), jr = _ffn_fwd(
            xc, vec(g[0]), vec(g[1]), W[l]["ffn1_w_gu"], W[l]["ffn1_w_down"], f"ffn1_fwd_{l}", jobs=[gather(l, ("w_in",))])
        place(l, ("w_in",), jr[0])
        sv["x1"] = xc
        (proj, sv["h2"]), jr = _norm_proj(xc, vec(g[2]), W[l]["w_in"], f"proj_fwd_{l}", jobs=[gather(l, ("ffn2_w_gu",))])
        place(l, ("ffn2_w_gu",), jr[0])
        sv["proj"] = proj
        o, jr = _ret_fwd(proj, cosf, sinf, lgs, B, S, D, f"retention_fwd_{l}", jobs=[gather(l, MIX_OUT)])
        place(l, MIX_OUT, jr[0])
        sv["o"] = o
        nxt = l + 1 < L
        (ya_in, yb_in, yc_in, sv["cv"], sv["u1"]), jr = _mixer_mid_fwd(
            proj, o, scw_full[l], cfw_full[l], vec(cf_dw_b[l]), vec(cf_ln_g[l]), vec(cf_ln_b[l]), S,
            f"mixer_mid_fwd_{l}", jobs=[gather(l, ("ffn2_w_down",))])
        place(l, ("ffn2_w_down",), jr[0])
        sv["ya_in"], sv["yb_in"], sv["yc_in"] = ya_in, yb_in, yc_in
        (xc, sv["ya"], sv["yb"], sv["yc"], sv["merged"], sv["m"]), jr = carried(
            _mixer_merge_fwd, xc, proj, ya_in, yb_in, yc_in, vec(g[3]), W[l]["w_ret_o"], W[l]["w_sc_o"], W[l]["w_cf_o"], W[l]["w_o"],
            f"mixer_merge_fwd_{l}", jobs=[gather(l + 1, ("ffn1_w_down",))] if nxt else [])
        if nxt:
            place(l + 1, ("ffn1_w_down",), jr[0])
        sv["x2"] = xc
        (xc, sv["h3"], sv["gate2"], sv["up2"], sv["f2"]), jr = carried(
            _ffn_fwd, xc, vec(g[4]), vec(g[5]), W[l]["ffn2_w_gu"], W[l]["ffn2_w_down"], f"ffn2_fwd_{l}",
            jobs=[gather(l + 1, ("ffn1_w_gu",))] if nxt else [])
        if nxt:
            place(l + 1, ("ffn1_w_gu",), jr[0])
        saved.append(sv)

    dx, loss_part = _loss_grad(xc, loss_target.reshape(T, D), "loss")
    loss = lax.psum(loss_part[0, 0], ("x", "y", "c"))

    to_sibling, to_chips, reduced = [], [], {}

    def grad_ready(n, l, g):
        if g.ndim == 2:
            g = g.reshape(N_DEV, g.shape[0] // N_DEV, g.shape[1])
        to_sibling.append((n, l, g.reshape(N_CHIP, 2, g.shape[1], g.shape[2])))

    def take_jobs(heavy):
        a = list(to_sibling)
        b = [e for e in to_chips if heavy or e[2].size <= LIGHT_CARRIER_ELEMS]
        to_sibling.clear()
        to_chips[:] = [e for e in to_chips if not any(e is t for t in b)]
        jobs = ([_to_chips_job([p for _, _, p in b])] if b else []) + ([_to_sibling_job([g for _, _, g in a])] if a else [])
        return jobs, (a, b)

    def settle(taken, jres):
        a, b = taken
        if b:
            for (n, l, _), got in zip(b, jres[0]):
                reduced[(n, l)] = got
        if a:
            for (n, l, g), got in zip(a, jres[-1]):
                to_chips.append((n, l, _pair_sum(core_idx, g, got, f"pair_sum_{n}_{l}")))

    def carrier(fn, *args, heavy=True, **kw):
        jobs, taken = take_jobs(heavy)
        out, jres = carried(fn, *args, jobs=jobs, **kw)
        settle(taken, jres)
        return out

    small_rows = [None] * L
    cb_gu = W[0]["ffn1_w_gu"].shape[2]
    cb_in = W[0]["w_in"].shape[2]

    def ffn_grads(tag, wgu_name, wd_name, dxo, x_in, ht, gate, up, f, g_pre, g_post, l):
        dxi, df, dgate, dup, act, dgpre, dgpost = carrier(_ffn_bwd, dxo, x_in, f, gate, up, g_pre, g_post, W[l][wgu_name],
                                                          W[l][wd_name], f"{tag}_bwd_{l}")
        grad_ready(wd_name, l, carrier(_mm_tn, act, df, f"{tag}_dwd_{l}", heavy=False))
        half = carrier(_mm_tn, ht, dgate, f"{tag}_dwg_{l}", slab_cols=cb_gu, n_slabs=N_DEV, a_is_transposed=True, heavy=False)
        grad_ready(wgu_name, l, carrier(_mm_tn, ht, dup, f"{tag}_dwu_{l}", slab_cols=cb_gu, n_slabs=N_DEV, first_slab=N_DEV // 2,
                                        into=half, a_is_transposed=True, heavy=False))
        return dxi, dgpre, dgpost

    for l in reversed(range(L)):
        sv = saved[l]
        g = norm_full[l]
        dx, dg4, dg5 = ffn_grads("ffn2", "ffn2_w_gu", "ffn2_w_down", dx, sv["x2"], sv["h3"], sv["gate2"], sv["up2"], sv["f2"],
                                 vec(g[4]), vec(g[5]), l)
        (dm, dya, dyb, dyc, dproj, do, dcv, du1, dg3, dlng, dlnb, dcfb) = carrier(
            _mixer_bwd_a, dx, sv["m"], vec(g[3]), sv["ya"], sv["yb"], sv["yc"], sv["proj"], sv["o"], sv["cv"], sv["u1"],
            vec(cf_ln_g[l]), vec(cf_ln_b[l]), W[l]["w_ret_o"], W[l]["w_sc_o"], W[l]["w_cf_o"], W[l]["w_o"], f"mixer_bwd_a_{l}")
        grad_ready("w_o", l, _mm_tn(sv["merged"], dm, f"dw_o_{l}", tt_pref=2048))
        grad_ready("w_ret_o", l, _mm_tn(sv["ya_in"], dya, f"dw_ret_o_{l}", tt_pref=2048))
        grad_ready("w_sc_o", l, _mm_tn(sv["yb_in"], dyb, f"dw_sc_o_{l}", tt_pref=2048))
        grad_ready("w_cf_o", l, _mm_tn(sv["yc_in"], dyc, f"dw_cf_o_{l}", tt_pref=2048))
        dproj, dscw, dcfw = carrier(_mixer_bwd_b, dcv, du1, sv["proj"], scw_full[l], cfw_full[l], dproj, S, f"mixer_bwd_b_{l}")
        dproj = carrier(_ret_bwd, do, sv["proj"], cosf, sinf, lgs, dproj, B, S, D, f"retention_bwd_{l}", heavy=False)
        grad_ready("w_in", l, carrier(_mm_tn, sv["h2"], dproj, f"dw_in_{l}", slab_cols=cb_in, tn_pref=cb_in, a_is_transposed=True))
        dx, dg2 = carrier(_proj_bwd, dproj, W[l]["w_in"], sv["x1"], vec(g[2]), dx, f"proj_bwd_{l}")
        dx, dg0, dg1 = ffn_grads("ffn1", "ffn1_w_gu", "ffn1_w_down", dx, sv["x0"], sv["h1"], sv["gate1"], sv["up1"], sv["f1"],
                                 vec(g[0]), vec(g[1]), l)
        small_rows[l] = (jnp.concatenate([dg0, dg1, dg2, dg3, dg4, dg5, dscw, dcfw], axis=0),
                         jnp.concatenate([dcfb, dlng, dlnb], axis=0))
    grad_x = dx.reshape(B, S, D)
    flush = 0
    while to_sibling or to_chips:
        jobs, taken = take_jobs(True)
        settle(taken, _run_jobs(jobs, f"grads_flush_{flush}"))
        flush += 1

    big_out = {}
    for n in BIG:
        res = None
        for l in reversed(range(L)):
            res = _final_adamw(reduced[(n, l)], w32[n], m32[n], v32[n], l, res, f"adamw_{n}_{l}")
        big_out[n] = res

    sh_part = jnp.concatenate([small_rows[l][0] for l in range(L)], axis=0)
    rep_part = jnp.concatenate([small_rows[l][1] for l in range(L)] + [jnp.zeros((8 - 3 * L % 8, D), F32)] * (1 if 3 * L % 8 else 0), axis=0)
    sh_g = _allgather_small(sh_part, "allgather_small_grads")
    rep_g = _allgather_small(rep_part, "allgather_replicated_grads")

    def pack_sh(a, b, c):
        return jnp.concatenate([a, b, c], axis=1).reshape(L * n_sh, DL)

    def pack_rep(a, b, c):
        rows = jnp.stack([a, b, c], axis=1).reshape(3 * L, D)
        return jnp.concatenate([rows, jnp.ones((rep_part.shape[0] - 3 * L, D), F32)], axis=0)

    sh_res = _small_adamw(dev_idx, sh_g, pack_sh(norm_g, sc_conv_w, cf_dw_w), pack_sh(m_norm_g, m_sc_conv_w, m_cf_dw_w),
                          pack_sh(v_norm_g, v_sc_conv_w, v_cf_dw_w), DL, "adamw_small_sharded")
    rep_res = _small_adamw(dev_idx, rep_g, pack_rep(cf_dw_b, cf_ln_g, cf_ln_b), pack_rep(m_cf_dw_b, m_cf_ln_g, m_cf_ln_b),
                           pack_rep(v_cf_dw_b, v_cf_ln_g, v_cf_ln_b), D, "adamw_small_replicated")

    def unpack_sh(a):
        a = a.reshape(L, n_sh, DL)
        return {"norm_g": a[:, :6], "sc_conv_w": a[:, 6:6 + SC_KERNEL], "cf_dw_w": a[:, 6 + SC_KERNEL:]}

    def unpack_rep(a):
        a = a[:3 * L].reshape(L, 3, D)
        return {"cf_dw_b": a[:, 0], "cf_ln_g": a[:, 1], "cf_ln_b": a[:, 2]}

    order = ("norm_g", "ffn1_w_gu", "ffn1_w_down", "w_in", "w_ret_o", "sc_conv_w", "w_sc_o", "cf_dw_w", "cf_dw_b", "cf_ln_g",
             "cf_ln_b", "w_cf_o", "w_o", "ffn2_w_gu", "ffn2_w_down")
    outs = []
    for kind in range(4):
        small = {**unpack_sh(sh_res[kind]), **unpack_rep(rep_res[kind])}
        outs += [big_out[n][kind] if n in big_out else small[n] for n in order]
    return (loss, grad_x, *outs)
```

```python
import functools

import jax
import jax.numpy as jnp
from jax import lax
from jax.experimental import pallas as pl
from jax.experimental.pallas import tpu as pltpu

F32 = jnp.float32
BF16 = jnp.bfloat16
MESH = pl.DeviceIdType.MESH
ANY = pl.BlockSpec(memory_space=pl.ANY)

N_DEV = 8
N_CHIP = 4
CHUNK = 64
HEADS = 4
ROPE_BASE = 10000.0
NORM_EPS = 1e-6
LN_EPS = 1e-5
SC_KERNEL = 3
CF_KERNEL = 31
HALO = 32
ADAM_LR = 0.001
ADAM_B1 = 0.9
ADAM_B2 = 0.999
ADAM_EPS = 1e-08
ADAM_WD = 0.01
ADAM_STEP = 10
VMEM_LIMIT_V7X = 56 * 1024 * 1024
LIGHT_CARRIER_ELEMS = 3 * 1024 * 1024
TAIL_PIECES = 4


class _Job:
    def __init__(self, ins, out_shapes, n_sems, n_local, start, finish):
        self.ins, self.out_shapes, self.n_sems, self.n_local = list(ins), list(out_shapes), n_sems, n_local
        self.start, self.finish = start, finish


def _pallas(body, *, name, grid, in_specs, out_specs, out_shape, scratch_shapes=(), aliases=None, nprefetch=0, jobs=()):
    extra = {}
    single = not isinstance(out_shape, (list, tuple))
    out_shape = [out_shape] if single else list(out_shape)
    out_specs = [out_specs] if single else list(out_specs)
    in_specs, scratch = list(in_specs), list(scratch_shapes)
    n_in, n_out, n_scr = len(in_specs), len(out_shape), len(scratch)
    for jb in jobs:
        in_specs += [ANY] * len(jb.ins)
        out_specs += [ANY] * len(jb.out_shapes)
        out_shape += jb.out_shapes
        scratch += [pltpu.SemaphoreType.DMA((jb.n_sems,)), pltpu.SemaphoreType.DMA((jb.n_sems,)),
                    pltpu.SemaphoreType.DMA((max(jb.n_local, 1),))]

    def with_jobs(*refs):
        pre, refs = refs[:nprefetch], refs[nprefetch:]
        ins, p = refs[:n_in], n_in
        jins = []
        for jb in jobs:
            jins.append(refs[p:p + len(jb.ins)])
            p += len(jb.ins)
        outs = refs[p:p + n_out]
        p += n_out
        jouts = []
        for jb in jobs:
            jouts.append(refs[p:p + len(jb.out_shapes)])
            p += len(jb.out_shapes)
        scr = refs[p:p + n_scr]
        p += n_scr
        pids = [pl.program_id(d) for d in range(len(grid))]
        first = functools.reduce(jnp.logical_and, [pid == 0 for pid in pids])
        last = functools.reduce(jnp.logical_and, [pid == g - 1 for pid, g in zip(pids, grid)])

        @pl.when(first)
        def _():
            for i, jb in enumerate(jobs):
                jb.start(jins[i], jouts[i], *refs[p + 3 * i:p + 3 * i + 3])

        body(*pre, *ins, *outs, *scr)

        @pl.when(last)
        def _():
            for i, jb in enumerate(jobs):
                jb.finish(jins[i], jouts[i], *refs[p + 3 * i:p + 3 * i + 3])

    params = pltpu.CompilerParams(dimension_semantics=("arbitrary",) * len(grid), vmem_limit_bytes=VMEM_LIMIT_V7X)
    spec = pltpu.PrefetchScalarGridSpec(num_scalar_prefetch=nprefetch, grid=grid, in_specs=in_specs,
                                        out_specs=out_specs, scratch_shapes=scratch)
    call = pl.pallas_call(with_jobs if jobs else body, name=name, grid_spec=spec, out_shape=out_shape, compiler_params=params,
                          input_output_aliases=aliases or {}, **extra)

    def run(*args):
        res = call(*args, *[a for jb in jobs for a in jb.ins])
        own = res[0] if single else list(res[:n_out])
        if not jobs:
            return own
        jres, p = [], n_out
        for jb in jobs:
            jres.append(list(res[p:p + len(jb.out_shapes)]))
            p += len(jb.out_shapes)
        return own, jres

    return run


def _comm_call(body, *, name, in_specs, out_specs, out_shape, scratch_shapes):
    extra = {}
    return pl.pallas_call(body, name=name, in_specs=in_specs, out_specs=out_specs, out_shape=out_shape,
                          scratch_shapes=scratch_shapes, **extra)


def _sds(shape, dtype):
    return jax.ShapeDtypeStruct(shape, dtype)


def _tile(n, pref):
    t = min(n, pref)
    assert n % t == 0, (n, pref)
    return t


def _sigmoid(x):
    return jax.nn.sigmoid(x)


def _rms_fwd(x, g):
    r = lax.rsqrt(jnp.mean(x * x, axis=-1, keepdims=True) + NORM_EPS)
    return x * r * g


def _rms_bwd(x, g, dy):
    r = lax.rsqrt(jnp.mean(x * x, axis=-1, keepdims=True) + NORM_EPS)
    xh = x * r
    dg = jnp.sum(dy * xh, axis=0, keepdims=True)
    dxh = dy * g
    dx = r * (dxh - xh * jnp.mean(dxh * xh, axis=-1, keepdims=True))
    return dx, dg


def _ln_stats(x):
    mu = jnp.mean(x, axis=-1, keepdims=True)
    xc = x - mu
    rstd = lax.rsqrt(jnp.mean(xc * xc, axis=-1, keepdims=True) + LN_EPS)
    return xc * rstd, rstd


def _ln_bwd(xh, rstd, dxh):
    return rstd * (dxh - jnp.mean(dxh, axis=-1, keepdims=True) - xh * jnp.mean(dxh * xh, axis=-1, keepdims=True))


def _dsilu(x, s):
    return s * (1.0 + x * (1.0 - s))


def _rot(x, cosf, sinf):
    return x * cosf + pltpu.roll(x, x.shape[-1] // 2, 1) * sinf


def _unrot(d, cosf, sinf):
    return d * cosf - pltpu.roll(d, d.shape[-1] // 2, 1) * sinf


def _dot_nn(a, b):
    return jnp.dot(a, b, preferred_element_type=F32)


def _dot_nt(a, b):
    return lax.dot_general(a, b, (((1,), (1,)), ((), ())), preferred_element_type=F32)


def _dot_tn(a, b):
    return lax.dot_general(a, b, (((0,), (0,)), ((), ())), preferred_element_type=F32)


def _ffn_tf(cb):
    return _tile(cb, 1024)


def _ffn_fwd(x, g_pre, g_post, wgu, wd, name, jobs=()):
    T, D = x.shape
    F = wd.shape[0]
    cb = wgu.shape[2]
    tm, tf = _tile(T, 512), _ffn_tf(cb)
    nj, nb = F // tf, cb // tf

    def body(x_ref, gpre_ref, gpost_ref, wg_ref, wu_ref, wd_ref, xo_ref, ht_ref, gate_ref, up_ref, f_ref, acc_ref, h_ref):
        j = pl.program_id(1)

        @pl.when(j == 0)
        def _():
            h = _rms_fwd(x_ref[...], gpre_ref[...])
            h_ref[...] = h.astype(BF16)
            ht_ref[...] = h.T.astype(BF16)
            acc_ref[...] = jnp.zeros_like(acc_ref)

        h = h_ref[...]
        gate = _dot_nn(h, wg_ref[...])
        up = _dot_nn(h, wu_ref[...])
        gate_ref[...] = gate.astype(BF16)
        up_ref[...] = up.astype(BF16)
        act = (gate * _sigmoid(gate) * up).astype(BF16)
        acc_ref[...] += _dot_nn(act, wd_ref[...])

        @pl.when(j == nj - 1)
        def _():
            f = acc_ref[...]
            f_ref[...] = f
            xo_ref[...] = x_ref[...] + 0.5 * _rms_fwd(f, gpost_ref[...])

    row = pl.BlockSpec((tm, D), lambda i, j: (i, 0))
    vec = pl.BlockSpec((1, D), lambda i, j: (0, 0))
    col = pl.BlockSpec((tm, tf), lambda i, j: (i, j))
    return _pallas(
        body, name=name, grid=(T // tm, nj),
        in_specs=[row, vec, vec,
                  pl.BlockSpec((None, D, tf), lambda i, j: (j // nb, 0, j % nb)),
                  pl.BlockSpec((None, D, tf), lambda i, j: ((nj + j) // nb, 0, j % nb)),
                  pl.BlockSpec((tf, D), lambda i, j: (j, 0))],
        out_specs=[row, pl.BlockSpec((D, tm), lambda i, j: (0, i)), col, col, row],
        out_shape=[_sds((T, D), F32), _sds((D, T), BF16), _sds((T, F), BF16), _sds((T, F), BF16), _sds((T, D), F32)],
        scratch_shapes=[pltpu.VMEM((tm, D), F32), pltpu.VMEM((tm, D), BF16)], jobs=jobs,
    )(x, g_pre, g_post, wgu, wgu, wd)


def _ffn_bwd(dxo, x, f, gate, up, g_pre, g_post, wgu, wd, name, jobs=()):
    T, D = x.shape
    F = wd.shape[0]
    cb = wgu.shape[2]
    tm, tf = _tile(T, 512), _ffn_tf(cb)
    nj, nb = F // tf, cb // tf

    def body(dxo_ref, x_ref, f_ref, gate_ref, up_ref, gpre_ref, gpost_ref, wg_ref, wu_ref, wd_ref,
             dx_ref, df_ref, dgate_ref, dup_ref, act_ref, dgpre_ref, dgpost_ref, acc_ref):
        i, j = pl.program_id(0), pl.program_id(1)

        @pl.when((i == 0) & (j == 0))
        def _():
            dgpre_ref[...] = jnp.zeros_like(dgpre_ref)
            dgpost_ref[...] = jnp.zeros_like(dgpost_ref)

        @pl.when(j == 0)
        def _():
            df, dgp = _rms_bwd(f_ref[...], gpost_ref[...], 0.5 * dxo_ref[...])
            df_ref[...] = df.astype(BF16)
            dgpost_ref[...] += dgp
            acc_ref[...] = jnp.zeros_like(acc_ref)

        dact = _dot_nt(df_ref[...], wd_ref[...])
        g = gate_ref[...].astype(F32)
        u = up_ref[...].astype(F32)
        s = _sigmoid(g)
        silu = g * s
        dgate = (dact * u * _dsilu(g, s)).astype(BF16)
        dup = (dact * silu).astype(BF16)
        act_ref[...] = (silu * u).astype(BF16)
        dgate_ref[...] = dgate
        dup_ref[...] = dup
        acc_ref[...] += _dot_nt(dgate, wg_ref[...]) + _dot_nt(dup, wu_ref[...])

        @pl.when(j == nj - 1)
        def _():
            dxin, dgp = _rms_bwd(x_ref[...], gpre_ref[...], acc_ref[...])
            dx_ref[...] = dxo_ref[...] + dxin
            dgpre_ref[...] += dgp

    row = pl.BlockSpec((tm, D), lambda i, j: (i, 0))
    vec = pl.BlockSpec((1, D), lambda i, j: (0, 0))
    col = pl.BlockSpec((tm, tf), lambda i, j: (i, j))
    return _pallas(
        body, name=name, grid=(T // tm, nj),
        in_specs=[row, row, row, col, col, vec, vec,
                  pl.BlockSpec((None, D, tf), lambda i, j: (j // nb, 0, j % nb)),
                  pl.BlockSpec((None, D, tf), lambda i, j: ((nj + j) // nb, 0, j % nb)),
                  pl.BlockSpec((tf, D), lambda i, j: (j, 0))],
        out_specs=[row, row, col, col, col, vec, vec],
        out_shape=[_sds((T, D), F32), _sds((T, D), BF16), _sds((T, F), BF16), _sds((T, F), BF16), _sds((T, F), BF16),
                   _sds((1, D), F32), _sds((1, D), F32)],
        scratch_shapes=[pltpu.VMEM((tm, D), F32)], jobs=jobs,
    )(dxo, x, f, gate, up, g_pre, g_post, wgu, wgu, wd)


def _mm_tn(a, b, name, slab_cols=None, n_slabs=None, first_slab=0, into=None, tn_pref=1024, tt_pref=2048,
           a_is_transposed=False, jobs=()):
    T, N = b.shape
    K = a.shape[0] if a_is_transposed else a.shape[1]
    tt = _tile(T, tt_pref)
    tko = _tile(K, 1024)
    tn = _tile(slab_cols or N, tn_pref)
    nt = T // tt

    def body(a_ref, b_ref, *rest):
        o_ref, acc_ref = rest[-2:]
        t = pl.program_id(2)

        @pl.when(t == 0)
        def _():
            acc_ref[...] = jnp.zeros_like(acc_ref)

        acc_ref[...] += (_dot_nn if a_is_transposed else _dot_tn)(a_ref[...], b_ref[...])

        @pl.when(t == nt - 1)
        def _():
            o_ref[...] = acc_ref[...].astype(o_ref.dtype)

    if slab_cols is None:
        shape = (K, N)
        ospec = pl.BlockSpec((tko, tn), lambda k, jn, t: (k, jn))
    else:
        nb = slab_cols // tn
        shape = (n_slabs or N // slab_cols, K, slab_cols)
        ospec = pl.BlockSpec((None, tko, tn), lambda k, jn, t: (first_slab + jn // nb, k, jn % nb))
    aspec = pl.BlockSpec((tko, tt), lambda k, jn, t: (k, t)) if a_is_transposed else pl.BlockSpec((tt, tko), lambda k, jn, t: (t, k))
    in_specs, args = [aspec, pl.BlockSpec((tt, tn), lambda k, jn, t: (t, jn))], [a, b]
    if into is not None:
        in_specs.append(ANY)
        args.append(into)
    return _pallas(body, name=name, grid=(K // tko, N // tn, nt), in_specs=in_specs, out_specs=ospec, out_shape=_sds(shape, BF16),
                   scratch_shapes=[pltpu.VMEM((tko, tn), F32)], aliases={2: 0} if into is not None else None, jobs=jobs)(*args)


def _norm_proj(x, g, w, name, jobs=()):
    T, D = x.shape
    nd, cb = w.shape[0], w.shape[2]
    tm = _tile(T, 1024)

    def body(x_ref, g_ref, w_ref, p_ref, ht_ref, h_ref):
        @pl.when(pl.program_id(1) == 0)
        def _():
            h = _rms_fwd(x_ref[...], g_ref[...])
            h_ref[...] = h.astype(BF16)
            ht_ref[...] = h.T.astype(BF16)

        p_ref[...] = _dot_nn(h_ref[...], w_ref[...])

    row = pl.BlockSpec((tm, D), lambda i, j: (i, 0))
    return _pallas(
        body, name=name, grid=(T // tm, nd),
        in_specs=[row, pl.BlockSpec((1, D), lambda i, j: (0, 0)),
                  pl.BlockSpec((None, D, cb), lambda i, j: (j, 0, 0))],
        out_specs=[pl.BlockSpec((tm, cb), lambda i, j: (i, j)), pl.BlockSpec((D, tm), lambda i, j: (0, i))],
        out_shape=[_sds((T, nd * cb), F32), _sds((D, T), BF16)],
        scratch_shapes=[pltpu.VMEM((tm, D), BF16)], jobs=jobs,
    )(x, g, w)


def _proj_bwd(dproj, w, x, g, dxo, name, jobs=()):
    T, D = x.shape
    nd, cb = w.shape[0], w.shape[2]
    tm = _tile(T, 1024)

    def body(dp_ref, w_ref, x_ref, g_ref, dxo_ref, dx_ref, dg_ref, acc_ref):
        i, j = pl.program_id(0), pl.program_id(1)

        @pl.when((i == 0) & (j == 0))
        def _():
            dg_ref[...] = jnp.zeros_like(dg_ref)

        @pl.when(j == 0)
        def _():
            acc_ref[...] = jnp.zeros_like(acc_ref)

        acc_ref[...] += _dot_nt(dp_ref[...], w_ref[...])

        @pl.when(j == nd - 1)
        def _():
            dxin, dgp = _rms_bwd(x_ref[...], g_ref[...], acc_ref[...])
            dx_ref[...] = dxo_ref[...] + dxin
            dg_ref[...] += dgp

    row = pl.BlockSpec((tm, D), lambda i, j: (i, 0))
    vec = pl.BlockSpec((1, D), lambda i, j: (0, 0))
    return _pallas(
        body, name=name, grid=(T // tm, nd),
        in_specs=[pl.BlockSpec((tm, cb), lambda i, j: (i, j)),
                  pl.BlockSpec((None, D, cb), lambda i, j: (j, 0, 0)), row, vec, row],
        out_specs=[row, vec],
        out_shape=[_sds((T, D), F32), _sds((1, D), F32)],
        scratch_shapes=[pltpu.VMEM((tm, D), F32)], jobs=jobs,
    )(dproj, w, x, g, dxo)


def _rope_tables(pos, invf, sgn, name):
    T = pos.shape[0]
    dk = invf.shape[1]
    tm = _tile(T, 1024)

    def body(p_ref, f_ref, s_ref, c_out, s_out):
        ang = p_ref[...].astype(F32) * f_ref[...]
        c_out[...] = jnp.cos(ang)
        s_out[...] = jnp.sin(ang) * s_ref[...]

    vec = pl.BlockSpec((1, dk), lambda i: (0, 0))
    out = pl.BlockSpec((tm, dk), lambda i: (i, 0))
    return _pallas(body, name=name, grid=(T // tm,), in_specs=[pl.BlockSpec((tm, 1), lambda i: (i, 0)), vec, vec],
                   out_specs=[out, out], out_shape=[_sds((T, dk), F32), _sds((T, dk), F32)])(pos, invf, sgn)


def _decay(lg, r0, tq, n):
    r = r0 + lax.broadcasted_iota(jnp.int32, (tq, n), 0)
    c = lax.broadcasted_iota(jnp.int32, (tq, n), 1)
    rc, cc = r // CHUNK, c // CHUNK
    d = (r - c).astype(F32)
    e = jnp.where(rc == cc, jnp.abs(d), d)
    return jnp.where(cc > rc, 0.0, jnp.exp(lg * e))


def _ret_fwd(proj, cosf, sinf, lgs, B, S, D, name, jobs=()):
    T = B * S
    dk, dv = D // 8, D // 4
    tq = _tile(S, 256)
    scale = dk ** -0.5

    def body(lg_ref, q_ref, k_ref, v_ref, c_ref, s_ref, o_ref, kr_ref, vb_ref):
        lg = lg_ref[pl.program_id(1)]
        kr_ref[...] = (_rot(k_ref[...], c_ref[...], s_ref[...]) * scale).astype(BF16)
        vb_ref[...] = v_ref[...].astype(BF16)
        for qi in range(S // tq):
            rows, n = slice(qi * tq, (qi + 1) * tq), (qi + 1) * tq
            q = _rot(q_ref[rows, :], c_ref[rows, :], s_ref[rows, :]).astype(BF16)
            p = (_dot_nt(q, kr_ref[:n, :]) * _decay(lg, qi * tq, tq, n)).astype(BF16)
            o_ref[rows, :] = _dot_nn(p, vb_ref[:n, :])

    return _pallas(
        body, name=name, grid=(B, HEADS),
        in_specs=[pl.BlockSpec(memory_space=pltpu.SMEM),
                  pl.BlockSpec((S, dk), lambda b, h: (b, h)),
                  pl.BlockSpec((S, dk), lambda b, h: (b, HEADS + h)),
                  pl.BlockSpec((S, dv), lambda b, h: (b, HEADS + h)),
                  pl.BlockSpec((S, dk), lambda b, h: (b, 0)),
                  pl.BlockSpec((S, dk), lambda b, h: (b, 0))],
        out_specs=pl.BlockSpec((S, dv), lambda b, h: (b, h)),
        out_shape=_sds((T, D), F32),
        scratch_shapes=[pltpu.VMEM((S, dk), BF16), pltpu.VMEM((S, dv), BF16)], jobs=jobs,
    )(lgs, proj, proj, proj, cosf, sinf)


def _ret_bwd(do, proj, cosf, sinf, lgs, dproj, B, S, D, name, jobs=()):
    dk, dv = D // 8, D // 4
    tq = _tile(S, 256)
    scale = dk ** -0.5

    def body(lg_ref, do_ref, q_ref, k_ref, v_ref, c_ref, s_ref, dproj_in, dproj_ref, kr_ref, vb_ref, dka_ref, dva_ref,
             dq_ref, dk_ref, dv_ref, copy_sems):
        b, h = pl.program_id(0), pl.program_id(1)
        lg = lg_ref[h]
        kr_ref[...] = (_rot(k_ref[...], c_ref[...], s_ref[...]) * scale).astype(BF16)
        vb_ref[...] = v_ref[...].astype(BF16)
        dka_ref[...] = jnp.zeros_like(dka_ref)
        dva_ref[...] = jnp.zeros_like(dva_ref)

        for qi in range(S // tq):
            rows, n = slice(qi * tq, (qi + 1) * tq), (qi + 1) * tq
            cq, sq = c_ref[rows, :], s_ref[rows, :]
            q = _rot(q_ref[rows, :], cq, sq).astype(BF16)
            dout = do_ref[rows, :].astype(BF16)
            w = _decay(lg, qi * tq, tq, n)
            p = (_dot_nt(q, kr_ref[:n, :]) * w).astype(BF16)
            ds = (_dot_nt(dout, vb_ref[:n, :]) * w).astype(BF16)
            dq_ref[rows, :] = _unrot(_dot_nn(ds, kr_ref[:n, :]), cq, sq).astype(BF16)
            dka_ref[:n, :] += _dot_tn(ds, q)
            dva_ref[:n, :] += _dot_tn(p, dout)

        dk_ref[...] = _unrot(dka_ref[...] * scale, c_ref[...], s_ref[...]).astype(BF16)
        dv_ref[...] = dva_ref[...].astype(BF16)
        rows = pl.ds(pl.multiple_of(b * S, S), S)
        copies = [pltpu.make_async_copy(dq_ref, dproj_ref.at[rows, pl.ds(pl.multiple_of(h * dk, dk), dk)], copy_sems.at[0]),
                  pltpu.make_async_copy(dk_ref, dproj_ref.at[rows, pl.ds(pl.multiple_of((HEADS + h) * dk, dk), dk)], copy_sems.at[1]),
                  pltpu.make_async_copy(dv_ref, dproj_ref.at[rows, pl.ds(pl.multiple_of((HEADS + h) * dv, dv), dv)], copy_sems.at[2])]
        for cp in copies:
            cp.start()
        for cp in copies:
            cp.wait()

    return _pallas(
        body, name=name, grid=(B, HEADS),
        in_specs=[pl.BlockSpec(memory_space=pltpu.SMEM),
                  pl.BlockSpec((S, dv), lambda b, h: (b, h)),
                  pl.BlockSpec((S, dk), lambda b, h: (b, h)),
                  pl.BlockSpec((S, dk), lambda b, h: (b, HEADS + h)),
                  pl.BlockSpec((S, dv), lambda b, h: (b, HEADS + h)),
                  pl.BlockSpec((S, dk), lambda b, h: (b, 0)),
                  pl.BlockSpec((S, dk), lambda b, h: (b, 0)), ANY],
        out_specs=ANY, out_shape=_sds(dproj.shape, BF16),
        scratch_shapes=[pltpu.VMEM((S, dk), BF16), pltpu.VMEM((S, dv), BF16), pltpu.VMEM((S, dk), F32), pltpu.VMEM((S, dv), F32),
                        pltpu.VMEM((S, dk), BF16), pltpu.VMEM((S, dk), BF16), pltpu.VMEM((S, dv), BF16), pltpu.SemaphoreType.DMA((3,))],
        aliases={7: 0}, jobs=jobs,
    )(lgs, do, proj, proj, proj, cosf, sinf, dproj)


COL_GRET, COL_SCB, COL_SCC, COL_SCX, COL_GLUA, COL_GLUB, COL_GL = 2, 3, 4, 5, 6, 7, 8


SUBLANES, LANES = 8, 128
CONV_BLOCK = 256
SC_FWD = tuple(HALO - (SC_KERNEL - 1) + j for j in range(SC_KERNEL))
CF_FWD = tuple(HALO - (CF_KERNEL - 1) + j for j in range(CF_KERNEL))
SC_BWD = tuple(SC_KERNEL - 1 - j for j in range(SC_KERNEL))
CF_BWD = tuple(CF_KERNEL - 1 - j for j in range(CF_KERNEL))


def _build_shifted(src_ref, e_ref, shifts):
    n = e_ref.shape[1]
    for b in sorted({s % SUBLANES for s in shifts} - {0}):
        e_ref[b - 1] = src_ref[pl.ds(b, n), :]


def _shifted_rows(src_ref, e_ref, s, r0, rows, lanes):
    a, b = divmod(s, SUBLANES)
    at = pl.ds(r0 + SUBLANES * a, rows)
    return src_ref[at, lanes] if b == 0 else e_ref[b - 1, at, lanes]


def _tree_sum(terms):
    while len(terms) > 1:
        terms = [a + b for a, b in zip(terms[::2], terms[1::2])] + ([terms[-1]] if len(terms) % 2 else [])
    return terms[0]


def _conv_taps(src_ref, e_ref, w_ref, shifts, tm, D, emit, u_ref=None, dw_ref=None):
    rows = min(tm, CONV_BLOCK if u_ref is None else CONV_BLOCK // 2)
    for lt in range(D // LANES):
        lanes = pl.ds(lt * LANES, LANES)
        for r0 in range(0, tm, rows):
            u = None if u_ref is None else u_ref[pl.ds(r0, rows), lanes]
            acc = None
            for j, s in enumerate(shifts):
                v = _shifted_rows(src_ref, e_ref, s, r0, rows, lanes)
                term = v * w_ref[j:j + 1, lanes]
                acc = term if acc is None else acc + term
                if u is not None:
                    prod = u * v
                    part = _tree_sum([prod[g * SUBLANES:(g + 1) * SUBLANES, :] for g in range(rows // SUBLANES)])
                    dw_ref[j:j + 1, lanes] += jnp.sum(part, axis=0, keepdims=True)
            emit(lt, pl.ds(r0, rows), acc)


def _mixer_mid_fwd(proj, o, sc_w, cf_w, cf_b, ln_g, ln_b, S, name, jobs=()):
    T, D = o.shape
    dv = D // HEADS
    tm = _tile(S, 256)
    per_seq = S // tm
    hb = tm // HALO

    def body(gret_ref, scb_ref, scc_ref, scx_ref, ga_ref, gb_ref, scc_h, scx_h, ga_h, gb_h, o_ref,
             scw_ref, cfw_ref, cfb_ref, lng_ref, lnb_ref, ya_ref, yb_ref, yc_ref, cv_ref, u1_ref, ext_ref, e_ref):
        first = (pl.program_id(0) % per_seq) == 0
        keep = jnp.where(first, 0.0, 1.0)
        gr = gret_ref[...]
        sg = gr * _sigmoid(gr)
        for hh in range(HEADS):
            cols = slice(hh * dv, (hh + 1) * dv)
            hn, _ = _ln_stats(o_ref[:, cols])
            ya_ref[:, cols] = (sg[:, cols] * hn).astype(BF16)
        ext_ref[:HALO, :] = scc_h[...] * scx_h[...] * keep
        ext_ref[HALO:, :] = scc_ref[...] * scx_ref[...]
        _build_shifted(ext_ref, e_ref, SC_FWD)

        def emit_cv(lt, rows, cv):
            lanes = pl.ds(lt * LANES, LANES)
            cv_ref[rows, lanes] = cv
            yb_ref[rows, lanes] = (scb_ref[rows, lanes] * cv).astype(BF16)

        _conv_taps(ext_ref, e_ref, scw_ref, SC_FWD, tm, D, emit_cv)
        ext_ref[:HALO, :] = ga_h[...] * _sigmoid(gb_h[...]) * keep
        ext_ref[HALO:, :] = ga_ref[...] * _sigmoid(gb_ref[...])
        _build_shifted(ext_ref, e_ref, CF_FWD)

        def emit_u1(lt, rows, acc):
            lanes = pl.ds(lt * LANES, LANES)
            u1_ref[rows, lanes] = acc + cfb_ref[:, lanes]

        _conv_taps(ext_ref, e_ref, cfw_ref, CF_FWD, tm, D, emit_u1)
        xh, _ = _ln_stats(u1_ref[...])
        u2 = xh * lng_ref[...] + lnb_ref[...]
        yc_ref[...] = (u2 * _sigmoid(u2)).astype(BF16)

    def colblk(c):
        return pl.BlockSpec((tm, D), lambda i: (i, c))

    def halo(c):
        return pl.BlockSpec((HALO, D), lambda i: (jnp.maximum(i * hb - 1, 0), c))

    row = pl.BlockSpec((tm, D), lambda i: (i, 0))
    vec = pl.BlockSpec((1, D), lambda i: (0, 0))
    return _pallas(
        body, name=name, grid=(T // tm,),
        in_specs=[colblk(COL_GRET), colblk(COL_SCB), colblk(COL_SCC), colblk(COL_SCX), colblk(COL_GLUA), colblk(COL_GLUB),
                  halo(COL_SCC), halo(COL_SCX), halo(COL_GLUA), halo(COL_GLUB), row,
                  pl.BlockSpec((SC_KERNEL, D), lambda i: (0, 0)), pl.BlockSpec((CF_KERNEL, D), lambda i: (0, 0)), vec, vec, vec],
        out_specs=[row, row, row, row, row],
        out_shape=[_sds((T, D), BF16)] * 3 + [_sds((T, D), F32)] * 2,
        scratch_shapes=[pltpu.VMEM((HALO + tm, D), F32), pltpu.VMEM((SUBLANES - 1, HALO + tm - SUBLANES, D), F32)], jobs=jobs,
    )(proj, proj, proj, proj, proj, proj, proj, proj, proj, proj, o, sc_w, cf_w, cf_b, ln_g, ln_b)


def _mixer_merge_fwd(x, proj, ya_in, yb_in, yc_in, g_post, w_ret, w_sc, w_cf, w_o, name, jobs=()):
    T, D = x.shape
    tm = _tile(T, 512)

    def body(x_ref, g0_ref, g1_ref, g2_ref, ya_in_ref, yb_in_ref, yc_in_ref, gp_ref, wr_ref, ws_ref, wc_ref, wo_ref,
             xo_ref, ya_ref, yb_ref, yc_ref, mg_ref, m_ref):
        ya = _dot_nn(ya_in_ref[...], wr_ref[...])
        yb = _dot_nn(yb_in_ref[...], ws_ref[...])
        yc = _dot_nn(yc_in_ref[...], wc_ref[...])
        ya_ref[...] = ya.astype(BF16)
        yb_ref[...] = yb.astype(BF16)
        yc_ref[...] = yc.astype(BF16)
        merged = (_sigmoid(g0_ref[...]) * ya + _sigmoid(g1_ref[...]) * yb + _sigmoid(g2_ref[...]) * yc).astype(BF16)
        mg_ref[...] = merged
        m = _dot_nn(merged, wo_ref[...])
        m_ref[...] = m
        xo_ref[...] = x_ref[...] + _rms_fwd(m, gp_ref[...])

    row = pl.BlockSpec((tm, D), lambda i: (i, 0))
    wsp = pl.BlockSpec((D, D), lambda i: (0, 0), pipeline_mode=pl.Buffered(1))

    def colblk(c):
        return pl.BlockSpec((tm, D), lambda i: (i, c))

    return _pallas(
        body, name=name, grid=(T // tm,),
        in_specs=[row, colblk(COL_GL), colblk(COL_GL + 1), colblk(COL_GL + 2), row, row, row,
                  pl.BlockSpec((1, D), lambda i: (0, 0)), wsp, wsp, wsp, wsp],
        out_specs=[row] * 6,
        out_shape=[_sds((T, D), F32)] + [_sds((T, D), BF16)] * 4 + [_sds((T, D), F32)], jobs=jobs,
    )(x, proj, proj, proj, ya_in, yb_in, yc_in, g_post, w_ret, w_sc, w_cf, w_o)


def _mixer_bwd_a(dxo, m, g_post, ya, yb, yc, proj, o, cv, u1, ln_g, ln_b, w_ret, w_sc, w_cf, w_o, name, jobs=()):
    T, D = m.shape
    dv = D // HEADS
    tm = _tile(T, 256)

    def body(dxo_ref, m_ref, gp_ref, ya_ref, yb_ref, yc_ref, g0_ref, g1_ref, g2_ref, gret_ref, scb_ref, o_ref, cv_ref, u1_ref,
             lng_ref, lnb_ref, wr_ref, ws_ref, wc_ref, wo_ref,
             dm_ref, dya_ref, dyb_ref, dyc_ref, dproj_ref, do_ref, dcv_ref, du1_ref, dgp_ref, dlng_ref, dlnb_ref, dcfb_ref,
             d2_ref, dgl_ref, copy_sems):
        @pl.when(pl.program_id(0) == 0)
        def _():
            for r in (dgp_ref, dlng_ref, dlnb_ref, dcfb_ref):
                r[...] = jnp.zeros_like(r)

        dm, dgp = _rms_bwd(m_ref[...], gp_ref[...], dxo_ref[...])
        dgp_ref[...] += dgp
        dmb = dm.astype(BF16)
        dm_ref[...] = dmb
        dmerged = _dot_nt(dmb, wo_ref[...])
        dys = []
        for k, (g_ref, y_ref, dy_ref) in enumerate(((g0_ref, ya_ref, dya_ref), (g1_ref, yb_ref, dyb_ref), (g2_ref, yc_ref, dyc_ref))):
            sg = _sigmoid(g_ref[...])
            dgl_ref[:, k * D:(k + 1) * D] = (dmerged * y_ref[...].astype(F32) * sg * (1.0 - sg)).astype(BF16)
            dy = (dmerged * sg).astype(BF16)
            dy_ref[...] = dy
            dys.append(dy)
        dya_in = _dot_nt(dys[0], wr_ref[...])
        gr = gret_ref[...]
        sr = _sigmoid(gr)
        for hh in range(HEADS):
            cols = slice(hh * dv, (hh + 1) * dv)
            hn, rstd = _ln_stats(o_ref[:, cols])
            d2_ref[:, cols] = (dya_in[:, cols] * hn * _dsilu(gr[:, cols], sr[:, cols])).astype(BF16)
            do_ref[:, cols] = _ln_bwd(hn, rstd, dya_in[:, cols] * gr[:, cols] * sr[:, cols])
        dyb_in = _dot_nt(dys[1], ws_ref[...])
        d2_ref[:, D:] = (dyb_in * cv_ref[...]).astype(BF16)
        dcv_ref[...] = dyb_in * scb_ref[...]
        dyc_in = _dot_nt(dys[2], wc_ref[...])
        xh, rstd = _ln_stats(u1_ref[...])
        u2 = xh * lng_ref[...] + lnb_ref[...]
        du2 = dyc_in * _dsilu(u2, _sigmoid(u2))
        dlng_ref[...] += jnp.sum(du2 * xh, axis=0, keepdims=True)
        dlnb_ref[...] += jnp.sum(du2, axis=0, keepdims=True)
        du1 = _ln_bwd(xh, rstd, du2 * lng_ref[...])
        du1_ref[...] = du1
        dcfb_ref[...] += jnp.sum(du1, axis=0, keepdims=True)
        rows = pl.ds(pl.multiple_of(pl.program_id(0) * tm, tm), tm)
        copies = [pltpu.make_async_copy(d2_ref, dproj_ref.at[rows, pl.ds(COL_GRET * D, 2 * D)], copy_sems.at[0]),
                  pltpu.make_async_copy(dgl_ref, dproj_ref.at[rows, pl.ds(COL_GL * D, 3 * D)], copy_sems.at[1])]
        for cp in copies:
            cp.start()
        for cp in copies:
            cp.wait()

    row = pl.BlockSpec((tm, D), lambda i: (i, 0))
    vec = pl.BlockSpec((1, D), lambda i: (0, 0))
    wsp = pl.BlockSpec((D, D), lambda i: (0, 0), pipeline_mode=pl.Buffered(1))

    def colblk(c):
        return pl.BlockSpec((tm, D), lambda i: (i, c))

    return _pallas(
        body, name=name, grid=(T // tm,),
        in_specs=[row, row, vec, row, row, row, colblk(COL_GL), colblk(COL_GL + 1), colblk(COL_GL + 2),
                  colblk(COL_GRET), colblk(COL_SCB), row, row, row, vec, vec, wsp, wsp, wsp, wsp],
        out_specs=[row, row, row, row, ANY, row, row, row, vec, vec, vec, vec],
        out_shape=[_sds((T, D), BF16)] * 4 + [_sds((T, proj.shape[1]), BF16)] + [_sds((T, D), F32)] * 3 + [_sds((1, D), F32)] * 4,
        scratch_shapes=[pltpu.VMEM((tm, 2 * D), BF16), pltpu.VMEM((tm, 3 * D), BF16), pltpu.SemaphoreType.DMA((2,))], jobs=jobs,
    )(dxo, m, g_post, ya, yb, yc, proj, proj, proj, proj, proj, o, cv, u1, ln_g, ln_b, w_ret, w_sc, w_cf, w_o)


def _mixer_bwd_b(dcv, du1, proj, sc_w, cf_w, dproj, S, name, jobs=()):
    T, D = dcv.shape
    tm = _tile(S, 256)
    per_seq = S // tm
    hb = tm // HALO
    last_hb = T // HALO - 1

    def body(dcv_ref, du1_ref, dcv_n, du1_n, scc_ref, scx_ref, ga_ref, gb_ref, scw_ref, cfw_ref, dproj_in,
             d4_ref, dscw_ref, dcfw_ref, u_ref, dext_ref, e_ref, sb_ref):
        i = pl.program_id(0)
        keep_next = jnp.where((i % per_seq) == per_seq - 1, 0.0, 1.0)

        @pl.when(i == 0)
        def _():
            dscw_ref[...] = jnp.zeros_like(dscw_ref)
            dcfw_ref[...] = jnp.zeros_like(dcfw_ref)

        dext_ref[:tm, :] = dcv_ref[...]
        dext_ref[tm:, :] = dcv_n[...] * keep_next
        _build_shifted(dext_ref, e_ref, SC_BWD)
        u_ref[...] = scc_ref[...] * scx_ref[...]

        def emit_dz(lt, rows, dz):
            lanes = pl.ds(lt * LANES, LANES)
            d4_ref[rows, pl.ds(lt * LANES, LANES)] = (dz * scx_ref[rows, lanes]).astype(BF16)
            d4_ref[rows, pl.ds(D + lt * LANES, LANES)] = (dz * scc_ref[rows, lanes]).astype(BF16)

        _conv_taps(dext_ref, e_ref, scw_ref, SC_BWD, tm, D, emit_dz, u_ref=u_ref, dw_ref=dscw_ref)
        dext_ref[:tm, :] = du1_ref[...]
        dext_ref[tm:, :] = du1_n[...] * keep_next
        _build_shifted(dext_ref, e_ref, CF_BWD)
        sb_ref[...] = _sigmoid(gb_ref[...])
        u_ref[...] = ga_ref[...] * sb_ref[...]

        def emit_du0(lt, rows, du0):
            lanes = pl.ds(lt * LANES, LANES)
            sb = sb_ref[rows, lanes]
            d4_ref[rows, pl.ds(2 * D + lt * LANES, LANES)] = (du0 * sb).astype(BF16)
            d4_ref[rows, pl.ds(3 * D + lt * LANES, LANES)] = (du0 * ga_ref[rows, lanes] * sb * (1.0 - sb)).astype(BF16)

        _conv_taps(dext_ref, e_ref, cfw_ref, CF_BWD, tm, D, emit_du0, u_ref=u_ref, dw_ref=dcfw_ref)

    row = pl.BlockSpec((tm, D), lambda i: (i, 0))
    nxt = pl.BlockSpec((HALO, D), lambda i: (jnp.minimum((i + 1) * hb, last_hb), 0))

    def colblk(c):
        return pl.BlockSpec((tm, D), lambda i: (i, c))

    return _pallas(
        body, name=name, grid=(T // tm,),
        in_specs=[row, row, nxt, nxt, colblk(COL_SCC), colblk(COL_SCX), colblk(COL_GLUA), colblk(COL_GLUB),
                  pl.BlockSpec((SC_KERNEL, D), lambda i: (0, 0)), pl.BlockSpec((CF_KERNEL, D), lambda i: (0, 0)), ANY],
        out_specs=[pl.BlockSpec((tm, 4 * D), lambda i: (i, COL_SCC // 4)), pl.BlockSpec((SC_KERNEL, D), lambda i: (0, 0)),
                   pl.BlockSpec((CF_KERNEL, D), lambda i: (0, 0))],
        out_shape=[_sds(dproj.shape, BF16), _sds((SC_KERNEL, D), F32), _sds((CF_KERNEL, D), F32)],
        scratch_shapes=[pltpu.VMEM((tm, D), F32), pltpu.VMEM((tm + HALO, D), F32),
                        pltpu.VMEM((SUBLANES - 1, HALO + tm - SUBLANES, D), F32), pltpu.VMEM((tm, D), F32)],
        aliases={10: 0}, jobs=jobs,
    )(dcv, du1, dcv, du1, proj, proj, proj, proj, sc_w, cf_w, dproj)


def _loss_grad(y, tgt, name):
    T, D = y.shape
    tm = _tile(T, 512)

    def body(y_ref, t_ref, dy_ref, loss_ref):
        @pl.when(pl.program_id(0) == 0)
        def _():
            loss_ref[...] = jnp.zeros_like(loss_ref)

        e = y_ref[...] - t_ref[...]
        dy_ref[...] = e * (1.0 / D)
        loss_ref[...] += 0.5 * jnp.sum(jnp.sum(e * e, axis=-1, keepdims=True) * (1.0 / D), axis=0, keepdims=True)

    row = pl.BlockSpec((tm, D), lambda i: (i, 0))
    return _pallas(body, name=name, grid=(T // tm,), in_specs=[row, row],
                   out_specs=[row, pl.BlockSpec((1, 1), lambda i: (0, 0))],
                   out_shape=[_sds((T, D), F32), _sds((1, 1), F32)])(y, tgt)


def _mesh_pos():
    return lax.axis_index("x"), lax.axis_index("y"), lax.axis_index("c")


def _other_chips(x, y):
    return [(1 - x, y), (x, 1 - y), (1 - x, 1 - y)]


def _allgather_job(shards):
    nt = len(shards)

    def parts(ins, outs, send, recv):
        x, y, c = _mesh_pos()

        def slab(t, px, py, pc):
            return outs[t].at[4 * px + 2 * py + pc]

        def copy(t, k, block, to, src=None):
            return pltpu.make_async_remote_copy(
                src_ref=slab(t, *block) if src is None else src, dst_ref=slab(t, *block),
                send_sem=send.at[7 * t + k], recv_sem=recv.at[7 * t + k], device_id=to, device_id_type=MESH)

        return (x, y, c), (x, y, 1 - c), _other_chips(x, y), c, slab, copy

    def start(ins, outs, send, recv, loc):
        me, sibling, chips, c, slab, copy = parts(ins, outs, send, recv)
        for t in range(nt):
            pltpu.make_async_copy(ins[t], slab(t, *me), loc.at[t]).start()
            copy(t, 0, me, sibling, src=ins[t]).start()
            for j, chip in enumerate(chips):
                copy(t, 1 + j, me, (*chip, c), src=ins[t]).start()

    def finish(ins, outs, send, recv, loc):
        me, sibling, chips, c, slab, copy = parts(ins, outs, send, recv)
        for j, chip in enumerate(chips):
            for t in range(nt):
                copy(t, 1 + j, (*chip, c), me).wait_recv()
                copy(t, 4 + j, (*chip, c), sibling).start()
        for t in range(nt):
            copy(t, 0, sibling, me).wait_recv()
            for j, chip in enumerate(chips):
                copy(t, 4 + j, (*chip, 1 - c), me).wait_recv()
        for t in range(nt):
            copy(t, 0, me, sibling, src=ins[t]).wait_send()
            for j, chip in enumerate(chips):
                copy(t, 1 + j, me, (*chip, c), src=ins[t]).wait_send()
                copy(t, 4 + j, (*chip, c), sibling).wait_send()
            pltpu.make_async_copy(ins[t], slab(t, *me), loc.at[t]).wait()

    return _Job(shards, [_sds((N_DEV,) + s.shape, s.dtype) for s in shards], 7 * nt, nt, start, finish)


def _to_sibling_job(grads):
    nt = len(grads)

    def copies(ins, outs, send, recv):
        x, y, c = _mesh_pos()
        return [pltpu.make_async_remote_copy(src_ref=ins[t].at[:, 1 - c], dst_ref=outs[t], send_sem=send.at[t], recv_sem=recv.at[t],
                                             device_id=(x, y, 1 - c), device_id_type=MESH) for t in range(nt)]

    def start(ins, outs, send, recv, loc):
        for cp in copies(ins, outs, send, recv):
            cp.start()

    def finish(ins, outs, send, recv, loc):
        for cp in copies(ins, outs, send, recv):
            cp.wait()

    return _Job(grads, [_sds((N_CHIP,) + g.shape[2:], g.dtype) for g in grads], nt, 0, start, finish)


def _to_chips_job(pieces):
    nt = len(pieces)
    pairs = [p for p, _, _ in pieces]

    def copies(ins, outs, send, recv, loc):
        x, y, c = _mesh_pos()

        def src(t, chip):
            return ins[t].at[chip, pl.ds(pieces[t][1], pieces[t][2])]

        remote = [pltpu.make_async_remote_copy(src_ref=src(t, 2 * px + py), dst_ref=outs[t].at[k], send_sem=send.at[3 * t + k],
                                               recv_sem=recv.at[3 * t + k], device_id=(px, py, c), device_id_type=MESH)
                  for t in range(nt) for k, (px, py) in enumerate(_other_chips(x, y))]
        local = [pltpu.make_async_copy(src(t, 2 * x + y), outs[t].at[3], loc.at[t]) for t in range(nt)]
        return remote + local

    def start(ins, outs, send, recv, loc):
        for cp in copies(ins, outs, send, recv, loc):
            cp.start()

    def finish(ins, outs, send, recv, loc):
        for cp in copies(ins, outs, send, recv, loc):
            cp.wait()

    return _Job(pairs, [_sds((N_CHIP, rows, p.shape[2]), p.dtype) for p, _, rows in pieces], 3 * nt, nt, start, finish)


def _run_jobs(jobs, name):
    def body(o_ref):
        o_ref[...] = jnp.zeros_like(o_ref)

    _, jres = _pallas(body, name=name, grid=(1,), in_specs=[], out_specs=pl.BlockSpec((8, 128), lambda i: (0, 0)),
                      out_shape=_sds((8, 128), F32), jobs=jobs)()
    return jres


def _allgather_small(v, name):
    R, C = v.shape

    def body(x_ref, out_ref, send_sems, recv_sems, local_sem):
        x, y, c = _mesh_pos()
        me, sibling = (x, y, c), (x, y, 1 - c)
        chips = _other_chips(x, y)

        def slab(px, py, pc):
            return out_ref.at[4 * px + 2 * py + pc]

        def copy(k, block, to, src=None):
            return pltpu.make_async_remote_copy(
                src_ref=slab(*block) if src is None else src, dst_ref=slab(*block),
                send_sem=send_sems.at[k], recv_sem=recv_sems.at[k], device_id=to, device_id_type=MESH)

        mine = pltpu.make_async_copy(x_ref, slab(*me), local_sem)
        mine.start()
        first = [copy(0, me, sibling, src=x_ref)]
        first += [copy(1 + j, me, (*chip, c), src=x_ref) for j, chip in enumerate(chips)]
        for cp in first:
            cp.start()
        passed = [copy(4 + j, (*chip, c), sibling) for j, chip in enumerate(chips)]
        for j, chip in enumerate(chips):
            copy(1 + j, (*chip, c), me).wait_recv()
            passed[j].start()
        copy(0, sibling, me).wait_recv()
        for j, chip in enumerate(chips):
            copy(4 + j, (*chip, 1 - c), me).wait_recv()
        for cp in first + passed:
            cp.wait_send()
        mine.wait()

    return _comm_call(
        body, name=name, in_specs=[pl.BlockSpec(memory_space=pltpu.VMEM)], out_specs=pl.BlockSpec(memory_space=pltpu.VMEM),
        out_shape=_sds((N_DEV, R, C), v.dtype),
        scratch_shapes=[pltpu.SemaphoreType.DMA((7,)), pltpu.SemaphoreType.DMA((7,)), pltpu.SemaphoreType.DMA],
    )(v)


def _pair_sum(core, grad, got, name):
    _, _, R, C = grad.shape
    tr = _tile(R, 1024)

    def body(core_ref, mine_ref, got_ref, out_ref):
        out_ref[...] = (mine_ref[...].astype(F32) + got_ref[...].astype(F32)).astype(BF16)

    blk = (None, tr, C)
    return _pallas(
        body, name=name, grid=(N_CHIP, R // tr), nprefetch=1,
        in_specs=[pl.BlockSpec((None, None, tr, C), lambda k, r, core: (k, core[0], r, 0)),
                  pl.BlockSpec(blk, lambda k, r, core: (k, r, 0))],
        out_specs=pl.BlockSpec(blk, lambda k, r, core: (k, r, 0)),
        out_shape=_sds((N_CHIP, R, C), BF16),
    )(core, grad, got)


def _adamw(w, g, m, v):
    m = ADAM_B1 * m + (1.0 - ADAM_B1) * g
    v = ADAM_B2 * v + (1.0 - ADAM_B2) * (g * g)
    m_hat = m / (1.0 - ADAM_B1 ** ADAM_STEP)
    v_hat = v / (1.0 - ADAM_B2 ** ADAM_STEP)
    delta = -ADAM_LR * (m_hat / (jnp.sqrt(v_hat) + ADAM_EPS) + ADAM_WD * w)
    return delta, m, v


def _final_adamw(sums, w, m, v, l, r0, into, name, jobs=()):
    L, R, C = w.shape
    rows = sums.shape[1]
    tr = _tile(rows, 512 if C <= 1024 else 256)
    assert r0 % tr == 0
    b0 = r0 // tr

    def body(s_ref, w_ref, m_ref, v_ref, *rest):
        g_out, d_out, m_out, v_out = rest[-4:]
        g = s_ref[0].astype(F32)
        for k in range(1, N_CHIP):
            g = g + s_ref[k].astype(F32)
        d, mn, vn = _adamw(w_ref[...], g, m_ref[...], v_ref[...])
        g_out[...] = g
        d_out[...] = d
        m_out[...] = mn
        v_out[...] = vn

    loc = pl.BlockSpec((None, tr, C), lambda r: (l, b0 + r, 0))
    in_specs = [pl.BlockSpec((N_CHIP, tr, C), lambda r: (0, r, 0)), loc, loc, loc]
    args = [sums, w, m, v]
    aliases = None
    if into is not None:
        in_specs += [ANY] * 4
        args += list(into)
        aliases = {4 + i: i for i in range(4)}
    return _pallas(body, name=name, grid=(rows // tr,), in_specs=in_specs, out_specs=[loc] * 4,
                   out_shape=[_sds((L, R, C), F32)] * 4, aliases=aliases, jobs=jobs)(*args)


def _small_adamw(dev, allg, w, m, v, width, name):
    R = w.shape[0]

    def body(dev_ref, a_ref, w_ref, m_ref, v_ref, g_out, d_out, m_out, v_out):
        g = a_ref[0]
        for k in range(1, N_DEV):
            g = g + a_ref[k]
        d, mn, vn = _adamw(w_ref[...], g, m_ref[...], v_ref[...])
        g_out[...] = g
        d_out[...] = d
        m_out[...] = mn
        v_out[...] = vn

    full = allg.shape[2] == width
    loc = pl.BlockSpec((R, width), lambda i, dev: (0, 0))
    return _pallas(
        body, name=name, grid=(1,), nprefetch=1,
        in_specs=[pl.BlockSpec((N_DEV, R, width), lambda i, dev: (0, 0, 0 if full else dev[0])), loc, loc, loc],
        out_specs=[loc] * 4, out_shape=[_sds((R, width), F32)] * 4,
    )(dev, allg, w, m, v)


BIG = ("ffn1_w_gu", "ffn1_w_down", "w_in", "w_ret_o", "w_sc_o", "w_cf_o", "w_o", "ffn2_w_gu", "ffn2_w_down")
FFN1 = ("ffn1_w_gu", "ffn1_w_down")
MIX_OUT = ("w_ret_o", "w_sc_o", "w_cf_o", "w_o")
ROW_BLOCKS = ("ffn1_w_down", "ffn2_w_down") + MIX_OUT


def kernel(x, positions, norm_g, ffn1_w_gu, ffn1_w_down, w_in, w_ret_o, sc_conv_w, w_sc_o, cf_dw_w, cf_dw_b, cf_ln_g, cf_ln_b, w_cf_o, w_o, ffn2_w_gu, ffn2_w_down, loss_target, m_norm_g, m_ffn1_w_gu, m_ffn1_w_down, m_w_in, m_w_ret_o, m_sc_conv_w, m_w_sc_o, m_cf_dw_w, m_cf_dw_b, m_cf_ln_g, m_cf_ln_b, m_w_cf_o, m_w_o, m_ffn2_w_gu, m_ffn2_w_down, v_norm_g, v_ffn1_w_gu, v_ffn1_w_down, v_w_in, v_w_ret_o, v_sc_conv_w, v_w_sc_o, v_cf_dw_w, v_cf_dw_b, v_cf_ln_g, v_cf_ln_b, v_w_cf_o, v_w_o, v_ffn2_w_gu, v_ffn2_w_down):
    B, S, D = x.shape
    T = B * S
    L = norm_g.shape[0]
    DL = norm_g.shape[2]
    dk = D // 8
    mx, my, mc = _mesh_pos()
    dev_idx = jnp.reshape(4 * mx + 2 * my + mc, (1,)).astype(jnp.int32)
    core_idx = jnp.reshape(mc, (1,)).astype(jnp.int32)

    w32 = dict(ffn1_w_gu=ffn1_w_gu, ffn1_w_down=ffn1_w_down, w_in=w_in, w_ret_o=w_ret_o, w_sc_o=w_sc_o, w_cf_o=w_cf_o,
               w_o=w_o, ffn2_w_gu=ffn2_w_gu, ffn2_w_down=ffn2_w_down)
    m32 = dict(ffn1_w_gu=m_ffn1_w_gu, ffn1_w_down=m_ffn1_w_down, w_in=m_w_in, w_ret_o=m_w_ret_o, w_sc_o=m_w_sc_o,
               w_cf_o=m_w_cf_o, w_o=m_w_o, ffn2_w_gu=m_ffn2_w_gu, ffn2_w_down=m_ffn2_w_down)
    v32 = dict(ffn1_w_gu=v_ffn1_w_gu, ffn1_w_down=v_ffn1_w_down, w_in=v_w_in, w_ret_o=v_w_ret_o, w_sc_o=v_w_sc_o,
               w_cf_o=v_w_cf_o, w_o=v_w_o, ffn2_w_gu=v_ffn2_w_gu, ffn2_w_down=v_ffn2_w_down)

    W = [dict() for _ in range(L)]

    def gather(l, names):
        return _allgather_job([w32[n][l].astype(BF16) for n in names])

    def place(l, names, slabs):
        for n, a in zip(names, slabs):
            W[l][n] = a.reshape(a.shape[0] * a.shape[1], a.shape[2]) if n in ROW_BLOCKS else a

    def carried(fn, *args, jobs, **kw):
        out = fn(*args, jobs=jobs, **kw)
        return out if jobs else (out, [])

    n_sh = 6 + SC_KERNEL + CF_KERNEL
    small_sh = jnp.concatenate([norm_g, sc_conv_w, cf_dw_w], axis=1).reshape(L * n_sh, DL)
    sh_all = _allgather_small(small_sh, "allgather_small_params")
    sh_full = jnp.transpose(sh_all, (1, 0, 2)).reshape(L, n_sh, D)
    norm_full = sh_full[:, :6]
    scw_full = sh_full[:, 6:6 + SC_KERNEL]
    cfw_full = sh_full[:, 6 + SC_KERNEL:]
    place(0, FFN1, _run_jobs([gather(0, FFN1)], "allgather_first")[0])

    half = dk // 2
    inv_freq = ROPE_BASE ** (-jnp.arange(half, dtype=F32) / half)
    invf = jnp.concatenate([inv_freq, inv_freq])[None, :]
    sgn = jnp.concatenate([-jnp.ones((half,), F32), jnp.ones((half,), F32)])[None, :]
    cosf, sinf = _rope_tables(positions.reshape(T, 1), invf, sgn, "rope_tables")
    lgs = jnp.log(1.0 - 2.0 ** (-5.0 - jnp.arange(HEADS, dtype=F32)))

    def vec(a):
        return a.reshape(1, D)

    xc = x.reshape(T, D)
    saved = []
    for l in range(L):
        g = norm_full[l]
        sv = {}
        sv["x0"] = xc
        (xc, sv["h1"], sv["gate1"], sv["up1"], sv["f1"]), jr = _ffn_fwd(
            xc, vec(g[0]), vec(g[1]), W[l]["ffn1_w_gu"], W[l]["ffn1_w_down"], f"ffn1_fwd_{l}", jobs=[gather(l, ("w_in",))])
        place(l, ("w_in",), jr[0])
        sv["x1"] = xc
        (proj, sv["h2"]), jr = _norm_proj(xc, vec(g[2]), W[l]["w_in"], f"proj_fwd_{l}", jobs=[gather(l, ("ffn2_w_gu",))])
        place(l, ("ffn2_w_gu",), jr[0])
        sv["proj"] = proj
        o, jr = _ret_fwd(proj, cosf, sinf, lgs, B, S, D, f"retention_fwd_{l}", jobs=[gather(l, MIX_OUT)])
        place(l, MIX_OUT, jr[0])
        sv["o"] = o
        nxt = l + 1 < L
        (ya_in, yb_in, yc_in, sv["cv"], sv["u1"]), jr = _mixer_mid_fwd(
            proj, o, scw_full[l], cfw_full[l], vec(cf_dw_b[l]), vec(cf_ln_g[l]), vec(cf_ln_b[l]), S,
            f"mixer_mid_fwd_{l}", jobs=[gather(l, ("ffn2_w_down",))])
        place(l, ("ffn2_w_down",), jr[0])
        sv["ya_in"], sv["yb_in"], sv["yc_in"] = ya_in, yb_in, yc_in
        (xc, sv["ya"], sv["yb"], sv["yc"], sv["merged"], sv["m"]), jr = carried(
            _mixer_merge_fwd, xc, proj, ya_in, yb_in, yc_in, vec(g[3]), W[l]["w_ret_o"], W[l]["w_sc_o"], W[l]["w_cf_o"], W[l]["w_o"],
            f"mixer_merge_fwd_{l}", jobs=[gather(l + 1, ("ffn1_w_down",))] if nxt else [])
        if nxt:
            place(l + 1, ("ffn1_w_down",), jr[0])
        sv["x2"] = xc
        (xc, sv["h3"], sv["gate2"], sv["up2"], sv["f2"]), jr = carried(
            _ffn_fwd, xc, vec(g[4]), vec(g[5]), W[l]["ffn2_w_gu"], W[l]["ffn2_w_down"], f"ffn2_fwd_{l}",
            jobs=[gather(l + 1, ("ffn1_w_gu",))] if nxt else [])
        if nxt:
            place(l + 1, ("ffn1_w_gu",), jr[0])
        saved.append(sv)

    dx, loss_part = _loss_grad(xc, loss_target.reshape(T, D), "loss")
    loss = lax.psum(loss_part[0, 0], ("x", "y", "c"))

    to_sibling, to_chips, reduced = [], [], {}

    def grad_ready(n, l, g):
        if g.ndim == 2:
            g = g.reshape(N_DEV, g.shape[0] // N_DEV, g.shape[1])
        to_sibling.append((n, l, g.reshape(N_CHIP, 2, g.shape[1], g.shape[2])))

    tail = []

    def take_jobs(heavy, max_pieces=None):
        a = list(to_sibling)
        b = [e for e in to_chips if heavy or N_CHIP * e[4] * e[2].shape[2] <= LIGHT_CARRIER_ELEMS][:max_pieces]
        to_sibling.clear()
        to_chips[:] = [e for e in to_chips if not any(e is t for t in b)]
        jobs = ([_to_chips_job([e[2:] for e in b])] if b else []) + ([_to_sibling_job([g for _, _, g in a])] if a else [])
        return jobs, (a, b)

    def settle(taken, jres):
        a, b = taken
        if b:
            for (n, l, _, r0, rows), got in zip(b, jres[0]):
                reduced.setdefault((n, l), []).append((r0, got))
        if a:
            for (n, l, g), got in zip(a, jres[-1]):
                pairs = _pair_sum(core_idx, g, got, f"pair_sum_{n}_{l}")
                R = pairs.shape[1]
                k = TAIL_PIECES if tail and R % (TAIL_PIECES * SUBLANES) == 0 else 1
                to_chips.extend((n, l, pairs, i * (R // k), R // k) for i in range(k))

    def carrier(fn, *args, heavy=True, **kw):
        jobs, taken = take_jobs(heavy)
        out, jres = carried(fn, *args, jobs=jobs, **kw)
        settle(taken, jres)
        return out

    small_rows = [None] * L
    cb_gu = W[0]["ffn1_w_gu"].shape[2]
    cb_in = W[0]["w_in"].shape[2]

    def ffn_grads(tag, wgu_name, wd_name, dxo, x_in, ht, gate, up, f, g_pre, g_post, l):
        dxi, df, dgate, dup, act, dgpre, dgpost = carrier(_ffn_bwd, dxo, x_in, f, gate, up, g_pre, g_post, W[l][wgu_name],
                                                          W[l][wd_name], f"{tag}_bwd_{l}")
        grad_ready(wd_name, l, carrier(_mm_tn, act, df, f"{tag}_dwd_{l}", heavy=False))
        half = carrier(_mm_tn, ht, dgate, f"{tag}_dwg_{l}", slab_cols=cb_gu, n_slabs=N_DEV, a_is_transposed=True, heavy=False)
        grad_ready(wgu_name, l, carrier(_mm_tn, ht, dup, f"{tag}_dwu_{l}", slab_cols=cb_gu, n_slabs=N_DEV, first_slab=N_DEV // 2,
                                        into=half, a_is_transposed=True, heavy=False))
        return dxi, dgpre, dgpost

    for l in reversed(range(L)):
        sv = saved[l]
        g = norm_full[l]
        dx, dg4, dg5 = ffn_grads("ffn2", "ffn2_w_gu", "ffn2_w_down", dx, sv["x2"], sv["h3"], sv["gate2"], sv["up2"], sv["f2"],
                                 vec(g[4]), vec(g[5]), l)
        (dm, dya, dyb, dyc, dproj, do, dcv, du1, dg3, dlng, dlnb, dcfb) = carrier(
            _mixer_bwd_a, dx, sv["m"], vec(g[3]), sv["ya"], sv["yb"], sv["yc"], sv["proj"], sv["o"], sv["cv"], sv["u1"],
            vec(cf_ln_g[l]), vec(cf_ln_b[l]), W[l]["w_ret_o"], W[l]["w_sc_o"], W[l]["w_cf_o"], W[l]["w_o"], f"mixer_bwd_a_{l}")
        grad_ready("w_o", l, _mm_tn(sv["merged"], dm, f"dw_o_{l}", tt_pref=2048))
        grad_ready("w_ret_o", l, _mm_tn(sv["ya_in"], dya, f"dw_ret_o_{l}", tt_pref=2048))
        grad_ready("w_sc_o", l, _mm_tn(sv["yb_in"], dyb, f"dw_sc_o_{l}", tt_pref=2048))
        grad_ready("w_cf_o", l, _mm_tn(sv["yc_in"], dyc, f"dw_cf_o_{l}", tt_pref=2048))
        dproj, dscw, dcfw = carrier(_mixer_bwd_b, dcv, du1, sv["proj"], scw_full[l], cfw_full[l], dproj, S, f"mixer_bwd_b_{l}")
        dproj = carrier(_ret_bwd, do, sv["proj"], cosf, sinf, lgs, dproj, B, S, D, f"retention_bwd_{l}", heavy=False)
        grad_ready("w_in", l, carrier(_mm_tn, sv["h2"], dproj, f"dw_in_{l}", slab_cols=cb_in, tn_pref=cb_in, a_is_transposed=True))
        dx, dg2 = carrier(_proj_bwd, dproj, W[l]["w_in"], sv["x1"], vec(g[2]), dx, f"proj_bwd_{l}")
        dx, dg0, dg1 = ffn_grads("ffn1", "ffn1_w_gu", "ffn1_w_down", dx, sv["x0"], sv["h1"], sv["gate1"], sv["up1"], sv["f1"],
                                 vec(g[0]), vec(g[1]), l)
        small_rows[l] = (jnp.concatenate([dg0, dg1, dg2, dg3, dg4, dg5, dscw, dcfw], axis=0),
                         jnp.concatenate([dcfb, dlng, dlnb], axis=0))
    grad_x = dx.reshape(B, S, D)
    tail.append(True)
    big_out, done = {}, set()

    def next_adamw():
        for n in BIG:
            for l in reversed(range(L)):
                for r0, got in reduced.get((n, l), []):
                    if (n, l, r0) not in done:
                        return n, l, r0, got
        return None

    def run_adamw(n, l, r0, got, jobs=()):
        out = _final_adamw(got, w32[n], m32[n], v32[n], l, r0, big_out.get(n), f"adamw_{n}_{l}_{r0}", jobs=jobs)
        big_out[n], jres = out if jobs else (out, [])
        done.add((n, l, r0))
        return jres

    flush = 0
    while to_sibling or to_chips:
        unit = next_adamw()
        jobs, taken = take_jobs(False, max_pieces=1) if unit else ([], None)
        if jobs:
            settle(taken, run_adamw(*unit, jobs=jobs))
        else:
            jobs, taken = take_jobs(True)
            settle(taken, _run_jobs(jobs, f"grads_flush_{flush}"))
            flush += 1
    while next_adamw():
        run_adamw(*next_adamw())

    sh_part = jnp.concatenate([small_rows[l][0] for l in range(L)], axis=0)
    rep_part = jnp.concatenate([small_rows[l][1] for l in range(L)] + [jnp.zeros((8 - 3 * L % 8, D), F32)] * (1 if 3 * L % 8 else 0), axis=0)
    sh_g = _allgather_small(sh_part, "allgather_small_grads")
    rep_g = _allgather_small(rep_part, "allgather_replicated_grads")

    def pack_sh(a, b, c):
        return jnp.concatenate([a, b, c], axis=1).reshape(L * n_sh, DL)

    def pack_rep(a, b, c):
        rows = jnp.stack([a, b, c], axis=1).reshape(3 * L, D)
        return jnp.concatenate([rows, jnp.ones((rep_part.shape[0] - 3 * L, D), F32)], axis=0)

    sh_res = _small_adamw(dev_idx, sh_g, pack_sh(norm_g, sc_conv_w, cf_dw_w), pack_sh(m_norm_g, m_sc_conv_w, m_cf_dw_w),
                          pack_sh(v_norm_g, v_sc_conv_w, v_cf_dw_w), DL, "adamw_small_sharded")
    rep_res = _small_adamw(dev_idx, rep_g, pack_rep(cf_dw_b, cf_ln_g, cf_ln_b), pack_rep(m_cf_dw_b, m_cf_ln_g, m_cf_ln_b),
                           pack_rep(v_cf_dw_b, v_cf_ln_g, v_cf_ln_b), D, "adamw_small_replicated")

    def unpack_sh(a):
        a = a.reshape(L, n_sh, DL)
        return {"norm_g": a[:, :6], "sc_conv_w": a[:, 6:6 + SC_KERNEL], "cf_dw_w": a[:, 6 + SC_KERNEL:]}

    def unpack_rep(a):
        a = a[:3 * L].reshape(L, 3, D)
        return {"cf_dw_b": a[:, 0], "cf_ln_g": a[:, 1], "cf_ln_b": a[:, 2]}

    order = ("norm_g", "ffn1_w_gu", "ffn1_w_down", "w_in", "w_ret_o", "sc_conv_w", "w_sc_o", "cf_dw_w", "cf_dw_b", "cf_ln_g",
             "cf_ln_b", "w_cf_o", "w_o", "ffn2_w_gu", "ffn2_w_down")
    outs = []
    for kind in range(4):
        small = {**unpack_sh(sh_res[kind]), **unpack_rep(rep_res[kind])}
        outs += [big_out[n][kind] if n in big_out else small[n] for n in order]
    return (loss, grad_x, *outs)
```

```python
import functools

import jax
import jax.numpy as jnp
from jax import lax
from jax.experimental import pallas as pl
from jax.experimental.pallas import tpu as pltpu

F32 = jnp.float32
BF16 = jnp.bfloat16
MESH = pl.DeviceIdType.MESH
ANY = pl.BlockSpec(memory_space=pl.ANY)

N_DEV = 8
N_CHIP = 4
CHUNK = 64
HEADS = 4
ROPE_BASE = 10000.0
NORM_EPS = 1e-6
LN_EPS = 1e-5
SC_KERNEL = 3
CF_KERNEL = 31
HALO = 32
ADAM_LR = 0.001
ADAM_B1 = 0.9
ADAM_B2 = 0.999
ADAM_EPS = 1e-08
ADAM_WD = 0.01
ADAM_STEP = 10
VMEM_LIMIT_V7X = 56 * 1024 * 1024
LIGHT_CARRIER_ELEMS = 3 * 1024 * 1024
TAIL_PIECES = 4


class _Job:
    def __init__(self, ins, out_shapes, n_sems, n_local, start, finish, peers):
        self.ins, self.out_shapes, self.n_sems, self.n_local = list(ins), list(out_shapes), n_sems, n_local
        self.start, self.finish = start, finish
        self.peers = frozenset(peers)


COLLECTIVE_IDS = {frozenset({"sibling"}): 0, frozenset({"chips"}): 1, frozenset({"sibling", "chips"}): 2}


def _pallas(body, *, name, grid, in_specs, out_specs, out_shape, scratch_shapes=(), aliases=None, nprefetch=0, jobs=()):
    extra = {}
    single = not isinstance(out_shape, (list, tuple))
    out_shape = [out_shape] if single else list(out_shape)
    out_specs = [out_specs] if single else list(out_specs)
    in_specs, scratch = list(in_specs), list(scratch_shapes)
    n_in, n_out, n_scr = len(in_specs), len(out_shape), len(scratch)
    for jb in jobs:
        in_specs += [ANY] * len(jb.ins)
        out_specs += [ANY] * len(jb.out_shapes)
        out_shape += jb.out_shapes
        scratch += [pltpu.SemaphoreType.DMA((jb.n_sems,)), pltpu.SemaphoreType.DMA((jb.n_sems,)),
                    pltpu.SemaphoreType.DMA((max(jb.n_local, 1),))]

    def with_jobs(*refs):
        pre, refs = refs[:nprefetch], refs[nprefetch:]
        ins, p = refs[:n_in], n_in
        jins = []
        for jb in jobs:
            jins.append(refs[p:p + len(jb.ins)])
            p += len(jb.ins)
        outs = refs[p:p + n_out]
        p += n_out
        jouts = []
        for jb in jobs:
            jouts.append(refs[p:p + len(jb.out_shapes)])
            p += len(jb.out_shapes)
        scr = refs[p:p + n_scr]
        p += n_scr
        pids = [pl.program_id(d) for d in range(len(grid))]
        first = functools.reduce(jnp.logical_and, [pid == 0 for pid in pids])
        last = functools.reduce(jnp.logical_and, [pid == g - 1 for pid, g in zip(pids, grid)])

        @pl.when(first)
        def _():
            x, y, c = _mesh_pos()
            peers = ([(x, y, 1 - c)] if "sibling" in kinds else []) + ([(px, py, c) for px, py in _other_chips(x, y)] if "chips" in kinds else [])
            barrier = pltpu.get_barrier_semaphore()
            for peer in peers:
                pl.semaphore_signal(barrier, inc=1, device_id=peer, device_id_type=MESH)
            pl.semaphore_wait(barrier, len(peers))
            for i, jb in enumerate(jobs):
                jb.start(jins[i], jouts[i], *refs[p + 3 * i:p + 3 * i + 3])

        body(*pre, *ins, *outs, *scr)

        @pl.when(last)
        def _():
            for i, jb in enumerate(jobs):
                jb.finish(jins[i], jouts[i], *refs[p + 3 * i:p + 3 * i + 3])

    kinds = frozenset().union(*[jb.peers for jb in jobs])
    params = pltpu.CompilerParams(dimension_semantics=("arbitrary",) * len(grid), vmem_limit_bytes=VMEM_LIMIT_V7X,
                                  collective_id=COLLECTIVE_IDS[kinds] if jobs else None)
    spec = pltpu.PrefetchScalarGridSpec(num_scalar_prefetch=nprefetch, grid=grid, in_specs=in_specs,
                                        out_specs=out_specs, scratch_shapes=scratch)
    call = pl.pallas_call(with_jobs if jobs else body, name=name, grid_spec=spec, out_shape=out_shape, compiler_params=params,
                          input_output_aliases=aliases or {}, **extra)

    def run(*args):
        res = call(*args, *[a for jb in jobs for a in jb.ins])
        own = res[0] if single else list(res[:n_out])
        if not jobs:
            return own
        jres, p = [], n_out
        for jb in jobs:
            jres.append(list(res[p:p + len(jb.out_shapes)]))
            p += len(jb.out_shapes)
        return own, jres

    return run


def _comm_call(body, *, name, in_specs, out_specs, out_shape, scratch_shapes):
    extra = {}
    return pl.pallas_call(body, name=name, in_specs=in_specs, out_specs=out_specs, out_shape=out_shape,
                          scratch_shapes=scratch_shapes, **extra)


def _sds(shape, dtype):
    return jax.ShapeDtypeStruct(shape, dtype)


def _tile(n, pref):
    t = min(n, pref)
    assert n % t == 0, (n, pref)
    return t


def _sigmoid(x):
    return jax.nn.sigmoid(x)


def _rms_fwd(x, g):
    r = lax.rsqrt(jnp.mean(x * x, axis=-1, keepdims=True) + NORM_EPS)
    return x * r * g


def _rms_bwd(x, g, dy):
    r = lax.rsqrt(jnp.mean(x * x, axis=-1, keepdims=True) + NORM_EPS)
    xh = x * r
    dg = jnp.sum(dy * xh, axis=0, keepdims=True)
    dxh = dy * g
    dx = r * (dxh - xh * jnp.mean(dxh * xh, axis=-1, keepdims=True))
    return dx, dg


def _ln_stats(x):
    mu = jnp.mean(x, axis=-1, keepdims=True)
    xc = x - mu
    rstd = lax.rsqrt(jnp.mean(xc * xc, axis=-1, keepdims=True) + LN_EPS)
    return xc * rstd, rstd


def _ln_bwd(xh, rstd, dxh):
    return rstd * (dxh - jnp.mean(dxh, axis=-1, keepdims=True) - xh * jnp.mean(dxh * xh, axis=-1, keepdims=True))


def _dsilu(x, s):
    return s * (1.0 + x * (1.0 - s))


def _rot(x, cosf, sinf):
    return x * cosf + pltpu.roll(x, x.shape[-1] // 2, 1) * sinf


def _unrot(d, cosf, sinf):
    return d * cosf - pltpu.roll(d, d.shape[-1] // 2, 1) * sinf


def _dot_nn(a, b):
    return jnp.dot(a, b, preferred_element_type=F32)


def _dot_nt(a, b):
    return lax.dot_general(a, b, (((1,), (1,)), ((), ())), preferred_element_type=F32)


def _dot_tn(a, b):
    return lax.dot_general(a, b, (((0,), (0,)), ((), ())), preferred_element_type=F32)


def _ffn_tf(cb):
    return _tile(cb, 1024)


def _ffn_fwd(x, g_pre, g_post, wgu, wd, name, jobs=()):
    T, D = x.shape
    F = wd.shape[0]
    cb = wgu.shape[2]
    tm, tf = _tile(T, 512), _ffn_tf(cb)
    nj, nb = F // tf, cb // tf

    def body(x_ref, gpre_ref, gpost_ref, wg_ref, wu_ref, wd_ref, xo_ref, ht_ref, gate_ref, up_ref, f_ref, acc_ref, h_ref):
        j = pl.program_id(1)

        @pl.when(j == 0)
        def _():
            h = _rms_fwd(x_ref[...], gpre_ref[...])
            h_ref[...] = h.astype(BF16)
            ht_ref[...] = h.T.astype(BF16)
            acc_ref[...] = jnp.zeros_like(acc_ref)

        h = h_ref[...]
        gate = _dot_nn(h, wg_ref[...])
        up = _dot_nn(h, wu_ref[...])
        gate_ref[...] = gate.astype(BF16)
        up_ref[...] = up.astype(BF16)
        act = (gate * _sigmoid(gate) * up).astype(BF16)
        acc_ref[...] += _dot_nn(act, wd_ref[...])

        @pl.when(j == nj - 1)
        def _():
            f = acc_ref[...]
            f_ref[...] = f
            xo_ref[...] = x_ref[...] + 0.5 * _rms_fwd(f, gpost_ref[...])

    row = pl.BlockSpec((tm, D), lambda i, j: (i, 0))
    vec = pl.BlockSpec((1, D), lambda i, j: (0, 0))
    col = pl.BlockSpec((tm, tf), lambda i, j: (i, j))
    return _pallas(
        body, name=name, grid=(T // tm, nj),
        in_specs=[row, vec, vec,
                  pl.BlockSpec((None, D, tf), lambda i, j: (j // nb, 0, j % nb)),
                  pl.BlockSpec((None, D, tf), lambda i, j: ((nj + j) // nb, 0, j % nb)),
                  pl.BlockSpec((tf, D), lambda i, j: (j, 0))],
        out_specs=[row, pl.BlockSpec((D, tm), lambda i, j: (0, i)), col, col, row],
        out_shape=[_sds((T, D), F32), _sds((D, T), BF16), _sds((T, F), BF16), _sds((T, F), BF16), _sds((T, D), F32)],
        scratch_shapes=[pltpu.VMEM((tm, D), F32), pltpu.VMEM((tm, D), BF16)], jobs=jobs,
    )(x, g_pre, g_post, wgu, wgu, wd)


def _ffn_bwd(dxo, x, f, gate, up, g_pre, g_post, wgu, wd, name, jobs=()):
    T, D = x.shape
    F = wd.shape[0]
    cb = wgu.shape[2]
    tm, tf = _tile(T, 512), _ffn_tf(cb)
    nj, nb = F // tf, cb // tf

    def body(dxo_ref, x_ref, f_ref, gate_ref, up_ref, gpre_ref, gpost_ref, wg_ref, wu_ref, wd_ref,
             dx_ref, df_ref, dgate_ref, dup_ref, act_ref, dgpre_ref, dgpost_ref, acc_ref):
        i, j = pl.program_id(0), pl.program_id(1)

        @pl.when((i == 0) & (j == 0))
        def _():
            dgpre_ref[...] = jnp.zeros_like(dgpre_ref)
            dgpost_ref[...] = jnp.zeros_like(dgpost_ref)

        @pl.when(j == 0)
        def _():
            df, dgp = _rms_bwd(f_ref[...], gpost_ref[...], 0.5 * dxo_ref[...])
            df_ref[...] = df.astype(BF16)
            dgpost_ref[...] += dgp
            acc_ref[...] = jnp.zeros_like(acc_ref)

        dact = _dot_nt(df_ref[...], wd_ref[...])
        g = gate_ref[...].astype(F32)
        u = up_ref[...].astype(F32)
        s = _sigmoid(g)
        silu = g * s
        dgate = (dact * u * _dsilu(g, s)).astype(BF16)
        dup = (dact * silu).astype(BF16)
        act_ref[...] = (silu * u).astype(BF16)
        dgate_ref[...] = dgate
        dup_ref[...] = dup
        acc_ref[...] += _dot_nt(dgate, wg_ref[...]) + _dot_nt(dup, wu_ref[...])

        @pl.when(j == nj - 1)
        def _():
            dxin, dgp = _rms_bwd(x_ref[...], gpre_ref[...], acc_ref[...])
            dx_ref[...] = dxo_ref[...] + dxin
            dgpre_ref[...] += dgp

    row = pl.BlockSpec((tm, D), lambda i, j: (i, 0))
    vec = pl.BlockSpec((1, D), lambda i, j: (0, 0))
    col = pl.BlockSpec((tm, tf), lambda i, j: (i, j))
    return _pallas(
        body, name=name, grid=(T // tm, nj),
        in_specs=[row, row, row, col, col, vec, vec,
                  pl.BlockSpec((None, D, tf), lambda i, j: (j // nb, 0, j % nb)),
                  pl.BlockSpec((None, D, tf), lambda i, j: ((nj + j) // nb, 0, j % nb)),
                  pl.BlockSpec((tf, D), lambda i, j: (j, 0))],
        out_specs=[row, row, col, col, col, vec, vec],
        out_shape=[_sds((T, D), F32), _sds((T, D), BF16), _sds((T, F), BF16), _sds((T, F), BF16), _sds((T, F), BF16),
                   _sds((1, D), F32), _sds((1, D), F32)],
        scratch_shapes=[pltpu.VMEM((tm, D), F32)], jobs=jobs,
    )(dxo, x, f, gate, up, g_pre, g_post, wgu, wgu, wd)


def _mm_tn(a, b, name, slab_cols=None, n_slabs=None, first_slab=0, into=None, tn_pref=1024, tt_pref=2048,
           a_is_transposed=False, jobs=()):
    T, N = b.shape
    K = a.shape[0] if a_is_transposed else a.shape[1]
    tt = _tile(T, tt_pref)
    tko = _tile(K, 1024)
    tn = _tile(slab_cols or N, tn_pref)
    nt = T // tt

    def body(a_ref, b_ref, *rest):
        o_ref, acc_ref = rest[-2:]
        t = pl.program_id(2)

        @pl.when(t == 0)
        def _():
            acc_ref[...] = jnp.zeros_like(acc_ref)

        acc_ref[...] += (_dot_nn if a_is_transposed else _dot_tn)(a_ref[...], b_ref[...])

        @pl.when(t == nt - 1)
        def _():
            o_ref[...] = acc_ref[...].astype(o_ref.dtype)

    if slab_cols is None:
        shape = (K, N)
        ospec = pl.BlockSpec((tko, tn), lambda k, jn, t: (k, jn))
    else:
        nb = slab_cols // tn
        shape = (n_slabs or N // slab_cols, K, slab_cols)
        ospec = pl.BlockSpec((None, tko, tn), lambda k, jn, t: (first_slab + jn // nb, k, jn % nb))
    aspec = pl.BlockSpec((tko, tt), lambda k, jn, t: (k, t)) if a_is_transposed else pl.BlockSpec((tt, tko), lambda k, jn, t: (t, k))
    in_specs, args = [aspec, pl.BlockSpec((tt, tn), lambda k, jn, t: (t, jn))], [a, b]
    if into is not None:
        in_specs.append(ANY)
        args.append(into)
    return _pallas(body, name=name, grid=(K // tko, N // tn, nt), in_specs=in_specs, out_specs=ospec, out_shape=_sds(shape, BF16),
                   scratch_shapes=[pltpu.VMEM((tko, tn), F32)], aliases={2: 0} if into is not None else None, jobs=jobs)(*args)


def _norm_proj(x, g, w, name, jobs=()):
    T, D = x.shape
    nd, cb = w.shape[0], w.shape[2]
    tm = _tile(T, 1024)

    def body(x_ref, g_ref, w_ref, p_ref, ht_ref, h_ref):
        @pl.when(pl.program_id(1) == 0)
        def _():
            h = _rms_fwd(x_ref[...], g_ref[...])
            h_ref[...] = h.astype(BF16)
            ht_ref[...] = h.T.astype(BF16)

        p_ref[...] = _dot_nn(h_ref[...], w_ref[...])

    row = pl.BlockSpec((tm, D), lambda i, j: (i, 0))
    return _pallas(
        body, name=name, grid=(T // tm, nd),
        in_specs=[row, pl.BlockSpec((1, D), lambda i, j: (0, 0)),
                  pl.BlockSpec((None, D, cb), lambda i, j: (j, 0, 0))],
        out_specs=[pl.BlockSpec((tm, cb), lambda i, j: (i, j)), pl.BlockSpec((D, tm), lambda i, j: (0, i))],
        out_shape=[_sds((T, nd * cb), F32), _sds((D, T), BF16)],
        scratch_shapes=[pltpu.VMEM((tm, D), BF16)], jobs=jobs,
    )(x, g, w)


def _proj_bwd(dproj, w, x, g, dxo, name, jobs=()):
    T, D = x.shape
    nd, cb = w.shape[0], w.shape[2]
    tm = _tile(T, 1024)

    def body(dp_ref, w_ref, x_ref, g_ref, dxo_ref, dx_ref, dg_ref, acc_ref):
        i, j = pl.program_id(0), pl.program_id(1)

        @pl.when((i == 0) & (j == 0))
        def _():
            dg_ref[...] = jnp.zeros_like(dg_ref)

        @pl.when(j == 0)
        def _():
            acc_ref[...] = jnp.zeros_like(acc_ref)

        acc_ref[...] += _dot_nt(dp_ref[...], w_ref[...])

        @pl.when(j == nd - 1)
        def _():
            dxin, dgp = _rms_bwd(x_ref[...], g_ref[...], acc_ref[...])
            dx_ref[...] = dxo_ref[...] + dxin
            dg_ref[...] += dgp

    row = pl.BlockSpec((tm, D), lambda i, j: (i, 0))
    vec = pl.BlockSpec((1, D), lambda i, j: (0, 0))
    return _pallas(
        body, name=name, grid=(T // tm, nd),
        in_specs=[pl.BlockSpec((tm, cb), lambda i, j: (i, j)),
                  pl.BlockSpec((None, D, cb), lambda i, j: (j, 0, 0)), row, vec, row],
        out_specs=[row, vec],
        out_shape=[_sds((T, D), F32), _sds((1, D), F32)],
        scratch_shapes=[pltpu.VMEM((tm, D), F32)], jobs=jobs,
    )(dproj, w, x, g, dxo)


def _rope_tables(pos, invf, sgn, name):
    T = pos.shape[0]
    dk = invf.shape[1]
    tm = _tile(T, 1024)

    def body(p_ref, f_ref, s_ref, c_out, s_out):
        ang = p_ref[...].astype(F32) * f_ref[...]
        c_out[...] = jnp.cos(ang)
        s_out[...] = jnp.sin(ang) * s_ref[...]

    vec = pl.BlockSpec((1, dk), lambda i: (0, 0))
    out = pl.BlockSpec((tm, dk), lambda i: (i, 0))
    return _pallas(body, name=name, grid=(T // tm,), in_specs=[pl.BlockSpec((tm, 1), lambda i: (i, 0)), vec, vec],
                   out_specs=[out, out], out_shape=[_sds((T, dk), F32), _sds((T, dk), F32)])(pos, invf, sgn)


def _decay(lg, r0, tq, n):
    r = r0 + lax.broadcasted_iota(jnp.int32, (tq, n), 0)
    c = lax.broadcasted_iota(jnp.int32, (tq, n), 1)
    rc, cc = r // CHUNK, c // CHUNK
    d = (r - c).astype(F32)
    e = jnp.where(rc == cc, jnp.abs(d), d)
    return jnp.where(cc > rc, 0.0, jnp.exp(lg * e))


def _ret_fwd(proj, cosf, sinf, lgs, B, S, D, name, jobs=()):
    T = B * S
    dk, dv = D // 8, D // 4
    tq = _tile(S, 256)
    scale = dk ** -0.5

    def body(lg_ref, q_ref, k_ref, v_ref, c_ref, s_ref, o_ref, kr_ref, vb_ref):
        lg = lg_ref[pl.program_id(1)]
        kr_ref[...] = (_rot(k_ref[...], c_ref[...], s_ref[...]) * scale).astype(BF16)
        vb_ref[...] = v_ref[...].astype(BF16)
        for qi in range(S // tq):
            rows, n = slice(qi * tq, (qi + 1) * tq), (qi + 1) * tq
            q = _rot(q_ref[rows, :], c_ref[rows, :], s_ref[rows, :]).astype(BF16)
            p = (_dot_nt(q, kr_ref[:n, :]) * _decay(lg, qi * tq, tq, n)).astype(BF16)
            o_ref[rows, :] = _dot_nn(p, vb_ref[:n, :])

    return _pallas(
        body, name=name, grid=(B, HEADS),
        in_specs=[pl.BlockSpec(memory_space=pltpu.SMEM),
                  pl.BlockSpec((S, dk), lambda b, h: (b, h)),
                  pl.BlockSpec((S, dk), lambda b, h: (b, HEADS + h)),
                  pl.BlockSpec((S, dv), lambda b, h: (b, HEADS + h)),
                  pl.BlockSpec((S, dk), lambda b, h: (b, 0)),
                  pl.BlockSpec((S, dk), lambda b, h: (b, 0))],
        out_specs=pl.BlockSpec((S, dv), lambda b, h: (b, h)),
        out_shape=_sds((T, D), F32),
        scratch_shapes=[pltpu.VMEM((S, dk), BF16), pltpu.VMEM((S, dv), BF16)], jobs=jobs,
    )(lgs, proj, proj, proj, cosf, sinf)


def _ret_bwd(do, proj, cosf, sinf, lgs, dproj, B, S, D, name, jobs=()):
    dk, dv = D // 8, D // 4
    tq = _tile(S, 256)
    scale = dk ** -0.5

    def body(lg_ref, do_ref, q_ref, k_ref, v_ref, c_ref, s_ref, dproj_in, dproj_ref, kr_ref, vb_ref, dka_ref, dva_ref,
             dq_ref, dk_ref, dv_ref, copy_sems):
        b, h = pl.program_id(0), pl.program_id(1)
        lg = lg_ref[h]
        kr_ref[...] = (_rot(k_ref[...], c_ref[...], s_ref[...]) * scale).astype(BF16)
        vb_ref[...] = v_ref[...].astype(BF16)
        dka_ref[...] = jnp.zeros_like(dka_ref)
        dva_ref[...] = jnp.zeros_like(dva_ref)

        for qi in range(S // tq):
            rows, n = slice(qi * tq, (qi + 1) * tq), (qi + 1) * tq
            cq, sq = c_ref[rows, :], s_ref[rows, :]
            q = _rot(q_ref[rows, :], cq, sq).astype(BF16)
            dout = do_ref[rows, :].astype(BF16)
            w = _decay(lg, qi * tq, tq, n)
            p = (_dot_nt(q, kr_ref[:n, :]) * w).astype(BF16)
            ds = (_dot_nt(dout, vb_ref[:n, :]) * w).astype(BF16)
            dq_ref[rows, :] = _unrot(_dot_nn(ds, kr_ref[:n, :]), cq, sq).astype(BF16)
            dka_ref[:n, :] += _dot_tn(ds, q)
            dva_ref[:n, :] += _dot_tn(p, dout)

        dk_ref[...] = _unrot(dka_ref[...] * scale, c_ref[...], s_ref[...]).astype(BF16)
        dv_ref[...] = dva_ref[...].astype(BF16)
        rows = pl.ds(pl.multiple_of(b * S, S), S)
        copies = [pltpu.make_async_copy(dq_ref, dproj_ref.at[rows, pl.ds(pl.multiple_of(h * dk, dk), dk)], copy_sems.at[0]),
                  pltpu.make_async_copy(dk_ref, dproj_ref.at[rows, pl.ds(pl.multiple_of((HEADS + h) * dk, dk), dk)], copy_sems.at[1]),
                  pltpu.make_async_copy(dv_ref, dproj_ref.at[rows, pl.ds(pl.multiple_of((HEADS + h) * dv, dv), dv)], copy_sems.at[2])]
        for cp in copies:
            cp.start()
        for cp in copies:
            cp.wait()

    return _pallas(
        body, name=name, grid=(B, HEADS),
        in_specs=[pl.BlockSpec(memory_space=pltpu.SMEM),
                  pl.BlockSpec((S, dv), lambda b, h: (b, h)),
                  pl.BlockSpec((S, dk), lambda b, h: (b, h)),
                  pl.BlockSpec((S, dk), lambda b, h: (b, HEADS + h)),
                  pl.BlockSpec((S, dv), lambda b, h: (b, HEADS + h)),
                  pl.BlockSpec((S, dk), lambda b, h: (b, 0)),
                  pl.BlockSpec((S, dk), lambda b, h: (b, 0)), ANY],
        out_specs=ANY, out_shape=_sds(dproj.shape, BF16),
        scratch_shapes=[pltpu.VMEM((S, dk), BF16), pltpu.VMEM((S, dv), BF16), pltpu.VMEM((S, dk), F32), pltpu.VMEM((S, dv), F32),
                        pltpu.VMEM((S, dk), BF16), pltpu.VMEM((S, dk), BF16), pltpu.VMEM((S, dv), BF16), pltpu.SemaphoreType.DMA((3,))],
        aliases={7: 0}, jobs=jobs,
    )(lgs, do, proj, proj, proj, cosf, sinf, dproj)


COL_GRET, COL_SCB, COL_SCC, COL_SCX, COL_GLUA, COL_GLUB, COL_GL = 2, 3, 4, 5, 6, 7, 8


SUBLANES, LANES = 8, 128
CONV_BLOCK = 256
SC_FWD = tuple(HALO - (SC_KERNEL - 1) + j for j in range(SC_KERNEL))
CF_FWD = tuple(HALO - (CF_KERNEL - 1) + j for j in range(CF_KERNEL))
SC_BWD = tuple(SC_KERNEL - 1 - j for j in range(SC_KERNEL))
CF_BWD = tuple(CF_KERNEL - 1 - j for j in range(CF_KERNEL))


def _build_shifted(src_ref, e_ref, shifts):
    n = e_ref.shape[1]
    for b in sorted({s % SUBLANES for s in shifts} - {0}):
        e_ref[b - 1] = src_ref[pl.ds(b, n), :]


def _shifted_rows(src_ref, e_ref, s, r0, rows, lanes):
    a, b = divmod(s, SUBLANES)
    at = pl.ds(r0 + SUBLANES * a, rows)
    return src_ref[at, lanes] if b == 0 else e_ref[b - 1, at, lanes]


def _tree_sum(terms):
    while len(terms) > 1:
        terms = [a + b for a, b in zip(terms[::2], terms[1::2])] + ([terms[-1]] if len(terms) % 2 else [])
    return terms[0]


def _conv_taps(src_ref, e_ref, w_ref, shifts, tm, D, emit, u_ref=None, dw_ref=None):
    rows = min(tm, CONV_BLOCK if u_ref is None else CONV_BLOCK // 2)
    for lt in range(D // LANES):
        lanes = pl.ds(lt * LANES, LANES)
        for r0 in range(0, tm, rows):
            u = None if u_ref is None else u_ref[pl.ds(r0, rows), lanes]
            acc = None
            for j, s in enumerate(shifts):
                v = _shifted_rows(src_ref, e_ref, s, r0, rows, lanes)
                term = v * w_ref[j:j + 1, lanes]
                acc = term if acc is None else acc + term
                if u is not None:
                    prod = u * v
                    part = _tree_sum([prod[g * SUBLANES:(g + 1) * SUBLANES, :] for g in range(rows // SUBLANES)])
                    dw_ref[j:j + 1, lanes] += jnp.sum(part, axis=0, keepdims=True)
            emit(lt, pl.ds(r0, rows), acc)


def _mixer_mid_fwd(proj, o, sc_w, cf_w, cf_b, ln_g, ln_b, S, name, jobs=()):
    T, D = o.shape
    dv = D // HEADS
    tm = _tile(S, 256)
    per_seq = S // tm
    hb = tm // HALO

    def body(gret_ref, scb_ref, scc_ref, scx_ref, ga_ref, gb_ref, scc_h, scx_h, ga_h, gb_h, o_ref,
             scw_ref, cfw_ref, cfb_ref, lng_ref, lnb_ref, ya_ref, yb_ref, yc_ref, cv_ref, u1_ref, ext_ref, e_ref):
        first = (pl.program_id(0) % per_seq) == 0
        keep = jnp.where(first, 0.0, 1.0)
        gr = gret_ref[...]
        sg = gr * _sigmoid(gr)
        for hh in range(HEADS):
            cols = slice(hh * dv, (hh + 1) * dv)
            hn, _ = _ln_stats(o_ref[:, cols])
            ya_ref[:, cols] = (sg[:, cols] * hn).astype(BF16)
        ext_ref[:HALO, :] = scc_h[...] * scx_h[...] * keep
        ext_ref[HALO:, :] = scc_ref[...] * scx_ref[...]
        _build_shifted(ext_ref, e_ref, SC_FWD)

        def emit_cv(lt, rows, cv):
            lanes = pl.ds(lt * LANES, LANES)
            cv_ref[rows, lanes] = cv
            yb_ref[rows, lanes] = (scb_ref[rows, lanes] * cv).astype(BF16)

        _conv_taps(ext_ref, e_ref, scw_ref, SC_FWD, tm, D, emit_cv)
        ext_ref[:HALO, :] = ga_h[...] * _sigmoid(gb_h[...]) * keep
        ext_ref[HALO:, :] = ga_ref[...] * _sigmoid(gb_ref[...])
        _build_shifted(ext_ref, e_ref, CF_FWD)

        def emit_u1(lt, rows, acc):
            lanes = pl.ds(lt * LANES, LANES)
            u1_ref[rows, lanes] = acc + cfb_ref[:, lanes]

        _conv_taps(ext_ref, e_ref, cfw_ref, CF_FWD, tm, D, emit_u1)
        xh, _ = _ln_stats(u1_ref[...])
        u2 = xh * lng_ref[...] + lnb_ref[...]
        yc_ref[...] = (u2 * _sigmoid(u2)).astype(BF16)

    def colblk(c):
        return pl.BlockSpec((tm, D), lambda i: (i, c))

    def halo(c):
        return pl.BlockSpec((HALO, D), lambda i: (jnp.maximum(i * hb - 1, 0), c))

    row = pl.BlockSpec((tm, D), lambda i: (i, 0))
    vec = pl.BlockSpec((1, D), lambda i: (0, 0))
    return _pallas(
        body, name=name, grid=(T // tm,),
        in_specs=[colblk(COL_GRET), colblk(COL_SCB), colblk(COL_SCC), colblk(COL_SCX), colblk(COL_GLUA), colblk(COL_GLUB),
                  halo(COL_SCC), halo(COL_SCX), halo(COL_GLUA), halo(COL_GLUB), row,
                  pl.BlockSpec((SC_KERNEL, D), lambda i: (0, 0)), pl.BlockSpec((CF_KERNEL, D), lambda i: (0, 0)), vec, vec, vec],
        out_specs=[row, row, row, row, row],
        out_shape=[_sds((T, D), BF16)] * 3 + [_sds((T, D), F32)] * 2,
        scratch_shapes=[pltpu.VMEM((HALO + tm, D), F32), pltpu.VMEM((SUBLANES - 1, HALO + tm - SUBLANES, D), F32)], jobs=jobs,
    )(proj, proj, proj, proj, proj, proj, proj, proj, proj, proj, o, sc_w, cf_w, cf_b, ln_g, ln_b)


def _mixer_merge_fwd(x, proj, ya_in, yb_in, yc_in, g_post, w_ret, w_sc, w_cf, w_o, name, jobs=()):
    T, D = x.shape
    tm = _tile(T, 512)

    def body(x_ref, g0_ref, g1_ref, g2_ref, ya_in_ref, yb_in_ref, yc_in_ref, gp_ref, wr_ref, ws_ref, wc_ref, wo_ref,
             xo_ref, ya_ref, yb_ref, yc_ref, mg_ref, m_ref):
        ya = _dot_nn(ya_in_ref[...], wr_ref[...])
        yb = _dot_nn(yb_in_ref[...], ws_ref[...])
        yc = _dot_nn(yc_in_ref[...], wc_ref[...])
        ya_ref[...] = ya.astype(BF16)
        yb_ref[...] = yb.astype(BF16)
        yc_ref[...] = yc.astype(BF16)
        merged = (_sigmoid(g0_ref[...]) * ya + _sigmoid(g1_ref[...]) * yb + _sigmoid(g2_ref[...]) * yc).astype(BF16)
        mg_ref[...] = merged
        m = _dot_nn(merged, wo_ref[...])
        m_ref[...] = m
        xo_ref[...] = x_ref[...] + _rms_fwd(m, gp_ref[...])

    row = pl.BlockSpec((tm, D), lambda i: (i, 0))
    wsp = pl.BlockSpec((D, D), lambda i: (0, 0), pipeline_mode=pl.Buffered(1))

    def colblk(c):
        return pl.BlockSpec((tm, D), lambda i: (i, c))

    return _pallas(
        body, name=name, grid=(T // tm,),
        in_specs=[row, colblk(COL_GL), colblk(COL_GL + 1), colblk(COL_GL + 2), row, row, row,
                  pl.BlockSpec((1, D), lambda i: (0, 0)), wsp, wsp, wsp, wsp],
        out_specs=[row] * 6,
        out_shape=[_sds((T, D), F32)] + [_sds((T, D), BF16)] * 4 + [_sds((T, D), F32)], jobs=jobs,
    )(x, proj, proj, proj, ya_in, yb_in, yc_in, g_post, w_ret, w_sc, w_cf, w_o)


def _mixer_bwd_a(dxo, m, g_post, ya, yb, yc, proj, o, cv, u1, ln_g, ln_b, w_ret, w_sc, w_cf, w_o, name, jobs=()):
    T, D = m.shape
    dv = D // HEADS
    tm = _tile(T, 256)

    def body(dxo_ref, m_ref, gp_ref, ya_ref, yb_ref, yc_ref, g0_ref, g1_ref, g2_ref, gret_ref, scb_ref, o_ref, cv_ref, u1_ref,
             lng_ref, lnb_ref, wr_ref, ws_ref, wc_ref, wo_ref,
             dm_ref, dya_ref, dyb_ref, dyc_ref, dproj_ref, do_ref, dcv_ref, du1_ref, dgp_ref, dlng_ref, dlnb_ref, dcfb_ref,
             d2_ref, dgl_ref, copy_sems):
        @pl.when(pl.program_id(0) == 0)
        def _():
            for r in (dgp_ref, dlng_ref, dlnb_ref, dcfb_ref):
                r[...] = jnp.zeros_like(r)

        dm, dgp = _rms_bwd(m_ref[...], gp_ref[...], dxo_ref[...])
        dgp_ref[...] += dgp
        dmb = dm.astype(BF16)
        dm_ref[...] = dmb
        dmerged = _dot_nt(dmb, wo_ref[...])
        dys = []
        for k, (g_ref, y_ref, dy_ref) in enumerate(((g0_ref, ya_ref, dya_ref), (g1_ref, yb_ref, dyb_ref), (g2_ref, yc_ref, dyc_ref))):
            sg = _sigmoid(g_ref[...])
            dgl_ref[:, k * D:(k + 1) * D] = (dmerged * y_ref[...].astype(F32) * sg * (1.0 - sg)).astype(BF16)
            dy = (dmerged * sg).astype(BF16)
            dy_ref[...] = dy
            dys.append(dy)
        dya_in = _dot_nt(dys[0], wr_ref[...])
        gr = gret_ref[...]
        sr = _sigmoid(gr)
        for hh in range(HEADS):
            cols = slice(hh * dv, (hh + 1) * dv)
            hn, rstd = _ln_stats(o_ref[:, cols])
            d2_ref[:, cols] = (dya_in[:, cols] * hn * _dsilu(gr[:, cols], sr[:, cols])).astype(BF16)
            do_ref[:, cols] = _ln_bwd(hn, rstd, dya_in[:, cols] * gr[:, cols] * sr[:, cols])
        dyb_in = _dot_nt(dys[1], ws_ref[...])
        d2_ref[:, D:] = (dyb_in * cv_ref[...]).astype(BF16)
        dcv_ref[...] = dyb_in * scb_ref[...]
        dyc_in = _dot_nt(dys[2], wc_ref[...])
        xh, rstd = _ln_stats(u1_ref[...])
        u2 = xh * lng_ref[...] + lnb_ref[...]
        du2 = dyc_in * _dsilu(u2, _sigmoid(u2))
        dlng_ref[...] += jnp.sum(du2 * xh, axis=0, keepdims=True)
        dlnb_ref[...] += jnp.sum(du2, axis=0, keepdims=True)
        du1 = _ln_bwd(xh, rstd, du2 * lng_ref[...])
        du1_ref[...] = du1
        dcfb_ref[...] += jnp.sum(du1, axis=0, keepdims=True)
        rows = pl.ds(pl.multiple_of(pl.program_id(0) * tm, tm), tm)
        copies = [pltpu.make_async_copy(d2_ref, dproj_ref.at[rows, pl.ds(COL_GRET * D, 2 * D)], copy_sems.at[0]),
                  pltpu.make_async_copy(dgl_ref, dproj_ref.at[rows, pl.ds(COL_GL * D, 3 * D)], copy_sems.at[1])]
        for cp in copies:
            cp.start()
        for cp in copies:
            cp.wait()

    row = pl.BlockSpec((tm, D), lambda i: (i, 0))
    vec = pl.BlockSpec((1, D), lambda i: (0, 0))
    wsp = pl.BlockSpec((D, D), lambda i: (0, 0), pipeline_mode=pl.Buffered(1))

    def colblk(c):
        return pl.BlockSpec((tm, D), lambda i: (i, c))

    return _pallas(
        body, name=name, grid=(T // tm,),
        in_specs=[row, row, vec, row, row, row, colblk(COL_GL), colblk(COL_GL + 1), colblk(COL_GL + 2),
                  colblk(COL_GRET), colblk(COL_SCB), row, row, row, vec, vec, wsp, wsp, wsp, wsp],
        out_specs=[row, row, row, row, ANY, row, row, row, vec, vec, vec, vec],
        out_shape=[_sds((T, D), BF16)] * 4 + [_sds((T, proj.shape[1]), BF16)] + [_sds((T, D), F32)] * 3 + [_sds((1, D), F32)] * 4,
        scratch_shapes=[pltpu.VMEM((tm, 2 * D), BF16), pltpu.VMEM((tm, 3 * D), BF16), pltpu.SemaphoreType.DMA((2,))], jobs=jobs,
    )(dxo, m, g_post, ya, yb, yc, proj, proj, proj, proj, proj, o, cv, u1, ln_g, ln_b, w_ret, w_sc, w_cf, w_o)


def _mixer_bwd_b(dcv, du1, proj, sc_w, cf_w, dproj, S, name, jobs=()):
    T, D = dcv.shape
    tm = _tile(S, 256)
    per_seq = S // tm
    hb = tm // HALO
    last_hb = T // HALO - 1

    def body(dcv_ref, du1_ref, dcv_n, du1_n, scc_ref, scx_ref, ga_ref, gb_ref, scw_ref, cfw_ref, dproj_in,
             d4_ref, dscw_ref, dcfw_ref, u_ref, dext_ref, e_ref, sb_ref):
        i = pl.program_id(0)
        keep_next = jnp.where((i % per_seq) == per_seq - 1, 0.0, 1.0)

        @pl.when(i == 0)
        def _():
            dscw_ref[...] = jnp.zeros_like(dscw_ref)
            dcfw_ref[...] = jnp.zeros_like(dcfw_ref)

        dext_ref[:tm, :] = dcv_ref[...]
        dext_ref[tm:, :] = dcv_n[...] * keep_next
        _build_shifted(dext_ref, e_ref, SC_BWD)
        u_ref[...] = scc_ref[...] * scx_ref[...]

        def emit_dz(lt, rows, dz):
            lanes = pl.ds(lt * LANES, LANES)
            d4_ref[rows, pl.ds(lt * LANES, LANES)] = (dz * scx_ref[rows, lanes]).astype(BF16)
            d4_ref[rows, pl.ds(D + lt * LANES, LANES)] = (dz * scc_ref[rows, lanes]).astype(BF16)

        _conv_taps(dext_ref, e_ref, scw_ref, SC_BWD, tm, D, emit_dz, u_ref=u_ref, dw_ref=dscw_ref)
        dext_ref[:tm, :] = du1_ref[...]
        dext_ref[tm:, :] = du1_n[...] * keep_next
        _build_shifted(dext_ref, e_ref, CF_BWD)
        sb_ref[...] = _sigmoid(gb_ref[...])
        u_ref[...] = ga_ref[...] * sb_ref[...]

        def emit_du0(lt, rows, du0):
            lanes = pl.ds(lt * LANES, LANES)
            sb = sb_ref[rows, lanes]
            d4_ref[rows, pl.ds(2 * D + lt * LANES, LANES)] = (du0 * sb).astype(BF16)
            d4_ref[rows, pl.ds(3 * D + lt * LANES, LANES)] = (du0 * ga_ref[rows, lanes] * sb * (1.0 - sb)).astype(BF16)

        _conv_taps(dext_ref, e_ref, cfw_ref, CF_BWD, tm, D, emit_du0, u_ref=u_ref, dw_ref=dcfw_ref)

    row = pl.BlockSpec((tm, D), lambda i: (i, 0))
    nxt = pl.BlockSpec((HALO, D), lambda i: (jnp.minimum((i + 1) * hb, last_hb), 0))

    def colblk(c):
        return pl.BlockSpec((tm, D), lambda i: (i, c))

    return _pallas(
        body, name=name, grid=(T // tm,),
        in_specs=[row, row, nxt, nxt, colblk(COL_SCC), colblk(COL_SCX), colblk(COL_GLUA), colblk(COL_GLUB),
                  pl.BlockSpec((SC_KERNEL, D), lambda i: (0, 0)), pl.BlockSpec((CF_KERNEL, D), lambda i: (0, 0)), ANY],
        out_specs=[pl.BlockSpec((tm, 4 * D), lambda i: (i, COL_SCC // 4)), pl.BlockSpec((SC_KERNEL, D), lambda i: (0, 0)),
                   pl.BlockSpec((CF_KERNEL, D), lambda i: (0, 0))],
        out_shape=[_sds(dproj.shape, BF16), _sds((SC_KERNEL, D), F32), _sds((CF_KERNEL, D), F32)],
        scratch_shapes=[pltpu.VMEM((tm, D), F32), pltpu.VMEM((tm + HALO, D), F32),
                        pltpu.VMEM((SUBLANES - 1, HALO + tm - SUBLANES, D), F32), pltpu.VMEM((tm, D), F32)],
        aliases={10: 0}, jobs=jobs,
    )(dcv, du1, dcv, du1, proj, proj, proj, proj, sc_w, cf_w, dproj)


def _loss_grad(y, tgt, name):
    T, D = y.shape
    tm = _tile(T, 512)

    def body(y_ref, t_ref, dy_ref, loss_ref):
        @pl.when(pl.program_id(0) == 0)
        def _():
            loss_ref[...] = jnp.zeros_like(loss_ref)

        e = y_ref[...] - t_ref[...]
        dy_ref[...] = e * (1.0 / D)
        loss_ref[...] += 0.5 * jnp.sum(jnp.sum(e * e, axis=-1, keepdims=True) * (1.0 / D), axis=0, keepdims=True)

    row = pl.BlockSpec((tm, D), lambda i: (i, 0))
    return _pallas(body, name=name, grid=(T // tm,), in_specs=[row, row],
                   out_specs=[row, pl.BlockSpec((1, 1), lambda i: (0, 0))],
                   out_shape=[_sds((T, D), F32), _sds((1, 1), F32)])(y, tgt)


def _mesh_pos():
    return lax.axis_index("x"), lax.axis_index("y"), lax.axis_index("c")


def _other_chips(x, y):
    return [(1 - x, y), (x, 1 - y), (1 - x, 1 - y)]


def _allgather_job(shards):
    nt = len(shards)

    def parts(ins, outs, send, recv):
        x, y, c = _mesh_pos()

        def slab(t, px, py, pc):
            return outs[t].at[4 * px + 2 * py + pc]

        def copy(t, k, block, to, src=None):
            return pltpu.make_async_remote_copy(
                src_ref=slab(t, *block) if src is None else src, dst_ref=slab(t, *block),
                send_sem=send.at[7 * t + k], recv_sem=recv.at[7 * t + k], device_id=to, device_id_type=MESH)

        return (x, y, c), (x, y, 1 - c), _other_chips(x, y), c, slab, copy

    def start(ins, outs, send, recv, loc):
        me, sibling, chips, c, slab, copy = parts(ins, outs, send, recv)
        for t in range(nt):
            pltpu.make_async_copy(ins[t], slab(t, *me), loc.at[t]).start()
            copy(t, 0, me, sibling, src=ins[t]).start()
            for j, chip in enumerate(chips):
                copy(t, 1 + j, me, (*chip, c), src=ins[t]).start()

    def finish(ins, outs, send, recv, loc):
        me, sibling, chips, c, slab, copy = parts(ins, outs, send, recv)
        for j, chip in enumerate(chips):
            for t in range(nt):
                copy(t, 1 + j, (*chip, c), me).wait_recv()
                copy(t, 4 + j, (*chip, c), sibling).start()
        for t in range(nt):
            copy(t, 0, sibling, me).wait_recv()
            for j, chip in enumerate(chips):
                copy(t, 4 + j, (*chip, 1 - c), me).wait_recv()
        for t in range(nt):
            copy(t, 0, me, sibling, src=ins[t]).wait_send()
            for j, chip in enumerate(chips):
                copy(t, 1 + j, me, (*chip, c), src=ins[t]).wait_send()
                copy(t, 4 + j, (*chip, c), sibling).wait_send()
            pltpu.make_async_copy(ins[t], slab(t, *me), loc.at[t]).wait()

    return _Job(shards, [_sds((N_DEV,) + s.shape, s.dtype) for s in shards], 7 * nt, nt, start, finish, ("sibling", "chips"))


def _to_sibling_job(grads):
    nt = len(grads)

    def copies(ins, outs, send, recv):
        x, y, c = _mesh_pos()
        return [pltpu.make_async_remote_copy(src_ref=ins[t].at[:, 1 - c], dst_ref=outs[t], send_sem=send.at[t], recv_sem=recv.at[t],
                                             device_id=(x, y, 1 - c), device_id_type=MESH) for t in range(nt)]

    def start(ins, outs, send, recv, loc):
        for cp in copies(ins, outs, send, recv):
            cp.start()

    def finish(ins, outs, send, recv, loc):
        for cp in copies(ins, outs, send, recv):
            cp.wait()

    return _Job(grads, [_sds((N_CHIP,) + g.shape[2:], g.dtype) for g in grads], nt, 0, start, finish, ("sibling",))


def _to_chips_job(pieces):
    nt = len(pieces)
    pairs = [p for p, _, _ in pieces]

    def copies(ins, outs, send, recv, loc):
        x, y, c = _mesh_pos()

        def src(t, chip):
            return ins[t].at[chip, pl.ds(pieces[t][1], pieces[t][2])]

        remote = [pltpu.make_async_remote_copy(src_ref=src(t, 2 * px + py), dst_ref=outs[t].at[k], send_sem=send.at[3 * t + k],
                                               recv_sem=recv.at[3 * t + k], device_id=(px, py, c), device_id_type=MESH)
                  for t in range(nt) for k, (px, py) in enumerate(_other_chips(x, y))]
        local = [pltpu.make_async_copy(src(t, 2 * x + y), outs[t].at[3], loc.at[t]) for t in range(nt)]
        return remote + local

    def start(ins, outs, send, recv, loc):
        for cp in copies(ins, outs, send, recv, loc):
            cp.start()

    def finish(ins, outs, send, recv, loc):
        for cp in copies(ins, outs, send, recv, loc):
            cp.wait()

    return _Job(pairs, [_sds((N_CHIP, rows, p.shape[2]), p.dtype) for p, _, rows in pieces], 3 * nt, nt, start, finish, ("chips",))


def _run_jobs(jobs, name):
    def body(o_ref):
        o_ref[...] = jnp.zeros_like(o_ref)

    _, jres = _pallas(body, name=name, grid=(1,), in_specs=[], out_specs=pl.BlockSpec((8, 128), lambda i: (0, 0)),
                      out_shape=_sds((8, 128), F32), jobs=jobs)()
    return jres


def _allgather_small(v, name):
    R, C = v.shape

    def body(x_ref, out_ref, send_sems, recv_sems, local_sem):
        x, y, c = _mesh_pos()
        me, sibling = (x, y, c), (x, y, 1 - c)
        chips = _other_chips(x, y)

        def slab(px, py, pc):
            return out_ref.at[4 * px + 2 * py + pc]

        def copy(k, block, to, src=None):
            return pltpu.make_async_remote_copy(
                src_ref=slab(*block) if src is None else src, dst_ref=slab(*block),
                send_sem=send_sems.at[k], recv_sem=recv_sems.at[k], device_id=to, device_id_type=MESH)

        mine = pltpu.make_async_copy(x_ref, slab(*me), local_sem)
        mine.start()
        first = [copy(0, me, sibling, src=x_ref)]
        first += [copy(1 + j, me, (*chip, c), src=x_ref) for j, chip in enumerate(chips)]
        for cp in first:
            cp.start()
        passed = [copy(4 + j, (*chip, c), sibling) for j, chip in enumerate(chips)]
        for j, chip in enumerate(chips):
            copy(1 + j, (*chip, c), me).wait_recv()
            passed[j].start()
        copy(0, sibling, me).wait_recv()
        for j, chip in enumerate(chips):
            copy(4 + j, (*chip, 1 - c), me).wait_recv()
        for cp in first + passed:
            cp.wait_send()
        mine.wait()

    return _comm_call(
        body, name=name, in_specs=[pl.BlockSpec(memory_space=pltpu.VMEM)], out_specs=pl.BlockSpec(memory_space=pltpu.VMEM),
        out_shape=_sds((N_DEV, R, C), v.dtype),
        scratch_shapes=[pltpu.SemaphoreType.DMA((7,)), pltpu.SemaphoreType.DMA((7,)), pltpu.SemaphoreType.DMA],
    )(v)


def _pair_sum(core, grad, got, name):
    _, _, R, C = grad.shape
    tr = _tile(R, 1024)

    def body(core_ref, mine_ref, got_ref, out_ref):
        out_ref[...] = (mine_ref[...].astype(F32) + got_ref[...].astype(F32)).astype(BF16)

    blk = (None, tr, C)
    return _pallas(
        body, name=name, grid=(N_CHIP, R // tr), nprefetch=1,
        in_specs=[pl.BlockSpec((None, None, tr, C), lambda k, r, core: (k, core[0], r, 0)),
                  pl.BlockSpec(blk, lambda k, r, core: (k, r, 0))],
        out_specs=pl.BlockSpec(blk, lambda k, r, core: (k, r, 0)),
        out_shape=_sds((N_CHIP, R, C), BF16),
    )(core, grad, got)


def _adamw(w, g, m, v):
    m = ADAM_B1 * m + (1.0 - ADAM_B1) * g
    v = ADAM_B2 * v + (1.0 - ADAM_B2) * (g * g)
    m_hat = m / (1.0 - ADAM_B1 ** ADAM_STEP)
    v_hat = v / (1.0 - ADAM_B2 ** ADAM_STEP)
    delta = -ADAM_LR * (m_hat / (jnp.sqrt(v_hat) + ADAM_EPS) + ADAM_WD * w)
    return delta, m, v


def _final_adamw(sums, w, m, v, l, r0, into, name, jobs=()):
    L, R, C = w.shape
    rows = sums.shape[1]
    tr = _tile(rows, 512 if C <= 1024 else 256)
    assert r0 % tr == 0
    b0 = r0 // tr

    def body(s_ref, w_ref, m_ref, v_ref, *rest):
        g_out, d_out, m_out, v_out = rest[-4:]
        g = s_ref[0].astype(F32)
        for k in range(1, N_CHIP):
            g = g + s_ref[k].astype(F32)
        d, mn, vn = _adamw(w_ref[...], g, m_ref[...], v_ref[...])
        g_out[...] = g
        d_out[...] = d
        m_out[...] = mn
        v_out[...] = vn

    loc = pl.BlockSpec((None, tr, C), lambda r: (l, b0 + r, 0))
    in_specs = [pl.BlockSpec((N_CHIP, tr, C), lambda r: (0, r, 0)), loc, loc, loc]
    args = [sums, w, m, v]
    aliases = None
    if into is not None:
        in_specs += [ANY] * 4
        args += list(into)
        aliases = {4 + i: i for i in range(4)}
    return _pallas(body, name=name, grid=(rows // tr,), in_specs=in_specs, out_specs=[loc] * 4,
                   out_shape=[_sds((L, R, C), F32)] * 4, aliases=aliases, jobs=jobs)(*args)


def _small_adamw(dev, allg, w, m, v, width, name):
    R = w.shape[0]

    def body(dev_ref, a_ref, w_ref, m_ref, v_ref, g_out, d_out, m_out, v_out):
        g = a_ref[0]
        for k in range(1, N_DEV):
            g = g + a_ref[k]
        d, mn, vn = _adamw(w_ref[...], g, m_ref[...], v_ref[...])
        g_out[...] = g
        d_out[...] = d
        m_out[...] = mn
        v_out[...] = vn

    full = allg.shape[2] == width
    loc = pl.BlockSpec((R, width), lambda i, dev: (0, 0))
    return _pallas(
        body, name=name, grid=(1,), nprefetch=1,
        in_specs=[pl.BlockSpec((N_DEV, R, width), lambda i, dev: (0, 0, 0 if full else dev[0])), loc, loc, loc],
        out_specs=[loc] * 4, out_shape=[_sds((R, width), F32)] * 4,
    )(dev, allg, w, m, v)


BIG = ("ffn1_w_gu", "ffn1_w_down", "w_in", "w_ret_o", "w_sc_o", "w_cf_o", "w_o", "ffn2_w_gu", "ffn2_w_down")
FFN1 = ("ffn1_w_gu", "ffn1_w_down")
MIX_OUT = ("w_ret_o", "w_sc_o", "w_cf_o", "w_o")
ROW_BLOCKS = ("ffn1_w_down", "ffn2_w_down") + MIX_OUT


def kernel(x, positions, norm_g, ffn1_w_gu, ffn1_w_down, w_in, w_ret_o, sc_conv_w, w_sc_o, cf_dw_w, cf_dw_b, cf_ln_g, cf_ln_b, w_cf_o, w_o, ffn2_w_gu, ffn2_w_down, loss_target, m_norm_g, m_ffn1_w_gu, m_ffn1_w_down, m_w_in, m_w_ret_o, m_sc_conv_w, m_w_sc_o, m_cf_dw_w, m_cf_dw_b, m_cf_ln_g, m_cf_ln_b, m_w_cf_o, m_w_o, m_ffn2_w_gu, m_ffn2_w_down, v_norm_g, v_ffn1_w_gu, v_ffn1_w_down, v_w_in, v_w_ret_o, v_sc_conv_w, v_w_sc_o, v_cf_dw_w, v_cf_dw_b, v_cf_ln_g, v_cf_ln_b, v_w_cf_o, v_w_o, v_ffn2_w_gu, v_ffn2_w_down):
    B, S, D = x.shape
    T = B * S
    L = norm_g.shape[0]
    DL = norm_g.shape[2]
    dk = D // 8
    mx, my, mc = _mesh_pos()
    dev_idx = jnp.reshape(4 * mx + 2 * my + mc, (1,)).astype(jnp.int32)
    core_idx = jnp.reshape(mc, (1,)).astype(jnp.int32)

    w32 = dict(ffn1_w_gu=ffn1_w_gu, ffn1_w_down=ffn1_w_down, w_in=w_in, w_ret_o=w_ret_o, w_sc_o=w_sc_o, w_cf_o=w_cf_o,
               w_o=w_o, ffn2_w_gu=ffn2_w_gu, ffn2_w_down=ffn2_w_down)
    m32 = dict(ffn1_w_gu=m_ffn1_w_gu, ffn1_w_down=m_ffn1_w_down, w_in=m_w_in, w_ret_o=m_w_ret_o, w_sc_o=m_w_sc_o,
               w_cf_o=m_w_cf_o, w_o=m_w_o, ffn2_w_gu=m_ffn2_w_gu, ffn2_w_down=m_ffn2_w_down)
    v32 = dict(ffn1_w_gu=v_ffn1_w_gu, ffn1_w_down=v_ffn1_w_down, w_in=v_w_in, w_ret_o=v_w_ret_o, w_sc_o=v_w_sc_o,
               w_cf_o=v_w_cf_o, w_o=v_w_o, ffn2_w_gu=v_ffn2_w_gu, ffn2_w_down=v_ffn2_w_down)

    W = [dict() for _ in range(L)]

    def gather(l, names):
        return _allgather_job([w32[n][l].astype(BF16) for n in names])

    def place(l, names, slabs):
        for n, a in zip(names, slabs):
            W[l][n] = a.reshape(a.shape[0] * a.shape[1], a.shape[2]) if n in ROW_BLOCKS else a

    def carried(fn, *args, jobs, **kw):
        out = fn(*args, jobs=jobs, **kw)
        return out if jobs else (out, [])

    n_sh = 6 + SC_KERNEL + CF_KERNEL
    small_sh = jnp.concatenate([norm_g, sc_conv_w, cf_dw_w], axis=1).reshape(L * n_sh, DL)
    sh_all = _allgather_small(small_sh, "allgather_small_params")
    sh_full = jnp.transpose(sh_all, (1, 0, 2)).reshape(L, n_sh, D)
    norm_full = sh_full[:, :6]
    scw_full = sh_full[:, 6:6 + SC_KERNEL]
    cfw_full = sh_full[:, 6 + SC_KERNEL:]
    place(0, FFN1, _run_jobs([gather(0, FFN1)], "allgather_first")[0])

    half = dk // 2
    inv_freq = ROPE_BASE ** (-jnp.arange(half, dtype=F32) / half)
    invf = jnp.concatenate([inv_freq, inv_freq])[None, :]
    sgn = jnp.concatenate([-jnp.ones((half,), F32), jnp.ones((half,), F32)])[None, :]
    cosf, sinf = _rope_tables(positions.reshape(T, 1), invf, sgn, "rope_tables")
    lgs = jnp.log(1.0 - 2.0 ** (-5.0 - jnp.arange(HEADS, dtype=F32)))

    def vec(a):
        return a.reshape(1, D)

    xc = x.reshape(T, D)
    saved = []
    for l in range(L):
        g = norm_full[l]
        sv = {}
        sv["x0"] = xc
        (xc, sv["h1"], sv["gate1"], sv["up1"], sv["f1"]), jr = _ffn_fwd(
            xc, vec(g[0]), vec(g[1]), W[l]["ffn1_w_gu"], W[l]["ffn1_w_down"], f"ffn1_fwd_{l}", jobs=[gather(l, ("w_in",))])
        place(l, ("w_in",), jr[0])
        sv["x1"] = xc
        (proj, sv["h2"]), jr = _norm_proj(xc, vec(g[2]), W[l]["w_in"], f"proj_fwd_{l}", jobs=[gather(l, ("ffn2_w_gu",))])
        place(l, ("ffn2_w_gu",), jr[0])
        sv["proj"] = proj
        o, jr = _ret_fwd(proj, cosf, sinf, lgs, B, S, D, f"retention_fwd_{l}", jobs=[gather(l, MIX_OUT)])
        place(l, MIX_OUT, jr[0])
        sv["o"] = o
        nxt = l + 1 < L
        (ya_in, yb_in, yc_in, sv["cv"], sv["u1"]), jr = _mixer_mid_fwd(
            proj, o, scw_full[l], cfw_full[l], vec(cf_dw_b[l]), vec(cf_ln_g[l]), vec(cf_ln_b[l]), S,
            f"mixer_mid_fwd_{l}", jobs=[gather(l, ("ffn2_w_down",))])
        place(l, ("ffn2_w_down",), jr[0])
        sv["ya_in"], sv["yb_in"], sv["yc_in"] = ya_in, yb_in, yc_in
        (xc, sv["ya"], sv["yb"], sv["yc"], sv["merged"], sv["m"]), jr = carried(
            _mixer_merge_fwd, xc, proj, ya_in, yb_in, yc_in, vec(g[3]), W[l]["w_ret_o"], W[l]["w_sc_o"], W[l]["w_cf_o"], W[l]["w_o"],
            f"mixer_merge_fwd_{l}", jobs=[gather(l + 1, ("ffn1_w_down",))] if nxt else [])
        if nxt:
            place(l + 1, ("ffn1_w_down",), jr[0])
        sv["x2"] = xc
        (xc, sv["h3"], sv["gate2"], sv["up2"], sv["f2"]), jr = carried(
            _ffn_fwd, xc, vec(g[4]), vec(g[5]), W[l]["ffn2_w_gu"], W[l]["ffn2_w_down"], f"ffn2_fwd_{l}",
            jobs=[gather(l + 1, ("ffn1_w_gu",))] if nxt else [])
        if nxt:
            place(l + 1, ("ffn1_w_gu",), jr[0])
        saved.append(sv)

    dx, loss_part = _loss_grad(xc, loss_target.reshape(T, D), "loss")
    loss = lax.psum(loss_part[0, 0], ("x", "y", "c"))

    to_sibling, to_chips, reduced = [], [], {}

    def grad_ready(n, l, g):
        if g.ndim == 2:
            g = g.reshape(N_DEV, g.shape[0] // N_DEV, g.shape[1])
        to_sibling.append((n, l, g.reshape(N_CHIP, 2, g.shape[1], g.shape[2])))

    tail = []

    def take_jobs(heavy, max_pieces=None):
        a = list(to_sibling)
        b = [e for e in to_chips if heavy or N_CHIP * e[4] * e[2].shape[2] <= LIGHT_CARRIER_ELEMS][:max_pieces]
        to_sibling.clear()
        to_chips[:] = [e for e in to_chips if not any(e is t for t in b)]
        jobs = ([_to_chips_job([e[2:] for e in b])] if b else []) + ([_to_sibling_job([g for _, _, g in a])] if a else [])
        return jobs, (a, b)

    def settle(taken, jres):
        a, b = taken
        if b:
            for (n, l, _, r0, rows), got in zip(b, jres[0]):
                reduced.setdefault((n, l), []).append((r0, got))
        if a:
            for (n, l, g), got in zip(a, jres[-1]):
                pairs = _pair_sum(core_idx, g, got, f"pair_sum_{n}_{l}")
                R = pairs.shape[1]
                k = TAIL_PIECES if tail and R % (TAIL_PIECES * SUBLANES) == 0 else 1
                to_chips.extend((n, l, pairs, i * (R // k), R // k) for i in range(k))

    def carrier(fn, *args, heavy=True, **kw):
        jobs, taken = take_jobs(heavy)
        out, jres = carried(fn, *args, jobs=jobs, **kw)
        settle(taken, jres)
        return out

    small_rows = [None] * L
    cb_gu = W[0]["ffn1_w_gu"].shape[2]
    cb_in = W[0]["w_in"].shape[2]

    def ffn_grads(tag, wgu_name, wd_name, dxo, x_in, ht, gate, up, f, g_pre, g_post, l):
        dxi, df, dgate, dup, act, dgpre, dgpost = carrier(_ffn_bwd, dxo, x_in, f, gate, up, g_pre, g_post, W[l][wgu_name],
                                                          W[l][wd_name], f"{tag}_bwd_{l}")
        grad_ready(wd_name, l, carrier(_mm_tn, act, df, f"{tag}_dwd_{l}", heavy=False))
        half = carrier(_mm_tn, ht, dgate, f"{tag}_dwg_{l}", slab_cols=cb_gu, n_slabs=N_DEV, a_is_transposed=True, heavy=False)
        grad_ready(wgu_name, l, carrier(_mm_tn, ht, dup, f"{tag}_dwu_{l}", slab_cols=cb_gu, n_slabs=N_DEV, first_slab=N_DEV // 2,
                                        into=half, a_is_transposed=True, heavy=False))
        return dxi, dgpre, dgpost

    for l in reversed(range(L)):
        sv = saved[l]
        g = norm_full[l]
        dx, dg4, dg5 = ffn_grads("ffn2", "ffn2_w_gu", "ffn2_w_down", dx, sv["x2"], sv["h3"], sv["gate2"], sv["up2"], sv["f2"],
                                 vec(g[4]), vec(g[5]), l)
        (dm, dya, dyb, dyc, dproj, do, dcv, du1, dg3, dlng, dlnb, dcfb) = carrier(
            _mixer_bwd_a, dx, sv["m"], vec(g[3]), sv["ya"], sv["yb"], sv["yc"], sv["proj"], sv["o"], sv["cv"], sv["u1"],
            vec(cf_ln_g[l]), vec(cf_ln_b[l]), W[l]["w_ret_o"], W[l]["w_sc_o"], W[l]["w_cf_o"], W[l]["w_o"], f"mixer_bwd_a_{l}")
        grad_ready("w_o", l, _mm_tn(sv["merged"], dm, f"dw_o_{l}", tt_pref=2048))
        grad_ready("w_ret_o", l, _mm_tn(sv["ya_in"], dya, f"dw_ret_o_{l}", tt_pref=2048))
        grad_ready("w_sc_o", l, _mm_tn(sv["yb_in"], dyb, f"dw_sc_o_{l}", tt_pref=2048))
        grad_ready("w_cf_o", l, _mm_tn(sv["yc_in"], dyc, f"dw_cf_o_{l}", tt_pref=2048))
        dproj, dscw, dcfw = carrier(_mixer_bwd_b, dcv, du1, sv["proj"], scw_full[l], cfw_full[l], dproj, S, f"mixer_bwd_b_{l}")
        dproj = carrier(_ret_bwd, do, sv["proj"], cosf, sinf, lgs, dproj, B, S, D, f"retention_bwd_{l}", heavy=False)
        grad_ready("w_in", l, carrier(_mm_tn, sv["h2"], dproj, f"dw_in_{l}", slab_cols=cb_in, tn_pref=cb_in, a_is_transposed=True))
        dx, dg2 = carrier(_proj_bwd, dproj, W[l]["w_in"], sv["x1"], vec(g[2]), dx, f"proj_bwd_{l}")
        dx, dg0, dg1 = ffn_grads("ffn1", "ffn1_w_gu", "ffn1_w_down", dx, sv["x0"], sv["h1"], sv["gate1"], sv["up1"], sv["f1"],
                                 vec(g[0]), vec(g[1]), l)
        small_rows[l] = (jnp.concatenate([dg0, dg1, dg2, dg3, dg4, dg5, dscw, dcfw], axis=0),
                         jnp.concatenate([dcfb, dlng, dlnb], axis=0))
    grad_x = dx.reshape(B, S, D)
    big_out, done = {}, set()

    def next_adamw():
        for n in BIG:
            for l in reversed(range(L)):
                for r0, got in reduced.get((n, l), []):
                    if (n, l, r0) not in done:
                        return n, l, r0, got
        return None

    def run_adamw(n, l, r0, got, jobs=()):
        out = _final_adamw(got, w32[n], m32[n], v32[n], l, r0, big_out.get(n), f"adamw_{n}_{l}_{r0}", jobs=jobs)
        big_out[n], jres = out if jobs else (out, [])
        done.add((n, l, r0))
        return jres

    flush = 0
    while to_sibling or to_chips:
        jobs, taken = take_jobs(True)
        settle(taken, _run_jobs(jobs, f"grads_flush_{flush}"))
        flush += 1
    while next_adamw():
        run_adamw(*next_adamw())

    sh_part = jnp.concatenate([small_rows[l][0] for l in range(L)], axis=0)
    rep_part = jnp.concatenate([small_rows[l][1] for l in range(L)] + [jnp.zeros((8 - 3 * L % 8, D), F32)] * (1 if 3 * L % 8 else 0), axis=0)
    sh_g = _allgather_small(sh_part, "allgather_small_grads")
    rep_g = _allgather_small(rep_part, "allgather_replicated_grads")

    def pack_sh(a, b, c):
        return jnp.concatenate([a, b, c], axis=1).reshape(L * n_sh, DL)

    def pack_rep(a, b, c):
        rows = jnp.stack([a, b, c], axis=1).reshape(3 * L, D)
        return jnp.concatenate([rows, jnp.ones((rep_part.shape[0] - 3 * L, D), F32)], axis=0)

    sh_res = _small_adamw(dev_idx, sh_g, pack_sh(norm_g, sc_conv_w, cf_dw_w), pack_sh(m_norm_g, m_sc_conv_w, m_cf_dw_w),
                          pack_sh(v_norm_g, v_sc_conv_w, v_cf_dw_w), DL, "adamw_small_sharded")
    rep_res = _small_adamw(dev_idx, rep_g, pack_rep(cf_dw_b, cf_ln_g, cf_ln_b), pack_rep(m_cf_dw_b, m_cf_ln_g, m_cf_ln_b),
                           pack_rep(v_cf_dw_b, v_cf_ln_g, v_cf_ln_b), D, "adamw_small_replicated")

    def unpack_sh(a):
        a = a.reshape(L, n_sh, DL)
        return {"norm_g": a[:, :6], "sc_conv_w": a[:, 6:6 + SC_KERNEL], "cf_dw_w": a[:, 6 + SC_KERNEL:]}

    def unpack_rep(a):
        a = a[:3 * L].reshape(L, 3, D)
        return {"cf_dw_b": a[:, 0], "cf_ln_g": a[:, 1], "cf_ln_b": a[:, 2]}

    order = ("norm_g", "ffn1_w_gu", "ffn1_w_down", "w_in", "w_ret_o", "sc_conv_w", "w_sc_o", "cf_dw_w", "cf_dw_b", "cf_ln_g",
             "cf_ln_b", "w_cf_o", "w_o", "ffn2_w_gu", "ffn2_w_down")
    outs = []
    for kind in range(4):
        small = {**unpack_sh(sh_res[kind]), **unpack_rep(rep_res[kind])}
        outs += [big_out[n][kind] if n in big_out else small[n] for n in order]
    return (loss, grad_x, *outs)
```

```python
import functools

import jax
import jax.numpy as jnp
from jax import lax
from jax.experimental import pallas as pl
from jax.experimental.pallas import tpu as pltpu

F32 = jnp.float32
BF16 = jnp.bfloat16
MESH = pl.DeviceIdType.MESH
ANY = pl.BlockSpec(memory_space=pl.ANY)

N_DEV = 8
N_CHIP = 4
CHUNK = 64
HEADS = 4
ROPE_BASE = 10000.0
NORM_EPS = 1e-6
LN_EPS = 1e-5
SC_KERNEL = 3
CF_KERNEL = 31
HALO = 32
ADAM_LR = 0.001
ADAM_B1 = 0.9
ADAM_B2 = 0.999
ADAM_EPS = 1e-08
ADAM_WD = 0.01
ADAM_STEP = 10
VMEM_LIMIT_V7X = 56 * 1024 * 1024
LIGHT_CARRIER_ELEMS = 3 * 1024 * 1024
TAIL_PIECES = 4


class _Job:
    def __init__(self, ins, out_shapes, n_sems, n_local, start, finish, peers):
        self.ins, self.out_shapes, self.n_sems, self.n_local = list(ins), list(out_shapes), n_sems, n_local
        self.start, self.finish = start, finish
        self.peers = frozenset(peers)


COLLECTIVE_IDS = {frozenset({"sibling"}): 0, frozenset({"chips"}): 1, frozenset({"sibling", "chips"}): 2}


def _pallas(body, *, name, grid, in_specs, out_specs, out_shape, scratch_shapes=(), aliases=None, nprefetch=0, jobs=()):
    extra = {}
    single = not isinstance(out_shape, (list, tuple))
    out_shape = [out_shape] if single else list(out_shape)
    out_specs = [out_specs] if single else list(out_specs)
    in_specs, scratch = list(in_specs), list(scratch_shapes)
    n_in, n_out, n_scr = len(in_specs), len(out_shape), len(scratch)
    for jb in jobs:
        in_specs += [ANY] * len(jb.ins)
        out_specs += [ANY] * len(jb.out_shapes)
        out_shape += jb.out_shapes
        scratch += [pltpu.SemaphoreType.DMA((jb.n_sems,)), pltpu.SemaphoreType.DMA((jb.n_sems,)),
                    pltpu.SemaphoreType.DMA((max(jb.n_local, 1),))]

    def with_jobs(*refs):
        pre, refs = refs[:nprefetch], refs[nprefetch:]
        ins, p = refs[:n_in], n_in
        jins = []
        for jb in jobs:
            jins.append(refs[p:p + len(jb.ins)])
            p += len(jb.ins)
        outs = refs[p:p + n_out]
        p += n_out
        jouts = []
        for jb in jobs:
            jouts.append(refs[p:p + len(jb.out_shapes)])
            p += len(jb.out_shapes)
        scr = refs[p:p + n_scr]
        p += n_scr
        pids = [pl.program_id(d) for d in range(len(grid))]
        first = functools.reduce(jnp.logical_and, [pid == 0 for pid in pids])
        last = functools.reduce(jnp.logical_and, [pid == g - 1 for pid, g in zip(pids, grid)])

        @pl.when(first)
        def _():
            x, y, c = _mesh_pos()
            peers = ([(x, y, 1 - c)] if "sibling" in kinds else []) + ([(px, py, c) for px, py in _other_chips(x, y)] if "chips" in kinds else [])
            barrier = pltpu.get_barrier_semaphore()
            for peer in peers:
                pl.semaphore_signal(barrier, inc=1, device_id=peer, device_id_type=MESH)
            pl.semaphore_wait(barrier, len(peers))
            for i, jb in enumerate(jobs):
                jb.start(jins[i], jouts[i], *refs[p + 3 * i:p + 3 * i + 3])

        body(*pre, *ins, *outs, *scr)

        @pl.when(last)
        def _():
            for i, jb in enumerate(jobs):
                jb.finish(jins[i], jouts[i], *refs[p + 3 * i:p + 3 * i + 3])

    kinds = frozenset().union(*[jb.peers for jb in jobs])
    params = pltpu.CompilerParams(dimension_semantics=("arbitrary",) * len(grid), vmem_limit_bytes=VMEM_LIMIT_V7X,
                                  collective_id=COLLECTIVE_IDS[kinds] if jobs else None)
    spec = pltpu.PrefetchScalarGridSpec(num_scalar_prefetch=nprefetch, grid=grid, in_specs=in_specs,
                                        out_specs=out_specs, scratch_shapes=scratch)
    call = pl.pallas_call(with_jobs if jobs else body, name=name, grid_spec=spec, out_shape=out_shape, compiler_params=params,
                          input_output_aliases=aliases or {}, **extra)

    def run(*args):
        res = call(*args, *[a for jb in jobs for a in jb.ins])
        own = res[0] if single else list(res[:n_out])
        if not jobs:
            return own
        jres, p = [], n_out
        for jb in jobs:
            jres.append(list(res[p:p + len(jb.out_shapes)]))
            p += len(jb.out_shapes)
        return own, jres

    return run


def _comm_call(body, *, name, in_specs, out_specs, out_shape, scratch_shapes):
    extra = {}
    return pl.pallas_call(body, name=name, in_specs=in_specs, out_specs=out_specs, out_shape=out_shape,
                          scratch_shapes=scratch_shapes, **extra)


def _sds(shape, dtype):
    return jax.ShapeDtypeStruct(shape, dtype)


def _tile(n, pref):
    t = min(n, pref)
    assert n % t == 0, (n, pref)
    return t


def _sigmoid(x):
    return jax.nn.sigmoid(x)


def _rms_fwd(x, g):
    r = lax.rsqrt(jnp.mean(x * x, axis=-1, keepdims=True) + NORM_EPS)
    return x * r * g


def _rms_bwd(x, g, dy):
    r = lax.rsqrt(jnp.mean(x * x, axis=-1, keepdims=True) + NORM_EPS)
    xh = x * r
    dg = jnp.sum(dy * xh, axis=0, keepdims=True)
    dxh = dy * g
    dx = r * (dxh - xh * jnp.mean(dxh * xh, axis=-1, keepdims=True))
    return dx, dg


def _ln_stats(x):
    mu = jnp.mean(x, axis=-1, keepdims=True)
    xc = x - mu
    rstd = lax.rsqrt(jnp.mean(xc * xc, axis=-1, keepdims=True) + LN_EPS)
    return xc * rstd, rstd


def _ln_bwd(xh, rstd, dxh):
    return rstd * (dxh - jnp.mean(dxh, axis=-1, keepdims=True) - xh * jnp.mean(dxh * xh, axis=-1, keepdims=True))


def _dsilu(x, s):
    return s * (1.0 + x * (1.0 - s))


def _rot(x, cosf, sinf):
    return x * cosf + pltpu.roll(x, x.shape[-1] // 2, 1) * sinf


def _unrot(d, cosf, sinf):
    return d * cosf - pltpu.roll(d, d.shape[-1] // 2, 1) * sinf


def _dot_nn(a, b):
    return jnp.dot(a, b, preferred_element_type=F32)


def _dot_nt(a, b):
    return lax.dot_general(a, b, (((1,), (1,)), ((), ())), preferred_element_type=F32)


def _dot_tn(a, b):
    return lax.dot_general(a, b, (((0,), (0,)), ((), ())), preferred_element_type=F32)


def _ffn_tf(cb):
    return _tile(cb, 1024)


def _ffn_fwd(x, g_pre, g_post, wgu, wd, name, jobs=()):
    T, D = x.shape
    F = wd.shape[0]
    cb = wgu.shape[2]
    tm, tf = _tile(T, 512), _ffn_tf(cb)
    nj, nb = F // tf, cb // tf

    def body(x_ref, gpre_ref, gpost_ref, wg_ref, wu_ref, wd_ref, xo_ref, ht_ref, gate_ref, up_ref, f_ref, acc_ref, h_ref):
        j = pl.program_id(1)

        @pl.when(j == 0)
        def _():
            h = _rms_fwd(x_ref[...], gpre_ref[...])
            h_ref[...] = h.astype(BF16)
            ht_ref[...] = h.T.astype(BF16)
            acc_ref[...] = jnp.zeros_like(acc_ref)

        h = h_ref[...]
        gate = _dot_nn(h, wg_ref[...])
        up = _dot_nn(h, wu_ref[...])
        gate_ref[...] = gate.astype(BF16)
        up_ref[...] = up.astype(BF16)
        act = (gate * _sigmoid(gate) * up).astype(BF16)
        acc_ref[...] += _dot_nn(act, wd_ref[...])

        @pl.when(j == nj - 1)
        def _():
            f = acc_ref[...]
            f_ref[...] = f
            xo_ref[...] = x_ref[...] + 0.5 * _rms_fwd(f, gpost_ref[...])

    row = pl.BlockSpec((tm, D), lambda i, j: (i, 0))
    vec = pl.BlockSpec((1, D), lambda i, j: (0, 0))
    col = pl.BlockSpec((tm, tf), lambda i, j: (i, j))
    return _pallas(
        body, name=name, grid=(T // tm, nj),
        in_specs=[row, vec, vec,
                  pl.BlockSpec((None, D, tf), lambda i, j: (j // nb, 0, j % nb)),
                  pl.BlockSpec((None, D, tf), lambda i, j: ((nj + j) // nb, 0, j % nb)),
                  pl.BlockSpec((tf, D), lambda i, j: (j, 0))],
        out_specs=[row, pl.BlockSpec((D, tm), lambda i, j: (0, i)), col, col, row],
        out_shape=[_sds((T, D), F32), _sds((D, T), BF16), _sds((T, F), BF16), _sds((T, F), BF16), _sds((T, D), F32)],
        scratch_shapes=[pltpu.VMEM((tm, D), F32), pltpu.VMEM((tm, D), BF16)], jobs=jobs,
    )(x, g_pre, g_post, wgu, wgu, wd)


def _ffn_bwd(dxo, x, f, gate, up, g_pre, g_post, wgu, wd, name, jobs=()):
    T, D = x.shape
    F = wd.shape[0]
    cb = wgu.shape[2]
    tm, tf = _tile(T, 512), _ffn_tf(cb)
    nj, nb = F // tf, cb // tf

    def body(dxo_ref, x_ref, f_ref, gate_ref, up_ref, gpre_ref, gpost_ref, wg_ref, wu_ref, wd_ref,
             dx_ref, df_ref, dgate_ref, dup_ref, act_ref, dgpre_ref, dgpost_ref, acc_ref):
        i, j = pl.program_id(0), pl.program_id(1)

        @pl.when((i == 0) & (j == 0))
        def _():
            dgpre_ref[...] = jnp.zeros_like(dgpre_ref)
            dgpost_ref[...] = jnp.zeros_like(dgpost_ref)

        @pl.when(j == 0)
        def _():
            df, dgp = _rms_bwd(f_ref[...], gpost_ref[...], 0.5 * dxo_ref[...])
            df_ref[...] = df.astype(BF16)
            dgpost_ref[...] += dgp
            acc_ref[...] = jnp.zeros_like(acc_ref)

        dact = _dot_nt(df_ref[...], wd_ref[...])
        g = gate_ref[...].astype(F32)
        u = up_ref[...].astype(F32)
        s = _sigmoid(g)
        silu = g * s
        dgate = (dact * u * _dsilu(g, s)).astype(BF16)
        dup = (dact * silu).astype(BF16)
        act_ref[...] = (silu * u).astype(BF16)
        dgate_ref[...] = dgate
        dup_ref[...] = dup
        acc_ref[...] += _dot_nt(dgate, wg_ref[...]) + _dot_nt(dup, wu_ref[...])

        @pl.when(j == nj - 1)
        def _():
            dxin, dgp = _rms_bwd(x_ref[...], gpre_ref[...], acc_ref[...])
            dx_ref[...] = dxo_ref[...] + dxin
            dgpre_ref[...] += dgp

    row = pl.BlockSpec((tm, D), lambda i, j: (i, 0))
    vec = pl.BlockSpec((1, D), lambda i, j: (0, 0))
    col = pl.BlockSpec((tm, tf), lambda i, j: (i, j))
    return _pallas(
        body, name=name, grid=(T // tm, nj),
        in_specs=[row, row, row, col, col, vec, vec,
                  pl.BlockSpec((None, D, tf), lambda i, j: (j // nb, 0, j % nb)),
                  pl.BlockSpec((None, D, tf), lambda i, j: ((nj + j) // nb, 0, j % nb)),
                  pl.BlockSpec((tf, D), lambda i, j: (j, 0))],
        out_specs=[row, row, col, col, col, vec, vec],
        out_shape=[_sds((T, D), F32), _sds((T, D), BF16), _sds((T, F), BF16), _sds((T, F), BF16), _sds((T, F), BF16),
                   _sds((1, D), F32), _sds((1, D), F32)],
        scratch_shapes=[pltpu.VMEM((tm, D), F32)], jobs=jobs,
    )(dxo, x, f, gate, up, g_pre, g_post, wgu, wgu, wd)


def _mm_tn(a, b, name, slab_cols=None, n_slabs=None, first_slab=0, into=None, tn_pref=1024, tt_pref=2048,
           a_is_transposed=False, jobs=()):
    T, N = b.shape
    K = a.shape[0] if a_is_transposed else a.shape[1]
    tt = _tile(T, tt_pref)
    tko = _tile(K, 1024)
    tn = _tile(slab_cols or N, tn_pref)
    nt = T // tt

    def body(a_ref, b_ref, *rest):
        o_ref, acc_ref = rest[-2:]
        t = pl.program_id(2)

        @pl.when(t == 0)
        def _():
            acc_ref[...] = jnp.zeros_like(acc_ref)

        acc_ref[...] += (_dot_nn if a_is_transposed else _dot_tn)(a_ref[...], b_ref[...])

        @pl.when(t == nt - 1)
        def _():
            o_ref[...] = acc_ref[...].astype(o_ref.dtype)

    if slab_cols is None:
        shape = (K, N)
        ospec = pl.BlockSpec((tko, tn), lambda k, jn, t: (k, jn))
    else:
        nb = slab_cols // tn
        shape = (n_slabs or N // slab_cols, K, slab_cols)
        ospec = pl.BlockSpec((None, tko, tn), lambda k, jn, t: (first_slab + jn // nb, k, jn % nb))
    aspec = pl.BlockSpec((tko, tt), lambda k, jn, t: (k, t)) if a_is_transposed else pl.BlockSpec((tt, tko), lambda k, jn, t: (t, k))
    in_specs, args = [aspec, pl.BlockSpec((tt, tn), lambda k, jn, t: (t, jn))], [a, b]
    if into is not None:
        in_specs.append(ANY)
        args.append(into)
    return _pallas(body, name=name, grid=(K // tko, N // tn, nt), in_specs=in_specs, out_specs=ospec, out_shape=_sds(shape, BF16),
                   scratch_shapes=[pltpu.VMEM((tko, tn), F32)], aliases={2: 0} if into is not None else None, jobs=jobs)(*args)


def _norm_proj(x, g, w, name, jobs=()):
    T, D = x.shape
    nd, cb = w.shape[0], w.shape[2]
    tm = _tile(T, 1024)

    def body(x_ref, g_ref, w_ref, p_ref, ht_ref, h_ref):
        @pl.when(pl.program_id(1) == 0)
        def _():
            h = _rms_fwd(x_ref[...], g_ref[...])
            h_ref[...] = h.astype(BF16)
            ht_ref[...] = h.T.astype(BF16)

        p_ref[...] = _dot_nn(h_ref[...], w_ref[...])

    row = pl.BlockSpec((tm, D), lambda i, j: (i, 0))
    return _pallas(
        body, name=name, grid=(T // tm, nd),
        in_specs=[row, pl.BlockSpec((1, D), lambda i, j: (0, 0)),
                  pl.BlockSpec((None, D, cb), lambda i, j: (j, 0, 0))],
        out_specs=[pl.BlockSpec((tm, cb), lambda i, j: (i, j)), pl.BlockSpec((D, tm), lambda i, j: (0, i))],
        out_shape=[_sds((T, nd * cb), F32), _sds((D, T), BF16)],
        scratch_shapes=[pltpu.VMEM((tm, D), BF16)], jobs=jobs,
    )(x, g, w)


def _proj_bwd(dproj, w, x, g, dxo, name, jobs=()):
    T, D = x.shape
    nd, cb = w.shape[0], w.shape[2]
    tm = _tile(T, 1024)

    def body(dp_ref, w_ref, x_ref, g_ref, dxo_ref, dx_ref, dg_ref, acc_ref):
        i, j = pl.program_id(0), pl.program_id(1)

        @pl.when((i == 0) & (j == 0))
        def _():
            dg_ref[...] = jnp.zeros_like(dg_ref)

        @pl.when(j == 0)
        def _():
            acc_ref[...] = jnp.zeros_like(acc_ref)

        acc_ref[...] += _dot_nt(dp_ref[...], w_ref[...])

        @pl.when(j == nd - 1)
        def _():
            dxin, dgp = _rms_bwd(x_ref[...], g_ref[...], acc_ref[...])
            dx_ref[...] = dxo_ref[...] + dxin
            dg_ref[...] += dgp

    row = pl.BlockSpec((tm, D), lambda i, j: (i, 0))
    vec = pl.BlockSpec((1, D), lambda i, j: (0, 0))
    return _pallas(
        body, name=name, grid=(T // tm, nd),
        in_specs=[pl.BlockSpec((tm, cb), lambda i, j: (i, j)),
                  pl.BlockSpec((None, D, cb), lambda i, j: (j, 0, 0)), row, vec, row],
        out_specs=[row, vec],
        out_shape=[_sds((T, D), F32), _sds((1, D), F32)],
        scratch_shapes=[pltpu.VMEM((tm, D), F32)], jobs=jobs,
    )(dproj, w, x, g, dxo)


def _rope_tables(pos, invf, sgn, name):
    T = pos.shape[0]
    dk = invf.shape[1]
    tm = _tile(T, 1024)

    def body(p_ref, f_ref, s_ref, c_out, s_out):
        ang = p_ref[...].astype(F32) * f_ref[...]
        c_out[...] = jnp.cos(ang)
        s_out[...] = jnp.sin(ang) * s_ref[...]

    vec = pl.BlockSpec((1, dk), lambda i: (0, 0))
    out = pl.BlockSpec((tm, dk), lambda i: (i, 0))
    return _pallas(body, name=name, grid=(T // tm,), in_specs=[pl.BlockSpec((tm, 1), lambda i: (i, 0)), vec, vec],
                   out_specs=[out, out], out_shape=[_sds((T, dk), F32), _sds((T, dk), F32)])(pos, invf, sgn)


def _decay(lg, r0, tq, n):
    r = r0 + lax.broadcasted_iota(jnp.int32, (tq, n), 0)
    c = lax.broadcasted_iota(jnp.int32, (tq, n), 1)
    rc, cc = r // CHUNK, c // CHUNK
    d = (r - c).astype(F32)
    e = jnp.where(rc == cc, jnp.abs(d), d)
    return jnp.where(cc > rc, 0.0, jnp.exp(lg * e))


def _decay_tiles(lg, w_ref, tq):
    for d in range(w_ref.shape[0]):
        w_ref[d] = _decay(lg, d * tq, tq, tq)


def _weighted(s, w_ref, qi, tq, out_ref):
    for kt in range(qi + 1):
        cols = slice(kt * tq, (kt + 1) * tq)
        out_ref[:, cols] = (s[:, cols] * w_ref[qi - kt]).astype(BF16)


def _ret_fwd(proj, cosf, sinf, lgs, B, S, D, name, jobs=()):
    T = B * S
    dk, dv = D // 8, D // 4
    tq = _tile(S, 256)
    scale = dk ** -0.5

    def body(lg_ref, q_ref, k_ref, v_ref, c_ref, s_ref, o_ref, kr_ref, vb_ref, w_ref, p_ref):
        kr_ref[...] = (_rot(k_ref[...], c_ref[...], s_ref[...]) * scale).astype(BF16)
        vb_ref[...] = v_ref[...].astype(BF16)
        _decay_tiles(lg_ref[pl.program_id(1)], w_ref, tq)
        for qi in range(S // tq):
            rows, n = slice(qi * tq, (qi + 1) * tq), (qi + 1) * tq
            q = _rot(q_ref[rows, :], c_ref[rows, :], s_ref[rows, :]).astype(BF16)
            _weighted(_dot_nt(q, kr_ref[:n, :]), w_ref, qi, tq, p_ref)
            o_ref[rows, :] = _dot_nn(p_ref[:, :n], vb_ref[:n, :])

    return _pallas(
        body, name=name, grid=(B, HEADS),
        in_specs=[pl.BlockSpec(memory_space=pltpu.SMEM),
                  pl.BlockSpec((S, dk), lambda b, h: (b, h)),
                  pl.BlockSpec((S, dk), lambda b, h: (b, HEADS + h)),
                  pl.BlockSpec((S, dv), lambda b, h: (b, HEADS + h)),
                  pl.BlockSpec((S, dk), lambda b, h: (b, 0)),
                  pl.BlockSpec((S, dk), lambda b, h: (b, 0))],
        out_specs=pl.BlockSpec((S, dv), lambda b, h: (b, h)),
        out_shape=_sds((T, D), F32),
        scratch_shapes=[pltpu.VMEM((S, dk), BF16), pltpu.VMEM((S, dv), BF16), pltpu.VMEM((S // tq, tq, tq), F32),
                        pltpu.VMEM((tq, S), BF16)], jobs=jobs,
    )(lgs, proj, proj, proj, cosf, sinf)


def _ret_bwd(do, proj, cosf, sinf, lgs, dproj, B, S, D, name, jobs=()):
    dk, dv = D // 8, D // 4
    tq = _tile(S, 256)
    scale = dk ** -0.5

    def body(lg_ref, do_ref, q_ref, k_ref, v_ref, c_ref, s_ref, dproj_in, dproj_ref, kr_ref, vb_ref, dka_ref, dva_ref,
             dq_ref, dk_ref, dv_ref, w_ref, p_ref, ds_ref, copy_sems):
        b, h = pl.program_id(0), pl.program_id(1)
        kr_ref[...] = (_rot(k_ref[...], c_ref[...], s_ref[...]) * scale).astype(BF16)
        vb_ref[...] = v_ref[...].astype(BF16)
        dka_ref[...] = jnp.zeros_like(dka_ref)
        dva_ref[...] = jnp.zeros_like(dva_ref)
        _decay_tiles(lg_ref[h], w_ref, tq)

        for qi in range(S // tq):
            rows, n = slice(qi * tq, (qi + 1) * tq), (qi + 1) * tq
            cq, sq = c_ref[rows, :], s_ref[rows, :]
            q = _rot(q_ref[rows, :], cq, sq).astype(BF16)
            dout = do_ref[rows, :].astype(BF16)
            _weighted(_dot_nt(q, kr_ref[:n, :]), w_ref, qi, tq, p_ref)
            _weighted(_dot_nt(dout, vb_ref[:n, :]), w_ref, qi, tq, ds_ref)
            p, ds = p_ref[:, :n], ds_ref[:, :n]
            dq_ref[rows, :] = _unrot(_dot_nn(ds, kr_ref[:n, :]), cq, sq).astype(BF16)
            dka_ref[:n, :] += _dot_tn(ds, q)
            dva_ref[:n, :] += _dot_tn(p, dout)

        dk_ref[...] = _unrot(dka_ref[...] * scale, c_ref[...], s_ref[...]).astype(BF16)
        dv_ref[...] = dva_ref[...].astype(BF16)
        rows = pl.ds(pl.multiple_of(b * S, S), S)
        copies = [pltpu.make_async_copy(dq_ref, dproj_ref.at[rows, pl.ds(pl.multiple_of(h * dk, dk), dk)], copy_sems.at[0]),
                  pltpu.make_async_copy(dk_ref, dproj_ref.at[rows, pl.ds(pl.multiple_of((HEADS + h) * dk, dk), dk)], copy_sems.at[1]),
                  pltpu.make_async_copy(dv_ref, dproj_ref.at[rows, pl.ds(pl.multiple_of((HEADS + h) * dv, dv), dv)], copy_sems.at[2])]
        for cp in copies:
            cp.start()
        for cp in copies:
            cp.wait()

    return _pallas(
        body, name=name, grid=(B, HEADS),
        in_specs=[pl.BlockSpec(memory_space=pltpu.SMEM),
                  pl.BlockSpec((S, dv), lambda b, h: (b, h)),
                  pl.BlockSpec((S, dk), lambda b, h: (b, h)),
                  pl.BlockSpec((S, dk), lambda b, h: (b, HEADS + h)),
                  pl.BlockSpec((S, dv), lambda b, h: (b, HEADS + h)),
                  pl.BlockSpec((S, dk), lambda b, h: (b, 0)),
                  pl.BlockSpec((S, dk), lambda b, h: (b, 0)), ANY],
        out_specs=ANY, out_shape=_sds(dproj.shape, BF16),
        scratch_shapes=[pltpu.VMEM((S, dk), BF16), pltpu.VMEM((S, dv), BF16), pltpu.VMEM((S, dk), F32), pltpu.VMEM((S, dv), F32),
                        pltpu.VMEM((S, dk), BF16), pltpu.VMEM((S, dk), BF16), pltpu.VMEM((S, dv), BF16),
                        pltpu.VMEM((S // tq, tq, tq), F32), pltpu.VMEM((tq, S), BF16), pltpu.VMEM((tq, S), BF16),
                        pltpu.SemaphoreType.DMA((3,))],
        aliases={7: 0}, jobs=jobs,
    )(lgs, do, proj, proj, proj, cosf, sinf, dproj)


COL_GRET, COL_SCB, COL_SCC, COL_SCX, COL_GLUA, COL_GLUB, COL_GL = 2, 3, 4, 5, 6, 7, 8


SUBLANES, LANES = 8, 128
CONV_BLOCK = 256
SC_FWD = tuple(HALO - (SC_KERNEL - 1) + j for j in range(SC_KERNEL))
CF_FWD = tuple(HALO - (CF_KERNEL - 1) + j for j in range(CF_KERNEL))
SC_BWD = tuple(SC_KERNEL - 1 - j for j in range(SC_KERNEL))
CF_BWD = tuple(CF_KERNEL - 1 - j for j in range(CF_KERNEL))


def _build_shifted(src_ref, e_ref, shifts):
    n = e_ref.shape[1]
    for b in sorted({s % SUBLANES for s in shifts} - {0}):
        e_ref[b - 1] = src_ref[pl.ds(b, n), :]


def _shifted_rows(src_ref, e_ref, s, r0, rows, lanes):
    a, b = divmod(s, SUBLANES)
    at = pl.ds(r0 + SUBLANES * a, rows)
    return src_ref[at, lanes] if b == 0 else e_ref[b - 1, at, lanes]


def _tree_sum(terms):
    while len(terms) > 1:
        terms = [a + b for a, b in zip(terms[::2], terms[1::2])] + ([terms[-1]] if len(terms) % 2 else [])
    return terms[0]


def _conv_taps(src_ref, e_ref, w_ref, shifts, tm, D, emit, u_ref=None, dw_ref=None):
    rows = min(tm, CONV_BLOCK if u_ref is None else CONV_BLOCK // 2)
    for lt in range(D // LANES):
        lanes = pl.ds(lt * LANES, LANES)
        for r0 in range(0, tm, rows):
            u = None if u_ref is None else u_ref[pl.ds(r0, rows), lanes]
            acc = None
            for j, s in enumerate(shifts):
                v = _shifted_rows(src_ref, e_ref, s, r0, rows, lanes)
                term = v * w_ref[j:j + 1, lanes]
                acc = term if acc is None else acc + term
                if u is not None:
                    prod = u * v
                    part = _tree_sum([prod[g * SUBLANES:(g + 1) * SUBLANES, :] for g in range(rows // SUBLANES)])
                    dw_ref[j:j + 1, lanes] += jnp.sum(part, axis=0, keepdims=True)
            emit(lt, pl.ds(r0, rows), acc)


def _mixer_mid_fwd(proj, o, sc_w, cf_w, cf_b, ln_g, ln_b, S, name, jobs=()):
    T, D = o.shape
    dv = D // HEADS
    tm = _tile(S, 256)
    per_seq = S // tm
    hb = tm // HALO

    def body(gret_ref, scb_ref, scc_ref, scx_ref, ga_ref, gb_ref, scc_h, scx_h, ga_h, gb_h, o_ref,
             scw_ref, cfw_ref, cfb_ref, lng_ref, lnb_ref, ya_ref, yb_ref, yc_ref, cv_ref, u1_ref, ext_ref, e_ref):
        first = (pl.program_id(0) % per_seq) == 0
        keep = jnp.where(first, 0.0, 1.0)
        gr = gret_ref[...]
        sg = gr * _sigmoid(gr)
        for hh in range(HEADS):
            cols = slice(hh * dv, (hh + 1) * dv)
            hn, _ = _ln_stats(o_ref[:, cols])
            ya_ref[:, cols] = (sg[:, cols] * hn).astype(BF16)
        ext_ref[:HALO, :] = scc_h[...] * scx_h[...] * keep
        ext_ref[HALO:, :] = scc_ref[...] * scx_ref[...]
        _build_shifted(ext_ref, e_ref, SC_FWD)

        def emit_cv(lt, rows, cv):
            lanes = pl.ds(lt * LANES, LANES)
            cv_ref[rows, lanes] = cv
            yb_ref[rows, lanes] = (scb_ref[rows, lanes] * cv).astype(BF16)

        _conv_taps(ext_ref, e_ref, scw_ref, SC_FWD, tm, D, emit_cv)
        ext_ref[:HALO, :] = ga_h[...] * _sigmoid(gb_h[...]) * keep
        ext_ref[HALO:, :] = ga_ref[...] * _sigmoid(gb_ref[...])
        _build_shifted(ext_ref, e_ref, CF_FWD)

        def emit_u1(lt, rows, acc):
            lanes = pl.ds(lt * LANES, LANES)
            u1_ref[rows, lanes] = acc + cfb_ref[:, lanes]

        _conv_taps(ext_ref, e_ref, cfw_ref, CF_FWD, tm, D, emit_u1)
        xh, _ = _ln_stats(u1_ref[...])
        u2 = xh * lng_ref[...] + lnb_ref[...]
        yc_ref[...] = (u2 * _sigmoid(u2)).astype(BF16)

    def colblk(c):
        return pl.BlockSpec((tm, D), lambda i: (i, c))

    def halo(c):
        return pl.BlockSpec((HALO, D), lambda i: (jnp.maximum(i * hb - 1, 0), c))

    row = pl.BlockSpec((tm, D), lambda i: (i, 0))
    vec = pl.BlockSpec((1, D), lambda i: (0, 0))
    return _pallas(
        body, name=name, grid=(T // tm,),
        in_specs=[colblk(COL_GRET), colblk(COL_SCB), colblk(COL_SCC), colblk(COL_SCX), colblk(COL_GLUA), colblk(COL_GLUB),
                  halo(COL_SCC), halo(COL_SCX), halo(COL_GLUA), halo(COL_GLUB), row,
                  pl.BlockSpec((SC_KERNEL, D), lambda i: (0, 0)), pl.BlockSpec((CF_KERNEL, D), lambda i: (0, 0)), vec, vec, vec],
        out_specs=[row, row, row, row, row],
        out_shape=[_sds((T, D), BF16)] * 3 + [_sds((T, D), F32)] * 2,
        scratch_shapes=[pltpu.VMEM((HALO + tm, D), F32), pltpu.VMEM((SUBLANES - 1, HALO + tm - SUBLANES, D), F32)], jobs=jobs,
    )(proj, proj, proj, proj, proj, proj, proj, proj, proj, proj, o, sc_w, cf_w, cf_b, ln_g, ln_b)


def _mixer_merge_fwd(x, proj, ya_in, yb_in, yc_in, g_post, w_ret, w_sc, w_cf, w_o, name, jobs=()):
    T, D = x.shape
    tm = _tile(T, 512)

    def body(x_ref, g0_ref, g1_ref, g2_ref, ya_in_ref, yb_in_ref, yc_in_ref, gp_ref, wr_ref, ws_ref, wc_ref, wo_ref,
             xo_ref, ya_ref, yb_ref, yc_ref, mg_ref, m_ref):
        ya = _dot_nn(ya_in_ref[...], wr_ref[...])
        yb = _dot_nn(yb_in_ref[...], ws_ref[...])
        yc = _dot_nn(yc_in_ref[...], wc_ref[...])
        ya_ref[...] = ya.astype(BF16)
        yb_ref[...] = yb.astype(BF16)
        yc_ref[...] = yc.astype(BF16)
        merged = (_sigmoid(g0_ref[...]) * ya + _sigmoid(g1_ref[...]) * yb + _sigmoid(g2_ref[...]) * yc).astype(BF16)
        mg_ref[...] = merged
        m = _dot_nn(merged, wo_ref[...])
        m_ref[...] = m
        xo_ref[...] = x_ref[...] + _rms_fwd(m, gp_ref[...])

    row = pl.BlockSpec((tm, D), lambda i: (i, 0))
    wsp = pl.BlockSpec((D, D), lambda i: (0, 0), pipeline_mode=pl.Buffered(1))

    def colblk(c):
        return pl.BlockSpec((tm, D), lambda i: (i, c))

    return _pallas(
        body, name=name, grid=(T // tm,),
        in_specs=[row, colblk(COL_GL), colblk(COL_GL + 1), colblk(COL_GL + 2), row, row, row,
                  pl.BlockSpec((1, D), lambda i: (0, 0)), wsp, wsp, wsp, wsp],
        out_specs=[row] * 6,
        out_shape=[_sds((T, D), F32)] + [_sds((T, D), BF16)] * 4 + [_sds((T, D), F32)], jobs=jobs,
    )(x, proj, proj, proj, ya_in, yb_in, yc_in, g_post, w_ret, w_sc, w_cf, w_o)


def _mixer_bwd_a(dxo, m, g_post, ya, yb, yc, proj, o, cv, u1, ln_g, ln_b, w_ret, w_sc, w_cf, w_o, name, jobs=()):
    T, D = m.shape
    dv = D // HEADS
    tm = _tile(T, 256)

    def body(dxo_ref, m_ref, gp_ref, ya_ref, yb_ref, yc_ref, g0_ref, g1_ref, g2_ref, gret_ref, scb_ref, o_ref, cv_ref, u1_ref,
             lng_ref, lnb_ref, wr_ref, ws_ref, wc_ref, wo_ref,
             dm_ref, dya_ref, dyb_ref, dyc_ref, dproj_ref, do_ref, dcv_ref, du1_ref, dgp_ref, dlng_ref, dlnb_ref, dcfb_ref,
             d2_ref, dgl_ref, copy_sems):
        @pl.when(pl.program_id(0) == 0)
        def _():
            for r in (dgp_ref, dlng_ref, dlnb_ref, dcfb_ref):
                r[...] = jnp.zeros_like(r)

        dm, dgp = _rms_bwd(m_ref[...], gp_ref[...], dxo_ref[...])
        dgp_ref[...] += dgp
        dmb = dm.astype(BF16)
        dm_ref[...] = dmb
        dmerged = _dot_nt(dmb, wo_ref[...])
        dys = []
        for k, (g_ref, y_ref, dy_ref) in enumerate(((g0_ref, ya_ref, dya_ref), (g1_ref, yb_ref, dyb_ref), (g2_ref, yc_ref, dyc_ref))):
            sg = _sigmoid(g_ref[...])
            dgl_ref[:, k * D:(k + 1) * D] = (dmerged * y_ref[...].astype(F32) * sg * (1.0 - sg)).astype(BF16)
            dy = (dmerged * sg).astype(BF16)
            dy_ref[...] = dy
            dys.append(dy)
        dya_in = _dot_nt(dys[0], wr_ref[...])
        gr = gret_ref[...]
        sr = _sigmoid(gr)
        for hh in range(HEADS):
            cols = slice(hh * dv, (hh + 1) * dv)
            hn, rstd = _ln_stats(o_ref[:, cols])
            d2_ref[:, cols] = (dya_in[:, cols] * hn * _dsilu(gr[:, cols], sr[:, cols])).astype(BF16)
            do_ref[:, cols] = _ln_bwd(hn, rstd, dya_in[:, cols] * gr[:, cols] * sr[:, cols])
        dyb_in = _dot_nt(dys[1], ws_ref[...])
        d2_ref[:, D:] = (dyb_in * cv_ref[...]).astype(BF16)
        dcv_ref[...] = dyb_in * scb_ref[...]
        dyc_in = _dot_nt(dys[2], wc_ref[...])
        xh, rstd = _ln_stats(u1_ref[...])
        u2 = xh * lng_ref[...] + lnb_ref[...]
        du2 = dyc_in * _dsilu(u2, _sigmoid(u2))
        dlng_ref[...] += jnp.sum(du2 * xh, axis=0, keepdims=True)
        dlnb_ref[...] += jnp.sum(du2, axis=0, keepdims=True)
        du1 = _ln_bwd(xh, rstd, du2 * lng_ref[...])
        du1_ref[...] = du1
        dcfb_ref[...] += jnp.sum(du1, axis=0, keepdims=True)
        rows = pl.ds(pl.multiple_of(pl.program_id(0) * tm, tm), tm)
        copies = [pltpu.make_async_copy(d2_ref, dproj_ref.at[rows, pl.ds(COL_GRET * D, 2 * D)], copy_sems.at[0]),
                  pltpu.make_async_copy(dgl_ref, dproj_ref.at[rows, pl.ds(COL_GL * D, 3 * D)], copy_sems.at[1])]
        for cp in copies:
            cp.start()
        for cp in copies:
            cp.wait()

    row = pl.BlockSpec((tm, D), lambda i: (i, 0))
    vec = pl.BlockSpec((1, D), lambda i: (0, 0))
    wsp = pl.BlockSpec((D, D), lambda i: (0, 0), pipeline_mode=pl.Buffered(1))

    def colblk(c):
        return pl.BlockSpec((tm, D), lambda i: (i, c))

    return _pallas(
        body, name=name, grid=(T // tm,),
        in_specs=[row, row, vec, row, row, row, colblk(COL_GL), colblk(COL_GL + 1), colblk(COL_GL + 2),
                  colblk(COL_GRET), colblk(COL_SCB), row, row, row, vec, vec, wsp, wsp, wsp, wsp],
        out_specs=[row, row, row, row, ANY, row, row, row, vec, vec, vec, vec],
        out_shape=[_sds((T, D), BF16)] * 4 + [_sds((T, proj.shape[1]), BF16)] + [_sds((T, D), F32)] * 3 + [_sds((1, D), F32)] * 4,
        scratch_shapes=[pltpu.VMEM((tm, 2 * D), BF16), pltpu.VMEM((tm, 3 * D), BF16), pltpu.SemaphoreType.DMA((2,))], jobs=jobs,
    )(dxo, m, g_post, ya, yb, yc, proj, proj, proj, proj, proj, o, cv, u1, ln_g, ln_b, w_ret, w_sc, w_cf, w_o)


def _mixer_bwd_b(dcv, du1, proj, sc_w, cf_w, dproj, S, name, jobs=()):
    T, D = dcv.shape
    tm = _tile(S, 256)
    per_seq = S // tm
    hb = tm // HALO
    last_hb = T // HALO - 1

    def body(dcv_ref, du1_ref, dcv_n, du1_n, scc_ref, scx_ref, ga_ref, gb_ref, scw_ref, cfw_ref, dproj_in,
             d4_ref, dscw_ref, dcfw_ref, u_ref, dext_ref, e_ref, sb_ref):
        i = pl.program_id(0)
        keep_next = jnp.where((i % per_seq) == per_seq - 1, 0.0, 1.0)

        @pl.when(i == 0)
        def _():
            dscw_ref[...] = jnp.zeros_like(dscw_ref)
            dcfw_ref[...] = jnp.zeros_like(dcfw_ref)

        dext_ref[:tm, :] = dcv_ref[...]
        dext_ref[tm:, :] = dcv_n[...] * keep_next
        _build_shifted(dext_ref, e_ref, SC_BWD)
        u_ref[...] = scc_ref[...] * scx_ref[...]

        def emit_dz(lt, rows, dz):
            lanes = pl.ds(lt * LANES, LANES)
            d4_ref[rows, pl.ds(lt * LANES, LANES)] = (dz * scx_ref[rows, lanes]).astype(BF16)
            d4_ref[rows, pl.ds(D + lt * LANES, LANES)] = (dz * scc_ref[rows, lanes]).astype(BF16)

        _conv_taps(dext_ref, e_ref, scw_ref, SC_BWD, tm, D, emit_dz, u_ref=u_ref, dw_ref=dscw_ref)
        dext_ref[:tm, :] = du1_ref[...]
        dext_ref[tm:, :] = du1_n[...] * keep_next
        _build_shifted(dext_ref, e_ref, CF_BWD)
        sb_ref[...] = _sigmoid(gb_ref[...])
        u_ref[...] = ga_ref[...] * sb_ref[...]

        def emit_du0(lt, rows, du0):
            lanes = pl.ds(lt * LANES, LANES)
            sb = sb_ref[rows, lanes]
            d4_ref[rows, pl.ds(2 * D + lt * LANES, LANES)] = (du0 * sb).astype(BF16)
            d4_ref[rows, pl.ds(3 * D + lt * LANES, LANES)] = (du0 * ga_ref[rows, lanes] * sb * (1.0 - sb)).astype(BF16)

        _conv_taps(dext_ref, e_ref, cfw_ref, CF_BWD, tm, D, emit_du0, u_ref=u_ref, dw_ref=dcfw_ref)

    row = pl.BlockSpec((tm, D), lambda i: (i, 0))
    nxt = pl.BlockSpec((HALO, D), lambda i: (jnp.minimum((i + 1) * hb, last_hb), 0))

    def colblk(c):
        return pl.BlockSpec((tm, D), lambda i: (i, c))

    return _pallas(
        body, name=name, grid=(T // tm,),
        in_specs=[row, row, nxt, nxt, colblk(COL_SCC), colblk(COL_SCX), colblk(COL_GLUA), colblk(COL_GLUB),
                  pl.BlockSpec((SC_KERNEL, D), lambda i: (0, 0)), pl.BlockSpec((CF_KERNEL, D), lambda i: (0, 0)), ANY],
        out_specs=[pl.BlockSpec((tm, 4 * D), lambda i: (i, COL_SCC // 4)), pl.BlockSpec((SC_KERNEL, D), lambda i: (0, 0)),
                   pl.BlockSpec((CF_KERNEL, D), lambda i: (0, 0))],
        out_shape=[_sds(dproj.shape, BF16), _sds((SC_KERNEL, D), F32), _sds((CF_KERNEL, D), F32)],
        scratch_shapes=[pltpu.VMEM((tm, D), F32), pltpu.VMEM((tm + HALO, D), F32),
                        pltpu.VMEM((SUBLANES - 1, HALO + tm - SUBLANES, D), F32), pltpu.VMEM((tm, D), F32)],
        aliases={10: 0}, jobs=jobs,
    )(dcv, du1, dcv, du1, proj, proj, proj, proj, sc_w, cf_w, dproj)


def _loss_grad(y, tgt, name):
    T, D = y.shape
    tm = _tile(T, 512)

    def body(y_ref, t_ref, dy_ref, loss_ref):
        @pl.when(pl.program_id(0) == 0)
        def _():
            loss_ref[...] = jnp.zeros_like(loss_ref)

        e = y_ref[...] - t_ref[...]
        dy_ref[...] = e * (1.0 / D)
        loss_ref[...] += 0.5 * jnp.sum(jnp.sum(e * e, axis=-1, keepdims=True) * (1.0 / D), axis=0, keepdims=True)

    row = pl.BlockSpec((tm, D), lambda i: (i, 0))
    return _pallas(body, name=name, grid=(T // tm,), in_specs=[row, row],
                   out_specs=[row, pl.BlockSpec((1, 1), lambda i: (0, 0))],
                   out_shape=[_sds((T, D), F32), _sds((1, 1), F32)])(y, tgt)


def _mesh_pos():
    return lax.axis_index("x"), lax.axis_index("y"), lax.axis_index("c")


def _other_chips(x, y):
    return [(1 - x, y), (x, 1 - y), (1 - x, 1 - y)]


def _allgather_job(shards):
    nt = len(shards)

    def parts(ins, outs, send, recv):
        x, y, c = _mesh_pos()

        def slab(t, px, py, pc):
            return outs[t].at[4 * px + 2 * py + pc]

        def copy(t, k, block, to, src=None):
            return pltpu.make_async_remote_copy(
                src_ref=slab(t, *block) if src is None else src, dst_ref=slab(t, *block),
                send_sem=send.at[7 * t + k], recv_sem=recv.at[7 * t + k], device_id=to, device_id_type=MESH)

        return (x, y, c), (x, y, 1 - c), _other_chips(x, y), c, slab, copy

    def start(ins, outs, send, recv, loc):
        me, sibling, chips, c, slab, copy = parts(ins, outs, send, recv)
        for t in range(nt):
            pltpu.make_async_copy(ins[t], slab(t, *me), loc.at[t]).start()
            copy(t, 0, me, sibling, src=ins[t]).start()
            for j, chip in enumerate(chips):
                copy(t, 1 + j, me, (*chip, c), src=ins[t]).start()

    def finish(ins, outs, send, recv, loc):
        me, sibling, chips, c, slab, copy = parts(ins, outs, send, recv)
        for j, chip in enumerate(chips):
            for t in range(nt):
                copy(t, 1 + j, (*chip, c), me).wait_recv()
                copy(t, 4 + j, (*chip, c), sibling).start()
        for t in range(nt):
            copy(t, 0, sibling, me).wait_recv()
            for j, chip in enumerate(chips):
                copy(t, 4 + j, (*chip, 1 - c), me).wait_recv()
        for t in range(nt):
            copy(t, 0, me, sibling, src=ins[t]).wait_send()
            for j, chip in enumerate(chips):
                copy(t, 1 + j, me, (*chip, c), src=ins[t]).wait_send()
                copy(t, 4 + j, (*chip, c), sibling).wait_send()
            pltpu.make_async_copy(ins[t], slab(t, *me), loc.at[t]).wait()

    return _Job(shards, [_sds((N_DEV,) + s.shape, s.dtype) for s in shards], 7 * nt, nt, start, finish, ("sibling", "chips"))


def _to_sibling_job(grads):
    nt = len(grads)

    def copies(ins, outs, send, recv):
        x, y, c = _mesh_pos()
        return [pltpu.make_async_remote_copy(src_ref=ins[t].at[:, 1 - c], dst_ref=outs[t], send_sem=send.at[t], recv_sem=recv.at[t],
                                             device_id=(x, y, 1 - c), device_id_type=MESH) for t in range(nt)]

    def start(ins, outs, send, recv, loc):
        for cp in copies(ins, outs, send, recv):
            cp.start()

    def finish(ins, outs, send, recv, loc):
        for cp in copies(ins, outs, send, recv):
            cp.wait()

    return _Job(grads, [_sds((N_CHIP,) + g.shape[2:], g.dtype) for g in grads], nt, 0, start, finish, ("sibling",))


def _to_chips_job(pieces):
    nt = len(pieces)
    pairs = [p for p, _, _ in pieces]

    def copies(ins, outs, send, recv, loc):
        x, y, c = _mesh_pos()

        def src(t, chip):
            return ins[t].at[chip, pl.ds(pieces[t][1], pieces[t][2])]

        remote = [pltpu.make_async_remote_copy(src_ref=src(t, 2 * px + py), dst_ref=outs[t].at[k], send_sem=send.at[3 * t + k],
                                               recv_sem=recv.at[3 * t + k], device_id=(px, py, c), device_id_type=MESH)
                  for t in range(nt) for k, (px, py) in enumerate(_other_chips(x, y))]
        local = [pltpu.make_async_copy(src(t, 2 * x + y), outs[t].at[3], loc.at[t]) for t in range(nt)]
        return remote + local

    def start(ins, outs, send, recv, loc):
        for cp in copies(ins, outs, send, recv, loc):
            cp.start()

    def finish(ins, outs, send, recv, loc):
        for cp in copies(ins, outs, send, recv, loc):
            cp.wait()

    return _Job(pairs, [_sds((N_CHIP, rows, p.shape[2]), p.dtype) for p, _, rows in pieces], 3 * nt, nt, start, finish, ("chips",))


def _run_jobs(jobs, name):
    def body(o_ref):
        o_ref[...] = jnp.zeros_like(o_ref)

    _, jres = _pallas(body, name=name, grid=(1,), in_specs=[], out_specs=pl.BlockSpec((8, 128), lambda i: (0, 0)),
                      out_shape=_sds((8, 128), F32), jobs=jobs)()
    return jres


def _allgather_small(v, name):
    R, C = v.shape

    def body(x_ref, out_ref, send_sems, recv_sems, local_sem):
        x, y, c = _mesh_pos()
        me, sibling = (x, y, c), (x, y, 1 - c)
        chips = _other_chips(x, y)

        def slab(px, py, pc):
            return out_ref.at[4 * px + 2 * py + pc]

        def copy(k, block, to, src=None):
            return pltpu.make_async_remote_copy(
                src_ref=slab(*block) if src is None else src, dst_ref=slab(*block),
                send_sem=send_sems.at[k], recv_sem=recv_sems.at[k], device_id=to, device_id_type=MESH)

        mine = pltpu.make_async_copy(x_ref, slab(*me), local_sem)
        mine.start()
        first = [copy(0, me, sibling, src=x_ref)]
        first += [copy(1 + j, me, (*chip, c), src=x_ref) for j, chip in enumerate(chips)]
        for cp in first:
            cp.start()
        passed = [copy(4 + j, (*chip, c), sibling) for j, chip in enumerate(chips)]
        for j, chip in enumerate(chips):
            copy(1 + j, (*chip, c), me).wait_recv()
            passed[j].start()
        copy(0, sibling, me).wait_recv()
        for j, chip in enumerate(chips):
            copy(4 + j, (*chip, 1 - c), me).wait_recv()
        for cp in first + passed:
            cp.wait_send()
        mine.wait()

    return _comm_call(
        body, name=name, in_specs=[pl.BlockSpec(memory_space=pltpu.VMEM)], out_specs=pl.BlockSpec(memory_space=pltpu.VMEM),
        out_shape=_sds((N_DEV, R, C), v.dtype),
        scratch_shapes=[pltpu.SemaphoreType.DMA((7,)), pltpu.SemaphoreType.DMA((7,)), pltpu.SemaphoreType.DMA],
    )(v)


def _pair_sum(core, grad, got, name):
    _, _, R, C = grad.shape
    tr = _tile(R, 1024)

    def body(core_ref, mine_ref, got_ref, out_ref):
        out_ref[...] = (mine_ref[...].astype(F32) + got_ref[...].astype(F32)).astype(BF16)

    blk = (None, tr, C)
    return _pallas(
        body, name=name, grid=(N_CHIP, R // tr), nprefetch=1,
        in_specs=[pl.BlockSpec((None, None, tr, C), lambda k, r, core: (k, core[0], r, 0)),
                  pl.BlockSpec(blk, lambda k, r, core: (k, r, 0))],
        out_specs=pl.BlockSpec(blk, lambda k, r, core: (k, r, 0)),
        out_shape=_sds((N_CHIP, R, C), BF16),
    )(core, grad, got)


def _adamw(w, g, m, v):
    m = ADAM_B1 * m + (1.0 - ADAM_B1) * g
    v = ADAM_B2 * v + (1.0 - ADAM_B2) * (g * g)
    m_hat = m / (1.0 - ADAM_B1 ** ADAM_STEP)
    v_hat = v / (1.0 - ADAM_B2 ** ADAM_STEP)
    delta = -ADAM_LR * (m_hat / (jnp.sqrt(v_hat) + ADAM_EPS) + ADAM_WD * w)
    return delta, m, v


def _final_adamw(sums, w, m, v, l, r0, into, name, jobs=()):
    L, R, C = w.shape
    rows = sums.shape[1]
    tr = _tile(rows, 512 if C <= 1024 else 256)
    assert r0 % tr == 0
    b0 = r0 // tr

    def body(s_ref, w_ref, m_ref, v_ref, *rest):
        g_out, d_out, m_out, v_out = rest[-4:]
        g = s_ref[0].astype(F32)
        for k in range(1, N_CHIP):
            g = g + s_ref[k].astype(F32)
        d, mn, vn = _adamw(w_ref[...], g, m_ref[...], v_ref[...])
        g_out[...] = g
        d_out[...] = d
        m_out[...] = mn
        v_out[...] = vn

    loc = pl.BlockSpec((None, tr, C), lambda r: (l, b0 + r, 0))
    in_specs = [pl.BlockSpec((N_CHIP, tr, C), lambda r: (0, r, 0)), loc, loc, loc]
    args = [sums, w, m, v]
    aliases = None
    if into is not None:
        in_specs += [ANY] * 4
        args += list(into)
        aliases = {4 + i: i for i in range(4)}
    return _pallas(body, name=name, grid=(rows // tr,), in_specs=in_specs, out_specs=[loc] * 4,
                   out_shape=[_sds((L, R, C), F32)] * 4, aliases=aliases, jobs=jobs)(*args)


def _small_adamw(dev, allg, w, m, v, width, name):
    R = w.shape[0]

    def body(dev_ref, a_ref, w_ref, m_ref, v_ref, g_out, d_out, m_out, v_out):
        g = a_ref[0]
        for k in range(1, N_DEV):
            g = g + a_ref[k]
        d, mn, vn = _adamw(w_ref[...], g, m_ref[...], v_ref[...])
        g_out[...] = g
        d_out[...] = d
        m_out[...] = mn
        v_out[...] = vn

    full = allg.shape[2] == width
    loc = pl.BlockSpec((R, width), lambda i, dev: (0, 0))
    return _pallas(
        body, name=name, grid=(1,), nprefetch=1,
        in_specs=[pl.BlockSpec((N_DEV, R, width), lambda i, dev: (0, 0, 0 if full else dev[0])), loc, loc, loc],
        out_specs=[loc] * 4, out_shape=[_sds((R, width), F32)] * 4,
    )(dev, allg, w, m, v)


BIG = ("ffn1_w_gu", "ffn1_w_down", "w_in", "w_ret_o", "w_sc_o", "w_cf_o", "w_o", "ffn2_w_gu", "ffn2_w_down")
FFN1 = ("ffn1_w_gu", "ffn1_w_down")
MIX_OUT = ("w_ret_o", "w_sc_o", "w_cf_o", "w_o")
ROW_BLOCKS = ("ffn1_w_down", "ffn2_w_down") + MIX_OUT


def kernel(x, positions, norm_g, ffn1_w_gu, ffn1_w_down, w_in, w_ret_o, sc_conv_w, w_sc_o, cf_dw_w, cf_dw_b, cf_ln_g, cf_ln_b, w_cf_o, w_o, ffn2_w_gu, ffn2_w_down, loss_target, m_norm_g, m_ffn1_w_gu, m_ffn1_w_down, m_w_in, m_w_ret_o, m_sc_conv_w, m_w_sc_o, m_cf_dw_w, m_cf_dw_b, m_cf_ln_g, m_cf_ln_b, m_w_cf_o, m_w_o, m_ffn2_w_gu, m_ffn2_w_down, v_norm_g, v_ffn1_w_gu, v_ffn1_w_down, v_w_in, v_w_ret_o, v_sc_conv_w, v_w_sc_o, v_cf_dw_w, v_cf_dw_b, v_cf_ln_g, v_cf_ln_b, v_w_cf_o, v_w_o, v_ffn2_w_gu, v_ffn2_w_down):
    B, S, D = x.shape
    T = B * S
    L = norm_g.shape[0]
    DL = norm_g.shape[2]
    dk = D // 8
    mx, my, mc = _mesh_pos()
    dev_idx = jnp.reshape(4 * mx + 2 * my + mc, (1,)).astype(jnp.int32)
    core_idx = jnp.reshape(mc, (1,)).astype(jnp.int32)

    w32 = dict(ffn1_w_gu=ffn1_w_gu, ffn1_w_down=ffn1_w_down, w_in=w_in, w_ret_o=w_ret_o, w_sc_o=w_sc_o, w_cf_o=w_cf_o,
               w_o=w_o, ffn2_w_gu=ffn2_w_gu, ffn2_w_down=ffn2_w_down)
    m32 = dict(ffn1_w_gu=m_ffn1_w_gu, ffn1_w_down=m_ffn1_w_down, w_in=m_w_in, w_ret_o=m_w_ret_o, w_sc_o=m_w_sc_o,
               w_cf_o=m_w_cf_o, w_o=m_w_o, ffn2_w_gu=m_ffn2_w_gu, ffn2_w_down=m_ffn2_w_down)
    v32 = dict(ffn1_w_gu=v_ffn1_w_gu, ffn1_w_down=v_ffn1_w_down, w_in=v_w_in, w_ret_o=v_w_ret_o, w_sc_o=v_w_sc_o,
               w_cf_o=v_w_cf_o, w_o=v_w_o, ffn2_w_gu=v_ffn2_w_gu, ffn2_w_down=v_ffn2_w_down)

    W = [dict() for _ in range(L)]

    def gather(l, names):
        return _allgather_job([w32[n][l].astype(BF16) for n in names])

    def place(l, names, slabs):
        for n, a in zip(names, slabs):
            W[l][n] = a.reshape(a.shape[0] * a.shape[1], a.shape[2]) if n in ROW_BLOCKS else a

    def carried(fn, *args, jobs, **kw):
        out = fn(*args, jobs=jobs, **kw)
        return out if jobs else (out, [])

    n_sh = 6 + SC_KERNEL + CF_KERNEL
    small_sh = jnp.concatenate([norm_g, sc_conv_w, cf_dw_w], axis=1).reshape(L * n_sh, DL)
    sh_all = _allgather_small(small_sh, "allgather_small_params")
    sh_full = jnp.transpose(sh_all, (1, 0, 2)).reshape(L, n_sh, D)
    norm_full = sh_full[:, :6]
    scw_full = sh_full[:, 6:6 + SC_KERNEL]
    cfw_full = sh_full[:, 6 + SC_KERNEL:]
    place(0, FFN1, _run_jobs([gather(0, FFN1)], "allgather_first")[0])

    half = dk // 2
    inv_freq = ROPE_BASE ** (-jnp.arange(half, dtype=F32) / half)
    invf = jnp.concatenate([inv_freq, inv_freq])[None, :]
    sgn = jnp.concatenate([-jnp.ones((half,), F32), jnp.ones((half,), F32)])[None, :]
    cosf, sinf = _rope_tables(positions.reshape(T, 1), invf, sgn, "rope_tables")
    lgs = jnp.log(1.0 - 2.0 ** (-5.0 - jnp.arange(HEADS, dtype=F32)))

    def vec(a):
        return a.reshape(1, D)

    xc = x.reshape(T, D)
    saved = []
    for l in range(L):
        g = norm_full[l]
        sv = {}
        sv["x0"] = xc
        (xc, sv["h1"], sv["gate1"], sv["up1"], sv["f1"]), jr = _ffn_fwd(
            xc, vec(g[0]), vec(g[1]), W[l]["ffn1_w_gu"], W[l]["ffn1_w_down"], f"ffn1_fwd_{l}", jobs=[gather(l, ("w_in",))])
        place(l, ("w_in",), jr[0])
        sv["x1"] = xc
        (proj, sv["h2"]), jr = _norm_proj(xc, vec(g[2]), W[l]["w_in"], f"proj_fwd_{l}", jobs=[gather(l, ("ffn2_w_gu",))])
        place(l, ("ffn2_w_gu",), jr[0])
        sv["proj"] = proj
        o, jr = _ret_fwd(proj, cosf, sinf, lgs, B, S, D, f"retention_fwd_{l}", jobs=[gather(l, MIX_OUT)])
        place(l, MIX_OUT, jr[0])
        sv["o"] = o
        nxt = l + 1 < L
        (ya_in, yb_in, yc_in, sv["cv"], sv["u1"]), jr = _mixer_mid_fwd(
            proj, o, scw_full[l], cfw_full[l], vec(cf_dw_b[l]), vec(cf_ln_g[l]), vec(cf_ln_b[l]), S,
            f"mixer_mid_fwd_{l}", jobs=[gather(l, ("ffn2_w_down",))])
        place(l, ("ffn2_w_down",), jr[0])
        sv["ya_in"], sv["yb_in"], sv["yc_in"] = ya_in, yb_in, yc_in
        (xc, sv["ya"], sv["yb"], sv["yc"], sv["merged"], sv["m"]), jr = carried(
            _mixer_merge_fwd, xc, proj, ya_in, yb_in, yc_in, vec(g[3]), W[l]["w_ret_o"], W[l]["w_sc_o"], W[l]["w_cf_o"], W[l]["w_o"],
            f"mixer_merge_fwd_{l}", jobs=[gather(l + 1, ("ffn1_w_down",))] if nxt else [])
        if nxt:
            place(l + 1, ("ffn1_w_down",), jr[0])
        sv["x2"] = xc
        (xc, sv["h3"], sv["gate2"], sv["up2"], sv["f2"]), jr = carried(
            _ffn_fwd, xc, vec(g[4]), vec(g[5]), W[l]["ffn2_w_gu"], W[l]["ffn2_w_down"], f"ffn2_fwd_{l}",
            jobs=[gather(l + 1, ("ffn1_w_gu",))] if nxt else [])
        if nxt:
            place(l + 1, ("ffn1_w_gu",), jr[0])
        saved.append(sv)

    dx, loss_part = _loss_grad(xc, loss_target.reshape(T, D), "loss")
    loss = lax.psum(loss_part[0, 0], ("x", "y", "c"))

    to_sibling, to_chips, reduced = [], [], {}

    def grad_ready(n, l, g):
        if g.ndim == 2:
            g = g.reshape(N_DEV, g.shape[0] // N_DEV, g.shape[1])
        to_sibling.append((n, l, g.reshape(N_CHIP, 2, g.shape[1], g.shape[2])))

    tail = []

    def take_jobs(heavy, max_pieces=None):
        a = list(to_sibling)
        b = [e for e in to_chips if heavy or N_CHIP * e[4] * e[2].shape[2] <= LIGHT_CARRIER_ELEMS][:max_pieces]
        to_sibling.clear()
        to_chips[:] = [e for e in to_chips if not any(e is t for t in b)]
        jobs = ([_to_chips_job([e[2:] for e in b])] if b else []) + ([_to_sibling_job([g for _, _, g in a])] if a else [])
        return jobs, (a, b)

    def settle(taken, jres):
        a, b = taken
        if b:
            for (n, l, _, r0, rows), got in zip(b, jres[0]):
                reduced.setdefault((n, l), []).append((r0, got))
        if a:
            for (n, l, g), got in zip(a, jres[-1]):
                pairs = _pair_sum(core_idx, g, got, f"pair_sum_{n}_{l}")
                R = pairs.shape[1]
                k = TAIL_PIECES if tail and R % (TAIL_PIECES * SUBLANES) == 0 else 1
                to_chips.extend((n, l, pairs, i * (R // k), R // k) for i in range(k))

    def carrier(fn, *args, heavy=True, **kw):
        jobs, taken = take_jobs(heavy)
        out, jres = carried(fn, *args, jobs=jobs, **kw)
        settle(taken, jres)
        return out

    small_rows = [None] * L
    cb_gu = W[0]["ffn1_w_gu"].shape[2]
    cb_in = W[0]["w_in"].shape[2]

    def ffn_grads(tag, wgu_name, wd_name, dxo, x_in, ht, gate, up, f, g_pre, g_post, l):
        dxi, df, dgate, dup, act, dgpre, dgpost = carrier(_ffn_bwd, dxo, x_in, f, gate, up, g_pre, g_post, W[l][wgu_name],
                                                          W[l][wd_name], f"{tag}_bwd_{l}")
        grad_ready(wd_name, l, carrier(_mm_tn, act, df, f"{tag}_dwd_{l}", heavy=False))
        half = carrier(_mm_tn, ht, dgate, f"{tag}_dwg_{l}", slab_cols=cb_gu, n_slabs=N_DEV, a_is_transposed=True, heavy=False)
        grad_ready(wgu_name, l, carrier(_mm_tn, ht, dup, f"{tag}_dwu_{l}", slab_cols=cb_gu, n_slabs=N_DEV, first_slab=N_DEV // 2,
                                        into=half, a_is_transposed=True, heavy=False))
        return dxi, dgpre, dgpost

    for l in reversed(range(L)):
        sv = saved[l]
        g = norm_full[l]
        dx, dg4, dg5 = ffn_grads("ffn2", "ffn2_w_gu", "ffn2_w_down", dx, sv["x2"], sv["h3"], sv["gate2"], sv["up2"], sv["f2"],
                                 vec(g[4]), vec(g[5]), l)
        (dm, dya, dyb, dyc, dproj, do, dcv, du1, dg3, dlng, dlnb, dcfb) = carrier(
            _mixer_bwd_a, dx, sv["m"], vec(g[3]), sv["ya"], sv["yb"], sv["yc"], sv["proj"], sv["o"], sv["cv"], sv["u1"],
            vec(cf_ln_g[l]), vec(cf_ln_b[l]), W[l]["w_ret_o"], W[l]["w_sc_o"], W[l]["w_cf_o"], W[l]["w_o"], f"mixer_bwd_a_{l}")
        grad_ready("w_o", l, _mm_tn(sv["merged"], dm, f"dw_o_{l}", tt_pref=2048))
        grad_ready("w_ret_o", l, _mm_tn(sv["ya_in"], dya, f"dw_ret_o_{l}", tt_pref=2048))
        grad_ready("w_sc_o", l, _mm_tn(sv["yb_in"], dyb, f"dw_sc_o_{l}", tt_pref=2048))
        grad_ready("w_cf_o", l, _mm_tn(sv["yc_in"], dyc, f"dw_cf_o_{l}", tt_pref=2048))
        dproj, dscw, dcfw = carrier(_mixer_bwd_b, dcv, du1, sv["proj"], scw_full[l], cfw_full[l], dproj, S, f"mixer_bwd_b_{l}")
        dproj = carrier(_ret_bwd, do, sv["proj"], cosf, sinf, lgs, dproj, B, S, D, f"retention_bwd_{l}", heavy=False)
        grad_ready("w_in", l, carrier(_mm_tn, sv["h2"], dproj, f"dw_in_{l}", slab_cols=cb_in, tn_pref=cb_in, a_is_transposed=True))
        dx, dg2 = carrier(_proj_bwd, dproj, W[l]["w_in"], sv["x1"], vec(g[2]), dx, f"proj_bwd_{l}")
        dx, dg0, dg1 = ffn_grads("ffn1", "ffn1_w_gu", "ffn1_w_down", dx, sv["x0"], sv["h1"], sv["gate1"], sv["up1"], sv["f1"],
                                 vec(g[0]), vec(g[1]), l)
        small_rows[l] = (jnp.concatenate([dg0, dg1, dg2, dg3, dg4, dg5, dscw, dcfw], axis=0),
                         jnp.concatenate([dcfb, dlng, dlnb], axis=0))
    grad_x = dx.reshape(B, S, D)
    big_out, done = {}, set()

    def next_adamw():
        for n in BIG:
            for l in reversed(range(L)):
                for r0, got in reduced.get((n, l), []):
                    if (n, l, r0) not in done:
                        return n, l, r0, got
        return None

    def run_adamw(n, l, r0, got, jobs=()):
        out = _final_adamw(got, w32[n], m32[n], v32[n], l, r0, big_out.get(n), f"adamw_{n}_{l}_{r0}", jobs=jobs)
        big_out[n], jres = out if jobs else (out, [])
        done.add((n, l, r0))
        return jres

    flush = 0
    while to_sibling or to_chips:
        jobs, taken = take_jobs(True)
        settle(taken, _run_jobs(jobs, f"grads_flush_{flush}"))
        flush += 1
    while next_adamw():
        run_adamw(*next_adamw())

    sh_part = jnp.concatenate([small_rows[l][0] for l in range(L)], axis=0)
    rep_part = jnp.concatenate([small_rows[l][1] for l in range(L)] + [jnp.zeros((8 - 3 * L % 8, D), F32)] * (1 if 3 * L % 8 else 0), axis=0)
    sh_g = _allgather_small(sh_part, "allgather_small_grads")
    rep_g = _allgather_small(rep_part, "allgather_replicated_grads")

    def pack_sh(a, b, c):
        return jnp.concatenate([a, b, c], axis=1).reshape(L * n_sh, DL)

    def pack_rep(a, b, c):
        rows = jnp.stack([a, b, c], axis=1).reshape(3 * L, D)
        return jnp.concatenate([rows, jnp.ones((rep_part.shape[0] - 3 * L, D), F32)], axis=0)

    sh_res = _small_adamw(dev_idx, sh_g, pack_sh(norm_g, sc_conv_w, cf_dw_w), pack_sh(m_norm_g, m_sc_conv_w, m_cf_dw_w),
                          pack_sh(v_norm_g, v_sc_conv_w, v_cf_dw_w), DL, "adamw_small_sharded")
    rep_res = _small_adamw(dev_idx, rep_g, pack_rep(cf_dw_b, cf_ln_g, cf_ln_b), pack_rep(m_cf_dw_b, m_cf_ln_g, m_cf_ln_b),
                           pack_rep(v_cf_dw_b, v_cf_ln_g, v_cf_ln_b), D, "adamw_small_replicated")

    def unpack_sh(a):
        a = a.reshape(L, n_sh, DL)
        return {"norm_g": a[:, :6], "sc_conv_w": a[:, 6:6 + SC_KERNEL], "cf_dw_w": a[:, 6 + SC_KERNEL:]}

    def unpack_rep(a):
        a = a[:3 * L].reshape(L, 3, D)
        return {"cf_dw_b": a[:, 0], "cf_ln_g": a[:, 1], "cf_ln_b": a[:, 2]}

    order = ("norm_g", "ffn1_w_gu", "ffn1_w_down", "w_in", "w_ret_o", "sc_conv_w", "w_sc_o", "cf_dw_w", "cf_dw_b", "cf_ln_g",
             "cf_ln_b", "w_cf_o", "w_o", "ffn2_w_gu", "ffn2_w_down")
    outs = []
    for kind in range(4):
        small = {**unpack_sh(sh_res[kind]), **unpack_rep(rep_res[kind])}
        outs += [big_out[n][kind] if n in big_out else small[n] for n in order]
    return (loss, grad_x, *outs)
```

```python
import functools

import jax
import jax.numpy as jnp
from jax import lax
from jax.experimental import pallas as pl
from jax.experimental.pallas import tpu as pltpu

F32 = jnp.float32
BF16 = jnp.bfloat16
MESH = pl.DeviceIdType.MESH
ANY = pl.BlockSpec(memory_space=pl.ANY)

N_DEV = 8
N_CHIP = 4
CHUNK = 64
HEADS = 4
ROPE_BASE = 10000.0
NORM_EPS = 1e-6
LN_EPS = 1e-5
SC_KERNEL = 3
CF_KERNEL = 31
HALO = 32
ADAM_LR = 0.001
ADAM_B1 = 0.9
ADAM_B2 = 0.999
ADAM_EPS = 1e-08
ADAM_WD = 0.01
ADAM_STEP = 10
VMEM_LIMIT_V7X = 56 * 1024 * 1024
LIGHT_CARRIER_ELEMS = 3 * 1024 * 1024
TAIL_PIECES = 4
EARLY_FORWARD = 0.8


class _Job:
    def __init__(self, ins, out_shapes, n_sems, n_local, start, finish, peers, mid=None, mid_at=None):
        self.ins, self.out_shapes, self.n_sems, self.n_local = list(ins), list(out_shapes), n_sems, n_local
        self.start, self.finish = start, finish
        self.mid, self.mid_at = mid, mid_at
        self.peers = frozenset(peers)


COLLECTIVE_IDS = {frozenset({"sibling"}): 0, frozenset({"chips"}): 1, frozenset({"sibling", "chips"}): 2}


def _pallas(body, *, name, grid, in_specs, out_specs, out_shape, scratch_shapes=(), aliases=None, nprefetch=0, jobs=()):
    extra = {}
    single = not isinstance(out_shape, (list, tuple))
    out_shape = [out_shape] if single else list(out_shape)
    out_specs = [out_specs] if single else list(out_specs)
    in_specs, scratch = list(in_specs), list(scratch_shapes)
    n_in, n_out, n_scr = len(in_specs), len(out_shape), len(scratch)
    for jb in jobs:
        in_specs += [ANY] * len(jb.ins)
        out_specs += [ANY] * len(jb.out_shapes)
        out_shape += jb.out_shapes
        scratch += [pltpu.SemaphoreType.DMA((jb.n_sems,)), pltpu.SemaphoreType.DMA((jb.n_sems,)),
                    pltpu.SemaphoreType.DMA((max(jb.n_local, 1),))]

    def with_jobs(*refs):
        pre, refs = refs[:nprefetch], refs[nprefetch:]
        ins, p = refs[:n_in], n_in
        jins = []
        for jb in jobs:
            jins.append(refs[p:p + len(jb.ins)])
            p += len(jb.ins)
        outs = refs[p:p + n_out]
        p += n_out
        jouts = []
        for jb in jobs:
            jouts.append(refs[p:p + len(jb.out_shapes)])
            p += len(jb.out_shapes)
        scr = refs[p:p + n_scr]
        p += n_scr
        pids = [pl.program_id(d) for d in range(len(grid))]
        first = functools.reduce(jnp.logical_and, [pid == 0 for pid in pids])
        last = functools.reduce(jnp.logical_and, [pid == g - 1 for pid, g in zip(pids, grid)])

        @pl.when(first)
        def _():
            x, y, c = _mesh_pos()
            peers = ([(x, y, 1 - c)] if "sibling" in kinds else []) + ([(px, py, c) for px, py in _other_chips(x, y)] if "chips" in kinds else [])
            barrier = pltpu.get_barrier_semaphore()
            for peer in peers:
                pl.semaphore_signal(barrier, inc=1, device_id=peer, device_id_type=MESH)
            pl.semaphore_wait(barrier, len(peers))
            for i, jb in enumerate(jobs):
                jb.start(jins[i], jouts[i], *refs[p + 3 * i:p + 3 * i + 3])

        body(*pre, *ins, *outs, *scr)

        n_steps = functools.reduce(lambda a, b: a * b, grid)
        step = functools.reduce(lambda acc, pg: acc * pg[1] + pg[0], zip(pids, grid), 0)
        for i, jb in enumerate(jobs):
            if jb.mid is not None:
                @pl.when(step == min(max(int(jb.mid_at * n_steps), 1), n_steps - 1))
                def _(i=i, jb=jb):
                    jb.mid(jins[i], jouts[i], *refs[p + 3 * i:p + 3 * i + 3])

        @pl.when(last)
        def _():
            for i, jb in enumerate(jobs):
                jb.finish(jins[i], jouts[i], *refs[p + 3 * i:p + 3 * i + 3])

    kinds = frozenset().union(*[jb.peers for jb in jobs])
    params = pltpu.CompilerParams(dimension_semantics=("arbitrary",) * len(grid), vmem_limit_bytes=VMEM_LIMIT_V7X,
                                  collective_id=COLLECTIVE_IDS[kinds] if jobs else None)
    spec = pltpu.PrefetchScalarGridSpec(num_scalar_prefetch=nprefetch, grid=grid, in_specs=in_specs,
                                        out_specs=out_specs, scratch_shapes=scratch)
    call = pl.pallas_call(with_jobs if jobs else body, name=name, grid_spec=spec, out_shape=out_shape, compiler_params=params,
                          input_output_aliases=aliases or {}, **extra)

    def run(*args):
        res = call(*args, *[a for jb in jobs for a in jb.ins])
        own = res[0] if single else list(res[:n_out])
        if not jobs:
            return own
        jres, p = [], n_out
        for jb in jobs:
            jres.append(list(res[p:p + len(jb.out_shapes)]))
            p += len(jb.out_shapes)
        return own, jres

    return run


def _comm_call(body, *, name, in_specs, out_specs, out_shape, scratch_shapes):
    extra = {}
    return pl.pallas_call(body, name=name, in_specs=in_specs, out_specs=out_specs, out_shape=out_shape,
                          scratch_shapes=scratch_shapes, **extra)


def _sds(shape, dtype):
    return jax.ShapeDtypeStruct(shape, dtype)


def _tile(n, pref):
    t = min(n, pref)
    assert n % t == 0, (n, pref)
    return t


def _sigmoid(x):
    return jax.nn.sigmoid(x)


def _rms_fwd(x, g):
    r = lax.rsqrt(jnp.mean(x * x, axis=-1, keepdims=True) + NORM_EPS)
    return x * r * g


def _rms_bwd(x, g, dy):
    r = lax.rsqrt(jnp.mean(x * x, axis=-1, keepdims=True) + NORM_EPS)
    xh = x * r
    dg = jnp.sum(dy * xh, axis=0, keepdims=True)
    dxh = dy * g
    dx = r * (dxh - xh * jnp.mean(dxh * xh, axis=-1, keepdims=True))
    return dx, dg


def _ln_stats(x):
    mu = jnp.mean(x, axis=-1, keepdims=True)
    xc = x - mu
    rstd = lax.rsqrt(jnp.mean(xc * xc, axis=-1, keepdims=True) + LN_EPS)
    return xc * rstd, rstd


def _ln_bwd(xh, rstd, dxh):
    return rstd * (dxh - jnp.mean(dxh, axis=-1, keepdims=True) - xh * jnp.mean(dxh * xh, axis=-1, keepdims=True))


def _dsilu(x, s):
    return s * (1.0 + x * (1.0 - s))


def _rot(x, cosf, sinf):
    return x * cosf + pltpu.roll(x, x.shape[-1] // 2, 1) * sinf


def _unrot(d, cosf, sinf):
    return d * cosf - pltpu.roll(d, d.shape[-1] // 2, 1) * sinf


def _dot_nn(a, b):
    return jnp.dot(a, b, preferred_element_type=F32)


def _dot_nt(a, b):
    return lax.dot_general(a, b, (((1,), (1,)), ((), ())), preferred_element_type=F32)


def _dot_tn(a, b):
    return lax.dot_general(a, b, (((0,), (0,)), ((), ())), preferred_element_type=F32)


def _ffn_tf(cb):
    return _tile(cb, 1024)


def _ffn_fwd(x, g_pre, g_post, wgu, wd, name, jobs=()):
    T, D = x.shape
    F = wd.shape[0]
    cb = wgu.shape[2]
    tm, tf = _tile(T, 512), _ffn_tf(cb)
    nj, nb = F // tf, cb // tf

    def body(x_ref, gpre_ref, gpost_ref, wg_ref, wu_ref, wd_ref, xo_ref, ht_ref, gate_ref, up_ref, f_ref, acc_ref, h_ref):
        j = pl.program_id(1)

        @pl.when(j == 0)
        def _():
            h = _rms_fwd(x_ref[...], gpre_ref[...])
            h_ref[...] = h.astype(BF16)
            ht_ref[...] = h.T.astype(BF16)
            acc_ref[...] = jnp.zeros_like(acc_ref)

        h = h_ref[...]
        gate = _dot_nn(h, wg_ref[...])
        up = _dot_nn(h, wu_ref[...])
        gate_ref[...] = gate.astype(BF16)
        up_ref[...] = up.astype(BF16)
        act = (gate * _sigmoid(gate) * up).astype(BF16)
        acc_ref[...] += _dot_nn(act, wd_ref[...])

        @pl.when(j == nj - 1)
        def _():
            f = acc_ref[...]
            f_ref[...] = f
            xo_ref[...] = x_ref[...] + 0.5 * _rms_fwd(f, gpost_ref[...])

    row = pl.BlockSpec((tm, D), lambda i, j: (i, 0))
    vec = pl.BlockSpec((1, D), lambda i, j: (0, 0))
    col = pl.BlockSpec((tm, tf), lambda i, j: (i, j))
    return _pallas(
        body, name=name, grid=(T // tm, nj),
        in_specs=[row, vec, vec,
                  pl.BlockSpec((None, D, tf), lambda i, j: (j // nb, 0, j % nb)),
                  pl.BlockSpec((None, D, tf), lambda i, j: ((nj + j) // nb, 0, j % nb)),
                  pl.BlockSpec((tf, D), lambda i, j: (j, 0))],
        out_specs=[row, pl.BlockSpec((D, tm), lambda i, j: (0, i)), col, col, row],
        out_shape=[_sds((T, D), F32), _sds((D, T), BF16), _sds((T, F), BF16), _sds((T, F), BF16), _sds((T, D), F32)],
        scratch_shapes=[pltpu.VMEM((tm, D), F32), pltpu.VMEM((tm, D), BF16)], jobs=jobs,
    )(x, g_pre, g_post, wgu, wgu, wd)


def _ffn_bwd(dxo, x, f, gate, up, g_pre, g_post, wgu, wd, name, jobs=()):
    T, D = x.shape
    F = wd.shape[0]
    cb = wgu.shape[2]
    tm, tf = _tile(T, 512), _ffn_tf(cb)
    nj, nb = F // tf, cb // tf

    def body(dxo_ref, x_ref, f_ref, gate_ref, up_ref, gpre_ref, gpost_ref, wg_ref, wu_ref, wd_ref,
             dx_ref, df_ref, dgate_ref, dup_ref, act_ref, dgpre_ref, dgpost_ref, acc_ref):
        i, j = pl.program_id(0), pl.program_id(1)

        @pl.when((i == 0) & (j == 0))
        def _():
            dgpre_ref[...] = jnp.zeros_like(dgpre_ref)
            dgpost_ref[...] = jnp.zeros_like(dgpost_ref)

        @pl.when(j == 0)
        def _():
            df, dgp = _rms_bwd(f_ref[...], gpost_ref[...], 0.5 * dxo_ref[...])
            df_ref[...] = df.astype(BF16)
            dgpost_ref[...] += dgp
            acc_ref[...] = jnp.zeros_like(acc_ref)

        dact = _dot_nt(df_ref[...], wd_ref[...])
        g = gate_ref[...].astype(F32)
        u = up_ref[...].astype(F32)
        s = _sigmoid(g)
        silu = g * s
        dgate = (dact * u * _dsilu(g, s)).astype(BF16)
        dup = (dact * silu).astype(BF16)
        act_ref[...] = (silu * u).astype(BF16)
        dgate_ref[...] = dgate
        dup_ref[...] = dup
        acc_ref[...] += _dot_nt(dgate, wg_ref[...]) + _dot_nt(dup, wu_ref[...])

        @pl.when(j == nj - 1)
        def _():
            dxin, dgp = _rms_bwd(x_ref[...], gpre_ref[...], acc_ref[...])
            dx_ref[...] = dxo_ref[...] + dxin
            dgpre_ref[...] += dgp

    row = pl.BlockSpec((tm, D), lambda i, j: (i, 0))
    vec = pl.BlockSpec((1, D), lambda i, j: (0, 0))
    col = pl.BlockSpec((tm, tf), lambda i, j: (i, j))
    return _pallas(
        body, name=name, grid=(T // tm, nj),
        in_specs=[row, row, row, col, col, vec, vec,
                  pl.BlockSpec((None, D, tf), lambda i, j: (j // nb, 0, j % nb)),
                  pl.BlockSpec((None, D, tf), lambda i, j: ((nj + j) // nb, 0, j % nb)),
                  pl.BlockSpec((tf, D), lambda i, j: (j, 0))],
        out_specs=[row, row, col, col, col, vec, vec],
        out_shape=[_sds((T, D), F32), _sds((T, D), BF16), _sds((T, F), BF16), _sds((T, F), BF16), _sds((T, F), BF16),
                   _sds((1, D), F32), _sds((1, D), F32)],
        scratch_shapes=[pltpu.VMEM((tm, D), F32)], jobs=jobs,
    )(dxo, x, f, gate, up, g_pre, g_post, wgu, wgu, wd)


def _mm_tn(a, b, name, slab_cols=None, n_slabs=None, first_slab=0, into=None, tn_pref=1024, tt_pref=2048,
           a_is_transposed=False, jobs=()):
    T, N = b.shape
    K = a.shape[0] if a_is_transposed else a.shape[1]
    tt = _tile(T, tt_pref)
    tko = _tile(K, 1024)
    tn = _tile(slab_cols or N, tn_pref)
    nt = T // tt

    def body(a_ref, b_ref, *rest):
        o_ref, acc_ref = rest[-2:]
        t = pl.program_id(2)

        @pl.when(t == 0)
        def _():
            acc_ref[...] = jnp.zeros_like(acc_ref)

        acc_ref[...] += (_dot_nn if a_is_transposed else _dot_tn)(a_ref[...], b_ref[...])

        @pl.when(t == nt - 1)
        def _():
            o_ref[...] = acc_ref[...].astype(o_ref.dtype)

    if slab_cols is None:
        shape = (K, N)
        ospec = pl.BlockSpec((tko, tn), lambda k, jn, t: (k, jn))
    else:
        nb = slab_cols // tn
        shape = (n_slabs or N // slab_cols, K, slab_cols)
        ospec = pl.BlockSpec((None, tko, tn), lambda k, jn, t: (first_slab + jn // nb, k, jn % nb))
    aspec = pl.BlockSpec((tko, tt), lambda k, jn, t: (k, t)) if a_is_transposed else pl.BlockSpec((tt, tko), lambda k, jn, t: (t, k))
    in_specs, args = [aspec, pl.BlockSpec((tt, tn), lambda k, jn, t: (t, jn))], [a, b]
    if into is not None:
        in_specs.append(ANY)
        args.append(into)
    return _pallas(body, name=name, grid=(K // tko, N // tn, nt), in_specs=in_specs, out_specs=ospec, out_shape=_sds(shape, BF16),
                   scratch_shapes=[pltpu.VMEM((tko, tn), F32)], aliases={2: 0} if into is not None else None, jobs=jobs)(*args)


def _norm_proj(x, g, w, name, jobs=()):
    T, D = x.shape
    nd, cb = w.shape[0], w.shape[2]
    tm = _tile(T, 1024)

    def body(x_ref, g_ref, w_ref, p_ref, ht_ref, h_ref):
        @pl.when(pl.program_id(1) == 0)
        def _():
            h = _rms_fwd(x_ref[...], g_ref[...])
            h_ref[...] = h.astype(BF16)
            ht_ref[...] = h.T.astype(BF16)

        p_ref[...] = _dot_nn(h_ref[...], w_ref[...])

    row = pl.BlockSpec((tm, D), lambda i, j: (i, 0))
    return _pallas(
        body, name=name, grid=(T // tm, nd),
        in_specs=[row, pl.BlockSpec((1, D), lambda i, j: (0, 0)),
                  pl.BlockSpec((None, D, cb), lambda i, j: (j, 0, 0))],
        out_specs=[pl.BlockSpec((tm, cb), lambda i, j: (i, j)), pl.BlockSpec((D, tm), lambda i, j: (0, i))],
        out_shape=[_sds((T, nd * cb), F32), _sds((D, T), BF16)],
        scratch_shapes=[pltpu.VMEM((tm, D), BF16)], jobs=jobs,
    )(x, g, w)


def _proj_bwd(dproj, w, x, g, dxo, name, jobs=()):
    T, D = x.shape
    nd, cb = w.shape[0], w.shape[2]
    tm = _tile(T, 1024)

    def body(dp_ref, w_ref, x_ref, g_ref, dxo_ref, dx_ref, dg_ref, acc_ref):
        i, j = pl.program_id(0), pl.program_id(1)

        @pl.when((i == 0) & (j == 0))
        def _():
            dg_ref[...] = jnp.zeros_like(dg_ref)

        @pl.when(j == 0)
        def _():
            acc_ref[...] = jnp.zeros_like(acc_ref)

        acc_ref[...] += _dot_nt(dp_ref[...], w_ref[...])

        @pl.when(j == nd - 1)
        def _():
            dxin, dgp = _rms_bwd(x_ref[...], g_ref[...], acc_ref[...])
            dx_ref[...] = dxo_ref[...] + dxin
            dg_ref[...] += dgp

    row = pl.BlockSpec((tm, D), lambda i, j: (i, 0))
    vec = pl.BlockSpec((1, D), lambda i, j: (0, 0))
    return _pallas(
        body, name=name, grid=(T // tm, nd),
        in_specs=[pl.BlockSpec((tm, cb), lambda i, j: (i, j)),
                  pl.BlockSpec((None, D, cb), lambda i, j: (j, 0, 0)), row, vec, row],
        out_specs=[row, vec],
        out_shape=[_sds((T, D), F32), _sds((1, D), F32)],
        scratch_shapes=[pltpu.VMEM((tm, D), F32)], jobs=jobs,
    )(dproj, w, x, g, dxo)


def _rope_tables(pos, invf, sgn, name):
    T = pos.shape[0]
    dk = invf.shape[1]
    tm = _tile(T, 1024)

    def body(p_ref, f_ref, s_ref, c_out, s_out):
        ang = p_ref[...].astype(F32) * f_ref[...]
        c_out[...] = jnp.cos(ang)
        s_out[...] = jnp.sin(ang) * s_ref[...]

    vec = pl.BlockSpec((1, dk), lambda i: (0, 0))
    out = pl.BlockSpec((tm, dk), lambda i: (i, 0))
    return _pallas(body, name=name, grid=(T // tm,), in_specs=[pl.BlockSpec((tm, 1), lambda i: (i, 0)), vec, vec],
                   out_specs=[out, out], out_shape=[_sds((T, dk), F32), _sds((T, dk), F32)])(pos, invf, sgn)


def _decay(lg, r0, tq, n):
    r = r0 + lax.broadcasted_iota(jnp.int32, (tq, n), 0)
    c = lax.broadcasted_iota(jnp.int32, (tq, n), 1)
    rc, cc = r // CHUNK, c // CHUNK
    d = (r - c).astype(F32)
    e = jnp.where(rc == cc, jnp.abs(d), d)
    return jnp.where(cc > rc, 0.0, jnp.exp(lg * e))


def _ret_fwd(proj, cosf, sinf, lgs, B, S, D, name, jobs=()):
    T = B * S
    dk, dv = D // 8, D // 4
    tq = _tile(S, 256)
    scale = dk ** -0.5

    def body(lg_ref, q_ref, k_ref, v_ref, c_ref, s_ref, o_ref, kr_ref, vb_ref):
        lg = lg_ref[pl.program_id(1)]
        kr_ref[...] = (_rot(k_ref[...], c_ref[...], s_ref[...]) * scale).astype(BF16)
        vb_ref[...] = v_ref[...].astype(BF16)
        for qi in range(S // tq):
            rows, n = slice(qi * tq, (qi + 1) * tq), (qi + 1) * tq
            q = _rot(q_ref[rows, :], c_ref[rows, :], s_ref[rows, :]).astype(BF16)
            p = (_dot_nt(q, kr_ref[:n, :]) * _decay(lg, qi * tq, tq, n)).astype(BF16)
            o_ref[rows, :] = _dot_nn(p, vb_ref[:n, :])

    return _pallas(
        body, name=name, grid=(B, HEADS),
        in_specs=[pl.BlockSpec(memory_space=pltpu.SMEM),
                  pl.BlockSpec((S, dk), lambda b, h: (b, h)),
                  pl.BlockSpec((S, dk), lambda b, h: (b, HEADS + h)),
                  pl.BlockSpec((S, dv), lambda b, h: (b, HEADS + h)),
                  pl.BlockSpec((S, dk), lambda b, h: (b, 0)),
                  pl.BlockSpec((S, dk), lambda b, h: (b, 0))],
        out_specs=pl.BlockSpec((S, dv), lambda b, h: (b, h)),
        out_shape=_sds((T, D), F32),
        scratch_shapes=[pltpu.VMEM((S, dk), BF16), pltpu.VMEM((S, dv), BF16)], jobs=jobs,
    )(lgs, proj, proj, proj, cosf, sinf)


def _ret_bwd(do, proj, cosf, sinf, lgs, dproj, B, S, D, name, jobs=()):
    dk, dv = D // 8, D // 4
    tq = _tile(S, 256)
    scale = dk ** -0.5

    def body(lg_ref, do_ref, q_ref, k_ref, v_ref, c_ref, s_ref, dproj_in, dproj_ref, kr_ref, vb_ref, dka_ref, dva_ref,
             dq_ref, dk_ref, dv_ref, copy_sems):
        b, h = pl.program_id(0), pl.program_id(1)
        lg = lg_ref[h]
        kr_ref[...] = (_rot(k_ref[...], c_ref[...], s_ref[...]) * scale).astype(BF16)
        vb_ref[...] = v_ref[...].astype(BF16)
        dka_ref[...] = jnp.zeros_like(dka_ref)
        dva_ref[...] = jnp.zeros_like(dva_ref)

        for qi in range(S // tq):
            rows, n = slice(qi * tq, (qi + 1) * tq), (qi + 1) * tq
            cq, sq = c_ref[rows, :], s_ref[rows, :]
            q = _rot(q_ref[rows, :], cq, sq).astype(BF16)
            dout = do_ref[rows, :].astype(BF16)
            w = _decay(lg, qi * tq, tq, n)
            p = (_dot_nt(q, kr_ref[:n, :]) * w).astype(BF16)
            ds = (_dot_nt(dout, vb_ref[:n, :]) * w).astype(BF16)
            dq_ref[rows, :] = _unrot(_dot_nn(ds, kr_ref[:n, :]), cq, sq).astype(BF16)
            dka_ref[:n, :] += _dot_tn(ds, q)
            dva_ref[:n, :] += _dot_tn(p, dout)

        dk_ref[...] = _unrot(dka_ref[...] * scale, c_ref[...], s_ref[...]).astype(BF16)
        dv_ref[...] = dva_ref[...].astype(BF16)
        rows = pl.ds(pl.multiple_of(b * S, S), S)
        copies = [pltpu.make_async_copy(dq_ref, dproj_ref.at[rows, pl.ds(pl.multiple_of(h * dk, dk), dk)], copy_sems.at[0]),
                  pltpu.make_async_copy(dk_ref, dproj_ref.at[rows, pl.ds(pl.multiple_of((HEADS + h) * dk, dk), dk)], copy_sems.at[1]),
                  pltpu.make_async_copy(dv_ref, dproj_ref.at[rows, pl.ds(pl.multiple_of((HEADS + h) * dv, dv), dv)], copy_sems.at[2])]
        for cp in copies:
            cp.start()
        for cp in copies:
            cp.wait()

    return _pallas(
        body, name=name, grid=(B, HEADS),
        in_specs=[pl.BlockSpec(memory_space=pltpu.SMEM),
                  pl.BlockSpec((S, dv), lambda b, h: (b, h)),
                  pl.BlockSpec((S, dk), lambda b, h: (b, h)),
                  pl.BlockSpec((S, dk), lambda b, h: (b, HEADS + h)),
                  pl.BlockSpec((S, dv), lambda b, h: (b, HEADS + h)),
                  pl.BlockSpec((S, dk), lambda b, h: (b, 0)),
                  pl.BlockSpec((S, dk), lambda b, h: (b, 0)), ANY],
        out_specs=ANY, out_shape=_sds(dproj.shape, BF16),
        scratch_shapes=[pltpu.VMEM((S, dk), BF16), pltpu.VMEM((S, dv), BF16), pltpu.VMEM((S, dk), F32), pltpu.VMEM((S, dv), F32),
                        pltpu.VMEM((S, dk), BF16), pltpu.VMEM((S, dk), BF16), pltpu.VMEM((S, dv), BF16), pltpu.SemaphoreType.DMA((3,))],
        aliases={7: 0}, jobs=jobs,
    )(lgs, do, proj, proj, proj, cosf, sinf, dproj)


COL_GRET, COL_SCB, COL_SCC, COL_SCX, COL_GLUA, COL_GLUB, COL_GL = 2, 3, 4, 5, 6, 7, 8


SUBLANES, LANES = 8, 128
CONV_BLOCK = 256
SC_FWD = tuple(HALO - (SC_KERNEL - 1) + j for j in range(SC_KERNEL))
CF_FWD = tuple(HALO - (CF_KERNEL - 1) + j for j in range(CF_KERNEL))
SC_BWD = tuple(SC_KERNEL - 1 - j for j in range(SC_KERNEL))
CF_BWD = tuple(CF_KERNEL - 1 - j for j in range(CF_KERNEL))


def _build_shifted(src_ref, e_ref, shifts):
    n = e_ref.shape[1]
    for b in sorted({s % SUBLANES for s in shifts} - {0}):
        e_ref[b - 1] = src_ref[pl.ds(b, n), :]


def _shifted_rows(src_ref, e_ref, s, r0, rows, lanes):
    a, b = divmod(s, SUBLANES)
    at = pl.ds(r0 + SUBLANES * a, rows)
    return src_ref[at, lanes] if b == 0 else e_ref[b - 1, at, lanes]


def _tree_sum(terms):
    while len(terms) > 1:
        terms = [a + b for a, b in zip(terms[::2], terms[1::2])] + ([terms[-1]] if len(terms) % 2 else [])
    return terms[0]


def _conv_taps(src_ref, e_ref, w_ref, shifts, tm, D, emit, u_ref=None, dw_ref=None):
    rows = min(tm, CONV_BLOCK if u_ref is None else CONV_BLOCK // 2)
    for lt in range(D // LANES):
        lanes = pl.ds(lt * LANES, LANES)
        for r0 in range(0, tm, rows):
            u = None if u_ref is None else u_ref[pl.ds(r0, rows), lanes]
            acc = None
            for j, s in enumerate(shifts):
                v = _shifted_rows(src_ref, e_ref, s, r0, rows, lanes)
                term = v * w_ref[j:j + 1, lanes]
                acc = term if acc is None else acc + term
                if u is not None:
                    prod = u * v
                    part = _tree_sum([prod[g * SUBLANES:(g + 1) * SUBLANES, :] for g in range(rows // SUBLANES)])
                    dw_ref[j:j + 1, lanes] += jnp.sum(part, axis=0, keepdims=True)
            emit(lt, pl.ds(r0, rows), acc)


def _mixer_mid_fwd(proj, o, sc_w, cf_w, cf_b, ln_g, ln_b, S, name, jobs=()):
    T, D = o.shape
    dv = D // HEADS
    tm = _tile(S, 256)
    per_seq = S // tm
    hb = tm // HALO

    def body(gret_ref, scb_ref, scc_ref, scx_ref, ga_ref, gb_ref, scc_h, scx_h, ga_h, gb_h, o_ref,
             scw_ref, cfw_ref, cfb_ref, lng_ref, lnb_ref, ya_ref, yb_ref, yc_ref, cv_ref, u1_ref, ext_ref, e_ref):
        first = (pl.program_id(0) % per_seq) == 0
        keep = jnp.where(first, 0.0, 1.0)
        gr = gret_ref[...]
        sg = gr * _sigmoid(gr)
        for hh in range(HEADS):
            cols = slice(hh * dv, (hh + 1) * dv)
            hn, _ = _ln_stats(o_ref[:, cols])
            ya_ref[:, cols] = (sg[:, cols] * hn).astype(BF16)
        ext_ref[:HALO, :] = scc_h[...] * scx_h[...] * keep
        ext_ref[HALO:, :] = scc_ref[...] * scx_ref[...]
        _build_shifted(ext_ref, e_ref, SC_FWD)

        def emit_cv(lt, rows, cv):
            lanes = pl.ds(lt * LANES, LANES)
            cv_ref[rows, lanes] = cv
            yb_ref[rows, lanes] = (scb_ref[rows, lanes] * cv).astype(BF16)

        _conv_taps(ext_ref, e_ref, scw_ref, SC_FWD, tm, D, emit_cv)
        ext_ref[:HALO, :] = ga_h[...] * _sigmoid(gb_h[...]) * keep
        ext_ref[HALO:, :] = ga_ref[...] * _sigmoid(gb_ref[...])
        _build_shifted(ext_ref, e_ref, CF_FWD)

        def emit_u1(lt, rows, acc):
            lanes = pl.ds(lt * LANES, LANES)
            u1_ref[rows, lanes] = acc + cfb_ref[:, lanes]

        _conv_taps(ext_ref, e_ref, cfw_ref, CF_FWD, tm, D, emit_u1)
        xh, _ = _ln_stats(u1_ref[...])
        u2 = xh * lng_ref[...] + lnb_ref[...]
        yc_ref[...] = (u2 * _sigmoid(u2)).astype(BF16)

    def colblk(c):
        return pl.BlockSpec((tm, D), lambda i: (i, c))

    def halo(c):
        return pl.BlockSpec((HALO, D), lambda i: (jnp.maximum(i * hb - 1, 0), c))

    row = pl.BlockSpec((tm, D), lambda i: (i, 0))
    vec = pl.BlockSpec((1, D), lambda i: (0, 0))
    return _pallas(
        body, name=name, grid=(T // tm,),
        in_specs=[colblk(COL_GRET), colblk(COL_SCB), colblk(COL_SCC), colblk(COL_SCX), colblk(COL_GLUA), colblk(COL_GLUB),
                  halo(COL_SCC), halo(COL_SCX), halo(COL_GLUA), halo(COL_GLUB), row,
                  pl.BlockSpec((SC_KERNEL, D), lambda i: (0, 0)), pl.BlockSpec((CF_KERNEL, D), lambda i: (0, 0)), vec, vec, vec],
        out_specs=[row, row, row, row, row],
        out_shape=[_sds((T, D), BF16)] * 3 + [_sds((T, D), F32)] * 2,
        scratch_shapes=[pltpu.VMEM((HALO + tm, D), F32), pltpu.VMEM((SUBLANES - 1, HALO + tm - SUBLANES, D), F32)], jobs=jobs,
    )(proj, proj, proj, proj, proj, proj, proj, proj, proj, proj, o, sc_w, cf_w, cf_b, ln_g, ln_b)


def _mixer_merge_fwd(x, proj, ya_in, yb_in, yc_in, g_post, w_ret, w_sc, w_cf, w_o, name, jobs=()):
    T, D = x.shape
    tm = _tile(T, 512)

    def body(x_ref, g0_ref, g1_ref, g2_ref, ya_in_ref, yb_in_ref, yc_in_ref, gp_ref, wr_ref, ws_ref, wc_ref, wo_ref,
             xo_ref, ya_ref, yb_ref, yc_ref, mg_ref, m_ref):
        ya = _dot_nn(ya_in_ref[...], wr_ref[...])
        yb = _dot_nn(yb_in_ref[...], ws_ref[...])
        yc = _dot_nn(yc_in_ref[...], wc_ref[...])
        ya_ref[...] = ya.astype(BF16)
        yb_ref[...] = yb.astype(BF16)
        yc_ref[...] = yc.astype(BF16)
        merged = (_sigmoid(g0_ref[...]) * ya + _sigmoid(g1_ref[...]) * yb + _sigmoid(g2_ref[...]) * yc).astype(BF16)
        mg_ref[...] = merged
        m = _dot_nn(merged, wo_ref[...])
        m_ref[...] = m
        xo_ref[...] = x_ref[...] + _rms_fwd(m, gp_ref[...])

    row = pl.BlockSpec((tm, D), lambda i: (i, 0))
    wsp = pl.BlockSpec((D, D), lambda i: (0, 0), pipeline_mode=pl.Buffered(1))

    def colblk(c):
        return pl.BlockSpec((tm, D), lambda i: (i, c))

    return _pallas(
        body, name=name, grid=(T // tm,),
        in_specs=[row, colblk(COL_GL), colblk(COL_GL + 1), colblk(COL_GL + 2), row, row, row,
                  pl.BlockSpec((1, D), lambda i: (0, 0)), wsp, wsp, wsp, wsp],
        out_specs=[row] * 6,
        out_shape=[_sds((T, D), F32)] + [_sds((T, D), BF16)] * 4 + [_sds((T, D), F32)], jobs=jobs,
    )(x, proj, proj, proj, ya_in, yb_in, yc_in, g_post, w_ret, w_sc, w_cf, w_o)


def _mixer_bwd_a(dxo, m, g_post, ya, yb, yc, proj, o, cv, u1, ln_g, ln_b, w_ret, w_sc, w_cf, w_o, name, jobs=()):
    T, D = m.shape
    dv = D // HEADS
    tm = _tile(T, 256)

    def body(dxo_ref, m_ref, gp_ref, ya_ref, yb_ref, yc_ref, g0_ref, g1_ref, g2_ref, gret_ref, scb_ref, o_ref, cv_ref, u1_ref,
             lng_ref, lnb_ref, wr_ref, ws_ref, wc_ref, wo_ref,
             dm_ref, dya_ref, dyb_ref, dyc_ref, dproj_ref, do_ref, dcv_ref, du1_ref, dgp_ref, dlng_ref, dlnb_ref, dcfb_ref,
             d2_ref, dgl_ref, copy_sems):
        @pl.when(pl.program_id(0) == 0)
        def _():
            for r in (dgp_ref, dlng_ref, dlnb_ref, dcfb_ref):
                r[...] = jnp.zeros_like(r)

        dm, dgp = _rms_bwd(m_ref[...], gp_ref[...], dxo_ref[...])
        dgp_ref[...] += dgp
        dmb = dm.astype(BF16)
        dm_ref[...] = dmb
        dmerged = _dot_nt(dmb, wo_ref[...])
        dys = []
        for k, (g_ref, y_ref, dy_ref) in enumerate(((g0_ref, ya_ref, dya_ref), (g1_ref, yb_ref, dyb_ref), (g2_ref, yc_ref, dyc_ref))):
            sg = _sigmoid(g_ref[...])
            dgl_ref[:, k * D:(k + 1) * D] = (dmerged * y_ref[...].astype(F32) * sg * (1.0 - sg)).astype(BF16)
            dy = (dmerged * sg).astype(BF16)
            dy_ref[...] = dy
            dys.append(dy)
        dya_in = _dot_nt(dys[0], wr_ref[...])
        gr = gret_ref[...]
        sr = _sigmoid(gr)
        for hh in range(HEADS):
            cols = slice(hh * dv, (hh + 1) * dv)
            hn, rstd = _ln_stats(o_ref[:, cols])
            d2_ref[:, cols] = (dya_in[:, cols] * hn * _dsilu(gr[:, cols], sr[:, cols])).astype(BF16)
            do_ref[:, cols] = _ln_bwd(hn, rstd, dya_in[:, cols] * gr[:, cols] * sr[:, cols])
        dyb_in = _dot_nt(dys[1], ws_ref[...])
        d2_ref[:, D:] = (dyb_in * cv_ref[...]).astype(BF16)
        dcv_ref[...] = dyb_in * scb_ref[...]
        dyc_in = _dot_nt(dys[2], wc_ref[...])
        xh, rstd = _ln_stats(u1_ref[...])
        u2 = xh * lng_ref[...] + lnb_ref[...]
        du2 = dyc_in * _dsilu(u2, _sigmoid(u2))
        dlng_ref[...] += jnp.sum(du2 * xh, axis=0, keepdims=True)
        dlnb_ref[...] += jnp.sum(du2, axis=0, keepdims=True)
        du1 = _ln_bwd(xh, rstd, du2 * lng_ref[...])
        du1_ref[...] = du1
        dcfb_ref[...] += jnp.sum(du1, axis=0, keepdims=True)
        rows = pl.ds(pl.multiple_of(pl.program_id(0) * tm, tm), tm)
        copies = [pltpu.make_async_copy(d2_ref, dproj_ref.at[rows, pl.ds(COL_GRET * D, 2 * D)], copy_sems.at[0]),
                  pltpu.make_async_copy(dgl_ref, dproj_ref.at[rows, pl.ds(COL_GL * D, 3 * D)], copy_sems.at[1])]
        for cp in copies:
            cp.start()
        for cp in copies:
            cp.wait()

    row = pl.BlockSpec((tm, D), lambda i: (i, 0))
    vec = pl.BlockSpec((1, D), lambda i: (0, 0))
    wsp = pl.BlockSpec((D, D), lambda i: (0, 0), pipeline_mode=pl.Buffered(1))

    def colblk(c):
        return pl.BlockSpec((tm, D), lambda i: (i, c))

    return _pallas(
        body, name=name, grid=(T // tm,),
        in_specs=[row, row, vec, row, row, row, colblk(COL_GL), colblk(COL_GL + 1), colblk(COL_GL + 2),
                  colblk(COL_GRET), colblk(COL_SCB), row, row, row, vec, vec, wsp, wsp, wsp, wsp],
        out_specs=[row, row, row, row, ANY, row, row, row, vec, vec, vec, vec],
        out_shape=[_sds((T, D), BF16)] * 4 + [_sds((T, proj.shape[1]), BF16)] + [_sds((T, D), F32)] * 3 + [_sds((1, D), F32)] * 4,
        scratch_shapes=[pltpu.VMEM((tm, 2 * D), BF16), pltpu.VMEM((tm, 3 * D), BF16), pltpu.SemaphoreType.DMA((2,))], jobs=jobs,
    )(dxo, m, g_post, ya, yb, yc, proj, proj, proj, proj, proj, o, cv, u1, ln_g, ln_b, w_ret, w_sc, w_cf, w_o)


def _mixer_bwd_b(dcv, du1, proj, sc_w, cf_w, dproj, S, name, jobs=()):
    T, D = dcv.shape
    tm = _tile(S, 256)
    per_seq = S // tm
    hb = tm // HALO
    last_hb = T // HALO - 1

    def body(dcv_ref, du1_ref, dcv_n, du1_n, scc_ref, scx_ref, ga_ref, gb_ref, scw_ref, cfw_ref, dproj_in,
             d4_ref, dscw_ref, dcfw_ref, u_ref, dext_ref, e_ref, sb_ref):
        i = pl.program_id(0)
        keep_next = jnp.where((i % per_seq) == per_seq - 1, 0.0, 1.0)

        @pl.when(i == 0)
        def _():
            dscw_ref[...] = jnp.zeros_like(dscw_ref)
            dcfw_ref[...] = jnp.zeros_like(dcfw_ref)

        dext_ref[:tm, :] = dcv_ref[...]
        dext_ref[tm:, :] = dcv_n[...] * keep_next
        _build_shifted(dext_ref, e_ref, SC_BWD)
        u_ref[...] = scc_ref[...] * scx_ref[...]

        def emit_dz(lt, rows, dz):
            lanes = pl.ds(lt * LANES, LANES)
            d4_ref[rows, pl.ds(lt * LANES, LANES)] = (dz * scx_ref[rows, lanes]).astype(BF16)
            d4_ref[rows, pl.ds(D + lt * LANES, LANES)] = (dz * scc_ref[rows, lanes]).astype(BF16)

        _conv_taps(dext_ref, e_ref, scw_ref, SC_BWD, tm, D, emit_dz, u_ref=u_ref, dw_ref=dscw_ref)
        dext_ref[:tm, :] = du1_ref[...]
        dext_ref[tm:, :] = du1_n[...] * keep_next
        _build_shifted(dext_ref, e_ref, CF_BWD)
        sb_ref[...] = _sigmoid(gb_ref[...])
        u_ref[...] = ga_ref[...] * sb_ref[...]

        def emit_du0(lt, rows, du0):
            lanes = pl.ds(lt * LANES, LANES)
            sb = sb_ref[rows, lanes]
            d4_ref[rows, pl.ds(2 * D + lt * LANES, LANES)] = (du0 * sb).astype(BF16)
            d4_ref[rows, pl.ds(3 * D + lt * LANES, LANES)] = (du0 * ga_ref[rows, lanes] * sb * (1.0 - sb)).astype(BF16)

        _conv_taps(dext_ref, e_ref, cfw_ref, CF_BWD, tm, D, emit_du0, u_ref=u_ref, dw_ref=dcfw_ref)

    row = pl.BlockSpec((tm, D), lambda i: (i, 0))
    nxt = pl.BlockSpec((HALO, D), lambda i: (jnp.minimum((i + 1) * hb, last_hb), 0))

    def colblk(c):
        return pl.BlockSpec((tm, D), lambda i: (i, c))

    return _pallas(
        body, name=name, grid=(T // tm,),
        in_specs=[row, row, nxt, nxt, colblk(COL_SCC), colblk(COL_SCX), colblk(COL_GLUA), colblk(COL_GLUB),
                  pl.BlockSpec((SC_KERNEL, D), lambda i: (0, 0)), pl.BlockSpec((CF_KERNEL, D), lambda i: (0, 0)), ANY],
        out_specs=[pl.BlockSpec((tm, 4 * D), lambda i: (i, COL_SCC // 4)), pl.BlockSpec((SC_KERNEL, D), lambda i: (0, 0)),
                   pl.BlockSpec((CF_KERNEL, D), lambda i: (0, 0))],
        out_shape=[_sds(dproj.shape, BF16), _sds((SC_KERNEL, D), F32), _sds((CF_KERNEL, D), F32)],
        scratch_shapes=[pltpu.VMEM((tm, D), F32), pltpu.VMEM((tm + HALO, D), F32),
                        pltpu.VMEM((SUBLANES - 1, HALO + tm - SUBLANES, D), F32), pltpu.VMEM((tm, D), F32)],
        aliases={10: 0}, jobs=jobs,
    )(dcv, du1, dcv, du1, proj, proj, proj, proj, sc_w, cf_w, dproj)


def _loss_grad(y, tgt, name):
    T, D = y.shape
    tm = _tile(T, 512)

    def body(y_ref, t_ref, dy_ref, loss_ref):
        @pl.when(pl.program_id(0) == 0)
        def _():
            loss_ref[...] = jnp.zeros_like(loss_ref)

        e = y_ref[...] - t_ref[...]
        dy_ref[...] = e * (1.0 / D)
        loss_ref[...] += 0.5 * jnp.sum(jnp.sum(e * e, axis=-1, keepdims=True) * (1.0 / D), axis=0, keepdims=True)

    row = pl.BlockSpec((tm, D), lambda i: (i, 0))
    return _pallas(body, name=name, grid=(T // tm,), in_specs=[row, row],
                   out_specs=[row, pl.BlockSpec((1, 1), lambda i: (0, 0))],
                   out_shape=[_sds((T, D), F32), _sds((1, 1), F32)])(y, tgt)


def _mesh_pos():
    return lax.axis_index("x"), lax.axis_index("y"), lax.axis_index("c")


def _other_chips(x, y):
    return [(1 - x, y), (x, 1 - y), (1 - x, 1 - y)]


def _allgather_job(shards, forward_at=None):
    nt = len(shards)

    def parts(ins, outs, send, recv):
        x, y, c = _mesh_pos()

        def slab(t, px, py, pc):
            return outs[t].at[4 * px + 2 * py + pc]

        def copy(t, k, block, to, src=None):
            return pltpu.make_async_remote_copy(
                src_ref=slab(t, *block) if src is None else src, dst_ref=slab(t, *block),
                send_sem=send.at[7 * t + k], recv_sem=recv.at[7 * t + k], device_id=to, device_id_type=MESH)

        return (x, y, c), (x, y, 1 - c), _other_chips(x, y), c, slab, copy

    def start(ins, outs, send, recv, loc):
        me, sibling, chips, c, slab, copy = parts(ins, outs, send, recv)
        for t in range(nt):
            pltpu.make_async_copy(ins[t], slab(t, *me), loc.at[t]).start()
            copy(t, 0, me, sibling, src=ins[t]).start()
            for j, chip in enumerate(chips):
                copy(t, 1 + j, me, (*chip, c), src=ins[t]).start()

    def forward(ins, outs, send, recv, loc):
        me, sibling, chips, c, slab, copy = parts(ins, outs, send, recv)
        for j, chip in enumerate(chips):
            for t in range(nt):
                copy(t, 1 + j, (*chip, c), me).wait_recv()
                copy(t, 4 + j, (*chip, c), sibling).start()

    def finish(ins, outs, send, recv, loc):
        me, sibling, chips, c, slab, copy = parts(ins, outs, send, recv)
        if forward_at is None:
            forward(ins, outs, send, recv, loc)
        for t in range(nt):
            copy(t, 0, sibling, me).wait_recv()
            for j, chip in enumerate(chips):
                copy(t, 4 + j, (*chip, 1 - c), me).wait_recv()
        for t in range(nt):
            copy(t, 0, me, sibling, src=ins[t]).wait_send()
            for j, chip in enumerate(chips):
                copy(t, 1 + j, me, (*chip, c), src=ins[t]).wait_send()
                copy(t, 4 + j, (*chip, c), sibling).wait_send()
            pltpu.make_async_copy(ins[t], slab(t, *me), loc.at[t]).wait()

    return _Job(shards, [_sds((N_DEV,) + s.shape, s.dtype) for s in shards], 7 * nt, nt, start, finish, ("sibling", "chips"),
                mid=None if forward_at is None else forward, mid_at=forward_at)


def _to_sibling_job(grads):
    nt = len(grads)

    def copies(ins, outs, send, recv):
        x, y, c = _mesh_pos()
        return [pltpu.make_async_remote_copy(src_ref=ins[t].at[:, 1 - c], dst_ref=outs[t], send_sem=send.at[t], recv_sem=recv.at[t],
                                             device_id=(x, y, 1 - c), device_id_type=MESH) for t in range(nt)]

    def start(ins, outs, send, recv, loc):
        for cp in copies(ins, outs, send, recv):
            cp.start()

    def finish(ins, outs, send, recv, loc):
        for cp in copies(ins, outs, send, recv):
            cp.wait()

    return _Job(grads, [_sds((N_CHIP,) + g.shape[2:], g.dtype) for g in grads], nt, 0, start, finish, ("sibling",))


def _to_chips_job(pieces):
    nt = len(pieces)
    pairs = [p for p, _, _ in pieces]

    def copies(ins, outs, send, recv, loc):
        x, y, c = _mesh_pos()

        def src(t, chip):
            return ins[t].at[chip, pl.ds(pieces[t][1], pieces[t][2])]

        remote = [pltpu.make_async_remote_copy(src_ref=src(t, 2 * px + py), dst_ref=outs[t].at[k], send_sem=send.at[3 * t + k],
                                               recv_sem=recv.at[3 * t + k], device_id=(px, py, c), device_id_type=MESH)
                  for t in range(nt) for k, (px, py) in enumerate(_other_chips(x, y))]
        local = [pltpu.make_async_copy(src(t, 2 * x + y), outs[t].at[3], loc.at[t]) for t in range(nt)]
        return remote + local

    def start(ins, outs, send, recv, loc):
        for cp in copies(ins, outs, send, recv, loc):
            cp.start()

    def finish(ins, outs, send, recv, loc):
        for cp in copies(ins, outs, send, recv, loc):
            cp.wait()

    return _Job(pairs, [_sds((N_CHIP, rows, p.shape[2]), p.dtype) for p, _, rows in pieces], 3 * nt, nt, start, finish, ("chips",))


def _run_jobs(jobs, name):
    def body(o_ref):
        o_ref[...] = jnp.zeros_like(o_ref)

    _, jres = _pallas(body, name=name, grid=(1,), in_specs=[], out_specs=pl.BlockSpec((8, 128), lambda i: (0, 0)),
                      out_shape=_sds((8, 128), F32), jobs=jobs)()
    return jres


def _allgather_small(v, name):
    R, C = v.shape

    def body(x_ref, out_ref, send_sems, recv_sems, local_sem):
        x, y, c = _mesh_pos()
        me, sibling = (x, y, c), (x, y, 1 - c)
        chips = _other_chips(x, y)

        def slab(px, py, pc):
            return out_ref.at[4 * px + 2 * py + pc]

        def copy(k, block, to, src=None):
            return pltpu.make_async_remote_copy(
                src_ref=slab(*block) if src is None else src, dst_ref=slab(*block),
                send_sem=send_sems.at[k], recv_sem=recv_sems.at[k], device_id=to, device_id_type=MESH)

        mine = pltpu.make_async_copy(x_ref, slab(*me), local_sem)
        mine.start()
        first = [copy(0, me, sibling, src=x_ref)]
        first += [copy(1 + j, me, (*chip, c), src=x_ref) for j, chip in enumerate(chips)]
        for cp in first:
            cp.start()
        passed = [copy(4 + j, (*chip, c), sibling) for j, chip in enumerate(chips)]
        for j, chip in enumerate(chips):
            copy(1 + j, (*chip, c), me).wait_recv()
            passed[j].start()
        copy(0, sibling, me).wait_recv()
        for j, chip in enumerate(chips):
            copy(4 + j, (*chip, 1 - c), me).wait_recv()
        for cp in first + passed:
            cp.wait_send()
        mine.wait()

    return _comm_call(
        body, name=name, in_specs=[pl.BlockSpec(memory_space=pltpu.VMEM)], out_specs=pl.BlockSpec(memory_space=pltpu.VMEM),
        out_shape=_sds((N_DEV, R, C), v.dtype),
        scratch_shapes=[pltpu.SemaphoreType.DMA((7,)), pltpu.SemaphoreType.DMA((7,)), pltpu.SemaphoreType.DMA],
    )(v)


def _pair_sum(core, grad, got, name):
    _, _, R, C = grad.shape
    tr = _tile(R, 1024)

    def body(core_ref, mine_ref, got_ref, out_ref):
        out_ref[...] = (mine_ref[...].astype(F32) + got_ref[...].astype(F32)).astype(BF16)

    blk = (None, tr, C)
    return _pallas(
        body, name=name, grid=(N_CHIP, R // tr), nprefetch=1,
        in_specs=[pl.BlockSpec((None, None, tr, C), lambda k, r, core: (k, core[0], r, 0)),
                  pl.BlockSpec(blk, lambda k, r, core: (k, r, 0))],
        out_specs=pl.BlockSpec(blk, lambda k, r, core: (k, r, 0)),
        out_shape=_sds((N_CHIP, R, C), BF16),
    )(core, grad, got)


def _adamw(w, g, m, v):
    m = ADAM_B1 * m + (1.0 - ADAM_B1) * g
    v = ADAM_B2 * v + (1.0 - ADAM_B2) * (g * g)
    m_hat = m / (1.0 - ADAM_B1 ** ADAM_STEP)
    v_hat = v / (1.0 - ADAM_B2 ** ADAM_STEP)
    delta = -ADAM_LR * (m_hat / (jnp.sqrt(v_hat) + ADAM_EPS) + ADAM_WD * w)
    return delta, m, v


def _final_adamw(sums, w, m, v, l, r0, into, name, jobs=()):
    L, R, C = w.shape
    rows = sums.shape[1]
    tr = _tile(rows, 512 if C <= 1024 else 256)
    assert r0 % tr == 0
    b0 = r0 // tr

    def body(s_ref, w_ref, m_ref, v_ref, *rest):
        g_out, d_out, m_out, v_out = rest[-4:]
        g = s_ref[0].astype(F32)
        for k in range(1, N_CHIP):
            g = g + s_ref[k].astype(F32)
        d, mn, vn = _adamw(w_ref[...], g, m_ref[...], v_ref[...])
        g_out[...] = g
        d_out[...] = d
        m_out[...] = mn
        v_out[...] = vn

    loc = pl.BlockSpec((None, tr, C), lambda r: (l, b0 + r, 0))
    in_specs = [pl.BlockSpec((N_CHIP, tr, C), lambda r: (0, r, 0)), loc, loc, loc]
    args = [sums, w, m, v]
    aliases = None
    if into is not None:
        in_specs += [ANY] * 4
        args += list(into)
        aliases = {4 + i: i for i in range(4)}
    return _pallas(body, name=name, grid=(rows // tr,), in_specs=in_specs, out_specs=[loc] * 4,
                   out_shape=[_sds((L, R, C), F32)] * 4, aliases=aliases, jobs=jobs)(*args)


def _small_adamw(dev, allg, w, m, v, width, name):
    R = w.shape[0]

    def body(dev_ref, a_ref, w_ref, m_ref, v_ref, g_out, d_out, m_out, v_out):
        g = a_ref[0]
        for k in range(1, N_DEV):
            g = g + a_ref[k]
        d, mn, vn = _adamw(w_ref[...], g, m_ref[...], v_ref[...])
        g_out[...] = g
        d_out[...] = d
        m_out[...] = mn
        v_out[...] = vn

    full = allg.shape[2] == width
    loc = pl.BlockSpec((R, width), lambda i, dev: (0, 0))
    return _pallas(
        body, name=name, grid=(1,), nprefetch=1,
        in_specs=[pl.BlockSpec((N_DEV, R, width), lambda i, dev: (0, 0, 0 if full else dev[0])), loc, loc, loc],
        out_specs=[loc] * 4, out_shape=[_sds((R, width), F32)] * 4,
    )(dev, allg, w, m, v)


BIG = ("ffn1_w_gu", "ffn1_w_down", "w_in", "w_ret_o", "w_sc_o", "w_cf_o", "w_o", "ffn2_w_gu", "ffn2_w_down")
FFN1 = ("ffn1_w_gu", "ffn1_w_down")
MIX_OUT = ("w_ret_o", "w_sc_o", "w_cf_o", "w_o")
ROW_BLOCKS = ("ffn1_w_down", "ffn2_w_down") + MIX_OUT


def kernel(x, positions, norm_g, ffn1_w_gu, ffn1_w_down, w_in, w_ret_o, sc_conv_w, w_sc_o, cf_dw_w, cf_dw_b, cf_ln_g, cf_ln_b, w_cf_o, w_o, ffn2_w_gu, ffn2_w_down, loss_target, m_norm_g, m_ffn1_w_gu, m_ffn1_w_down, m_w_in, m_w_ret_o, m_sc_conv_w, m_w_sc_o, m_cf_dw_w, m_cf_dw_b, m_cf_ln_g, m_cf_ln_b, m_w_cf_o, m_w_o, m_ffn2_w_gu, m_ffn2_w_down, v_norm_g, v_ffn1_w_gu, v_ffn1_w_down, v_w_in, v_w_ret_o, v_sc_conv_w, v_w_sc_o, v_cf_dw_w, v_cf_dw_b, v_cf_ln_g, v_cf_ln_b, v_w_cf_o, v_w_o, v_ffn2_w_gu, v_ffn2_w_down):
    B, S, D = x.shape
    T = B * S
    L = norm_g.shape[0]
    DL = norm_g.shape[2]
    dk = D // 8
    mx, my, mc = _mesh_pos()
    dev_idx = jnp.reshape(4 * mx + 2 * my + mc, (1,)).astype(jnp.int32)
    core_idx = jnp.reshape(mc, (1,)).astype(jnp.int32)

    w32 = dict(ffn1_w_gu=ffn1_w_gu, ffn1_w_down=ffn1_w_down, w_in=w_in, w_ret_o=w_ret_o, w_sc_o=w_sc_o, w_cf_o=w_cf_o,
               w_o=w_o, ffn2_w_gu=ffn2_w_gu, ffn2_w_down=ffn2_w_down)
    m32 = dict(ffn1_w_gu=m_ffn1_w_gu, ffn1_w_down=m_ffn1_w_down, w_in=m_w_in, w_ret_o=m_w_ret_o, w_sc_o=m_w_sc_o,
               w_cf_o=m_w_cf_o, w_o=m_w_o, ffn2_w_gu=m_ffn2_w_gu, ffn2_w_down=m_ffn2_w_down)
    v32 = dict(ffn1_w_gu=v_ffn1_w_gu, ffn1_w_down=v_ffn1_w_down, w_in=v_w_in, w_ret_o=v_w_ret_o, w_sc_o=v_w_sc_o,
               w_cf_o=v_w_cf_o, w_o=v_w_o, ffn2_w_gu=v_ffn2_w_gu, ffn2_w_down=v_ffn2_w_down)

    W = [dict() for _ in range(L)]

    def gather(l, names, forward_at=None):
        return _allgather_job([w32[n][l].astype(BF16) for n in names], forward_at)

    def place(l, names, slabs):
        for n, a in zip(names, slabs):
            W[l][n] = a.reshape(a.shape[0] * a.shape[1], a.shape[2]) if n in ROW_BLOCKS else a

    def carried(fn, *args, jobs, **kw):
        out = fn(*args, jobs=jobs, **kw)
        return out if jobs else (out, [])

    n_sh = 6 + SC_KERNEL + CF_KERNEL
    small_sh = jnp.concatenate([norm_g, sc_conv_w, cf_dw_w], axis=1).reshape(L * n_sh, DL)
    sh_all = _allgather_small(small_sh, "allgather_small_params")
    sh_full = jnp.transpose(sh_all, (1, 0, 2)).reshape(L, n_sh, D)
    norm_full = sh_full[:, :6]
    scw_full = sh_full[:, 6:6 + SC_KERNEL]
    cfw_full = sh_full[:, 6 + SC_KERNEL:]
    place(0, FFN1, _run_jobs([gather(0, FFN1)], "allgather_first")[0])

    half = dk // 2
    inv_freq = ROPE_BASE ** (-jnp.arange(half, dtype=F32) / half)
    invf = jnp.concatenate([inv_freq, inv_freq])[None, :]
    sgn = jnp.concatenate([-jnp.ones((half,), F32), jnp.ones((half,), F32)])[None, :]
    cosf, sinf = _rope_tables(positions.reshape(T, 1), invf, sgn, "rope_tables")
    lgs = jnp.log(1.0 - 2.0 ** (-5.0 - jnp.arange(HEADS, dtype=F32)))

    def vec(a):
        return a.reshape(1, D)

    xc = x.reshape(T, D)
    saved = []
    for l in range(L):
        g = norm_full[l]
        sv = {}
        sv["x0"] = xc
        (xc, sv["h1"], sv["gate1"], sv["up1"], sv["f1"]), jr = _ffn_fwd(
            xc, vec(g[0]), vec(g[1]), W[l]["ffn1_w_gu"], W[l]["ffn1_w_down"], f"ffn1_fwd_{l}", jobs=[gather(l, ("w_in",))])
        place(l, ("w_in",), jr[0])
        sv["x1"] = xc
        (proj, sv["h2"]), jr = _norm_proj(xc, vec(g[2]), W[l]["w_in"], f"proj_fwd_{l}", jobs=[gather(l, ("ffn2_w_gu",), EARLY_FORWARD)])
        place(l, ("ffn2_w_gu",), jr[0])
        sv["proj"] = proj
        o, jr = _ret_fwd(proj, cosf, sinf, lgs, B, S, D, f"retention_fwd_{l}", jobs=[gather(l, MIX_OUT)])
        place(l, MIX_OUT, jr[0])
        sv["o"] = o
        nxt = l + 1 < L
        (ya_in, yb_in, yc_in, sv["cv"], sv["u1"]), jr = _mixer_mid_fwd(
            proj, o, scw_full[l], cfw_full[l], vec(cf_dw_b[l]), vec(cf_ln_g[l]), vec(cf_ln_b[l]), S,
            f"mixer_mid_fwd_{l}", jobs=[gather(l, ("ffn2_w_down",), EARLY_FORWARD)])
        place(l, ("ffn2_w_down",), jr[0])
        sv["ya_in"], sv["yb_in"], sv["yc_in"] = ya_in, yb_in, yc_in
        (xc, sv["ya"], sv["yb"], sv["yc"], sv["merged"], sv["m"]), jr = carried(
            _mixer_merge_fwd, xc, proj, ya_in, yb_in, yc_in, vec(g[3]), W[l]["w_ret_o"], W[l]["w_sc_o"], W[l]["w_cf_o"], W[l]["w_o"],
            f"mixer_merge_fwd_{l}", jobs=[gather(l + 1, ("ffn1_w_down",))] if nxt else [])
        if nxt:
            place(l + 1, ("ffn1_w_down",), jr[0])
        sv["x2"] = xc
        (xc, sv["h3"], sv["gate2"], sv["up2"], sv["f2"]), jr = carried(
            _ffn_fwd, xc, vec(g[4]), vec(g[5]), W[l]["ffn2_w_gu"], W[l]["ffn2_w_down"], f"ffn2_fwd_{l}",
            jobs=[gather(l + 1, ("ffn1_w_gu",), EARLY_FORWARD)] if nxt else [])
        if nxt:
            place(l + 1, ("ffn1_w_gu",), jr[0])
        saved.append(sv)

    dx, loss_part = _loss_grad(xc, loss_target.reshape(T, D), "loss")
    loss = lax.psum(loss_part[0, 0], ("x", "y", "c"))

    to_sibling, to_chips, reduced = [], [], {}

    def grad_ready(n, l, g):
        if g.ndim == 2:
            g = g.reshape(N_DEV, g.shape[0] // N_DEV, g.shape[1])
        to_sibling.append((n, l, g.reshape(N_CHIP, 2, g.shape[1], g.shape[2])))

    tail = []

    def take_jobs(heavy, max_pieces=None):
        a = list(to_sibling)
        b = [e for e in to_chips if heavy or N_CHIP * e[4] * e[2].shape[2] <= LIGHT_CARRIER_ELEMS][:max_pieces]
        to_sibling.clear()
        to_chips[:] = [e for e in to_chips if not any(e is t for t in b)]
        jobs = ([_to_chips_job([e[2:] for e in b])] if b else []) + ([_to_sibling_job([g for _, _, g in a])] if a else [])
        return jobs, (a, b)

    def settle(taken, jres):
        a, b = taken
        if b:
            for (n, l, _, r0, rows), got in zip(b, jres[0]):
                reduced.setdefault((n, l), []).append((r0, got))
        if a:
            for (n, l, g), got in zip(a, jres[-1]):
                pairs = _pair_sum(core_idx, g, got, f"pair_sum_{n}_{l}")
                R = pairs.shape[1]
                k = TAIL_PIECES if tail and R % (TAIL_PIECES * SUBLANES) == 0 else 1
                to_chips.extend((n, l, pairs, i * (R // k), R // k) for i in range(k))

    def carrier(fn, *args, heavy=True, **kw):
        jobs, taken = take_jobs(heavy)
        out, jres = carried(fn, *args, jobs=jobs, **kw)
        settle(taken, jres)
        return out

    small_rows = [None] * L
    cb_gu = W[0]["ffn1_w_gu"].shape[2]
    cb_in = W[0]["w_in"].shape[2]

    def ffn_grads(tag, wgu_name, wd_name, dxo, x_in, ht, gate, up, f, g_pre, g_post, l):
        dxi, df, dgate, dup, act, dgpre, dgpost = carrier(_ffn_bwd, dxo, x_in, f, gate, up, g_pre, g_post, W[l][wgu_name],
                                                          W[l][wd_name], f"{tag}_bwd_{l}")
        grad_ready(wd_name, l, carrier(_mm_tn, act, df, f"{tag}_dwd_{l}", heavy=False))
        half = carrier(_mm_tn, ht, dgate, f"{tag}_dwg_{l}", slab_cols=cb_gu, n_slabs=N_DEV, a_is_transposed=True, heavy=False)
        grad_ready(wgu_name, l, carrier(_mm_tn, ht, dup, f"{tag}_dwu_{l}", slab_cols=cb_gu, n_slabs=N_DEV, first_slab=N_DEV // 2,
                                        into=half, a_is_transposed=True, heavy=False))
        return dxi, dgpre, dgpost

    for l in reversed(range(L)):
        sv = saved[l]
        g = norm_full[l]
        dx, dg4, dg5 = ffn_grads("ffn2", "ffn2_w_gu", "ffn2_w_down", dx, sv["x2"], sv["h3"], sv["gate2"], sv["up2"], sv["f2"],
                                 vec(g[4]), vec(g[5]), l)
        (dm, dya, dyb, dyc, dproj, do, dcv, du1, dg3, dlng, dlnb, dcfb) = carrier(
            _mixer_bwd_a, dx, sv["m"], vec(g[3]), sv["ya"], sv["yb"], sv["yc"], sv["proj"], sv["o"], sv["cv"], sv["u1"],
            vec(cf_ln_g[l]), vec(cf_ln_b[l]), W[l]["w_ret_o"], W[l]["w_sc_o"], W[l]["w_cf_o"], W[l]["w_o"], f"mixer_bwd_a_{l}")
        grad_ready("w_o", l, _mm_tn(sv["merged"], dm, f"dw_o_{l}", tt_pref=2048))
        grad_ready("w_ret_o", l, _mm_tn(sv["ya_in"], dya, f"dw_ret_o_{l}", tt_pref=2048))
        grad_ready("w_sc_o", l, _mm_tn(sv["yb_in"], dyb, f"dw_sc_o_{l}", tt_pref=2048))
        grad_ready("w_cf_o", l, _mm_tn(sv["yc_in"], dyc, f"dw_cf_o_{l}", tt_pref=2048))
        dproj, dscw, dcfw = carrier(_mixer_bwd_b, dcv, du1, sv["proj"], scw_full[l], cfw_full[l], dproj, S, f"mixer_bwd_b_{l}")
        dproj = carrier(_ret_bwd, do, sv["proj"], cosf, sinf, lgs, dproj, B, S, D, f"retention_bwd_{l}", heavy=False)
        grad_ready("w_in", l, carrier(_mm_tn, sv["h2"], dproj, f"dw_in_{l}", slab_cols=cb_in, tn_pref=cb_in, a_is_transposed=True))
        dx, dg2 = carrier(_proj_bwd, dproj, W[l]["w_in"], sv["x1"], vec(g[2]), dx, f"proj_bwd_{l}")
        dx, dg0, dg1 = ffn_grads("ffn1", "ffn1_w_gu", "ffn1_w_down", dx, sv["x0"], sv["h1"], sv["gate1"], sv["up1"], sv["f1"],
                                 vec(g[0]), vec(g[1]), l)
        small_rows[l] = (jnp.concatenate([dg0, dg1, dg2, dg3, dg4, dg5, dscw, dcfw], axis=0),
                         jnp.concatenate([dcfb, dlng, dlnb], axis=0))
    grad_x = dx.reshape(B, S, D)
    big_out, done = {}, set()

    def next_adamw():
        for n in BIG:
            for l in reversed(range(L)):
                for r0, got in reduced.get((n, l), []):
                    if (n, l, r0) not in done:
                        return n, l, r0, got
        return None

    def run_adamw(n, l, r0, got, jobs=()):
        out = _final_adamw(got, w32[n], m32[n], v32[n], l, r0, big_out.get(n), f"adamw_{n}_{l}_{r0}", jobs=jobs)
        big_out[n], jres = out if jobs else (out, [])
        done.add((n, l, r0))
        return jres

    flush = 0
    while to_sibling or to_chips:
        jobs, taken = take_jobs(True)
        settle(taken, _run_jobs(jobs, f"grads_flush_{flush}"))
        flush += 1
    while next_adamw():
        run_adamw(*next_adamw())

    sh_part = jnp.concatenate([small_rows[l][0] for l in range(L)], axis=0)
    rep_part = jnp.concatenate([small_rows[l][1] for l in range(L)] + [jnp.zeros((8 - 3 * L % 8, D), F32)] * (1 if 3 * L % 8 else 0), axis=0)
    sh_g = _allgather_small(sh_part, "allgather_small_grads")
    rep_g = _allgather_small(rep_part, "allgather_replicated_grads")

    def pack_sh(a, b, c):
        return jnp.concatenate([a, b, c], axis=1).reshape(L * n_sh, DL)

    def pack_rep(a, b, c):
        rows = jnp.stack([a, b, c], axis=1).reshape(3 * L, D)
        return jnp.concatenate([rows, jnp.ones((rep_part.shape[0] - 3 * L, D), F32)], axis=0)

    sh_res = _small_adamw(dev_idx, sh_g, pack_sh(norm_g, sc_conv_w, cf_dw_w), pack_sh(m_norm_g, m_sc_conv_w, m_cf_dw_w),
                          pack_sh(v_norm_g, v_sc_conv_w, v_cf_dw_w), DL, "adamw_small_sharded")
    rep_res = _small_adamw(dev_idx, rep_g, pack_rep(cf_dw_b, cf_ln_g, cf_ln_b), pack_rep(m_cf_dw_b, m_cf_ln_g, m_cf_ln_b),
                           pack_rep(v_cf_dw_b, v_cf_ln_g, v_cf_ln_b), D, "adamw_small_replicated")

    def unpack_sh(a):
        a = a.reshape(L, n_sh, DL)
        return {"norm_g": a[:, :6], "sc_conv_w": a[:, 6:6 + SC_KERNEL], "cf_dw_w": a[:, 6 + SC_KERNEL:]}

    def unpack_rep(a):
        a = a[:3 * L].reshape(L, 3, D)
        return {"cf_dw_b": a[:, 0], "cf_ln_g": a[:, 1], "cf_ln_b": a[:, 2]}

    order = ("norm_g", "ffn1_w_gu", "ffn1_w_down", "w_in", "w_ret_o", "sc_conv_w", "w_sc_o", "cf_dw_w", "cf_dw_b", "cf_ln_g",
             "cf_ln_b", "w_cf_o", "w_o", "ffn2_w_gu", "ffn2_w_down")
    outs = []
    for kind in range(4):
        small = {**unpack_sh(sh_res[kind]), **unpack_rep(rep_res[kind])}
        outs += [big_out[n][kind] if n in big_out else small[n] for n in order]
    return (loss, grad_x, *outs)
```
